```python
import math
import jax
import jax.numpy as jnp
from jax import lax
import numpy as np

D_MODEL = 2048
BATCH = 1
SEQ = 8192
DEPTH = 2

N_EVEN = (DEPTH + 1) // 2
N_ODD = DEPTH // 2
Q_BLOCK = 128
EPS = 1e-6
NEG_INF = -1e30

NSA_HEADS = 8
NSA_KV_HEADS = 2
NSA_GROUP = NSA_HEADS // NSA_KV_HEADS
NSA_HEAD_DIM = 128
CMP_LEN = 32
CMP_STRIDE = 16
CMP_HIDDEN = 256
SLC_LEN = 64
SLC_TOPN = 16
WINDOW = 512

MLA_HEADS = 8
MLA_Q_RANK = 512
MLA_KV_RANK = 512
MLA_NOPE = 128
MLA_ROPE = 64
MLA_V = 128
ROPE_THETA = 10000.0

FOX_HEADS = 16
FOX_HEAD_DIM = D_MODEL // FOX_HEADS

REL_BUCKETS = 32
REL_MAX_DIST = 4096

FFN_HIDDEN = ((8 * D_MODEL + 2) // 3 + 255) // 256 * 256

EVEN_SPLITS = (NSA_HEADS * NSA_HEAD_DIM, 3 * 2 * NSA_KV_HEADS * NSA_HEAD_DIM, 3 * NSA_HEADS,
               MLA_Q_RANK, MLA_KV_RANK, MLA_ROPE)
EVEN_IN = sum(EVEN_SPLITS)
EVEN_MIX = NSA_HEADS * NSA_HEAD_DIM + MLA_HEADS * MLA_V
FOX_SPLITS = (D_MODEL, D_MODEL, D_MODEL, FOX_HEADS, D_MODEL)
FOX_IN = sum(FOX_SPLITS)

kernel_name = 'hybrid_nsa_mla_fox_trunk'


def _split(a, sizes):
    return jnp.split(a, np.cumsum(sizes)[:-1].tolist(), axis=-1)


def rmsnorm(x, g):
    xf = x.astype(jnp.float32)
    y = xf * lax.rsqrt(jnp.mean(xf * xf, axis=-1, keepdims=True) + EPS)
    return (y * g.astype(jnp.float32)).astype(x.dtype)


def masked_softmax(logits, mask):
    l = jnp.where(mask, logits.astype(jnp.float32), NEG_INF)
    m = jnp.max(l, axis=-1, keepdims=True)
    p = jnp.where(mask, jnp.exp(l - m), 0.0)
    return p / jnp.maximum(jnp.sum(p, axis=-1, keepdims=True), 1e-30)


def t5_bucket(dist):
    max_exact = REL_BUCKETS // 2
    d = jnp.maximum(dist, 0)
    ratio = jnp.log(jnp.maximum(d, max_exact).astype(jnp.float32) / max_exact) / math.log(REL_MAX_DIST / max_exact)
    large = jnp.minimum(max_exact + (ratio * (REL_BUCKETS - max_exact)).astype(jnp.int32), REL_BUCKETS - 1)
    return jnp.where(d < max_exact, d, large)


def rope(x, positions):
    half = x.shape[-1] // 2
    inv = ROPE_THETA ** (-jnp.arange(half, dtype=jnp.float32) / half)
    ang = positions.astype(jnp.float32)[..., None] * inv
    ang = ang.reshape(ang.shape[:2] + (1,) * (x.ndim - 3) + (half,))
    cos, sin = jnp.cos(ang), jnp.sin(ang)
    x1, x2 = x[..., :half], x[..., half:]
    return jnp.concatenate([x1 * cos - x2 * sin, x1 * sin + x2 * cos], axis=-1).astype(x.dtype)


def swiglu(h, w1, w3, w2):
    return (jax.nn.silu(h @ w1) * (h @ w3)) @ w2


def sweep_query_blocks(fn, batch, seq):
    out = lax.map(fn, jnp.arange(seq // Q_BLOCK))
    return jnp.moveaxis(out, 0, 1).reshape(batch, seq, out.shape[-1])


def selection_map(n_cmp, n_slc):
    start = np.arange(n_cmp)[:, None] * CMP_STRIDE
    j0 = np.arange(n_slc)[None, :] * SLC_LEN
    return jnp.asarray(((start < j0 + SLC_LEN) & (start + CMP_LEN > j0)).astype(np.float32))


def nsa_compress(k, pos_emb, w1, w2):
    B, S, G, d = k.shape
    n_cmp = (S - CMP_LEN) // CMP_STRIDE + 1
    idx = np.arange(n_cmp)[:, None] * CMP_STRIDE + np.arange(CMP_LEN)[None, :]
    blk = k[:, idx] + pos_emb[:, None, :]
    blk = blk.transpose(0, 1, 3, 2, 4).reshape(B, n_cmp, G, CMP_LEN * d)
    return jax.nn.silu(blk @ w1) @ w2


def nsa_attend(q, kc, vc, ks, vs, kw, vw, gates, positions, rel_bias):
    B, S = q.shape[0], q.shape[1]
    G, R, d = NSA_KV_HEADS, NSA_GROUP, NSA_HEAD_DIM
    n_cmp, n_slc = kc.shape[1], S // SLC_LEN
    top_n = min(SLC_TOPN, n_slc)
    kwin = WINDOW + Q_BLOCK
    n_sel_tok = top_n * SLC_LEN
    scale = NSA_HEAD_DIM ** -0.5
    tbl = rel_bias.T
    tbl_g = tbl.reshape(G, R, REL_BUCKETS)
    cmp_end = jnp.arange(n_cmp) * CMP_STRIDE + (CMP_LEN - 1)
    cmp_pos = positions[:, cmp_end]
    sel_map = selection_map(n_cmp, n_slc)
    ks_blk = ks.reshape(B, n_slc, SLC_LEN, G, d).transpose(0, 3, 1, 2, 4)
    vs_blk = vs.reshape(B, n_slc, SLC_LEN, G, d).transpose(0, 3, 1, 2, 4)
    pad = ((0, 0), (WINDOW, 0), (0, 0), (0, 0))
    kw_pad, vw_pad = jnp.pad(kw, pad), jnp.pad(vw, pad)
    pos_pad = jnp.pad(positions, ((0, 0), (WINDOW, 0)))
    bi = jnp.arange(B)[:, None, None, None]
    gi = jnp.arange(G)[None, :, None, None]
    ri = jnp.arange(R)[None, None, :, None, None]
    blk_ids = jnp.arange(n_slc)

    def block(qb):
        qs = qb * Q_BLOCK
        t = qs + jnp.arange(Q_BLOCK)
        qq = lax.dynamic_slice_in_dim(q, qs, Q_BLOCK, 1)
        pq = lax.dynamic_slice_in_dim(positions, qs, Q_BLOCK, 1)

        bias_c = tbl[:, t5_bucket(pq[:, :, None] - cmp_pos[:, None, :])]
        bias_c = bias_c.transpose(1, 0, 2, 3).reshape(B, G, R, Q_BLOCK, n_cmp)
        logit_c = jnp.einsum('bqgrd,bngd->bgrqn', qq, kc).astype(jnp.float32) * scale + bias_c
        p_c = masked_softmax(logit_c, cmp_end[None, :] <= t[:, None])
        o_c = jnp.einsum('bgrqn,bngd->bqgrd', p_c.astype(vc.dtype), vc)

        imp = jnp.einsum('bgrqn,nj->bgqj', p_c, sel_map)
        cur = (t // SLC_LEN)[:, None]
        forced = (blk_ids == 0) | (blk_ids == cur) | (blk_ids == cur - 1)
        imp = jnp.where(forced, 1e9, jnp.where(blk_ids <= cur, imp, -1e9))
        _, sel = lax.top_k(imp, top_n)
        k_sel = ks_blk[bi, gi, sel].reshape(B, G, Q_BLOCK, n_sel_tok, d)
        v_sel = vs_blk[bi, gi, sel].reshape(B, G, Q_BLOCK, n_sel_tok, d)
        tok = (sel[..., None] * SLC_LEN + jnp.arange(SLC_LEN)).reshape(B, G, Q_BLOCK, n_sel_tok)
        bucket_s = t5_bucket(pq[:, None, :, None] - positions[bi, tok])
        bias_s = tbl_g[gi[..., None], ri, bucket_s[:, :, None]]
        logit_s = jnp.einsum('bqgrd,bgqkd->bgrqk', qq, k_sel).astype(jnp.float32) * scale + bias_s
        p_s = masked_softmax(logit_s, (tok <= t[:, None])[:, :, None])
        o_s = jnp.einsum('bgrqk,bgqkd->bqgrd', p_s.astype(v_sel.dtype), v_sel)

        kk = lax.dynamic_slice_in_dim(kw_pad, qs, kwin, 1)
        vv = lax.dynamic_slice_in_dim(vw_pad, qs, kwin, 1)
        pk = lax.dynamic_slice_in_dim(pos_pad, qs, kwin, 1)
        s_idx = qs - WINDOW + jnp.arange(kwin)
        rel = t[:, None] - s_idx[None, :]
        mask_w = (s_idx[None, :] >= 0) & (rel >= 0) & (rel < WINDOW)
        bias_w = tbl[:, t5_bucket(pq[:, :, None] - pk[:, None, :])]
        bias_w = bias_w.transpose(1, 0, 2, 3).reshape(B, G, R, Q_BLOCK, kwin)
        logit_w = jnp.einsum('bqgrd,bkgd->bgrqk', qq, kk).astype(jnp.float32) * scale + bias_w
        p_w = masked_softmax(logit_w, mask_w)
        o_w = jnp.einsum('bgrqk,bkgd->bqgrd', p_w.astype(vv.dtype), vv)

        g = lax.dynamic_slice_in_dim(gates, qs, Q_BLOCK, 1).reshape(B, Q_BLOCK, G, R, 3)
        o = g[..., 0:1] * o_c + g[..., 1:2] * o_s + g[..., 2:3] * o_w
        return o.reshape(B, Q_BLOCK, G * R * d)

    return sweep_query_blocks(block, B, S)


def mla_attend(q_nope, q_rope, k_nope, k_rope, v):
    B, S, H, dv = v.shape
    scale = (MLA_NOPE + MLA_ROPE) ** -0.5
    kpos = jnp.arange(S)

    def block(qb):
        qs = qb * Q_BLOCK
        t = qs + jnp.arange(Q_BLOCK)
        qn = lax.dynamic_slice_in_dim(q_nope, qs, Q_BLOCK, 1)
        qr = lax.dynamic_slice_in_dim(q_rope, qs, Q_BLOCK, 1)
        logit = (jnp.einsum('bqhd,bkhd->bhqk', qn, k_nope)
                 + jnp.einsum('bqhd,bkd->bhqk', qr, k_rope)).astype(jnp.float32) * scale
        p = masked_softmax(logit, kpos[None, :] <= t[:, None])
        return jnp.einsum('bhqk,bkhd->bqhd', p.astype(v.dtype), v).reshape(B, Q_BLOCK, H * dv)

    return sweep_query_blocks(block, B, S)


def nsa_mla_mixer(h, positions, rel_bias, w_in, w_out, gate_b,
                  pos_k, w1_k, w2_k, pos_v, w1_v, w2_v,
                  q_norm, w_uq, kv_norm, w_ukv):
    B, S, _ = h.shape
    G, R, d = NSA_KV_HEADS, NSA_GROUP, NSA_HEAD_DIM
    q_nsa, kv_nsa, g_nsa, cq, ckv, kr = _split(h @ w_in, EVEN_SPLITS)
    q_nsa = q_nsa.reshape(B, S, G, R, d)
    kv = kv_nsa.reshape(B, S, 3, 2, G, d)
    gates = jax.nn.sigmoid(g_nsa + gate_b).reshape(B, S, NSA_HEADS, 3)
    kc = nsa_compress(kv[:, :, 0, 0], pos_k, w1_k, w2_k)
    vc = nsa_compress(kv[:, :, 0, 1], pos_v, w1_v, w2_v)
    o_nsa = nsa_attend(q_nsa, kc, vc, kv[:, :, 1, 0], kv[:, :, 1, 1], kv[:, :, 2, 0], kv[:, :, 2, 1],
                       gates, positions, rel_bias)
    qh = (rmsnorm(cq, q_norm) @ w_uq).reshape(B, S, MLA_HEADS, MLA_NOPE + MLA_ROPE)
    q_nope, q_rope = qh[..., :MLA_NOPE], rope(qh[..., MLA_NOPE:], positions)
    kvh = (rmsnorm(ckv, kv_norm) @ w_ukv).reshape(B, S, MLA_HEADS, MLA_NOPE + MLA_V)
    k_nope, v = kvh[..., :MLA_NOPE], kvh[..., MLA_NOPE:]
    o_mla = mla_attend(q_nope, q_rope, k_nope, rope(kr, positions), v)
    return jnp.concatenate([o_nsa, o_mla], axis=-1) @ w_out


def fox_mixer(h, w_in, w_out, f_b, q_norm, k_norm):
    B, S, _ = h.shape
    H, dh = FOX_HEADS, FOX_HEAD_DIM
    q, k, v, f, og = _split(h @ w_in, FOX_SPLITS)
    q = rmsnorm(q.reshape(B, S, H, dh), q_norm)
    k = rmsnorm(k.reshape(B, S, H, dh), k_norm)
    v = v.reshape(B, S, H, dh)
    cum = lax.cumsum(jax.nn.log_sigmoid((f + f_b).astype(jnp.float32)), axis=1).transpose(0, 2, 1)
    scale = dh ** -0.5
    kpos = jnp.arange(S)

    def block(qb):
        qs = qb * Q_BLOCK
        t = qs + jnp.arange(Q_BLOCK)
        qq = lax.dynamic_slice_in_dim(q, qs, Q_BLOCK, 1)
        cq = lax.dynamic_slice_in_dim(cum, qs, Q_BLOCK, 2)
        logit = (jnp.einsum('bqhd,bkhd->bhqk', qq, k).astype(jnp.float32) * scale
                 + cq[..., None] - cum[:, :, None, :])
        p = masked_softmax(logit, kpos[None, :] <= t[:, None])
        return jnp.einsum('bhqk,bkhd->bqhd', p.astype(v.dtype), v).reshape(B, Q_BLOCK, H * dh)

    o = sweep_query_blocks(block, B, S)
    return (o * jax.nn.sigmoid(og)) @ w_out


def setup_inputs(seed: int = 0) -> dict:
    key = jax.random.key(seed)
    ks = jax.random.split(key, 32)
    f32 = jnp.float32
    D, F = D_MODEL, FFN_HIDDEN

    def nrm(k, shape, fan_in, scale=1.0):
        return jax.random.normal(k, shape, f32) * (scale * fan_in ** -0.5)

    def gain(k, shape):
        return 1.0 + 0.05 * jax.random.normal(k, shape, f32)

    return {
        'x': jax.random.normal(ks[0], (BATCH, SEQ, D), f32),
        'c': jax.random.normal(ks[1], (BATCH, D), f32),
        'positions': jnp.tile(jnp.arange(SEQ, dtype=jnp.int32)[None, :], (BATCH, 1)),
        'rel_bias': 0.5 * jax.random.normal(ks[2], (REL_BUCKETS, NSA_HEADS), f32),
        'ada_w': nrm(ks[3], (DEPTH, D, 6 * D), D, 0.5),
        'ada_b': 0.02 * jax.random.normal(ks[4], (DEPTH, 6 * D), f32),
        'norm_mix': gain(ks[5], (DEPTH, D)),
        'norm_ffn': gain(ks[6], (DEPTH, D)),
        'ffn_w1': nrm(ks[7], (DEPTH, D, F), D),
        'ffn_w3': nrm(ks[8], (DEPTH, D, F), D),
        'ffn_w2': nrm(ks[9], (DEPTH, F, D), F),
        'even_w_in': nrm(ks[10], (N_EVEN, D, EVEN_IN), D),
        'even_w_out': nrm(ks[11], (N_EVEN, EVEN_MIX, D), EVEN_MIX),
        'nsa_gate_b': 0.1 * jax.random.normal(ks[12], (N_EVEN, 3 * NSA_HEADS), f32),
        'nsa_cmp_pos_k': 0.1 * jax.random.normal(ks[13], (N_EVEN, CMP_LEN, NSA_HEAD_DIM), f32),
        'nsa_cmp_w1_k': nrm(ks[14], (N_EVEN, CMP_LEN * NSA_HEAD_DIM, CMP_HIDDEN), CMP_LEN * NSA_HEAD_DIM),
        'nsa_cmp_w2_k': nrm(ks[15], (N_EVEN, CMP_HIDDEN, NSA_HEAD_DIM), CMP_HIDDEN),
        'nsa_cmp_pos_v': 0.1 * jax.random.normal(ks[16], (N_EVEN, CMP_LEN, NSA_HEAD_DIM), f32),
        'nsa_cmp_w1_v': nrm(ks[17], (N_EVEN, CMP_LEN * NSA_HEAD_DIM, CMP_HIDDEN), CMP_LEN * NSA_HEAD_DIM),
        'nsa_cmp_w2_v': nrm(ks[18], (N_EVEN, CMP_HIDDEN, NSA_HEAD_DIM), CMP_HIDDEN),
        'mla_q_norm': gain(ks[19], (N_EVEN, MLA_Q_RANK)),
        'mla_w_uq': nrm(ks[20], (N_EVEN, MLA_Q_RANK, MLA_HEADS * (MLA_NOPE + MLA_ROPE)), MLA_Q_RANK),
        'mla_kv_norm': gain(ks[21], (N_EVEN, MLA_KV_RANK)),
        'mla_w_ukv': nrm(ks[22], (N_EVEN, MLA_KV_RANK, MLA_HEADS * (MLA_NOPE + MLA_V)), MLA_KV_RANK),
        'fox_w_in': nrm(ks[23], (N_ODD, D, FOX_IN), D),
        'fox_w_out': nrm(ks[24], (N_ODD, D, D), D),
        'fox_f_b': 2.0 + 4.0 * jax.random.uniform(ks[25], (N_ODD, FOX_HEADS), f32),
        'fox_q_norm': gain(ks[26], (N_ODD, FOX_HEAD_DIM)),
        'fox_k_norm': gain(ks[27], (N_ODD, FOX_HEAD_DIM)),
        'final_norm': gain(ks[28], (D,)),
    }


def reference(x, c, positions, rel_bias, ada_w, ada_b, norm_mix, norm_ffn,
              ffn_w1, ffn_w3, ffn_w2, even_w_in, even_w_out, nsa_gate_b,
              nsa_cmp_pos_k, nsa_cmp_w1_k, nsa_cmp_w2_k, nsa_cmp_pos_v, nsa_cmp_w1_v, nsa_cmp_w2_v,
              mla_q_norm, mla_w_uq, mla_kv_norm, mla_w_ukv,
              fox_w_in, fox_w_out, fox_f_b, fox_q_norm, fox_k_norm, final_norm):
    cond = jax.nn.silu(c)
    for i in range(DEPTH):
        mod = (cond @ ada_w[i] + ada_b[i])[:, None, :]
        sh1, sc1, g1, sh2, sc2, g2 = jnp.split(mod, 6, axis=-1)
        h = rmsnorm(x, norm_mix[i]) * (1.0 + sc1) + sh1
        if i % 2 == 0:
            e = i // 2
            y = nsa_mla_mixer(h, positions, rel_bias, even_w_in[e], even_w_out[e], nsa_gate_b[e],
                              nsa_cmp_pos_k[e], nsa_cmp_w1_k[e], nsa_cmp_w2_k[e],
                              nsa_cmp_pos_v[e], nsa_cmp_w1_v[e], nsa_cmp_w2_v[e],
                              mla_q_norm[e], mla_w_uq[e], mla_kv_norm[e], mla_w_ukv[e])
        else:
            o = i // 2
            y = fox_mixer(h, fox_w_in[o], fox_w_out[o], fox_f_b[o], fox_q_norm[o], fox_k_norm[o])
        x = x + g1 * y
        h = rmsnorm(x, norm_ffn[i]) * (1.0 + sc2) + sh2
        x = x + g2 * swiglu(h, ffn_w1[i], ffn_w3[i], ffn_w2[i])
    return rmsnorm(x, final_norm)
```

```python
import functools
import math

import numpy as np
import jax
import jax.numpy as jnp
from jax import lax
from jax.experimental import pallas as pl
from jax.experimental.pallas import tpu as pltpu

D_MODEL = 2048
DEPTH = 2
EPS = 1e-6
NEG_INF = -1e30

NSA_HEADS = 8
NSA_KV_HEADS = 2
NSA_GROUP = NSA_HEADS // NSA_KV_HEADS
NSA_HEAD_DIM = 128
CMP_LEN = 32
CMP_STRIDE = 16
CMP_HIDDEN = 256
SLC_LEN = 64
SLC_TOPN = 16
WINDOW = 512
Q_BLOCK = 128

MLA_HEADS = 8
MLA_Q_RANK = 512
MLA_KV_RANK = 512
MLA_NOPE = 128
MLA_ROPE = 64
MLA_V = 128
ROPE_THETA = 10000.0

FOX_HEADS = 16
FOX_HEAD_DIM = D_MODEL // FOX_HEADS

REL_BUCKETS = 32
REL_MAX_DIST = 4096

FFN_HIDDEN = ((8 * D_MODEL + 2) // 3 + 255) // 256 * 256

LANES = 128
VMEM_LIMIT_BYTES = 56 * 1024 * 1024

BF = jnp.bfloat16
F32 = jnp.float32
_NT = (((1,), (1,)), ((), ()))


def _cparams(*sem):
    return pltpu.CompilerParams(dimension_semantics=sem, vmem_limit_bytes=VMEM_LIMIT_BYTES)


def _rms(x):
    return x * lax.rsqrt(jnp.mean(x * x, axis=-1, keepdims=True) + EPS)


def _adaln_kernel(c_ref, w_ref, b_ref, o_ref):
    c = c_ref[...]
    cond = c * jax.nn.sigmoid(c)
    acc = jnp.dot(cond.astype(BF), w_ref[0].astype(BF), preferred_element_type=F32)
    o_ref[0] = acc + b_ref[0]


def _adaln(c, ada_w, ada_b):
    depth, d, n = ada_w.shape
    tn = 1024
    c8 = jnp.broadcast_to(c, (8, d))
    out = pl.pallas_call(
        _adaln_kernel,
        grid=(depth, n // tn),
        in_specs=[pl.BlockSpec((8, d), lambda i, j: (0, 0)),
                  pl.BlockSpec((1, d, tn), lambda i, j: (i, 0, j)),
                  pl.BlockSpec((1, 1, tn), lambda i, j: (i, 0, j))],
        out_specs=pl.BlockSpec((1, 8, tn), lambda i, j: (i, 0, j)),
        out_shape=jax.ShapeDtypeStruct((depth, 8, n), F32),
        compiler_params=_cparams("arbitrary", "arbitrary"),
        name="adaln",
    )(c8, ada_w, ada_b.reshape(depth, 1, n))
    return out[:, 0, :]


def _normmod_kernel(x_ref, g_ref, sc_ref, sh_ref, o_ref):
    y = _rms(x_ref[...])
    o_ref[...] = ((y * g_ref[...]) * (1.0 + sc_ref[...]) + sh_ref[...]).astype(o_ref.dtype)


def _normmod(x, g, sc, sh):
    s, d = x.shape
    tm = 512
    row = pl.BlockSpec((1, d), lambda i: (0, 0))
    return pl.pallas_call(
        _normmod_kernel,
        grid=(s // tm,),
        in_specs=[pl.BlockSpec((tm, d), lambda i: (i, 0)), row, row, row],
        out_specs=pl.BlockSpec((tm, d), lambda i: (i, 0)),
        out_shape=jax.ShapeDtypeStruct((s, d), BF),
        compiler_params=_cparams("arbitrary"),
        name="normmod",
    )(x, g, sc, sh)


def _proj_kernel(*refs, n_lhs, n_epi, epi, head_major):
    lhs = refs[:n_lhs]
    ws = refs[n_lhs:2 * n_lhs]
    epis = refs[2 * n_lhs:2 * n_lhs + n_epi]
    o_ref = refs[-1]
    acc = None
    for a, w in zip(lhs, ws):
        d = jnp.dot(a[...], w[...], preferred_element_type=F32)
        acc = d if acc is None else acc + d
    res = epi(acc, *[e[...] for e in epis])
    if head_major:
        for r in range(o_ref.shape[0]):
            o_ref[r] = res[:, r * LANES:(r + 1) * LANES].astype(o_ref.dtype)
    else:
        o_ref[...] = res.astype(o_ref.dtype)


def _proj(lhs, ws, epi, epi_in=(), *, out_dtype, head_major=False, tm=1024, tn=512,
          lhs_col_block=None, name="proj"):
    m = lhs[0].shape[0]
    n = ws[0].shape[1]
    tm = min(tm, m)
    tn = min(tn, n)
    if lhs_col_block is None:
        lhs_col_block = [0] * len(lhs)
    in_specs = []
    for a, w, cb in zip(lhs, ws, lhs_col_block):
        in_specs.append(pl.BlockSpec((tm, w.shape[0]), lambda i, j, cb=cb: (i, cb)))
    for w in ws:
        in_specs.append(pl.BlockSpec((w.shape[0], tn), lambda i, j: (0, j)))
    arrays = list(lhs) + list(ws)
    for arr, kind in epi_in:
        if kind == "row":
            in_specs.append(pl.BlockSpec((1, tn), lambda i, j: (0, j)))
        elif kind == "const":
            in_specs.append(pl.BlockSpec(arr.shape, lambda i, j: (0, 0)))
        elif kind == "tile":
            in_specs.append(pl.BlockSpec((tm, tn), lambda i, j: (i, j)))
        elif kind == "rowtile":
            in_specs.append(pl.BlockSpec((tm, arr.shape[1]), lambda i, j: (i, 0)))
        else:
            raise ValueError(kind)
        arrays.append(arr)
    if head_major:
        hpt = tn // LANES
        out_spec = pl.BlockSpec((hpt, tm, LANES), lambda i, j: (j, i, 0))
        out_shape = jax.ShapeDtypeStruct((n // LANES, m, LANES), out_dtype)
    else:
        out_spec = pl.BlockSpec((tm, tn), lambda i, j: (i, j))
        out_shape = jax.ShapeDtypeStruct((m, n), out_dtype)
    kern = functools.partial(_proj_kernel, n_lhs=len(lhs), n_epi=len(epi_in), epi=epi,
                             head_major=head_major)
    return pl.pallas_call(
        kern,
        grid=(m // tm, n // tn),
        in_specs=in_specs,
        out_specs=out_spec,
        out_shape=out_shape,
        compiler_params=_cparams("arbitrary", "arbitrary"),
        name=name,
    )(*arrays)


def _epi_id(acc):
    return acc


def _epi_scale(scale, acc):
    return acc * scale


def _epi_sigmoid_bias(acc, b):
    return jax.nn.sigmoid(acc + b)


def _epi_logsigmoid_bias(acc, b):
    return jax.nn.log_sigmoid(acc + b)


def _epi_sigmoid(acc):
    return jax.nn.sigmoid(acc)


def _epi_rmsnorm(acc, g):
    return _rms(acc) * g


def _epi_headnorm(scale, acc, g):
    outs = []
    for r in range(acc.shape[1] // LANES):
        outs.append(_rms(acc[:, r * LANES:(r + 1) * LANES]) * g * scale)
    return jnp.concatenate(outs, axis=1)


def _epi_rope(scale, acc, c, s1, s2):
    reps = acc.shape[1] // LANES
    half = MLA_ROPE // 2
    if reps > 1:
        c = jnp.concatenate([c] * reps, axis=1)
        s1 = jnp.concatenate([s1] * reps, axis=1)
        s2 = jnp.concatenate([s2] * reps, axis=1)
    n = acc.shape[1]
    out = acc * c + pltpu.roll(acc, n - half, 1) * s1 + pltpu.roll(acc, half, 1) * s2
    return out * scale


def _epi_residual(acc, x, g):
    return x + g * acc


def _ffn_kernel(*refs, final):
    if final:
        x_ref, g_ref, sc_ref, sh_ref, g2_ref, w1_ref, w3_ref, w2_ref, fn_ref, o_ref, hn_ref, acc_ref = refs
    else:
        x_ref, g_ref, sc_ref, sh_ref, g2_ref, w1_ref, w3_ref, w2_ref, o_ref, hn_ref, acc_ref = refs
    f = pl.program_id(1)

    @pl.when(f == 0)
    def _():
        y = _rms(x_ref[...])
        hn_ref[...] = ((y * g_ref[...]) * (1.0 + sc_ref[...]) + sh_ref[...]).astype(BF)
        acc_ref[...] = jnp.zeros_like(acc_ref)

    h = hn_ref[...]
    h1 = jnp.dot(h, w1_ref[...], preferred_element_type=F32)
    h3 = jnp.dot(h, w3_ref[...], preferred_element_type=F32)
    a = (h1 * jax.nn.sigmoid(h1)) * h3
    acc_ref[...] += jnp.dot(a.astype(BF), w2_ref[...], preferred_element_type=F32)

    @pl.when(f == pl.num_programs(1) - 1)
    def _():
        xn = x_ref[...] + g2_ref[...] * acc_ref[...]
        if final:
            xn = _rms(xn) * fn_ref[...]
        o_ref[...] = xn


def _ffn(x, g, sc, sh, g2, w1, w3, w2, final_gain=None):
    s, d = x.shape
    fdim = w1.shape[1]
    tm = min(512, s)
    tf = 512
    row = pl.BlockSpec((1, d), lambda i, f: (0, 0))
    in_specs = [pl.BlockSpec((tm, d), lambda i, f: (i, 0)), row, row, row, row,
                pl.BlockSpec((d, tf), lambda i, f: (0, f)),
                pl.BlockSpec((d, tf), lambda i, f: (0, f)),
                pl.BlockSpec((tf, d), lambda i, f: (f, 0))]
    arrays = [x, g, sc, sh, g2, w1, w3, w2]
    if final_gain is not None:
        in_specs.append(row)
        arrays.append(final_gain)
    return pl.pallas_call(
        functools.partial(_ffn_kernel, final=final_gain is not None),
        grid=(s // tm, fdim // tf),
        in_specs=in_specs,
        out_specs=pl.BlockSpec((tm, d), lambda i, f: (i, 0)),
        out_shape=jax.ShapeDtypeStruct((s, d), F32),
        scratch_shapes=[pltpu.VMEM((tm, d), BF), pltpu.VMEM((tm, d), F32)],
        compiler_params=_cparams("arbitrary", "arbitrary"),
        name="ffn",
    )(*arrays)


def _online_update(s, v, m_ref, l_ref, acc_ref):
    m_prev = m_ref[...]
    m_new = jnp.maximum(m_prev, jnp.max(s, axis=1, keepdims=True))
    alpha = jnp.exp(m_prev - m_new)
    p = jnp.exp(s - m_new)
    l_ref[...] = alpha * l_ref[...] + jnp.sum(p, axis=1, keepdims=True)
    acc_ref[...] = alpha * acc_ref[...] + jnp.dot(p.astype(BF), v, preferred_element_type=F32)
    m_ref[...] = m_new


def _init_state(m_ref, l_ref, acc_ref):
    m_ref[...] = jnp.full_like(m_ref, NEG_INF)
    l_ref[...] = jnp.zeros_like(l_ref)
    acc_ref[...] = jnp.zeros_like(acc_ref)


def _causal_mask(t):
    rows = lax.broadcasted_iota(jnp.int32, (t, t), 0)
    cols = lax.broadcasted_iota(jnp.int32, (t, t), 1)
    return cols <= rows


def _mla_kernel(qn_ref, qr_ref, kn_ref, kr_ref, v_ref, o_ref, m_ref, l_ref, acc_ref, *, t):
    qi = pl.program_id(1)
    q = jnp.concatenate([qn_ref[0], qr_ref[0]], axis=1)
    _init_state(m_ref, l_ref, acc_ref)

    def tile(j, masked):
        off = pl.multiple_of(j * t, t)
        k = jnp.concatenate([kn_ref[0, pl.ds(off, t), :], kr_ref[pl.ds(off, t), :]], axis=1)
        s = lax.dot_general(q, k, _NT, preferred_element_type=F32)
        if masked:
            s = jnp.where(_causal_mask(t), s, NEG_INF)
        _online_update(s, v_ref[0, pl.ds(off, t), :], m_ref, l_ref, acc_ref)

    def body(j, carry):
        tile(j, False)
        return carry

    lax.fori_loop(0, qi, body, 0)
    tile(qi, True)
    o_ref[...] = (acc_ref[...] / l_ref[...]).astype(o_ref.dtype)


def _mla_attention(qn, qr, kvh, kr):
    h, s, _ = qn.shape
    t = min(512, s)
    return pl.pallas_call(
        functools.partial(_mla_kernel, t=t),
        grid=(h, s // t),
        in_specs=[pl.BlockSpec((1, t, LANES), lambda hh, i: (hh, i, 0)),
                  pl.BlockSpec((1, t, LANES), lambda hh, i: (hh, i, 0)),
                  pl.BlockSpec((1, s, LANES), lambda hh, i: (2 * hh, 0, 0)),
                  pl.BlockSpec((s, LANES), lambda hh, i: (0, 0)),
                  pl.BlockSpec((1, s, LANES), lambda hh, i: (2 * hh + 1, 0, 0))],
        out_specs=pl.BlockSpec((t, LANES), lambda hh, i: (i, hh)),
        out_shape=jax.ShapeDtypeStruct((s, h * LANES), BF),
        scratch_shapes=[pltpu.VMEM((t, 1), F32), pltpu.VMEM((t, 1), F32), pltpu.VMEM((t, LANES), F32)],
        compiler_params=_cparams("arbitrary", "arbitrary"),
        name="mla_attn",
    )(qn, qr, kvh, kr, kvh)


def _fox_kernel(q_ref, k_ref, v_ref, ck_ref, cum_ref, og_ref, o_ref, m_ref, l_ref, acc_ref, *, t):
    hh = pl.program_id(0)
    qi = pl.program_id(1)
    q = q_ref[0]
    _init_state(m_ref, l_ref, acc_ref)
    lane = lax.broadcasted_iota(jnp.int32, (t, LANES), 1)
    cq = jnp.sum(jnp.where(lane == hh, cum_ref[...], 0.0), axis=1, keepdims=True)

    def tile(j, masked):
        off = pl.multiple_of(j * t, t)
        s = lax.dot_general(q, k_ref[0, pl.ds(off, t), :], _NT, preferred_element_type=F32)
        s = (s + cq) - ck_ref[0, :, pl.ds(off, t)]
        if masked:
            s = jnp.where(_causal_mask(t), s, NEG_INF)
        _online_update(s, v_ref[0, pl.ds(off, t), :], m_ref, l_ref, acc_ref)

    def body(j, carry):
        tile(j, False)
        return carry

    lax.fori_loop(0, qi, body, 0)
    tile(qi, True)
    o_ref[...] = ((acc_ref[...] / l_ref[...]) * og_ref[...]).astype(o_ref.dtype)


def _fox_attention(q, k, v, cum_t, cum, sig_og):
    h, s, _ = q.shape
    t = min(512, s)
    hm = pl.BlockSpec((1, s, LANES), lambda hh, i: (hh, 0, 0))
    return pl.pallas_call(
        functools.partial(_fox_kernel, t=t),
        grid=(h, s // t),
        in_specs=[pl.BlockSpec((1, t, LANES), lambda hh, i: (hh, i, 0)), hm, hm,
                  pl.BlockSpec((1, 1, s), lambda hh, i: (hh, 0, 0)),
                  pl.BlockSpec((t, LANES), lambda hh, i: (i, 0)),
                  pl.BlockSpec((t, LANES), lambda hh, i: (i, hh))],
        out_specs=pl.BlockSpec((t, LANES), lambda hh, i: (i, hh)),
        out_shape=jax.ShapeDtypeStruct((s, h * LANES), BF),
        scratch_shapes=[pltpu.VMEM((t, 1), F32), pltpu.VMEM((t, 1), F32), pltpu.VMEM((t, LANES), F32)],
        compiler_params=_cparams("arbitrary", "arbitrary"),
        name="fox_attn",
    )(q, k, v, cum_t, cum, sig_og)


def _cumsum_kernel(x_ref, o_ref, ot_ref, carry_ref, *, t):
    @pl.when(pl.program_id(0) == 0)
    def _():
        carry_ref[...] = jnp.zeros_like(carry_ref)

    x = x_ref[...]
    rows = lax.broadcasted_iota(jnp.int32, (t, t), 0)
    cols = lax.broadcasted_iota(jnp.int32, (t, t), 1)
    tri = jnp.where(cols <= rows, 1.0, 0.0).astype(BF)
    hi = x.astype(BF)
    r1 = x - hi.astype(F32)
    mid = r1.astype(BF)
    lo = (r1 - mid.astype(F32)).astype(BF)
    cum = (jnp.dot(tri, hi, preferred_element_type=F32) + jnp.dot(tri, mid, preferred_element_type=F32)
           + jnp.dot(tri, lo, preferred_element_type=F32)) + carry_ref[...]
    o_ref[...] = cum
    ot_ref[...] = cum.T
    carry_ref[...] = cum[t - 1:t, :]


def _cumsum_tokens(x):
    s, n = x.shape
    t = min(256, s)
    return pl.pallas_call(
        functools.partial(_cumsum_kernel, t=t),
        grid=(s // t,),
        in_specs=[pl.BlockSpec((t, n), lambda i: (i, 0))],
        out_specs=[pl.BlockSpec((t, n), lambda i: (i, 0)), pl.BlockSpec((n, t), lambda i: (0, i))],
        out_shape=[jax.ShapeDtypeStruct((s, n), F32), jax.ShapeDtypeStruct((n, s), F32)],
        scratch_shapes=[pltpu.VMEM((1, n), F32)],
        compiler_params=_cparams("arbitrary"),
        name="cumsum",
    )(x)


def _compress_kernel(a_ref, pos_ref, w1_ref, w2_ref, o_ref):
    a = a_ref[0]
    nc = a.shape[0]
    p1 = jnp.dot((a + pos_ref[0, 0:1, :]).astype(BF), w1_ref[0, 0], preferred_element_type=F32)
    p2 = jnp.dot((a + pos_ref[0, 1:2, :]).astype(BF), w1_ref[0, 1], preferred_element_type=F32)
    h = p1 + pltpu.roll(p2, nc - 1, 0)
    act = h * jax.nn.sigmoid(h)
    o_ref[0] = jnp.dot(act.astype(BF), w2_ref[0], preferred_element_type=F32).astype(o_ref.dtype)


def _nsa_compress(kv_cmp, pos, w1, w2):
    c, nc, gw = kv_cmp.shape
    return pl.pallas_call(
        _compress_kernel,
        grid=(c,),
        in_specs=[pl.BlockSpec((1, nc, gw), lambda i: (i, 0, 0)),
                  pl.BlockSpec((1, 2, gw), lambda i: (i // 2, 0, 0)),
                  pl.BlockSpec((1, 2, gw, CMP_HIDDEN), lambda i: (i // 2, 0, 0, 0)),
                  pl.BlockSpec((1, CMP_HIDDEN, NSA_HEAD_DIM), lambda i: (i // 2, 0, 0))],
        out_specs=pl.BlockSpec((1, nc, NSA_HEAD_DIM), lambda i: (i, 0, 0)),
        out_shape=jax.ShapeDtypeStruct((c, nc, NSA_HEAD_DIM), BF),
        compiler_params=_cparams("arbitrary"),
        name="nsa_compress",
    )(kv_cmp, pos, w1, w2)


def _nsa_cmp_kernel(q_ref, kc_ref, vct_ref, trc_ref, smt_ref, gate_ref, oc_ref, sel_ref, *, qblocks, topn):
    qb = pl.program_id(1)
    rq = NSA_GROUP * Q_BLOCK
    q = q_ref[...].reshape(rq, NSA_HEAD_DIM)
    kc = kc_ref[0]
    nc = kc.shape[0]
    nslc = smt_ref.shape[0]
    s = lax.dot_general(kc, q, _NT, preferred_element_type=F32)
    n_io = lax.broadcasted_iota(jnp.int32, (nc, Q_BLOCK), 0)
    i_io = lax.broadcasted_iota(jnp.int32, (nc, Q_BLOCK), 1)
    mask = (n_io * CMP_STRIDE + (CMP_LEN - 1)) <= (qb * Q_BLOCK + i_io)
    off = pl.multiple_of((qblocks - 1 - qb) * 8, 8)
    ps = []
    for r in range(NSA_GROUP):
        l = jnp.where(mask, s[:, r * Q_BLOCK:(r + 1) * Q_BLOCK] + trc_ref[r, pl.ds(off, nc), :], NEG_INF)
        m = jnp.max(l, axis=0, keepdims=True)
        p = jnp.where(mask, jnp.exp(l - m), 0.0)
        den = jnp.maximum(jnp.sum(p, axis=0, keepdims=True), 1e-30)
        ps.append(p / den)
    p_all = jnp.concatenate(ps, axis=1).astype(BF)
    oc_t = jnp.dot(vct_ref[0], p_all, preferred_element_type=F32)
    imp4 = jnp.dot(smt_ref[...], p_all, preferred_element_type=F32)
    imp = imp4[:, 0:Q_BLOCK]
    for r in range(1, NSA_GROUP):
        imp = imp + imp4[:, r * Q_BLOCK:(r + 1) * Q_BLOCK]

    j_io = lax.broadcasted_iota(jnp.int32, (nslc, Q_BLOCK), 0)
    t_io = qb * Q_BLOCK + lax.broadcasted_iota(jnp.int32, (nslc, Q_BLOCK), 1)
    cur = t_io >> int(math.log2(SLC_LEN))
    forced = jnp.logical_or(j_io == 0, jnp.logical_or(j_io == cur, j_io == cur - 1))
    val = jnp.where(forced, 1e9, jnp.where(j_io <= cur, imp, -1e9))
    sel = jnp.zeros((nslc, Q_BLOCK), F32)
    for _ in range(topn):
        mx = jnp.max(val, axis=0, keepdims=True)
        cand = jnp.where(val == mx, j_io, nslc)
        jmin = jnp.min(cand, axis=0, keepdims=True)
        pick = j_io == jmin
        sel = jnp.where(pick, 1.0, sel)
        val = jnp.where(pick, -3e38, val)
    sel_ref[0] = sel.T.astype(sel_ref.dtype)

    gates = gate_ref[0]
    for r in range(NSA_GROUP):
        o_r = oc_t[:, r * Q_BLOCK:(r + 1) * Q_BLOCK].T
        oc_ref[:, r * NSA_HEAD_DIM:(r + 1) * NSA_HEAD_DIM] = o_r * gates[:, 3 * r:3 * r + 1]


def _nsa_cmp(q, kc, vct, trc, smt, gates):
    h, s, d = q.shape
    g = NSA_KV_HEADS
    qblocks = s // Q_BLOCK
    nc = kc.shape[1]
    nslc = smt.shape[0]
    topn = min(SLC_TOPN, nslc)
    return pl.pallas_call(
        functools.partial(_nsa_cmp_kernel, qblocks=qblocks, topn=topn),
        grid=(g, qblocks),
        in_specs=[pl.BlockSpec((NSA_GROUP, Q_BLOCK, d), lambda gg, i: (gg, i, 0)),
                  pl.BlockSpec((1, nc, d), lambda gg, i: (gg, 0, 0)),
                  pl.BlockSpec((1, d, nc), lambda gg, i: (gg, 0, 0)),
                  pl.BlockSpec((NSA_GROUP, trc.shape[1], Q_BLOCK), lambda gg, i: (gg, 0, 0)),
                  pl.BlockSpec(smt.shape, lambda gg, i: (0, 0)),
                  pl.BlockSpec((1, Q_BLOCK, LANES), lambda gg, i: (gg, i, 0))],
        out_specs=[pl.BlockSpec((Q_BLOCK, NSA_GROUP * d), lambda gg, i: (i, gg)),
                   pl.BlockSpec((1, Q_BLOCK, nslc), lambda gg, i: (gg, i, 0))],
        out_shape=[jax.ShapeDtypeStruct((s, h * d), F32),
                   jax.ShapeDtypeStruct((g, s, nslc), BF)],
        compiler_params=_cparams("arbitrary", "arbitrary"),
        name="nsa_cmp",
    )(q, kc, vct, trc, smt, gates)


def _nsa_sw_kernel(q_ref, ks_ref, vs_ref, kw_ref, vw_ref, tb_ref, sel_ref, e_ref, gate_ref, oc_ref,
                   o_ref, m_ref, l_ref, acc_ref, *, tk, mc):
    qb = pl.program_id(1)
    rq = NSA_GROUP * Q_BLOCK
    q = q_ref[...].reshape(rq, NSA_HEAD_DIM)
    sel01 = sel_ref[0]
    gates = gate_ref[0]
    nsub = tk // Q_BLOCK
    wblocks = WINDOW // Q_BLOCK

    def finalize(branch, base):
        o = acc_ref[...] / l_ref[...]
        outs = []
        for r in range(NSA_GROUP):
            o_r = o[r * Q_BLOCK:(r + 1) * Q_BLOCK, :] * gates[:, 3 * r + branch:3 * r + branch + 1]
            outs.append(base[:, r * NSA_HEAD_DIM:(r + 1) * NSA_HEAD_DIM] + o_r)
        return jnp.concatenate(outs, axis=1)

    _init_state(m_ref, l_ref, acc_ref)

    def sel_tile(j, causal):
        off = pl.multiple_of(j * tk, tk)
        s = lax.dot_general(q, ks_ref[0, pl.ds(off, tk), :], _NT, preferred_element_type=F32)
        msk = jnp.dot(sel01, e_ref[j], preferred_element_type=F32) > 0.5
        if causal:
            kk = j * tk + lax.broadcasted_iota(jnp.int32, (Q_BLOCK, tk), 1)
            ii = qb * Q_BLOCK + lax.broadcasted_iota(jnp.int32, (Q_BLOCK, tk), 0)
            msk = jnp.logical_and(msk, kk <= ii)
        parts = []
        for r in range(NSA_GROUP):
            bias = jnp.concatenate(
                [tb_ref[r, jnp.clip(qb - (j * nsub + c), 0, mc)] for c in range(nsub)], axis=1)
            parts.append(jnp.where(msk, s[r * Q_BLOCK:(r + 1) * Q_BLOCK, :] + bias, NEG_INF))
        _online_update(jnp.concatenate(parts, axis=0), vs_ref[0, pl.ds(off, tk), :], m_ref, l_ref, acc_ref)

    n_tiles = (qb + nsub) // nsub

    def body(j, carry):
        sel_tile(j, False)
        return carry

    lax.fori_loop(0, n_tiles - 1, body, 0)
    sel_tile(n_tiles - 1, True)
    out = finalize(1, oc_ref[...])

    _init_state(m_ref, l_ref, acc_ref)
    ii = lax.broadcasted_iota(jnp.int32, (Q_BLOCK, Q_BLOCK), 0)
    kk = lax.broadcasted_iota(jnp.int32, (Q_BLOCK, Q_BLOCK), 1)
    for mm in range(wblocks, -1, -1):
        kb = qb - mm

        @pl.when(kb >= 0)
        def _(mm=mm, kb=kb):
            off = pl.multiple_of(kb * Q_BLOCK, Q_BLOCK)
            s = lax.dot_general(q, kw_ref[0, pl.ds(off, Q_BLOCK), :], _NT, preferred_element_type=F32)
            parts = []
            for r in range(NSA_GROUP):
                lr = s[r * Q_BLOCK:(r + 1) * Q_BLOCK, :] + tb_ref[r, mm]
                if mm == wblocks:
                    lr = jnp.where(ii < kk, lr, NEG_INF)
                elif mm == 0:
                    lr = jnp.where(kk <= ii, lr, NEG_INF)
                parts.append(lr)
            _online_update(jnp.concatenate(parts, axis=0), vw_ref[0, pl.ds(off, Q_BLOCK), :],
                           m_ref, l_ref, acc_ref)

    o_ref[...] = finalize(2, out).astype(o_ref.dtype)


def _nsa_sw(q, kvsw, tb, sel, emat, gates, oc):
    h, s, d = q.shape
    g = NSA_KV_HEADS
    qblocks = s // Q_BLOCK
    tk = emat.shape[2]
    mc = tb.shape[1] - 1
    nslc = sel.shape[2]
    rq = NSA_GROUP * Q_BLOCK

    def kv(slot):
        return pl.BlockSpec((1, s, d), lambda gg, i, slot=slot: (slot + gg, 0, 0))

    return pl.pallas_call(
        functools.partial(_nsa_sw_kernel, tk=tk, mc=mc),
        grid=(g, qblocks),
        in_specs=[pl.BlockSpec((NSA_GROUP, Q_BLOCK, d), lambda gg, i: (gg, i, 0)),
                  kv(0), kv(2), kv(4), kv(6),
                  pl.BlockSpec((NSA_GROUP,) + tb.shape[1:], lambda gg, i: (gg, 0, 0, 0)),
                  pl.BlockSpec((1, Q_BLOCK, nslc), lambda gg, i: (gg, i, 0)),
                  pl.BlockSpec(emat.shape, lambda gg, i: (0, 0, 0)),
                  pl.BlockSpec((1, Q_BLOCK, LANES), lambda gg, i: (gg, i, 0)),
                  pl.BlockSpec((Q_BLOCK, NSA_GROUP * d), lambda gg, i: (i, gg))],
        out_specs=pl.BlockSpec((Q_BLOCK, NSA_GROUP * d), lambda gg, i: (i, gg)),
        out_shape=jax.ShapeDtypeStruct((s, h * d), BF),
        scratch_shapes=[pltpu.VMEM((rq, 1), F32), pltpu.VMEM((rq, 1), F32), pltpu.VMEM((rq, d), F32)],
        compiler_params=_cparams("arbitrary", "arbitrary"),
        name="nsa_sel_win",
    )(q, kvsw, kvsw, kvsw, kvsw, tb, sel, emat, gates, oc)


def _t5_bucket(dist):
    max_exact = REL_BUCKETS // 2
    d = jnp.maximum(dist, 0)
    ratio = jnp.log(jnp.maximum(d, max_exact).astype(F32) / max_exact) / math.log(REL_MAX_DIST / max_exact)
    large = jnp.minimum(max_exact + (ratio * (REL_BUCKETS - max_exact)).astype(jnp.int32), REL_BUCKETS - 1)
    return jnp.where(d < max_exact, d, large)


def _bias_tables(rel_bias, s):
    qblocks = s // Q_BLOCK
    fh = rel_bias.T[:, _t5_bucket(jnp.arange(s, dtype=jnp.int32))]
    max_exact = REL_BUCKETS // 2
    d_const = int(math.ceil(max_exact * (REL_MAX_DIST / max_exact)
                            ** ((REL_BUCKETS - max_exact - 1) / (REL_BUCKETS - max_exact)))) + 1
    mc = min(-(-(d_const + Q_BLOCK - 1) // Q_BLOCK), qblocks - 1)
    m = np.arange(mc + 1)[:, None, None]
    i = np.arange(Q_BLOCK)[None, :, None]
    j = np.arange(Q_BLOCK)[None, None, :]
    tb = fh[:, np.clip(Q_BLOCK * m + i - j, 0, s - 1)]
    rho = np.arange(2 * qblocks * 8)[:, None]
    mm = qblocks - 1 - rho // 8
    nn = rho % 8
    dist = Q_BLOCK * mm + np.arange(Q_BLOCK)[None, :] - CMP_STRIDE * nn - (CMP_LEN - 1)
    trc = fh[:, np.clip(dist, 0, s - 1)]
    return tb, trc


def _selection_map_t(nc, nslc):
    n = np.arange(nc)[None, :] * CMP_STRIDE
    j0 = np.arange(nslc)[:, None] * SLC_LEN
    valid = np.arange(nc)[None, :] < nc - 1
    return jnp.asarray(((n < j0 + SLC_LEN) & (n + CMP_LEN > j0) & valid).astype(np.float32), dtype=BF)


def _expand_matrix(s, nslc, tk):
    tok = np.arange(s).reshape(s // tk, 1, tk)
    j = np.arange(nslc).reshape(1, nslc, 1)
    return jnp.asarray((tok // SLC_LEN == j).astype(np.float32), dtype=BF)


def _rope_tables(positions):
    half = MLA_ROPE // 2
    inv = ROPE_THETA ** (-jnp.arange(half, dtype=F32) / half)
    ang = positions.astype(F32)[:, None] * inv
    cos, sin = jnp.cos(ang), jnp.sin(ang)
    z = jnp.zeros_like(cos)
    zpad = jnp.zeros((positions.shape[0], LANES - MLA_ROPE), F32)
    c = jnp.concatenate([cos, cos, zpad], axis=1)
    s1 = jnp.concatenate([-sin, z, zpad], axis=1)
    s2 = jnp.concatenate([z, sin, zpad], axis=1)
    return c, s1, s2


def _pad_cols(w, n):
    return jnp.pad(w, ((0, 0), (0, n - w.shape[1])))


def _nsa_mla_mixer(hn, pos, rel_bias, w_in, w_out, gate_b, pos_k, w1_k, w2_k, pos_v, w1_v, w2_v,
                   q_norm, w_uq, kv_norm, w_ukv, x, g1):
    s = hn.shape[0]
    d = NSA_HEAD_DIM
    nq = NSA_HEADS * d
    nkv = 2 * NSA_KV_HEADS * d
    o0 = 0
    w_q = w_in[:, o0:o0 + nq]; o0 += nq
    w_kvc = w_in[:, o0:o0 + nkv]; o0 += nkv
    w_kvsw = w_in[:, o0:o0 + 2 * nkv]; o0 += 2 * nkv
    w_g = w_in[:, o0:o0 + 3 * NSA_HEADS]; o0 += 3 * NSA_HEADS
    w_lat = w_in[:, o0:o0 + MLA_Q_RANK + MLA_KV_RANK]; o0 += MLA_Q_RANK + MLA_KV_RANK
    w_kr = w_in[:, o0:o0 + MLA_ROPE]

    q_nsa = _proj([hn], [w_q.astype(BF)], functools.partial(_epi_scale, d ** -0.5),
                  out_dtype=BF, head_major=True, tn=1024, name="proj_q_nsa")
    kv_cmp = _proj([hn], [w_kvc.astype(BF)], _epi_id, out_dtype=F32, head_major=True, name="proj_kv_cmp")
    kv_sw = _proj([hn], [w_kvsw.astype(BF)], _epi_id, out_dtype=BF, head_major=True, tn=1024,
                  name="proj_kv_sw")
    per_g = 3 * NSA_GROUP
    w_gp = jnp.concatenate([_pad_cols(w_g[:, g * per_g:(g + 1) * per_g], LANES) for g in range(NSA_KV_HEADS)], 1)
    b_gp = jnp.concatenate([_pad_cols(gate_b[None, g * per_g:(g + 1) * per_g], LANES)
                            for g in range(NSA_KV_HEADS)], 1)
    gates = _proj([hn], [w_gp.astype(BF)], _epi_sigmoid_bias, [(b_gp, "row")], out_dtype=F32,
                  head_major=True, tn=LANES, name="proj_gates")

    nc = s // CMP_STRIDE
    gw = CMP_STRIDE * d
    pos_kv = jnp.stack([pos_k.reshape(2, gw), pos_v.reshape(2, gw)])
    w1_kv = jnp.stack([w1_k.reshape(2, gw, CMP_HIDDEN), w1_v.reshape(2, gw, CMP_HIDDEN)]).astype(BF)
    w2_kv = jnp.stack([w2_k, w2_v]).astype(BF)
    kvc = _nsa_compress(kv_cmp.reshape(2 * NSA_KV_HEADS, nc, gw), pos_kv, w1_kv, w2_kv)
    kc = kvc[:NSA_KV_HEADS]
    vct = jnp.swapaxes(kvc[NSA_KV_HEADS:], 1, 2)

    nslc = s // SLC_LEN
    tb, trc = _bias_tables(rel_bias, s)
    smt = _selection_map_t(nc, nslc)
    oc, sel = _nsa_cmp(q_nsa, kc, vct, trc, smt, gates)
    tk = min(256, s)
    o_nsa = _nsa_sw(q_nsa, kv_sw, tb, sel, _expand_matrix(s, nslc, tk), gates, oc)

    lat = _proj([hn], [w_lat.astype(BF)], _epi_rmsnorm,
                [(jnp.concatenate([q_norm, kv_norm])[None, :], "row")], out_dtype=BF, tn=MLA_Q_RANK,
                name="proj_mla_latent")
    c, s1, s2 = _rope_tables(pos)
    kr = _proj([hn], [_pad_cols(w_kr, LANES).astype(BF)], functools.partial(_epi_rope, 1.0),
               [(c, "rowtile"), (s1, "rowtile"), (s2, "rowtile")], out_dtype=BF, name="proj_mla_kr")
    scale = (MLA_NOPE + MLA_ROPE) ** -0.5
    w_uq3 = w_uq.reshape(MLA_Q_RANK, MLA_HEADS, MLA_NOPE + MLA_ROPE)
    w_uqn = w_uq3[:, :, :MLA_NOPE].reshape(MLA_Q_RANK, MLA_HEADS * MLA_NOPE)
    w_uqr = jnp.pad(w_uq3[:, :, MLA_NOPE:], ((0, 0), (0, 0), (0, LANES - MLA_ROPE))).reshape(
        MLA_Q_RANK, MLA_HEADS * LANES)
    qn = _proj([lat], [w_uqn.astype(BF)], functools.partial(_epi_scale, scale), out_dtype=BF,
               head_major=True, tn=1024, lhs_col_block=[0], name="proj_mla_qn")
    qr = _proj([lat], [w_uqr.astype(BF)], functools.partial(_epi_rope, scale),
               [(c, "rowtile"), (s1, "rowtile"), (s2, "rowtile")], out_dtype=BF, head_major=True,
               lhs_col_block=[0], name="proj_mla_qr")
    kvh = _proj([lat], [w_ukv.astype(BF)], _epi_id, out_dtype=BF, head_major=True, tn=1024,
                lhs_col_block=[1], name="proj_mla_kv")
    o_mla = _mla_attention(qn, qr, kvh, kr)

    return _proj([o_nsa, o_mla], [w_out[:nq].astype(BF), w_out[nq:].astype(BF)], _epi_residual,
                 [(x, "tile"), (g1, "row")], out_dtype=F32, tn=1024, name="proj_even_out")


def _fox_mixer(hn, w_in, w_out, f_b, q_norm, k_norm, x, g1):
    d = D_MODEL
    dh = FOX_HEAD_DIM
    w_q, w_k, w_v = w_in[:, 0:d], w_in[:, d:2 * d], w_in[:, 2 * d:3 * d]
    w_f = w_in[:, 3 * d:3 * d + FOX_HEADS]
    w_og = w_in[:, 3 * d + FOX_HEADS:]
    q = _proj([hn], [w_q.astype(BF)], functools.partial(_epi_headnorm, dh ** -0.5), [(q_norm[None, :], "const")],
              out_dtype=BF, head_major=True, tn=1024, name="proj_fox_q")
    k = _proj([hn], [w_k.astype(BF)], functools.partial(_epi_headnorm, 1.0), [(k_norm[None, :], "const")],
              out_dtype=BF, head_major=True, tn=1024, name="proj_fox_k")
    v = _proj([hn], [w_v.astype(BF)], _epi_id, out_dtype=BF, head_major=True, tn=1024, name="proj_fox_v")
    lf = _proj([hn], [_pad_cols(w_f, LANES).astype(BF)], _epi_logsigmoid_bias,
               [(_pad_cols(f_b[None, :], LANES), "row")], out_dtype=F32, name="proj_fox_f")
    sig_og = _proj([hn], [w_og.astype(BF)], _epi_sigmoid, out_dtype=F32, tn=1024, name="proj_fox_og")
    cum, cum_t = _cumsum_tokens(lf)
    cum_t = cum_t[:FOX_HEADS].reshape(FOX_HEADS, 1, -1)
    o = _fox_attention(q, k, v, cum_t, cum, sig_og)
    return _proj([o], [w_out.astype(BF)], _epi_residual, [(x, "tile"), (g1, "row")], out_dtype=F32, tn=1024,
                 name="proj_fox_out")


def kernel(x, c, positions, rel_bias, ada_w, ada_b, norm_mix, norm_ffn, ffn_w1, ffn_w3, ffn_w2, even_w_in, even_w_out, nsa_gate_b, nsa_cmp_pos_k, nsa_cmp_w1_k, nsa_cmp_w2_k, nsa_cmp_pos_v, nsa_cmp_w1_v, nsa_cmp_w2_v, mla_q_norm, mla_w_uq, mla_kv_norm, mla_w_ukv, fox_w_in, fox_w_out, fox_f_b, fox_q_norm, fox_k_norm, final_norm):
    b, s, d = x.shape
    assert b == 1 and d == D_MODEL and s % 512 == 0
    xs = x[0]
    pos = positions[0]
    mod = _adaln(c, ada_w, ada_b)
    depth = ada_w.shape[0]
    for i in range(depth):
        sh1, sc1, g1, sh2, sc2, g2 = [mod[i:i + 1, k * d:(k + 1) * d] for k in range(6)]
        hn = _normmod(xs, norm_mix[i][None, :], sc1, sh1)
        if i % 2 == 0:
            e = i // 2
            xs = _nsa_mla_mixer(hn, pos, rel_bias, even_w_in[e], even_w_out[e], nsa_gate_b[e],
                                nsa_cmp_pos_k[e], nsa_cmp_w1_k[e], nsa_cmp_w2_k[e],
                                nsa_cmp_pos_v[e], nsa_cmp_w1_v[e], nsa_cmp_w2_v[e],
                                mla_q_norm[e], mla_w_uq[e], mla_kv_norm[e], mla_w_ukv[e], xs, g1)
        else:
            o = i // 2
            xs = _fox_mixer(hn, fox_w_in[o], fox_w_out[o], fox_f_b[o], fox_q_norm[o], fox_k_norm[o], xs, g1)
        fin = final_norm[None, :] if i == depth - 1 else None
        xs = _ffn(xs, norm_ffn[i][None, :], sc2, sh2, g2, ffn_w1[i].astype(BF), ffn_w3[i].astype(BF),
                  ffn_w2[i].astype(BF), fin)
    return xs[None]
```

```python
import functools
import math

import numpy as np
import jax
import jax.numpy as jnp
from jax import lax
from jax.experimental import pallas as pl
from jax.experimental.pallas import tpu as pltpu

D_MODEL = 2048
DEPTH = 2
EPS = 1e-6
NEG_INF = -1e30

NSA_HEADS = 8
NSA_KV_HEADS = 2
NSA_GROUP = NSA_HEADS // NSA_KV_HEADS
NSA_HEAD_DIM = 128
CMP_LEN = 32
CMP_STRIDE = 16
CMP_HIDDEN = 256
SLC_LEN = 64
SLC_TOPN = 16
WINDOW = 512
Q_BLOCK = 128

MLA_HEADS = 8
MLA_Q_RANK = 512
MLA_KV_RANK = 512
MLA_NOPE = 128
MLA_ROPE = 64
MLA_V = 128
ROPE_THETA = 10000.0

FOX_HEADS = 16
FOX_HEAD_DIM = D_MODEL // FOX_HEADS

REL_BUCKETS = 32
REL_MAX_DIST = 4096

FFN_HIDDEN = ((8 * D_MODEL + 2) // 3 + 255) // 256 * 256

LANES = 128
SUBLANES = 8
VMEM_LIMIT_BYTES = 56 * 1024 * 1024
LOG2E = math.log2(math.e)

BF = jnp.bfloat16
F32 = jnp.float32
_NT = (((1,), (1,)), ((), ()))


def _cparams(*sem):
    return pltpu.CompilerParams(dimension_semantics=sem, vmem_limit_bytes=VMEM_LIMIT_BYTES)


def _rms(x):
    return x * lax.rsqrt(jnp.mean(x * x, axis=-1, keepdims=True) + EPS)


def _adaln_kernel(c_ref, w_ref, b_ref, o_ref):
    c = c_ref[...]
    cond = c * jax.nn.sigmoid(c)
    acc = jnp.dot(cond.astype(BF), w_ref[0].astype(BF), preferred_element_type=F32)
    o_ref[0] = acc + b_ref[0]


def _adaln(c, ada_w, ada_b):
    depth, d, n = ada_w.shape
    tn = 1024
    c8 = jnp.broadcast_to(c, (SUBLANES, d))
    out = pl.pallas_call(
        _adaln_kernel,
        grid=(depth, n // tn),
        in_specs=[pl.BlockSpec((SUBLANES, d), lambda i, j: (0, 0)),
                  pl.BlockSpec((1, d, tn), lambda i, j: (i, 0, j)),
                  pl.BlockSpec((1, 1, tn), lambda i, j: (i, 0, j))],
        out_specs=pl.BlockSpec((1, SUBLANES, tn), lambda i, j: (i, 0, j)),
        out_shape=jax.ShapeDtypeStruct((depth, SUBLANES, n), F32),
        compiler_params=_cparams("arbitrary", "arbitrary"),
        name="adaln",
    )(c8, ada_w, ada_b.reshape(depth, 1, n))
    return out[:, 0, :]


def _normmod_kernel(x_ref, g_ref, sc_ref, sh_ref, o_ref):
    y = _rms(x_ref[...])
    o_ref[...] = ((y * g_ref[...]) * (1.0 + sc_ref[...]) + sh_ref[...]).astype(o_ref.dtype)


def _normmod(x, g, sc, sh):
    s, d = x.shape
    tm = 512
    row = pl.BlockSpec((1, d), lambda i: (0, 0))
    return pl.pallas_call(
        _normmod_kernel,
        grid=(s // tm,),
        in_specs=[pl.BlockSpec((tm, d), lambda i: (i, 0)), row, row, row],
        out_specs=pl.BlockSpec((tm, d), lambda i: (i, 0)),
        out_shape=jax.ShapeDtypeStruct((s, d), BF),
        compiler_params=_cparams("arbitrary"),
        name="normmod",
    )(x, g, sc, sh)


def _proj_kernel(*refs, n_lhs, n_epi, epi, head_major):
    lhs = refs[:n_lhs]
    ws = refs[n_lhs:2 * n_lhs]
    epis = refs[2 * n_lhs:2 * n_lhs + n_epi]
    o_ref = refs[-1]
    acc = None
    for a, w in zip(lhs, ws):
        d = jnp.dot(a[...], w[...], preferred_element_type=F32)
        acc = d if acc is None else acc + d
    res = epi(acc, *[e[...] for e in epis])
    if head_major:
        for r in range(o_ref.shape[0]):
            o_ref[r] = res[:, r * LANES:(r + 1) * LANES].astype(o_ref.dtype)
    else:
        o_ref[...] = res.astype(o_ref.dtype)


def _proj(lhs, ws, epi, epi_in=(), *, out_dtype, head_major=False, tm=1024, tn=512,
          lhs_col_block=None, name="proj"):
    m = lhs[0].shape[0]
    n = ws[0].shape[1]
    tm = min(tm, m)
    tn = min(tn, n)
    if lhs_col_block is None:
        lhs_col_block = [0] * len(lhs)
    in_specs = []
    for a, w, cb in zip(lhs, ws, lhs_col_block):
        in_specs.append(pl.BlockSpec((tm, w.shape[0]), lambda i, j, cb=cb: (i, cb)))
    for w in ws:
        in_specs.append(pl.BlockSpec((w.shape[0], tn), lambda i, j: (0, j)))
    arrays = list(lhs) + list(ws)
    for arr, kind in epi_in:
        if kind == "row":
            in_specs.append(pl.BlockSpec((1, tn), lambda i, j: (0, j)))
        elif kind == "const":
            in_specs.append(pl.BlockSpec(arr.shape, lambda i, j: (0, 0)))
        elif kind == "tile":
            in_specs.append(pl.BlockSpec((tm, tn), lambda i, j: (i, j)))
        elif kind == "rowtile":
            in_specs.append(pl.BlockSpec((tm, arr.shape[1]), lambda i, j: (i, 0)))
        else:
            raise ValueError(kind)
        arrays.append(arr)
    if head_major:
        hpt = tn // LANES
        out_spec = pl.BlockSpec((hpt, tm, LANES), lambda i, j: (j, i, 0))
        out_shape = jax.ShapeDtypeStruct((n // LANES, m, LANES), out_dtype)
    else:
        out_spec = pl.BlockSpec((tm, tn), lambda i, j: (i, j))
        out_shape = jax.ShapeDtypeStruct((m, n), out_dtype)
    kern = functools.partial(_proj_kernel, n_lhs=len(lhs), n_epi=len(epi_in), epi=epi,
                             head_major=head_major)
    return pl.pallas_call(
        kern,
        grid=(m // tm, n // tn),
        in_specs=in_specs,
        out_specs=out_spec,
        out_shape=out_shape,
        compiler_params=_cparams("arbitrary", "arbitrary"),
        name=name,
    )(*arrays)


def _epi_id(acc):
    return acc


def _epi_scale(scale, acc):
    return acc * scale


def _epi_sigmoid_bias(acc, b):
    return jax.nn.sigmoid(acc + b)


def _epi_logsigmoid_bias(acc, b):
    return jax.nn.log_sigmoid(acc + b)


def _epi_sigmoid(acc):
    return jax.nn.sigmoid(acc)


def _epi_rmsnorm(acc, g):
    return _rms(acc) * g


def _epi_headnorm(scale, acc, g):
    outs = []
    for r in range(acc.shape[1] // LANES):
        outs.append(_rms(acc[:, r * LANES:(r + 1) * LANES]) * g * scale)
    return jnp.concatenate(outs, axis=1)


def _epi_rope(scale, acc, c, s1, s2):
    reps = acc.shape[1] // LANES
    half = MLA_ROPE // 2
    if reps > 1:
        c = jnp.concatenate([c] * reps, axis=1)
        s1 = jnp.concatenate([s1] * reps, axis=1)
        s2 = jnp.concatenate([s2] * reps, axis=1)
    n = acc.shape[1]
    out = acc * c + pltpu.roll(acc, n - half, 1) * s1 + pltpu.roll(acc, half, 1) * s2
    return out * scale


def _epi_residual(acc, x, g):
    return x + g * acc


def _ffn_kernel(*refs, final):
    if final:
        x_ref, g_ref, sc_ref, sh_ref, g2_ref, w1_ref, w3_ref, w2_ref, fn_ref, o_ref, hn_ref, acc_ref = refs
    else:
        x_ref, g_ref, sc_ref, sh_ref, g2_ref, w1_ref, w3_ref, w2_ref, o_ref, hn_ref, acc_ref = refs
    f = pl.program_id(1)

    @pl.when(f == 0)
    def _():
        y = _rms(x_ref[...])
        hn_ref[...] = ((y * g_ref[...]) * (1.0 + sc_ref[...]) + sh_ref[...]).astype(BF)
        acc_ref[...] = jnp.zeros_like(acc_ref)

    h = hn_ref[...]
    h1 = jnp.dot(h, w1_ref[...], preferred_element_type=F32)
    h3 = jnp.dot(h, w3_ref[...], preferred_element_type=F32)
    a = (h1 * jax.nn.sigmoid(h1)) * h3
    acc_ref[...] += jnp.dot(a.astype(BF), w2_ref[...], preferred_element_type=F32)

    @pl.when(f == pl.num_programs(1) - 1)
    def _():
        xn = x_ref[...] + g2_ref[...] * acc_ref[...]
        if final:
            xn = _rms(xn) * fn_ref[...]
        o_ref[...] = xn


def _ffn(x, g, sc, sh, g2, w1, w3, w2, final_gain=None):
    s, d = x.shape
    fdim = w1.shape[1]
    tm = min(512, s)
    tf = 512
    row = pl.BlockSpec((1, d), lambda i, f: (0, 0))
    in_specs = [pl.BlockSpec((tm, d), lambda i, f: (i, 0)), row, row, row, row,
                pl.BlockSpec((d, tf), lambda i, f: (0, f)),
                pl.BlockSpec((d, tf), lambda i, f: (0, f)),
                pl.BlockSpec((tf, d), lambda i, f: (f, 0))]
    arrays = [x, g, sc, sh, g2, w1, w3, w2]
    if final_gain is not None:
        in_specs.append(row)
        arrays.append(final_gain)
    return pl.pallas_call(
        functools.partial(_ffn_kernel, final=final_gain is not None),
        grid=(s // tm, fdim // tf),
        in_specs=in_specs,
        out_specs=pl.BlockSpec((tm, d), lambda i, f: (i, 0)),
        out_shape=jax.ShapeDtypeStruct((s, d), F32),
        scratch_shapes=[pltpu.VMEM((tm, d), BF), pltpu.VMEM((tm, d), F32)],
        compiler_params=_cparams("arbitrary", "arbitrary"),
        name="ffn",
    )(*arrays)


def _flash_update(s, vp, m_ref, acc_ref, row_shift=None):
    m_prev = m_ref[...]
    m_tile = jnp.max(s, axis=1, keepdims=True)
    if row_shift is not None:
        m_tile = m_tile + row_shift
    m_new = jnp.maximum(m_prev, m_tile)
    alpha = jnp.exp2(m_prev - m_new)
    sub = m_new if row_shift is None else m_new - row_shift
    p = jnp.exp2(s - sub).astype(BF)
    acc_ref[...] = alpha * acc_ref[...] + jnp.dot(p, vp, preferred_element_type=F32)
    m_ref[...] = m_new


def _init_state(m_ref, acc_ref):
    m_ref[...] = jnp.full_like(m_ref, NEG_INF)
    acc_ref[...] = jnp.zeros_like(acc_ref)


def _with_ones(v):
    return jnp.concatenate([v, jnp.ones(v.shape, v.dtype)], axis=1)


def _normalized(acc):
    return acc[:, :LANES] / acc[:, LANES:]


def _causal_mask(t):
    rows = lax.broadcasted_iota(jnp.int32, (t, t), 0)
    cols = lax.broadcasted_iota(jnp.int32, (t, t), 1)
    return cols <= rows


def _mla_kernel(qn_ref, qr_ref, kn_ref, kr_ref, v_ref, o_ref, m_ref, acc_ref, *, tc, nch):
    qi = pl.program_id(1)
    w = tc * nch
    qs = [jnp.concatenate([qn_ref[0, c * tc:(c + 1) * tc, :], qr_ref[0, c * tc:(c + 1) * tc, :]], axis=1)
          for c in range(nch)]
    _init_state(m_ref, acc_ref)

    def load(off, width):
        k = jnp.concatenate([kn_ref[0, pl.ds(off, width), :], kr_ref[pl.ds(off, width), :]], axis=1)
        return k, _with_ones(v_ref[0, pl.ds(off, width), :])

    def upd(c, k, vp, masked):
        s = lax.dot_general(qs[c], k, _NT, preferred_element_type=F32)
        if masked:
            s = jnp.where(_causal_mask(tc), s, NEG_INF)
        _flash_update(s, vp, m_ref.at[c], acc_ref.at[c])

    def body(j, carry):
        k, vp = load(pl.multiple_of(j * w, w), w)
        for c in range(nch):
            upd(c, k, vp, False)
        return carry

    lax.fori_loop(0, qi, body, 0)
    for kc in range(nch):
        k, vp = load(pl.multiple_of(qi * w + kc * tc, tc), tc)
        for c in range(kc, nch):
            upd(c, k, vp, c == kc)
    for c in range(nch):
        o_ref[c * tc:(c + 1) * tc, :] = _normalized(acc_ref[c]).astype(o_ref.dtype)


def _mla_attention(qn, qr, kvh, kr):
    h, s, _ = qn.shape
    tc = min(512, s // 2)
    nch = 2
    w = tc * nch
    return pl.pallas_call(
        functools.partial(_mla_kernel, tc=tc, nch=nch),
        grid=(h, s // w),
        in_specs=[pl.BlockSpec((1, w, LANES), lambda hh, i: (hh, i, 0)),
                  pl.BlockSpec((1, w, LANES), lambda hh, i: (hh, i, 0)),
                  pl.BlockSpec((1, s, LANES), lambda hh, i: (2 * hh, 0, 0)),
                  pl.BlockSpec((s, LANES), lambda hh, i: (0, 0)),
                  pl.BlockSpec((1, s, LANES), lambda hh, i: (2 * hh + 1, 0, 0))],
        out_specs=pl.BlockSpec((w, LANES), lambda hh, i: (i, hh)),
        out_shape=jax.ShapeDtypeStruct((s, h * LANES), BF),
        scratch_shapes=[pltpu.VMEM((nch, tc, 1), F32), pltpu.VMEM((nch, tc, 2 * LANES), F32)],
        compiler_params=_cparams("arbitrary", "arbitrary"),
        name="mla_attn",
    )(qn, qr, kvh, kr, kvh)


def _fox_kernel(q_ref, k_ref, v_ref, ck_ref, cum_ref, og_ref, o_ref, m_ref, acc_ref, *, tc, nch):
    hh = pl.program_id(0)
    qi = pl.program_id(1)
    w = tc * nch
    qs = [q_ref[0, c * tc:(c + 1) * tc, :] for c in range(nch)]
    lane = lax.broadcasted_iota(jnp.int32, (tc, LANES), 1)
    cqs = [jnp.sum(jnp.where(lane == hh, cum_ref[c * tc:(c + 1) * tc, :], 0.0), axis=1, keepdims=True)
           for c in range(nch)]
    _init_state(m_ref, acc_ref)

    def load(off, width):
        return (k_ref[0, pl.ds(off, width), :], _with_ones(v_ref[0, pl.ds(off, width), :]),
                ck_ref[0, :, pl.ds(off, width)])

    def upd(c, k, vp, ck, masked):
        s = lax.dot_general(qs[c], k, _NT, preferred_element_type=F32) - ck
        if masked:
            s = jnp.where(_causal_mask(tc), s, NEG_INF)
        _flash_update(s, vp, m_ref.at[c], acc_ref.at[c], row_shift=cqs[c])

    def body(j, carry):
        k, vp, ck = load(pl.multiple_of(j * w, w), w)
        for c in range(nch):
            upd(c, k, vp, ck, False)
        return carry

    lax.fori_loop(0, qi, body, 0)
    for kc in range(nch):
        k, vp, ck = load(pl.multiple_of(qi * w + kc * tc, tc), tc)
        for c in range(kc, nch):
            upd(c, k, vp, ck, c == kc)
    for c in range(nch):
        rows = slice(c * tc, (c + 1) * tc)
        o_ref[rows, :] = (_normalized(acc_ref[c]) * og_ref[rows, :]).astype(o_ref.dtype)


def _fox_attention(q, k, v, cum_t, cum, sig_og):
    h, s, _ = q.shape
    tc = min(512, s // 2)
    nch = 2
    w = tc * nch
    hm = pl.BlockSpec((1, s, LANES), lambda hh, i: (hh, 0, 0))
    return pl.pallas_call(
        functools.partial(_fox_kernel, tc=tc, nch=nch),
        grid=(h, s // w),
        in_specs=[pl.BlockSpec((1, w, LANES), lambda hh, i: (hh, i, 0)), hm, hm,
                  pl.BlockSpec((1, 1, s), lambda hh, i: (hh, 0, 0)),
                  pl.BlockSpec((w, LANES), lambda hh, i: (i, 0)),
                  pl.BlockSpec((w, LANES), lambda hh, i: (i, hh))],
        out_specs=pl.BlockSpec((w, LANES), lambda hh, i: (i, hh)),
        out_shape=jax.ShapeDtypeStruct((s, h * LANES), BF),
        scratch_shapes=[pltpu.VMEM((nch, tc, 1), F32), pltpu.VMEM((nch, tc, 2 * LANES), F32)],
        compiler_params=_cparams("arbitrary", "arbitrary"),
        name="fox_attn",
    )(q, k, v, cum_t, cum, sig_og)


def _cumsum_kernel(x_ref, o_ref, ot_ref, carry_ref, *, t, out_scale):
    @pl.when(pl.program_id(0) == 0)
    def _():
        carry_ref[...] = jnp.zeros_like(carry_ref)

    x = x_ref[...]
    rows = lax.broadcasted_iota(jnp.int32, (t, t), 0)
    cols = lax.broadcasted_iota(jnp.int32, (t, t), 1)
    tri = jnp.where(cols <= rows, 1.0, 0.0).astype(BF)
    hi = x.astype(BF)
    r1 = x - hi.astype(F32)
    mid = r1.astype(BF)
    lo = (r1 - mid.astype(F32)).astype(BF)
    cum = (jnp.dot(tri, hi, preferred_element_type=F32) + jnp.dot(tri, mid, preferred_element_type=F32)
           + jnp.dot(tri, lo, preferred_element_type=F32)) + carry_ref[...]
    scaled = cum * out_scale
    o_ref[...] = scaled
    ot_ref[...] = scaled.T
    carry_ref[...] = cum[t - 1:t, :]


def _cumsum_tokens(x, out_scale):
    s, n = x.shape
    t = min(256, s)
    return pl.pallas_call(
        functools.partial(_cumsum_kernel, t=t, out_scale=out_scale),
        grid=(s // t,),
        in_specs=[pl.BlockSpec((t, n), lambda i: (i, 0))],
        out_specs=[pl.BlockSpec((t, n), lambda i: (i, 0)), pl.BlockSpec((n, t), lambda i: (0, i))],
        out_shape=[jax.ShapeDtypeStruct((s, n), F32), jax.ShapeDtypeStruct((n, s), F32)],
        scratch_shapes=[pltpu.VMEM((1, n), F32)],
        compiler_params=_cparams("arbitrary"),
        name="cumsum",
    )(x)


def _compress_kernel(a_ref, pos_ref, w1_ref, w2_ref, o_ref):
    a = a_ref[0]
    nc = a.shape[0]
    p1 = jnp.dot((a + pos_ref[0, 0:1, :]).astype(BF), w1_ref[0, 0], preferred_element_type=F32)
    p2 = jnp.dot((a + pos_ref[0, 1:2, :]).astype(BF), w1_ref[0, 1], preferred_element_type=F32)
    h = p1 + pltpu.roll(p2, nc - 1, 0)
    act = h * jax.nn.sigmoid(h)
    o_ref[0] = jnp.dot(act.astype(BF), w2_ref[0], preferred_element_type=F32).astype(o_ref.dtype)


def _nsa_compress(kv_cmp, pos, w1, w2):
    c, nc, gw = kv_cmp.shape
    return pl.pallas_call(
        _compress_kernel,
        grid=(c,),
        in_specs=[pl.BlockSpec((1, nc, gw), lambda i: (i, 0, 0)),
                  pl.BlockSpec((1, 2, gw), lambda i: (i // 2, 0, 0)),
                  pl.BlockSpec((1, 2, gw, CMP_HIDDEN), lambda i: (i // 2, 0, 0, 0)),
                  pl.BlockSpec((1, CMP_HIDDEN, NSA_HEAD_DIM), lambda i: (i // 2, 0, 0))],
        out_specs=pl.BlockSpec((1, nc, NSA_HEAD_DIM), lambda i: (i, 0, 0)),
        out_shape=jax.ShapeDtypeStruct((c, nc, NSA_HEAD_DIM), BF),
        compiler_params=_cparams("arbitrary"),
        name="nsa_compress",
    )(kv_cmp, pos, w1, w2)


def _bias_table_kernel(brev_ref, basc_ref, tbl_ref, tb_ref, asc_ref, *, s, mc):
    tbl = tbl_ref[0] * LOG2E

    def lookup(bkt):
        out = jnp.zeros(bkt.shape, F32)
        for b in range(REL_BUCKETS):
            out = jnp.where(bkt == b, tbl[:, b:b + 1], out)
        return out

    rev = lookup(brev_ref[...])
    asc_ref[0] = lookup(basc_ref[...])
    for m in range(mc + 1):
        win = rev[:, s - Q_BLOCK * m:s - Q_BLOCK * m + 2 * Q_BLOCK]
        rolled = pltpu.roll(jnp.broadcast_to(win, (Q_BLOCK, 2 * Q_BLOCK)), Q_BLOCK + 1, 1,
                            stride=1, stride_axis=0)
        tb_ref[0, m] = rolled[:, :Q_BLOCK]


def _t5_bucket(dist):
    max_exact = REL_BUCKETS // 2
    d = jnp.maximum(dist, 0)
    ratio = jnp.log(jnp.maximum(d, max_exact).astype(F32) / max_exact) / math.log(REL_MAX_DIST / max_exact)
    large = jnp.minimum(max_exact + (ratio * (REL_BUCKETS - max_exact)).astype(jnp.int32), REL_BUCKETS - 1)
    return jnp.where(d < max_exact, d, large)


def _bias_tables(rel_bias, s):
    assert Q_BLOCK == SUBLANES * CMP_STRIDE
    h = rel_bias.shape[1]
    qblocks = s // Q_BLOCK
    max_exact = REL_BUCKETS // 2
    d_const = int(math.ceil(max_exact * (REL_MAX_DIST / max_exact)
                            ** ((REL_BUCKETS - max_exact - 1) / (REL_BUCKETS - max_exact)))) + 1
    mc = min(-(-(d_const + Q_BLOCK - 1) // Q_BLOCK), qblocks - 1)
    nrev = s + 2 * Q_BLOCK
    nasc = s + 3 * Q_BLOCK
    brev = _t5_bucket(s + Q_BLOCK - 1 - jnp.arange(nrev, dtype=jnp.int32))[None, :]
    basc = _t5_bucket(jnp.arange(nasc, dtype=jnp.int32) - 2 * Q_BLOCK)[None, :]
    tb, asc = pl.pallas_call(
        functools.partial(_bias_table_kernel, s=s, mc=mc),
        grid=(h,),
        in_specs=[pl.BlockSpec((1, nrev), lambda i: (0, 0)),
                  pl.BlockSpec((1, nasc), lambda i: (0, 0)),
                  pl.BlockSpec((1, 1, REL_BUCKETS), lambda i: (i, 0, 0))],
        out_specs=[pl.BlockSpec((1, mc + 1, Q_BLOCK, Q_BLOCK), lambda i: (i, 0, 0, 0)),
                   pl.BlockSpec((1, 1, nasc), lambda i: (i, 0, 0))],
        out_shape=[jax.ShapeDtypeStruct((h, mc + 1, Q_BLOCK, Q_BLOCK), F32),
                   jax.ShapeDtypeStruct((h, 1, nasc), F32)],
        compiler_params=_cparams("arbitrary"),
        name="bias_tables",
    )(brev, basc, rel_bias.T.reshape(h, 1, REL_BUCKETS))
    asc = asc[:, 0, :]
    base = 2 * Q_BLOCK - (CMP_LEN - 1)
    segs = [asc[:, base - CMP_STRIDE * nn:base - CMP_STRIDE * nn + s].reshape(h, qblocks, Q_BLOCK)
            for nn in range(SUBLANES)]
    trc = jnp.flip(jnp.stack(segs, axis=2), axis=1).reshape(h, qblocks * SUBLANES, Q_BLOCK)
    return tb, jnp.pad(trc, ((0, 0), (0, qblocks * SUBLANES), (0, 0)))


def _nsa_cmp_kernel(q_ref, kc_ref, vct_ref, trc_ref, smt_ref, gate_ref, oc_ref, sel_ref, *, qblocks, topn):
    qb = pl.program_id(1)
    rq = NSA_GROUP * Q_BLOCK
    q = q_ref[...].reshape(rq, NSA_HEAD_DIM)
    kc = kc_ref[0]
    nc = kc.shape[0]
    nslc = smt_ref.shape[0]
    s = lax.dot_general(kc, q, _NT, preferred_element_type=F32)
    n_io = lax.broadcasted_iota(jnp.int32, (nc, Q_BLOCK), 0)
    i_io = lax.broadcasted_iota(jnp.int32, (nc, Q_BLOCK), 1)
    mask = (n_io * CMP_STRIDE + (CMP_LEN - 1)) <= (qb * Q_BLOCK + i_io)
    off = pl.multiple_of((qblocks - 1 - qb) * SUBLANES, SUBLANES)
    ps = []
    for r in range(NSA_GROUP):
        l = jnp.where(mask, s[:, r * Q_BLOCK:(r + 1) * Q_BLOCK] + trc_ref[r, pl.ds(off, nc), :], NEG_INF)
        m = jnp.max(l, axis=0, keepdims=True)
        p = jnp.where(mask, jnp.exp2(l - m), 0.0)
        den = jnp.maximum(jnp.sum(p, axis=0, keepdims=True), 1e-30)
        ps.append(p / den)
    p_all = jnp.concatenate(ps, axis=1).astype(BF)
    oc_t = jnp.dot(vct_ref[0], p_all, preferred_element_type=F32)
    imp4 = jnp.dot(smt_ref[...], p_all, preferred_element_type=F32)
    imp = imp4[:, 0:Q_BLOCK]
    for r in range(1, NSA_GROUP):
        imp = imp + imp4[:, r * Q_BLOCK:(r + 1) * Q_BLOCK]

    j_io = lax.broadcasted_iota(jnp.int32, (nslc, Q_BLOCK), 0)
    t_io = qb * Q_BLOCK + lax.broadcasted_iota(jnp.int32, (nslc, Q_BLOCK), 1)
    cur = t_io >> int(math.log2(SLC_LEN))
    forced = jnp.logical_or(j_io == 0, jnp.logical_or(j_io == cur, j_io == cur - 1))
    val = jnp.where(forced, 1e9, jnp.where(j_io <= cur, imp, -1e9))
    sel = jnp.zeros((nslc, Q_BLOCK), F32)
    for _ in range(topn):
        mx = jnp.max(val, axis=0, keepdims=True)
        cand = jnp.where(val == mx, j_io, nslc)
        jmin = jnp.min(cand, axis=0, keepdims=True)
        pick = j_io == jmin
        sel = jnp.where(pick, 1.0, sel)
        val = jnp.where(pick, -3e38, val)
    sel_ref[0] = sel.T.astype(sel_ref.dtype)

    gates = gate_ref[0]
    for r in range(NSA_GROUP):
        o_r = oc_t[:, r * Q_BLOCK:(r + 1) * Q_BLOCK].T
        oc_ref[:, r * NSA_HEAD_DIM:(r + 1) * NSA_HEAD_DIM] = o_r * gates[:, 3 * r:3 * r + 1]


def _nsa_cmp(q, kc, vct, trc, smt, gates):
    h, s, d = q.shape
    g = NSA_KV_HEADS
    qblocks = s // Q_BLOCK
    nc = kc.shape[1]
    nslc = smt.shape[0]
    topn = min(SLC_TOPN, nslc)
    return pl.pallas_call(
        functools.partial(_nsa_cmp_kernel, qblocks=qblocks, topn=topn),
        grid=(g, qblocks),
        in_specs=[pl.BlockSpec((NSA_GROUP, Q_BLOCK, d), lambda gg, i: (gg, i, 0)),
                  pl.BlockSpec((1, nc, d), lambda gg, i: (gg, 0, 0)),
                  pl.BlockSpec((1, d, nc), lambda gg, i: (gg, 0, 0)),
                  pl.BlockSpec((NSA_GROUP, trc.shape[1], Q_BLOCK), lambda gg, i: (gg, 0, 0)),
                  pl.BlockSpec(smt.shape, lambda gg, i: (0, 0)),
                  pl.BlockSpec((1, Q_BLOCK, LANES), lambda gg, i: (gg, i, 0))],
        out_specs=[pl.BlockSpec((Q_BLOCK, NSA_GROUP * d), lambda gg, i: (i, gg)),
                   pl.BlockSpec((1, Q_BLOCK, nslc), lambda gg, i: (gg, i, 0))],
        out_shape=[jax.ShapeDtypeStruct((s, h * d), F32),
                   jax.ShapeDtypeStruct((g, s, nslc), BF)],
        compiler_params=_cparams("arbitrary", "arbitrary"),
        name="nsa_cmp",
    )(q, kc, vct, trc, smt, gates)


def _nsa_sw_kernel(q_ref, ks_ref, vs_ref, kw_ref, vw_ref, tb_ref, sel_ref, e_ref, gate_ref, oc_ref,
                   o_ref, m_ref, acc_ref, *, tk, mc, nch):
    i = pl.program_id(1)
    rq = NSA_GROUP * Q_BLOCK
    nsub = tk // Q_BLOCK
    qbs = [i * nch + c for c in range(nch)]
    qs = [q_ref[:, c * Q_BLOCK:(c + 1) * Q_BLOCK, :].reshape(rq, NSA_HEAD_DIM) for c in range(nch)]
    sels = [sel_ref[0, c * Q_BLOCK:(c + 1) * Q_BLOCK, :] for c in range(nch)]

    def biased(c, s, msk, kb0, nblk):
        parts = []
        for r in range(NSA_GROUP):
            bias = jnp.concatenate(
                [tb_ref[r, jnp.clip(qbs[c] - (kb0 + b), 0, mc)] for b in range(nblk)], axis=1)
            parts.append(jnp.where(msk, s[r * Q_BLOCK:(r + 1) * Q_BLOCK, :] + bias, NEG_INF))
        return jnp.concatenate(parts, axis=0)

    def gated(c, o, branch, base):
        gates = gate_ref[0, c * Q_BLOCK:(c + 1) * Q_BLOCK, :]
        outs = []
        for r in range(NSA_GROUP):
            o_r = o[r * Q_BLOCK:(r + 1) * Q_BLOCK, :] * gates[:, 3 * r + branch:3 * r + branch + 1]
            outs.append(base[:, r * NSA_HEAD_DIM:(r + 1) * NSA_HEAD_DIM] + o_r)
        return jnp.concatenate(outs, axis=1)

    _init_state(m_ref, acc_ref)

    def sel_tile(j, causal):
        off = pl.multiple_of(j * tk, tk)
        k = ks_ref[0, pl.ds(off, tk), :]
        vp = _with_ones(vs_ref[0, pl.ds(off, tk), :])
        e = e_ref[j]
        for c in range(nch):
            s = lax.dot_general(qs[c], k, _NT, preferred_element_type=F32)
            msk = jnp.dot(sels[c], e, preferred_element_type=F32) > 0.5
            if causal:
                kk = j * tk + lax.broadcasted_iota(jnp.int32, (Q_BLOCK, tk), 1)
                ii = qbs[c] * Q_BLOCK + lax.broadcasted_iota(jnp.int32, (Q_BLOCK, tk), 0)
                msk = jnp.logical_and(msk, kk <= ii)
            _flash_update(biased(c, s, msk, j * nsub, nsub), vp, m_ref.at[c], acc_ref.at[c])

    n_full = (i * nch) // nsub

    def body(j, carry):
        sel_tile(j, False)
        return carry

    lax.fori_loop(0, n_full, body, 0)
    sel_tile(n_full, True)
    outs = [gated(c, _normalized(acc_ref[c]), 1, oc_ref[c * Q_BLOCK:(c + 1) * Q_BLOCK, :]) for c in range(nch)]

    wk = WINDOW + Q_BLOCK
    wblk = wk // Q_BLOCK
    for c in range(nch):
        kb0 = jnp.maximum(qbs[c] - WINDOW // Q_BLOCK, 0)
        off = pl.multiple_of(kb0 * Q_BLOCK, Q_BLOCK)
        s = lax.dot_general(qs[c], kw_ref[0, pl.ds(off, wk), :], _NT, preferred_element_type=F32)
        rel = (qbs[c] * Q_BLOCK + lax.broadcasted_iota(jnp.int32, (Q_BLOCK, wk), 0)) - (
            off + lax.broadcasted_iota(jnp.int32, (Q_BLOCK, wk), 1))
        msk = jnp.logical_and(rel >= 0, rel < WINDOW)
        l = biased(c, s, msk, kb0, wblk)
        p = jnp.exp2(l - jnp.max(l, axis=1, keepdims=True)).astype(BF)
        ow = jnp.dot(p, _with_ones(vw_ref[0, pl.ds(off, wk), :]), preferred_element_type=F32)
        o_ref[c * Q_BLOCK:(c + 1) * Q_BLOCK, :] = gated(c, _normalized(ow), 2, outs[c]).astype(o_ref.dtype)


def _nsa_sw(q, kvsw, tb, sel, emat, gates, oc):
    h, s, d = q.shape
    g = NSA_KV_HEADS
    nch = 2
    qrows = nch * Q_BLOCK
    tk = emat.shape[2]
    mc = tb.shape[1] - 1
    nslc = sel.shape[2]
    rq = NSA_GROUP * Q_BLOCK
    assert tk % qrows == 0 and WINDOW + Q_BLOCK <= s

    def kv(slot):
        return pl.BlockSpec((1, s, d), lambda gg, i, slot=slot: (slot + gg, 0, 0))

    return pl.pallas_call(
        functools.partial(_nsa_sw_kernel, tk=tk, mc=mc, nch=nch),
        grid=(g, s // qrows),
        in_specs=[pl.BlockSpec((NSA_GROUP, qrows, d), lambda gg, i: (gg, i, 0)),
                  kv(0), kv(2), kv(4), kv(6),
                  pl.BlockSpec((NSA_GROUP,) + tb.shape[1:], lambda gg, i: (gg, 0, 0, 0)),
                  pl.BlockSpec((1, qrows, nslc), lambda gg, i: (gg, i, 0)),
                  pl.BlockSpec(emat.shape, lambda gg, i: (0, 0, 0)),
                  pl.BlockSpec((1, qrows, LANES), lambda gg, i: (gg, i, 0)),
                  pl.BlockSpec((qrows, NSA_GROUP * d), lambda gg, i: (i, gg))],
        out_specs=pl.BlockSpec((qrows, NSA_GROUP * d), lambda gg, i: (i, gg)),
        out_shape=jax.ShapeDtypeStruct((s, h * d), BF),
        scratch_shapes=[pltpu.VMEM((nch, rq, 1), F32), pltpu.VMEM((nch, rq, 2 * LANES), F32)],
        compiler_params=_cparams("arbitrary", "arbitrary"),
        name="nsa_sel_win",
    )(q, kvsw, kvsw, kvsw, kvsw, tb, sel, emat, gates, oc)


def _selection_map_t(nc, nslc):
    n = np.arange(nc)[None, :] * CMP_STRIDE
    j0 = np.arange(nslc)[:, None] * SLC_LEN
    valid = np.arange(nc)[None, :] < nc - 1
    return jnp.asarray(((n < j0 + SLC_LEN) & (n + CMP_LEN > j0) & valid).astype(np.float32), dtype=BF)


def _expand_matrix(s, nslc, tk):
    tok = np.arange(s).reshape(s // tk, 1, tk)
    j = np.arange(nslc).reshape(1, nslc, 1)
    return jnp.asarray((tok // SLC_LEN == j).astype(np.float32), dtype=BF)


def _rope_tables(positions):
    half = MLA_ROPE // 2
    inv = ROPE_THETA ** (-jnp.arange(half, dtype=F32) / half)
    ang = positions.astype(F32)[:, None] * inv
    cos, sin = jnp.cos(ang), jnp.sin(ang)
    z = jnp.zeros_like(cos)
    zpad = jnp.zeros((positions.shape[0], LANES - MLA_ROPE), F32)
    c = jnp.concatenate([cos, cos, zpad], axis=1)
    s1 = jnp.concatenate([-sin, z, zpad], axis=1)
    s2 = jnp.concatenate([z, sin, zpad], axis=1)
    return c, s1, s2


def _pad_cols(w, n):
    return jnp.pad(w, ((0, 0), (0, n - w.shape[1])))


def _nsa_mla_mixer(hn, pos, rel_bias, w_in, w_out, gate_b, pos_k, w1_k, w2_k, pos_v, w1_v, w2_v,
                   q_norm, w_uq, kv_norm, w_ukv, x, g1):
    s = hn.shape[0]
    d = NSA_HEAD_DIM
    nq = NSA_HEADS * d
    nkv = 2 * NSA_KV_HEADS * d
    o0 = 0
    w_q = w_in[:, o0:o0 + nq]; o0 += nq
    w_kvc = w_in[:, o0:o0 + nkv]; o0 += nkv
    w_kvsw = w_in[:, o0:o0 + 2 * nkv]; o0 += 2 * nkv
    w_g = w_in[:, o0:o0 + 3 * NSA_HEADS]; o0 += 3 * NSA_HEADS
    w_lat = w_in[:, o0:o0 + MLA_Q_RANK + MLA_KV_RANK]; o0 += MLA_Q_RANK + MLA_KV_RANK
    w_kr = w_in[:, o0:o0 + MLA_ROPE]

    q_nsa = _proj([hn], [w_q.astype(BF)], functools.partial(_epi_scale, d ** -0.5 * LOG2E),
                  out_dtype=BF, head_major=True, tn=1024, name="proj_q_nsa")
    kv_cmp = _proj([hn], [w_kvc.astype(BF)], _epi_id, out_dtype=F32, head_major=True, name="proj_kv_cmp")
    kv_sw = _proj([hn], [w_kvsw.astype(BF)], _epi_id, out_dtype=BF, head_major=True, tn=1024,
                  name="proj_kv_sw")
    per_g = 3 * NSA_GROUP
    w_gp = jnp.concatenate([_pad_cols(w_g[:, g * per_g:(g + 1) * per_g], LANES) for g in range(NSA_KV_HEADS)], 1)
    b_gp = jnp.concatenate([_pad_cols(gate_b[None, g * per_g:(g + 1) * per_g], LANES)
                            for g in range(NSA_KV_HEADS)], 1)
    gates = _proj([hn], [w_gp.astype(BF)], _epi_sigmoid_bias, [(b_gp, "row")], out_dtype=F32,
                  head_major=True, tn=LANES, name="proj_gates")

    nc = s // CMP_STRIDE
    gw = CMP_STRIDE * d
    pos_kv = jnp.stack([pos_k.reshape(2, gw), pos_v.reshape(2, gw)])
    w1_kv = jnp.stack([w1_k.reshape(2, gw, CMP_HIDDEN), w1_v.reshape(2, gw, CMP_HIDDEN)]).astype(BF)
    w2_kv = jnp.stack([w2_k, w2_v]).astype(BF)
    kvc = _nsa_compress(kv_cmp.reshape(2 * NSA_KV_HEADS, nc, gw), pos_kv, w1_kv, w2_kv)
    kc = kvc[:NSA_KV_HEADS]
    vct = jnp.swapaxes(kvc[NSA_KV_HEADS:], 1, 2)

    nslc = s // SLC_LEN
    tb, trc = _bias_tables(rel_bias, s)
    smt = _selection_map_t(nc, nslc)
    oc, sel = _nsa_cmp(q_nsa, kc, vct, trc, smt, gates)
    tk = min(512, s)
    o_nsa = _nsa_sw(q_nsa, kv_sw, tb, sel, _expand_matrix(s, nslc, tk), gates, oc)

    lat = _proj([hn], [w_lat.astype(BF)], _epi_rmsnorm,
                [(jnp.concatenate([q_norm, kv_norm])[None, :], "row")], out_dtype=BF, tn=MLA_Q_RANK,
                name="proj_mla_latent")
    c, s1, s2 = _rope_tables(pos)
    kr = _proj([hn], [_pad_cols(w_kr, LANES).astype(BF)], functools.partial(_epi_rope, 1.0),
               [(c, "rowtile"), (s1, "rowtile"), (s2, "rowtile")], out_dtype=BF, name="proj_mla_kr")
    scale = (MLA_NOPE + MLA_ROPE) ** -0.5 * LOG2E
    w_uq3 = w_uq.reshape(MLA_Q_RANK, MLA_HEADS, MLA_NOPE + MLA_ROPE)
    w_uqn = w_uq3[:, :, :MLA_NOPE].reshape(MLA_Q_RANK, MLA_HEADS * MLA_NOPE)
    w_uqr = jnp.pad(w_uq3[:, :, MLA_NOPE:], ((0, 0), (0, 0), (0, LANES - MLA_ROPE))).reshape(
        MLA_Q_RANK, MLA_HEADS * LANES)
    qn = _proj([lat], [w_uqn.astype(BF)], functools.partial(_epi_scale, scale), out_dtype=BF,
               head_major=True, tn=1024, lhs_col_block=[0], name="proj_mla_qn")
    qr = _proj([lat], [w_uqr.astype(BF)], functools.partial(_epi_rope, scale),
               [(c, "rowtile"), (s1, "rowtile"), (s2, "rowtile")], out_dtype=BF, head_major=True,
               lhs_col_block=[0], name="proj_mla_qr")
    kvh = _proj([lat], [w_ukv.astype(BF)], _epi_id, out_dtype=BF, head_major=True, tn=1024,
                lhs_col_block=[1], name="proj_mla_kv")
    o_mla = _mla_attention(qn, qr, kvh, kr)

    return _proj([o_nsa, o_mla], [w_out[:nq].astype(BF), w_out[nq:].astype(BF)], _epi_residual,
                 [(x, "tile"), (g1, "row")], out_dtype=F32, tn=1024, name="proj_even_out")


def _fox_mixer(hn, w_in, w_out, f_b, q_norm, k_norm, x, g1):
    d = D_MODEL
    dh = FOX_HEAD_DIM
    w_q, w_k, w_v = w_in[:, 0:d], w_in[:, d:2 * d], w_in[:, 2 * d:3 * d]
    w_f = w_in[:, 3 * d:3 * d + FOX_HEADS]
    w_og = w_in[:, 3 * d + FOX_HEADS:]
    q = _proj([hn], [w_q.astype(BF)], functools.partial(_epi_headnorm, dh ** -0.5 * LOG2E),
              [(q_norm[None, :], "const")], out_dtype=BF, head_major=True, tn=1024, name="proj_fox_q")
    k = _proj([hn], [w_k.astype(BF)], functools.partial(_epi_headnorm, 1.0), [(k_norm[None, :], "const")],
              out_dtype=BF, head_major=True, tn=1024, name="proj_fox_k")
    v = _proj([hn], [w_v.astype(BF)], _epi_id, out_dtype=BF, head_major=True, tn=1024, name="proj_fox_v")
    lf = _proj([hn], [_pad_cols(w_f, LANES).astype(BF)], _epi_logsigmoid_bias,
               [(_pad_cols(f_b[None, :], LANES), "row")], out_dtype=F32, name="proj_fox_f")
    sig_og = _proj([hn], [w_og.astype(BF)], _epi_sigmoid, out_dtype=F32, tn=1024, name="proj_fox_og")
    cum, cum_t = _cumsum_tokens(lf, LOG2E)
    cum_t = cum_t[:FOX_HEADS].reshape(FOX_HEADS, 1, -1)
    o = _fox_attention(q, k, v, cum_t, cum, sig_og)
    return _proj([o], [w_out.astype(BF)], _epi_residual, [(x, "tile"), (g1, "row")], out_dtype=F32, tn=1024,
                 name="proj_fox_out")


def kernel(x, c, positions, rel_bias, ada_w, ada_b, norm_mix, norm_ffn, ffn_w1, ffn_w3, ffn_w2, even_w_in, even_w_out, nsa_gate_b, nsa_cmp_pos_k, nsa_cmp_w1_k, nsa_cmp_w2_k, nsa_cmp_pos_v, nsa_cmp_w1_v, nsa_cmp_w2_v, mla_q_norm, mla_w_uq, mla_kv_norm, mla_w_ukv, fox_w_in, fox_w_out, fox_f_b, fox_q_norm, fox_k_norm, final_norm):
    b, s, d = x.shape
    assert b == 1 and d == D_MODEL and s % 1024 == 0
    xs = x[0]
    pos = positions[0]
    mod = _adaln(c, ada_w, ada_b)
    depth = ada_w.shape[0]
    for i in range(depth):
        sh1, sc1, g1, sh2, sc2, g2 = [mod[i:i + 1, k * d:(k + 1) * d] for k in range(6)]
        hn = _normmod(xs, norm_mix[i][None, :], sc1, sh1)
        if i % 2 == 0:
            e = i // 2
            xs = _nsa_mla_mixer(hn, pos, rel_bias, even_w_in[e], even_w_out[e], nsa_gate_b[e],
                                nsa_cmp_pos_k[e], nsa_cmp_w1_k[e], nsa_cmp_w2_k[e],
                                nsa_cmp_pos_v[e], nsa_cmp_w1_v[e], nsa_cmp_w2_v[e],
                                mla_q_norm[e], mla_w_uq[e], mla_kv_norm[e], mla_w_ukv[e], xs, g1)
        else:
            o = i // 2
            xs = _fox_mixer(hn, fox_w_in[o], fox_w_out[o], fox_f_b[o], fox_q_norm[o], fox_k_norm[o], xs, g1)
        fin = final_norm[None, :] if i == depth - 1 else None
        xs = _ffn(xs, norm_ffn[i][None, :], sc2, sh2, g2, ffn_w1[i].astype(BF), ffn_w3[i].astype(BF),
                  ffn_w2[i].astype(BF), fin)
    return xs[None]
```

```python
import functools
import math

import numpy as np
import jax
import jax.numpy as jnp
from jax import lax
from jax.experimental import pallas as pl
from jax.experimental.pallas import tpu as pltpu

D_MODEL = 2048
DEPTH = 2
EPS = 1e-6
NEG_INF = -1e30

NSA_HEADS = 8
NSA_KV_HEADS = 2
NSA_GROUP = NSA_HEADS // NSA_KV_HEADS
NSA_HEAD_DIM = 128
CMP_LEN = 32
CMP_STRIDE = 16
CMP_HIDDEN = 256
SLC_LEN = 64
SLC_TOPN = 16
WINDOW = 512
Q_BLOCK = 128

MLA_HEADS = 8
MLA_Q_RANK = 512
MLA_KV_RANK = 512
MLA_NOPE = 128
MLA_ROPE = 64
MLA_V = 128
ROPE_THETA = 10000.0

FOX_HEADS = 16
FOX_HEAD_DIM = D_MODEL // FOX_HEADS

REL_BUCKETS = 32
REL_MAX_DIST = 4096

FFN_HIDDEN = ((8 * D_MODEL + 2) // 3 + 255) // 256 * 256

LANES = 128
SUBLANES = 8
VMEM_LIMIT_BYTES = 56 * 1024 * 1024
LOG2E = math.log2(math.e)

BF = jnp.bfloat16
F32 = jnp.float32
_NT = (((1,), (1,)), ((), ()))


def _cparams(*sem):
    return pltpu.CompilerParams(dimension_semantics=sem, vmem_limit_bytes=VMEM_LIMIT_BYTES)


def _rms(x):
    return x * lax.rsqrt(jnp.mean(x * x, axis=-1, keepdims=True) + EPS)


def _adaln_kernel(c_ref, w_ref, b_ref, o_ref):
    c = c_ref[...]
    cond = c * jax.nn.sigmoid(c)
    acc = jnp.dot(cond.astype(BF), w_ref[0].astype(BF), preferred_element_type=F32)
    o_ref[0] = acc + b_ref[0]


def _adaln(c, ada_w, ada_b):
    depth, d, n = ada_w.shape
    tn = 1024
    c8 = jnp.broadcast_to(c, (SUBLANES, d))
    out = pl.pallas_call(
        _adaln_kernel,
        grid=(depth, n // tn),
        in_specs=[pl.BlockSpec((SUBLANES, d), lambda i, j: (0, 0)),
                  pl.BlockSpec((1, d, tn), lambda i, j: (i, 0, j)),
                  pl.BlockSpec((1, 1, tn), lambda i, j: (i, 0, j))],
        out_specs=pl.BlockSpec((1, SUBLANES, tn), lambda i, j: (i, 0, j)),
        out_shape=jax.ShapeDtypeStruct((depth, SUBLANES, n), F32),
        compiler_params=_cparams("arbitrary", "arbitrary"),
        name="adaln",
    )(c8, ada_w, ada_b.reshape(depth, 1, n))
    return out[:, 0, :]


def _normmod_kernel(x_ref, g_ref, sc_ref, sh_ref, o_ref):
    y = _rms(x_ref[...])
    o_ref[...] = ((y * g_ref[...]) * (1.0 + sc_ref[...]) + sh_ref[...]).astype(o_ref.dtype)


def _normmod(x, g, sc, sh):
    s, d = x.shape
    tm = 512
    row = pl.BlockSpec((1, d), lambda i: (0, 0))
    return pl.pallas_call(
        _normmod_kernel,
        grid=(s // tm,),
        in_specs=[pl.BlockSpec((tm, d), lambda i: (i, 0)), row, row, row],
        out_specs=pl.BlockSpec((tm, d), lambda i: (i, 0)),
        out_shape=jax.ShapeDtypeStruct((s, d), BF),
        compiler_params=_cparams("arbitrary"),
        name="normmod",
    )(x, g, sc, sh)


def _proj_kernel(*refs, n_lhs, n_epi, epi, head_major):
    lhs = refs[:n_lhs]
    ws = refs[n_lhs:2 * n_lhs]
    epis = refs[2 * n_lhs:2 * n_lhs + n_epi]
    o_ref = refs[-1]
    acc = None
    for a, w in zip(lhs, ws):
        d = jnp.dot(a[...], w[...], preferred_element_type=F32)
        acc = d if acc is None else acc + d
    res = epi(acc, *[e[...] for e in epis])
    if head_major:
        for r in range(o_ref.shape[0]):
            o_ref[r] = res[:, r * LANES:(r + 1) * LANES].astype(o_ref.dtype)
    else:
        o_ref[...] = res.astype(o_ref.dtype)


def _proj(lhs, ws, epi, epi_in=(), *, out_dtype, head_major=False, tm=1024, tn=512,
          lhs_col_block=None, name="proj"):
    m = lhs[0].shape[0]
    n = ws[0].shape[1]
    tm = min(tm, m)
    tn = min(tn, n)
    if lhs_col_block is None:
        lhs_col_block = [0] * len(lhs)
    in_specs = []
    for a, w, cb in zip(lhs, ws, lhs_col_block):
        in_specs.append(pl.BlockSpec((tm, w.shape[0]), lambda i, j, cb=cb: (i, cb)))
    for w in ws:
        in_specs.append(pl.BlockSpec((w.shape[0], tn), lambda i, j: (0, j)))
    arrays = list(lhs) + list(ws)
    for arr, kind in epi_in:
        if kind == "row":
            in_specs.append(pl.BlockSpec((1, tn), lambda i, j: (0, j)))
        elif kind == "const":
            in_specs.append(pl.BlockSpec(arr.shape, lambda i, j: (0, 0)))
        elif kind == "tile":
            in_specs.append(pl.BlockSpec((tm, tn), lambda i, j: (i, j)))
        elif kind == "rowtile":
            in_specs.append(pl.BlockSpec((tm, arr.shape[1]), lambda i, j: (i, 0)))
        else:
            raise ValueError(kind)
        arrays.append(arr)
    if head_major:
        hpt = tn // LANES
        out_spec = pl.BlockSpec((hpt, tm, LANES), lambda i, j: (j, i, 0))
        out_shape = jax.ShapeDtypeStruct((n // LANES, m, LANES), out_dtype)
    else:
        out_spec = pl.BlockSpec((tm, tn), lambda i, j: (i, j))
        out_shape = jax.ShapeDtypeStruct((m, n), out_dtype)
    kern = functools.partial(_proj_kernel, n_lhs=len(lhs), n_epi=len(epi_in), epi=epi,
                             head_major=head_major)
    return pl.pallas_call(
        kern,
        grid=(m // tm, n // tn),
        in_specs=in_specs,
        out_specs=out_spec,
        out_shape=out_shape,
        compiler_params=_cparams("arbitrary", "arbitrary"),
        name=name,
    )(*arrays)


def _epi_id(acc):
    return acc


def _epi_scale(scale, acc):
    return acc * scale


def _epi_sigmoid_bias(acc, b):
    return jax.nn.sigmoid(acc + b)


def _epi_logsigmoid_bias(acc, b):
    return jax.nn.log_sigmoid(acc + b)


def _epi_sigmoid(acc):
    return jax.nn.sigmoid(acc)


def _epi_rmsnorm(acc, g):
    return _rms(acc) * g


def _epi_headnorm(scale, acc, g):
    outs = []
    for r in range(acc.shape[1] // LANES):
        outs.append(_rms(acc[:, r * LANES:(r + 1) * LANES]) * g * scale)
    return jnp.concatenate(outs, axis=1)


def _epi_rope(scale, acc, c, s1, s2):
    reps = acc.shape[1] // LANES
    half = MLA_ROPE // 2
    if reps > 1:
        c = jnp.concatenate([c] * reps, axis=1)
        s1 = jnp.concatenate([s1] * reps, axis=1)
        s2 = jnp.concatenate([s2] * reps, axis=1)
    n = acc.shape[1]
    out = acc * c + pltpu.roll(acc, n - half, 1) * s1 + pltpu.roll(acc, half, 1) * s2
    return out * scale


def _epi_residual(acc, x, g):
    return x + g * acc


def _ffn_kernel(*refs, final):
    if final:
        x_ref, g_ref, sc_ref, sh_ref, g2_ref, w1_ref, w3_ref, w2_ref, fn_ref, o_ref, hn_ref, acc_ref = refs
    else:
        x_ref, g_ref, sc_ref, sh_ref, g2_ref, w1_ref, w3_ref, w2_ref, o_ref, hn_ref, acc_ref = refs
    f = pl.program_id(1)

    @pl.when(f == 0)
    def _():
        y = _rms(x_ref[...])
        hn_ref[...] = ((y * g_ref[...]) * (1.0 + sc_ref[...]) + sh_ref[...]).astype(BF)
        acc_ref[...] = jnp.zeros_like(acc_ref)

    h = hn_ref[...]
    h1 = jnp.dot(h, w1_ref[...], preferred_element_type=F32)
    h3 = jnp.dot(h, w3_ref[...], preferred_element_type=F32)
    a = (h1 * jax.nn.sigmoid(h1)) * h3
    acc_ref[...] += jnp.dot(a.astype(BF), w2_ref[...], preferred_element_type=F32)

    @pl.when(f == pl.num_programs(1) - 1)
    def _():
        xn = x_ref[...] + g2_ref[...] * acc_ref[...]
        if final:
            xn = _rms(xn) * fn_ref[...]
        o_ref[...] = xn


def _ffn(x, g, sc, sh, g2, w1, w3, w2, final_gain=None):
    s, d = x.shape
    fdim = w1.shape[1]
    tm = min(512, s)
    tf = 512
    row = pl.BlockSpec((1, d), lambda i, f: (0, 0))
    in_specs = [pl.BlockSpec((tm, d), lambda i, f: (i, 0)), row, row, row, row,
                pl.BlockSpec((d, tf), lambda i, f: (0, f)),
                pl.BlockSpec((d, tf), lambda i, f: (0, f)),
                pl.BlockSpec((tf, d), lambda i, f: (f, 0))]
    arrays = [x, g, sc, sh, g2, w1, w3, w2]
    if final_gain is not None:
        in_specs.append(row)
        arrays.append(final_gain)
    return pl.pallas_call(
        functools.partial(_ffn_kernel, final=final_gain is not None),
        grid=(s // tm, fdim // tf),
        in_specs=in_specs,
        out_specs=pl.BlockSpec((tm, d), lambda i, f: (i, 0)),
        out_shape=jax.ShapeDtypeStruct((s, d), F32),
        scratch_shapes=[pltpu.VMEM((tm, d), BF), pltpu.VMEM((tm, d), F32)],
        compiler_params=_cparams("arbitrary", "arbitrary"),
        name="ffn",
    )(*arrays)


def _flash_update(s, vp, m_ref, acc_ref, row_shift=None):
    m_prev = m_ref[...]
    m_tile = jnp.max(s[...], axis=1, keepdims=True)
    if row_shift is not None:
        m_tile = m_tile + row_shift
    m_new = jnp.maximum(m_prev, m_tile)
    alpha = jnp.exp2(m_prev - m_new)
    sub = m_new if row_shift is None else m_new - row_shift
    p = jnp.exp2(s[...] - sub).astype(BF)
    acc_ref[...] = alpha * acc_ref[...] + jnp.dot(p, vp, preferred_element_type=F32)
    m_ref[...] = m_new


def _init_state(m_ref, acc_ref):
    m_ref[...] = jnp.full_like(m_ref, NEG_INF)
    acc_ref[...] = jnp.zeros_like(acc_ref)


def _with_ones(v):
    return jnp.concatenate([v, jnp.ones(v.shape, v.dtype)], axis=1)


def _normalized(acc):
    return acc[:, :LANES] / acc[:, LANES:]


def _causal_mask(t):
    rows = lax.broadcasted_iota(jnp.int32, (t, t), 0)
    cols = lax.broadcasted_iota(jnp.int32, (t, t), 1)
    return cols <= rows


def _causal_sweep(qi, tc, s_bufs, m_ref, acc_ref, kside_fn, logits_fn, vp_fn, row_shifts=None):
    s0, s1 = s_bufs
    causal = _causal_mask(tc)

    def pre(t, s_ref, chains, masked_chain=None):
        kside = kside_fn(pl.multiple_of(t * tc, tc))
        for c in chains:
            s = logits_fn(c, kside)
            if c == masked_chain:
                s = jnp.where(causal, s, NEG_INF)
            s_ref[c] = s

    def process(t, s_ref, chains):
        vp = vp_fn(pl.multiple_of(t * tc, tc))
        for c in chains:
            _flash_update(s_ref.at[c], vp, m_ref.at[c], acc_ref.at[c],
                          row_shift=None if row_shifts is None else row_shifts[c])

    d0 = 2 * qi
    pre(d0, s0, (0, 1), 0)
    pre(d0 + 1, s1, (1,), 1)
    process(d0, s0, (0, 1))
    pre(0, s0, (0, 1))
    process(d0 + 1, s1, (1,))

    def body(u, carry):
        t = 2 * u
        pre(t + 1, s1, (0, 1))
        process(t, s0, (0, 1))
        pre(t + 2, s0, (0, 1))
        process(t + 1, s1, (0, 1))
        return carry

    lax.fori_loop(0, qi, body, 0)


def _sweep_scratch(tc):
    return [pltpu.VMEM((2, tc, tc), F32), pltpu.VMEM((2, tc, tc), F32),
            pltpu.VMEM((2, tc, 1), F32), pltpu.VMEM((2, tc, 2 * LANES), F32)]


def _mla_kernel(qn_ref, qr_ref, kn_ref, kr_ref, v_ref, o_ref, s0_ref, s1_ref, m_ref, acc_ref, *, tc):
    qi = pl.program_id(1)
    qs = [jnp.concatenate([qn_ref[0, c * tc:(c + 1) * tc, :], qr_ref[0, c * tc:(c + 1) * tc, :]], axis=1)
          for c in range(2)]
    _init_state(m_ref, acc_ref)

    def kside(off):
        return jnp.concatenate([kn_ref[0, pl.ds(off, tc), :], kr_ref[pl.ds(off, tc), :]], axis=1)

    def logits(c, k):
        return lax.dot_general(qs[c], k, _NT, preferred_element_type=F32)

    def vp(off):
        return _with_ones(v_ref[0, pl.ds(off, tc), :])

    _causal_sweep(qi, tc, (s0_ref, s1_ref), m_ref, acc_ref, kside, logits, vp)
    for c in range(2):
        o_ref[c * tc:(c + 1) * tc, :] = _normalized(acc_ref[c]).astype(o_ref.dtype)


def _mla_attention(qn, qr, kvh, kr):
    h, s, _ = qn.shape
    tc = min(512, s // 2)
    w = 2 * tc
    return pl.pallas_call(
        functools.partial(_mla_kernel, tc=tc),
        grid=(h, s // w),
        in_specs=[pl.BlockSpec((1, w, LANES), lambda hh, i: (hh, i, 0)),
                  pl.BlockSpec((1, w, LANES), lambda hh, i: (hh, i, 0)),
                  pl.BlockSpec((1, s, LANES), lambda hh, i: (2 * hh, 0, 0)),
                  pl.BlockSpec((s, LANES), lambda hh, i: (0, 0)),
                  pl.BlockSpec((1, s, LANES), lambda hh, i: (2 * hh + 1, 0, 0))],
        out_specs=pl.BlockSpec((w, LANES), lambda hh, i: (i, hh)),
        out_shape=jax.ShapeDtypeStruct((s, h * LANES), BF),
        scratch_shapes=_sweep_scratch(tc),
        compiler_params=_cparams("arbitrary", "arbitrary"),
        name="mla_attn",
    )(qn, qr, kvh, kr, kvh)


def _fox_kernel(q_ref, k_ref, v_ref, ck_ref, cum_ref, og_ref, o_ref, s0_ref, s1_ref, m_ref, acc_ref, *, tc):
    hh = pl.program_id(0)
    qi = pl.program_id(1)
    qs = [q_ref[0, c * tc:(c + 1) * tc, :] for c in range(2)]
    lane = lax.broadcasted_iota(jnp.int32, (tc, LANES), 1)
    cqs = [jnp.sum(jnp.where(lane == hh, cum_ref[c * tc:(c + 1) * tc, :], 0.0), axis=1, keepdims=True)
           for c in range(2)]
    _init_state(m_ref, acc_ref)

    def kside(off):
        return k_ref[0, pl.ds(off, tc), :], ck_ref[0, :, pl.ds(off, tc)]

    def logits(c, kc):
        return lax.dot_general(qs[c], kc[0], _NT, preferred_element_type=F32) - kc[1]

    def vp(off):
        return _with_ones(v_ref[0, pl.ds(off, tc), :])

    _causal_sweep(qi, tc, (s0_ref, s1_ref), m_ref, acc_ref, kside, logits, vp, row_shifts=cqs)
    for c in range(2):
        rows = slice(c * tc, (c + 1) * tc)
        o_ref[rows, :] = (_normalized(acc_ref[c]) * og_ref[rows, :]).astype(o_ref.dtype)


def _fox_attention(q, k, v, cum_t, cum, sig_og):
    h, s, _ = q.shape
    tc = min(512, s // 2)
    w = 2 * tc
    hm = pl.BlockSpec((1, s, LANES), lambda hh, i: (hh, 0, 0))
    return pl.pallas_call(
        functools.partial(_fox_kernel, tc=tc),
        grid=(h, s // w),
        in_specs=[pl.BlockSpec((1, w, LANES), lambda hh, i: (hh, i, 0)), hm, hm,
                  pl.BlockSpec((1, 1, s), lambda hh, i: (hh, 0, 0)),
                  pl.BlockSpec((w, LANES), lambda hh, i: (i, 0)),
                  pl.BlockSpec((w, LANES), lambda hh, i: (i, hh))],
        out_specs=pl.BlockSpec((w, LANES), lambda hh, i: (i, hh)),
        out_shape=jax.ShapeDtypeStruct((s, h * LANES), BF),
        scratch_shapes=_sweep_scratch(tc),
        compiler_params=_cparams("arbitrary", "arbitrary"),
        name="fox_attn",
    )(q, k, v, cum_t, cum, sig_og)


def _cumsum_kernel(x_ref, o_ref, ot_ref, carry_ref, *, t, out_scale):
    @pl.when(pl.program_id(0) == 0)
    def _():
        carry_ref[...] = jnp.zeros_like(carry_ref)

    x = x_ref[...]
    rows = lax.broadcasted_iota(jnp.int32, (t, t), 0)
    cols = lax.broadcasted_iota(jnp.int32, (t, t), 1)
    tri = jnp.where(cols <= rows, 1.0, 0.0).astype(BF)
    hi = x.astype(BF)
    r1 = x - hi.astype(F32)
    mid = r1.astype(BF)
    lo = (r1 - mid.astype(F32)).astype(BF)
    cum = (jnp.dot(tri, hi, preferred_element_type=F32) + jnp.dot(tri, mid, preferred_element_type=F32)
           + jnp.dot(tri, lo, preferred_element_type=F32)) + carry_ref[...]
    scaled = cum * out_scale
    o_ref[...] = scaled
    ot_ref[...] = scaled.T
    carry_ref[...] = cum[t - 1:t, :]


def _cumsum_tokens(x, out_scale):
    s, n = x.shape
    t = min(256, s)
    return pl.pallas_call(
        functools.partial(_cumsum_kernel, t=t, out_scale=out_scale),
        grid=(s // t,),
        in_specs=[pl.BlockSpec((t, n), lambda i: (i, 0))],
        out_specs=[pl.BlockSpec((t, n), lambda i: (i, 0)), pl.BlockSpec((n, t), lambda i: (0, i))],
        out_shape=[jax.ShapeDtypeStruct((s, n), F32), jax.ShapeDtypeStruct((n, s), F32)],
        scratch_shapes=[pltpu.VMEM((1, n), F32)],
        compiler_params=_cparams("arbitrary"),
        name="cumsum",
    )(x)


def _compress_kernel(a_ref, pos_ref, w1_ref, w2_ref, o_ref):
    a = a_ref[0]
    nc = a.shape[0]
    p1 = jnp.dot((a + pos_ref[0, 0:1, :]).astype(BF), w1_ref[0, 0], preferred_element_type=F32)
    p2 = jnp.dot((a + pos_ref[0, 1:2, :]).astype(BF), w1_ref[0, 1], preferred_element_type=F32)
    h = p1 + pltpu.roll(p2, nc - 1, 0)
    act = h * jax.nn.sigmoid(h)
    o_ref[0] = jnp.dot(act.astype(BF), w2_ref[0], preferred_element_type=F32).astype(o_ref.dtype)


def _nsa_compress(kv_cmp, pos, w1, w2):
    c, nc, gw = kv_cmp.shape
    return pl.pallas_call(
        _compress_kernel,
        grid=(c,),
        in_specs=[pl.BlockSpec((1, nc, gw), lambda i: (i, 0, 0)),
                  pl.BlockSpec((1, 2, gw), lambda i: (i // 2, 0, 0)),
                  pl.BlockSpec((1, 2, gw, CMP_HIDDEN), lambda i: (i // 2, 0, 0, 0)),
                  pl.BlockSpec((1, CMP_HIDDEN, NSA_HEAD_DIM), lambda i: (i // 2, 0, 0))],
        out_specs=pl.BlockSpec((1, nc, NSA_HEAD_DIM), lambda i: (i, 0, 0)),
        out_shape=jax.ShapeDtypeStruct((c, nc, NSA_HEAD_DIM), BF),
        compiler_params=_cparams("arbitrary"),
        name="nsa_compress",
    )(kv_cmp, pos, w1, w2)


def _bias_table_kernel(brev_ref, basc_ref, tbl_ref, tb_ref, asc_ref, *, s, mc):
    tbl = tbl_ref[0] * LOG2E

    def lookup(bkt):
        out = jnp.zeros(bkt.shape, F32)
        for b in range(REL_BUCKETS):
            out = jnp.where(bkt == b, tbl[:, b:b + 1], out)
        return out

    rev = lookup(brev_ref[...])
    asc_ref[0] = lookup(basc_ref[...])
    for m in range(mc + 1):
        win = rev[:, s - Q_BLOCK * m:s - Q_BLOCK * m + 2 * Q_BLOCK]
        rolled = pltpu.roll(jnp.broadcast_to(win, (Q_BLOCK, 2 * Q_BLOCK)), Q_BLOCK + 1, 1,
                            stride=1, stride_axis=0)
        tb_ref[0, m] = rolled[:, :Q_BLOCK]


def _t5_bucket(dist):
    max_exact = REL_BUCKETS // 2
    d = jnp.maximum(dist, 0)
    ratio = jnp.log(jnp.maximum(d, max_exact).astype(F32) / max_exact) / math.log(REL_MAX_DIST / max_exact)
    large = jnp.minimum(max_exact + (ratio * (REL_BUCKETS - max_exact)).astype(jnp.int32), REL_BUCKETS - 1)
    return jnp.where(d < max_exact, d, large)


def _bias_tables(rel_bias, s):
    assert Q_BLOCK == SUBLANES * CMP_STRIDE
    h = rel_bias.shape[1]
    qblocks = s // Q_BLOCK
    max_exact = REL_BUCKETS // 2
    d_const = int(math.ceil(max_exact * (REL_MAX_DIST / max_exact)
                            ** ((REL_BUCKETS - max_exact - 1) / (REL_BUCKETS - max_exact)))) + 1
    mc = min(-(-(d_const + Q_BLOCK - 1) // Q_BLOCK), qblocks - 1)
    nrev = s + 2 * Q_BLOCK
    nasc = s + 3 * Q_BLOCK
    brev = _t5_bucket(s + Q_BLOCK - 1 - jnp.arange(nrev, dtype=jnp.int32))[None, :]
    basc = _t5_bucket(jnp.arange(nasc, dtype=jnp.int32) - 2 * Q_BLOCK)[None, :]
    tb, asc = pl.pallas_call(
        functools.partial(_bias_table_kernel, s=s, mc=mc),
        grid=(h,),
        in_specs=[pl.BlockSpec((1, nrev), lambda i: (0, 0)),
                  pl.BlockSpec((1, nasc), lambda i: (0, 0)),
                  pl.BlockSpec((1, 1, REL_BUCKETS), lambda i: (i, 0, 0))],
        out_specs=[pl.BlockSpec((1, mc + 1, Q_BLOCK, Q_BLOCK), lambda i: (i, 0, 0, 0)),
                   pl.BlockSpec((1, 1, nasc), lambda i: (i, 0, 0))],
        out_shape=[jax.ShapeDtypeStruct((h, mc + 1, Q_BLOCK, Q_BLOCK), F32),
                   jax.ShapeDtypeStruct((h, 1, nasc), F32)],
        compiler_params=_cparams("arbitrary"),
        name="bias_tables",
    )(brev, basc, rel_bias.T.reshape(h, 1, REL_BUCKETS))
    asc = asc[:, 0, :]
    base = 2 * Q_BLOCK - (CMP_LEN - 1)
    segs = [asc[:, base - CMP_STRIDE * nn:base - CMP_STRIDE * nn + s].reshape(h, qblocks, Q_BLOCK)
            for nn in range(SUBLANES)]
    trc = jnp.flip(jnp.stack(segs, axis=2), axis=1).reshape(h, qblocks * SUBLANES, Q_BLOCK)
    return tb, jnp.pad(trc, ((0, 0), (0, qblocks * SUBLANES), (0, 0)))


def _nsa_cmp_kernel(q_ref, kc_ref, vct_ref, trc_ref, smt_ref, gate_ref, oc_ref, sel_ref, *, qblocks, topn):
    qb = pl.program_id(1)
    rq = NSA_GROUP * Q_BLOCK
    q = q_ref[...].reshape(rq, NSA_HEAD_DIM)
    kc = kc_ref[0]
    nc = kc.shape[0]
    nslc = smt_ref.shape[0]
    s = lax.dot_general(kc, q, _NT, preferred_element_type=F32)
    n_io = lax.broadcasted_iota(jnp.int32, (nc, Q_BLOCK), 0)
    i_io = lax.broadcasted_iota(jnp.int32, (nc, Q_BLOCK), 1)
    mask = (n_io * CMP_STRIDE + (CMP_LEN - 1)) <= (qb * Q_BLOCK + i_io)
    off = pl.multiple_of((qblocks - 1 - qb) * SUBLANES, SUBLANES)
    ps = []
    for r in range(NSA_GROUP):
        l = jnp.where(mask, s[:, r * Q_BLOCK:(r + 1) * Q_BLOCK] + trc_ref[r, pl.ds(off, nc), :], NEG_INF)
        m = jnp.max(l, axis=0, keepdims=True)
        p = jnp.where(mask, jnp.exp2(l - m), 0.0)
        den = jnp.maximum(jnp.sum(p, axis=0, keepdims=True), 1e-30)
        ps.append(p / den)
    p_all = jnp.concatenate(ps, axis=1).astype(BF)
    oc_t = jnp.dot(vct_ref[0], p_all, preferred_element_type=F32)
    imp4 = jnp.dot(smt_ref[...], p_all, preferred_element_type=F32)
    imp = imp4[:, 0:Q_BLOCK]
    for r in range(1, NSA_GROUP):
        imp = imp + imp4[:, r * Q_BLOCK:(r + 1) * Q_BLOCK]

    j_io = lax.broadcasted_iota(jnp.int32, (nslc, Q_BLOCK), 0)
    t_io = qb * Q_BLOCK + lax.broadcasted_iota(jnp.int32, (nslc, Q_BLOCK), 1)
    cur = t_io >> int(math.log2(SLC_LEN))
    forced = jnp.logical_or(j_io == 0, jnp.logical_or(j_io == cur, j_io == cur - 1))
    val = jnp.where(forced, 1e9, jnp.where(j_io <= cur, imp, -1e9))
    sel = jnp.zeros((nslc, Q_BLOCK), F32)
    for _ in range(topn):
        mx = jnp.max(val, axis=0, keepdims=True)
        cand = jnp.where(val == mx, j_io, nslc)
        jmin = jnp.min(cand, axis=0, keepdims=True)
        pick = j_io == jmin
        sel = jnp.where(pick, 1.0, sel)
        val = jnp.where(pick, -3e38, val)
    sel_ref[0] = sel.T.astype(sel_ref.dtype)

    gates = gate_ref[0]
    for r in range(NSA_GROUP):
        o_r = oc_t[:, r * Q_BLOCK:(r + 1) * Q_BLOCK].T
        oc_ref[:, r * NSA_HEAD_DIM:(r + 1) * NSA_HEAD_DIM] = o_r * gates[:, 3 * r:3 * r + 1]


def _nsa_cmp(q, kc, vct, trc, smt, gates):
    h, s, d = q.shape
    g = NSA_KV_HEADS
    qblocks = s // Q_BLOCK
    nc = kc.shape[1]
    nslc = smt.shape[0]
    topn = min(SLC_TOPN, nslc)
    return pl.pallas_call(
        functools.partial(_nsa_cmp_kernel, qblocks=qblocks, topn=topn),
        grid=(g, qblocks),
        in_specs=[pl.BlockSpec((NSA_GROUP, Q_BLOCK, d), lambda gg, i: (gg, i, 0)),
                  pl.BlockSpec((1, nc, d), lambda gg, i: (gg, 0, 0)),
                  pl.BlockSpec((1, d, nc), lambda gg, i: (gg, 0, 0)),
                  pl.BlockSpec((NSA_GROUP, trc.shape[1], Q_BLOCK), lambda gg, i: (gg, 0, 0)),
                  pl.BlockSpec(smt.shape, lambda gg, i: (0, 0)),
                  pl.BlockSpec((1, Q_BLOCK, LANES), lambda gg, i: (gg, i, 0))],
        out_specs=[pl.BlockSpec((Q_BLOCK, NSA_GROUP * d), lambda gg, i: (i, gg)),
                   pl.BlockSpec((1, Q_BLOCK, nslc), lambda gg, i: (gg, i, 0))],
        out_shape=[jax.ShapeDtypeStruct((s, h * d), F32),
                   jax.ShapeDtypeStruct((g, s, nslc), BF)],
        compiler_params=_cparams("arbitrary", "arbitrary"),
        name="nsa_cmp",
    )(q, kc, vct, trc, smt, gates)


def _nsa_sw_kernel(q_ref, ks_ref, vs_ref, kw_ref, vw_ref, tb_ref, sel_ref, e_ref, gate_ref, oc_ref,
                   o_ref, s0_ref, s1_ref, m_ref, acc_ref, *, tk, mc, nch):
    i = pl.program_id(1)
    rq = NSA_GROUP * Q_BLOCK
    nsub = tk // Q_BLOCK
    qbs = [i * nch + c for c in range(nch)]
    qs = [q_ref[:, c * Q_BLOCK:(c + 1) * Q_BLOCK, :].reshape(rq, NSA_HEAD_DIM) for c in range(nch)]
    sels = [sel_ref[0, c * Q_BLOCK:(c + 1) * Q_BLOCK, :] for c in range(nch)]

    def biased(c, s, msk, kb0, nblk):
        parts = []
        for r in range(NSA_GROUP):
            bias = jnp.concatenate(
                [tb_ref[r, jnp.clip(qbs[c] - (kb0 + b), 0, mc)] for b in range(nblk)], axis=1)
            parts.append(jnp.where(msk, s[r * Q_BLOCK:(r + 1) * Q_BLOCK, :] + bias, NEG_INF))
        return jnp.concatenate(parts, axis=0)

    def gated(c, o, branch, base):
        gates = gate_ref[0, c * Q_BLOCK:(c + 1) * Q_BLOCK, :]
        outs = []
        for r in range(NSA_GROUP):
            o_r = o[r * Q_BLOCK:(r + 1) * Q_BLOCK, :] * gates[:, 3 * r + branch:3 * r + branch + 1]
            outs.append(base[:, r * NSA_HEAD_DIM:(r + 1) * NSA_HEAD_DIM] + o_r)
        return jnp.concatenate(outs, axis=1)

    _init_state(m_ref, acc_ref)
    last_tile = e_ref.shape[0] - 1

    def pre(j, s_ref):
        jl = jnp.minimum(j, last_tile)
        k = ks_ref[0, pl.ds(pl.multiple_of(jl * tk, tk), tk), :]
        e = e_ref[jl]
        kk = j * tk + lax.broadcasted_iota(jnp.int32, (Q_BLOCK, tk), 1)
        for c in range(nch):
            s = lax.dot_general(qs[c], k, _NT, preferred_element_type=F32)
            ii = qbs[c] * Q_BLOCK + lax.broadcasted_iota(jnp.int32, (Q_BLOCK, tk), 0)
            msk = jnp.logical_and(jnp.dot(sels[c], e, preferred_element_type=F32) > 0.5, kk <= ii)
            s_ref[c] = biased(c, s, msk, j * nsub, nsub)

    def process(j, s_ref):
        vp = _with_ones(vs_ref[0, pl.ds(pl.multiple_of(j * tk, tk), tk), :])
        for c in range(nch):
            _flash_update(s_ref.at[c], vp, m_ref.at[c], acc_ref.at[c])

    n_tiles = (i * nch) // nsub + 1

    def body(u, carry):
        t = 2 * u
        pre(t + 1, s1_ref)
        process(t, s0_ref)
        pre(t + 2, s0_ref)
        process(t + 1, s1_ref)
        return carry

    pre(0, s0_ref)
    lax.fori_loop(0, n_tiles // 2, body, 0)

    @pl.when(n_tiles % 2 == 1)
    def _():
        process(n_tiles - 1, s0_ref)

    outs = [gated(c, _normalized(acc_ref[c]), 1, oc_ref[c * Q_BLOCK:(c + 1) * Q_BLOCK, :]) for c in range(nch)]

    wk = WINDOW + Q_BLOCK
    wblk = wk // Q_BLOCK
    for c in range(nch):
        kb0 = jnp.maximum(qbs[c] - WINDOW // Q_BLOCK, 0)
        off = pl.multiple_of(kb0 * Q_BLOCK, Q_BLOCK)
        s = lax.dot_general(qs[c], kw_ref[0, pl.ds(off, wk), :], _NT, preferred_element_type=F32)
        rel = (qbs[c] * Q_BLOCK + lax.broadcasted_iota(jnp.int32, (Q_BLOCK, wk), 0)) - (
            off + lax.broadcasted_iota(jnp.int32, (Q_BLOCK, wk), 1))
        msk = jnp.logical_and(rel >= 0, rel < WINDOW)
        l = biased(c, s, msk, kb0, wblk)
        p = jnp.exp2(l - jnp.max(l, axis=1, keepdims=True)).astype(BF)
        ow = jnp.dot(p, _with_ones(vw_ref[0, pl.ds(off, wk), :]), preferred_element_type=F32)
        o_ref[c * Q_BLOCK:(c + 1) * Q_BLOCK, :] = gated(c, _normalized(ow), 2, outs[c]).astype(o_ref.dtype)


def _nsa_sw(q, kvsw, tb, sel, emat, gates, oc):
    h, s, d = q.shape
    g = NSA_KV_HEADS
    nch = 2
    qrows = nch * Q_BLOCK
    tk = emat.shape[2]
    mc = tb.shape[1] - 1
    nslc = sel.shape[2]
    rq = NSA_GROUP * Q_BLOCK
    assert tk % qrows == 0 and WINDOW + Q_BLOCK <= s

    def kv(slot):
        return pl.BlockSpec((1, s, d), lambda gg, i, slot=slot: (slot + gg, 0, 0))

    return pl.pallas_call(
        functools.partial(_nsa_sw_kernel, tk=tk, mc=mc, nch=nch),
        grid=(g, s // qrows),
        in_specs=[pl.BlockSpec((NSA_GROUP, qrows, d), lambda gg, i: (gg, i, 0)),
                  kv(0), kv(2), kv(4), kv(6),
                  pl.BlockSpec((NSA_GROUP,) + tb.shape[1:], lambda gg, i: (gg, 0, 0, 0)),
                  pl.BlockSpec((1, qrows, nslc), lambda gg, i: (gg, i, 0)),
                  pl.BlockSpec(emat.shape, lambda gg, i: (0, 0, 0)),
                  pl.BlockSpec((1, qrows, LANES), lambda gg, i: (gg, i, 0)),
                  pl.BlockSpec((qrows, NSA_GROUP * d), lambda gg, i: (i, gg))],
        out_specs=pl.BlockSpec((qrows, NSA_GROUP * d), lambda gg, i: (i, gg)),
        out_shape=jax.ShapeDtypeStruct((s, h * d), BF),
        scratch_shapes=[pltpu.VMEM((nch, rq, tk), F32), pltpu.VMEM((nch, rq, tk), F32),
                        pltpu.VMEM((nch, rq, 1), F32), pltpu.VMEM((nch, rq, 2 * LANES), F32)],
        compiler_params=_cparams("arbitrary", "arbitrary"),
        name="nsa_sel_win",
    )(q, kvsw, kvsw, kvsw, kvsw, tb, sel, emat, gates, oc)


def _selection_map_t(nc, nslc):
    n = np.arange(nc)[None, :] * CMP_STRIDE
    j0 = np.arange(nslc)[:, None] * SLC_LEN
    valid = np.arange(nc)[None, :] < nc - 1
    return jnp.asarray(((n < j0 + SLC_LEN) & (n + CMP_LEN > j0) & valid).astype(np.float32), dtype=BF)


def _expand_matrix(s, nslc, tk):
    tok = np.arange(s).reshape(s // tk, 1, tk)
    j = np.arange(nslc).reshape(1, nslc, 1)
    return jnp.asarray((tok // SLC_LEN == j).astype(np.float32), dtype=BF)


def _rope_tables(positions):
    half = MLA_ROPE // 2
    inv = ROPE_THETA ** (-jnp.arange(half, dtype=F32) / half)
    ang = positions.astype(F32)[:, None] * inv
    cos, sin = jnp.cos(ang), jnp.sin(ang)
    z = jnp.zeros_like(cos)
    zpad = jnp.zeros((positions.shape[0], LANES - MLA_ROPE), F32)
    c = jnp.concatenate([cos, cos, zpad], axis=1)
    s1 = jnp.concatenate([-sin, z, zpad], axis=1)
    s2 = jnp.concatenate([z, sin, zpad], axis=1)
    return c, s1, s2


def _pad_cols(w, n):
    return jnp.pad(w, ((0, 0), (0, n - w.shape[1])))


def _nsa_mla_mixer(hn, pos, rel_bias, w_in, w_out, gate_b, pos_k, w1_k, w2_k, pos_v, w1_v, w2_v,
                   q_norm, w_uq, kv_norm, w_ukv, x, g1):
    s = hn.shape[0]
    d = NSA_HEAD_DIM
    nq = NSA_HEADS * d
    nkv = 2 * NSA_KV_HEADS * d
    o0 = 0
    w_q = w_in[:, o0:o0 + nq]; o0 += nq
    w_kvc = w_in[:, o0:o0 + nkv]; o0 += nkv
    w_kvsw = w_in[:, o0:o0 + 2 * nkv]; o0 += 2 * nkv
    w_g = w_in[:, o0:o0 + 3 * NSA_HEADS]; o0 += 3 * NSA_HEADS
    w_lat = w_in[:, o0:o0 + MLA_Q_RANK + MLA_KV_RANK]; o0 += MLA_Q_RANK + MLA_KV_RANK
    w_kr = w_in[:, o0:o0 + MLA_ROPE]

    q_nsa = _proj([hn], [w_q.astype(BF)], functools.partial(_epi_scale, d ** -0.5 * LOG2E),
                  out_dtype=BF, head_major=True, tn=1024, name="proj_q_nsa")
    kv_cmp = _proj([hn], [w_kvc.astype(BF)], _epi_id, out_dtype=F32, head_major=True, name="proj_kv_cmp")
    kv_sw = _proj([hn], [w_kvsw.astype(BF)], _epi_id, out_dtype=BF, head_major=True, tn=1024,
                  name="proj_kv_sw")
    per_g = 3 * NSA_GROUP
    w_gp = jnp.concatenate([_pad_cols(w_g[:, g * per_g:(g + 1) * per_g], LANES) for g in range(NSA_KV_HEADS)], 1)
    b_gp = jnp.concatenate([_pad_cols(gate_b[None, g * per_g:(g + 1) * per_g], LANES)
                            for g in range(NSA_KV_HEADS)], 1)
    gates = _proj([hn], [w_gp.astype(BF)], _epi_sigmoid_bias, [(b_gp, "row")], out_dtype=F32,
                  head_major=True, tn=LANES, name="proj_gates")

    nc = s // CMP_STRIDE
    gw = CMP_STRIDE * d
    pos_kv = jnp.stack([pos_k.reshape(2, gw), pos_v.reshape(2, gw)])
    w1_kv = jnp.stack([w1_k.reshape(2, gw, CMP_HIDDEN), w1_v.reshape(2, gw, CMP_HIDDEN)]).astype(BF)
    w2_kv = jnp.stack([w2_k, w2_v]).astype(BF)
    kvc = _nsa_compress(kv_cmp.reshape(2 * NSA_KV_HEADS, nc, gw), pos_kv, w1_kv, w2_kv)
    kc = kvc[:NSA_KV_HEADS]
    vct = jnp.swapaxes(kvc[NSA_KV_HEADS:], 1, 2)

    nslc = s // SLC_LEN
    tb, trc = _bias_tables(rel_bias, s)
    smt = _selection_map_t(nc, nslc)
    oc, sel = _nsa_cmp(q_nsa, kc, vct, trc, smt, gates)
    tk = min(512, s)
    o_nsa = _nsa_sw(q_nsa, kv_sw, tb, sel, _expand_matrix(s, nslc, tk), gates, oc)

    lat = _proj([hn], [w_lat.astype(BF)], _epi_rmsnorm,
                [(jnp.concatenate([q_norm, kv_norm])[None, :], "row")], out_dtype=BF, tn=MLA_Q_RANK,
                name="proj_mla_latent")
    c, s1, s2 = _rope_tables(pos)
    kr = _proj([hn], [_pad_cols(w_kr, LANES).astype(BF)], functools.partial(_epi_rope, 1.0),
               [(c, "rowtile"), (s1, "rowtile"), (s2, "rowtile")], out_dtype=BF, name="proj_mla_kr")
    scale = (MLA_NOPE + MLA_ROPE) ** -0.5 * LOG2E
    w_uq3 = w_uq.reshape(MLA_Q_RANK, MLA_HEADS, MLA_NOPE + MLA_ROPE)
    w_uqn = w_uq3[:, :, :MLA_NOPE].reshape(MLA_Q_RANK, MLA_HEADS * MLA_NOPE)
    w_uqr = jnp.pad(w_uq3[:, :, MLA_NOPE:], ((0, 0), (0, 0), (0, LANES - MLA_ROPE))).reshape(
        MLA_Q_RANK, MLA_HEADS * LANES)
    qn = _proj([lat], [w_uqn.astype(BF)], functools.partial(_epi_scale, scale), out_dtype=BF,
               head_major=True, tn=1024, lhs_col_block=[0], name="proj_mla_qn")
    qr = _proj([lat], [w_uqr.astype(BF)], functools.partial(_epi_rope, scale),
               [(c, "rowtile"), (s1, "rowtile"), (s2, "rowtile")], out_dtype=BF, head_major=True,
               lhs_col_block=[0], name="proj_mla_qr")
    kvh = _proj([lat], [w_ukv.astype(BF)], _epi_id, out_dtype=BF, head_major=True, tn=1024,
                lhs_col_block=[1], name="proj_mla_kv")
    o_mla = _mla_attention(qn, qr, kvh, kr)

    return _proj([o_nsa, o_mla], [w_out[:nq].astype(BF), w_out[nq:].astype(BF)], _epi_residual,
                 [(x, "tile"), (g1, "row")], out_dtype=F32, tn=1024, name="proj_even_out")


def _fox_mixer(hn, w_in, w_out, f_b, q_norm, k_norm, x, g1):
    d = D_MODEL
    dh = FOX_HEAD_DIM
    w_q, w_k, w_v = w_in[:, 0:d], w_in[:, d:2 * d], w_in[:, 2 * d:3 * d]
    w_f = w_in[:, 3 * d:3 * d + FOX_HEADS]
    w_og = w_in[:, 3 * d + FOX_HEADS:]
    q = _proj([hn], [w_q.astype(BF)], functools.partial(_epi_headnorm, dh ** -0.5 * LOG2E),
              [(q_norm[None, :], "const")], out_dtype=BF, head_major=True, tn=1024, name="proj_fox_q")
    k = _proj([hn], [w_k.astype(BF)], functools.partial(_epi_headnorm, 1.0), [(k_norm[None, :], "const")],
              out_dtype=BF, head_major=True, tn=1024, name="proj_fox_k")
    v = _proj([hn], [w_v.astype(BF)], _epi_id, out_dtype=BF, head_major=True, tn=1024, name="proj_fox_v")
    lf = _proj([hn], [_pad_cols(w_f, LANES).astype(BF)], _epi_logsigmoid_bias,
               [(_pad_cols(f_b[None, :], LANES), "row")], out_dtype=F32, name="proj_fox_f")
    sig_og = _proj([hn], [w_og.astype(BF)], _epi_sigmoid, out_dtype=F32, tn=1024, name="proj_fox_og")
    cum, cum_t = _cumsum_tokens(lf, LOG2E)
    cum_t = cum_t[:FOX_HEADS].reshape(FOX_HEADS, 1, -1)
    o = _fox_attention(q, k, v, cum_t, cum, sig_og)
    return _proj([o], [w_out.astype(BF)], _epi_residual, [(x, "tile"), (g1, "row")], out_dtype=F32, tn=1024,
                 name="proj_fox_out")


def kernel(x, c, positions, rel_bias, ada_w, ada_b, norm_mix, norm_ffn, ffn_w1, ffn_w3, ffn_w2, even_w_in, even_w_out, nsa_gate_b, nsa_cmp_pos_k, nsa_cmp_w1_k, nsa_cmp_w2_k, nsa_cmp_pos_v, nsa_cmp_w1_v, nsa_cmp_w2_v, mla_q_norm, mla_w_uq, mla_kv_norm, mla_w_ukv, fox_w_in, fox_w_out, fox_f_b, fox_q_norm, fox_k_norm, final_norm):
    b, s, d = x.shape
    assert b == 1 and d == D_MODEL and s % 1024 == 0
    xs = x[0]
    pos = positions[0]
    mod = _adaln(c, ada_w, ada_b)
    depth = ada_w.shape[0]
    for i in range(depth):
        sh1, sc1, g1, sh2, sc2, g2 = [mod[i:i + 1, k * d:(k + 1) * d] for k in range(6)]
        hn = _normmod(xs, norm_mix[i][None, :], sc1, sh1)
        if i % 2 == 0:
            e = i // 2
            xs = _nsa_mla_mixer(hn, pos, rel_bias, even_w_in[e], even_w_out[e], nsa_gate_b[e],
                                nsa_cmp_pos_k[e], nsa_cmp_w1_k[e], nsa_cmp_w2_k[e],
                                nsa_cmp_pos_v[e], nsa_cmp_w1_v[e], nsa_cmp_w2_v[e],
                                mla_q_norm[e], mla_w_uq[e], mla_kv_norm[e], mla_w_ukv[e], xs, g1)
        else:
            o = i // 2
            xs = _fox_mixer(hn, fox_w_in[o], fox_w_out[o], fox_f_b[o], fox_q_norm[o], fox_k_norm[o], xs, g1)
        fin = final_norm[None, :] if i == depth - 1 else None
        xs = _ffn(xs, norm_ffn[i][None, :], sc2, sh2, g2, ffn_w1[i].astype(BF), ffn_w3[i].astype(BF),
                  ffn_w2[i].astype(BF), fin)
    return xs[None]
```

```python
import functools
import math

import numpy as np
import jax
import jax.numpy as jnp
from jax import lax
from jax.experimental import pallas as pl
from jax.experimental.pallas import tpu as pltpu

D_MODEL = 2048
DEPTH = 2
EPS = 1e-6
NEG_INF = -1e30

NSA_HEADS = 8
NSA_KV_HEADS = 2
NSA_GROUP = NSA_HEADS // NSA_KV_HEADS
NSA_HEAD_DIM = 128
CMP_LEN = 32
CMP_STRIDE = 16
CMP_HIDDEN = 256
SLC_LEN = 64
SLC_TOPN = 16
WINDOW = 512
Q_BLOCK = 128

MLA_HEADS = 8
MLA_Q_RANK = 512
MLA_KV_RANK = 512
MLA_NOPE = 128
MLA_ROPE = 64
MLA_V = 128
ROPE_THETA = 10000.0

FOX_HEADS = 16
FOX_HEAD_DIM = D_MODEL // FOX_HEADS

REL_BUCKETS = 32
REL_MAX_DIST = 4096

FFN_HIDDEN = ((8 * D_MODEL + 2) // 3 + 255) // 256 * 256

LANES = 128
SUBLANES = 8
VMEM_LIMIT_BYTES = 56 * 1024 * 1024
LOG2E = math.log2(math.e)

BF = jnp.bfloat16
F32 = jnp.float32
_NT = (((1,), (1,)), ((), ()))


def _cparams(*sem):
    return pltpu.CompilerParams(dimension_semantics=sem, vmem_limit_bytes=VMEM_LIMIT_BYTES)


def _rms(x):
    return x * lax.rsqrt(jnp.mean(x * x, axis=-1, keepdims=True) + EPS)


def _adaln_kernel(c_ref, w_ref, b_ref, o_ref):
    c = c_ref[...]
    cond = c * jax.nn.sigmoid(c)
    acc = jnp.dot(cond.astype(BF), w_ref[0].astype(BF), preferred_element_type=F32)
    o_ref[0] = acc + b_ref[0]


def _adaln(c, ada_w, ada_b):
    depth, d, n = ada_w.shape
    tn = 1024
    c8 = jnp.broadcast_to(c, (SUBLANES, d))
    out = pl.pallas_call(
        _adaln_kernel,
        grid=(depth, n // tn),
        in_specs=[pl.BlockSpec((SUBLANES, d), lambda i, j: (0, 0)),
                  pl.BlockSpec((1, d, tn), lambda i, j: (i, 0, j)),
                  pl.BlockSpec((1, 1, tn), lambda i, j: (i, 0, j))],
        out_specs=pl.BlockSpec((1, SUBLANES, tn), lambda i, j: (i, 0, j)),
        out_shape=jax.ShapeDtypeStruct((depth, SUBLANES, n), F32),
        compiler_params=_cparams("arbitrary", "arbitrary"),
        name="adaln",
    )(c8, ada_w, ada_b.reshape(depth, 1, n))
    return out[:, 0, :]


def _normmod_kernel(x_ref, g_ref, sc_ref, sh_ref, o_ref):
    y = _rms(x_ref[...])
    o_ref[...] = ((y * g_ref[...]) * (1.0 + sc_ref[...]) + sh_ref[...]).astype(o_ref.dtype)


def _normmod(x, g, sc, sh):
    s, d = x.shape
    tm = 512
    row = pl.BlockSpec((1, d), lambda i: (0, 0))
    return pl.pallas_call(
        _normmod_kernel,
        grid=(s // tm,),
        in_specs=[pl.BlockSpec((tm, d), lambda i: (i, 0)), row, row, row],
        out_specs=pl.BlockSpec((tm, d), lambda i: (i, 0)),
        out_shape=jax.ShapeDtypeStruct((s, d), BF),
        compiler_params=_cparams("arbitrary"),
        name="normmod",
    )(x, g, sc, sh)


def _proj_kernel(*refs, n_lhs, n_epi, epi, head_major):
    lhs = refs[:n_lhs]
    ws = refs[n_lhs:2 * n_lhs]
    epis = refs[2 * n_lhs:2 * n_lhs + n_epi]
    o_ref = refs[2 * n_lhs + n_epi]
    wbf = refs[2 * n_lhs + n_epi + 1:]

    @pl.when(pl.program_id(1) == 0)
    def _():
        for w, wb in zip(ws, wbf):
            wb[...] = w[...].astype(BF)

    acc = None
    for a, wb in zip(lhs, wbf):
        d = jnp.dot(a[...], wb[...], preferred_element_type=F32)
        acc = d if acc is None else acc + d
    res = epi(acc, *[e[...] for e in epis])
    if head_major:
        for r in range(o_ref.shape[0]):
            o_ref[r] = res[:, r * LANES:(r + 1) * LANES].astype(o_ref.dtype)
    else:
        o_ref[...] = res.astype(o_ref.dtype)


def _proj(lhs, ws, epi, epi_in=(), *, n, out_dtype, head_major=False, tm=1024, tn=512,
          lhs_col_block=None, name="proj"):
    m = lhs[0].shape[0]
    tm = min(tm, m)
    tn = min(tn, n)
    if lhs_col_block is None:
        lhs_col_block = [0] * len(lhs)
    in_specs = []
    for (_, k, _, _), cb in zip(ws, lhs_col_block):
        in_specs.append(pl.BlockSpec((tm, k), lambda j, i, cb=cb: (i, cb)))
    for _, k, rb, col0 in ws:
        assert col0 % tn == 0
        in_specs.append(pl.BlockSpec((k, tn), lambda j, i, rb=rb, cb0=col0 // tn: (rb, cb0 + j)))
    arrays = list(lhs) + [w[0] for w in ws]
    for arr, kind in epi_in:
        if kind == "row":
            in_specs.append(pl.BlockSpec((1, tn), lambda j, i: (0, j)))
        elif kind == "const":
            in_specs.append(pl.BlockSpec(arr.shape, lambda j, i: (0, 0)))
        elif kind == "tile":
            in_specs.append(pl.BlockSpec((tm, tn), lambda j, i: (i, j)))
        elif kind == "rowtile":
            in_specs.append(pl.BlockSpec((tm, arr.shape[1]), lambda j, i: (i, 0)))
        else:
            raise ValueError(kind)
        arrays.append(arr)
    if head_major:
        hpt = tn // LANES
        out_spec = pl.BlockSpec((hpt, tm, LANES), lambda j, i: (j, i, 0))
        out_shape = jax.ShapeDtypeStruct((n // LANES, m, LANES), out_dtype)
    else:
        out_spec = pl.BlockSpec((tm, tn), lambda j, i: (i, j))
        out_shape = jax.ShapeDtypeStruct((m, n), out_dtype)
    kern = functools.partial(_proj_kernel, n_lhs=len(lhs), n_epi=len(epi_in), epi=epi,
                             head_major=head_major)
    return pl.pallas_call(
        kern,
        grid=(n // tn, m // tm),
        in_specs=in_specs,
        out_specs=out_spec,
        out_shape=out_shape,
        scratch_shapes=[pltpu.VMEM((k, tn), BF) for _, k, _, _ in ws],
        compiler_params=_cparams("arbitrary", "arbitrary"),
        name=name,
    )(*arrays)


def _epi_id(acc):
    return acc


def _epi_scale(scale, acc):
    return acc * scale


def _epi_sigmoid_bias(acc, b):
    return jax.nn.sigmoid(acc + b)


def _epi_logsigmoid_bias(acc, b):
    return jax.nn.log_sigmoid(acc + b)


def _epi_sigmoid(acc):
    return jax.nn.sigmoid(acc)


def _epi_rmsnorm(acc, g):
    return _rms(acc) * g


def _epi_headnorm(scale, acc, g):
    outs = []
    for r in range(acc.shape[1] // LANES):
        outs.append(_rms(acc[:, r * LANES:(r + 1) * LANES]) * g * scale)
    return jnp.concatenate(outs, axis=1)


def _epi_rope(scale, acc, c, s1, s2):
    reps = acc.shape[1] // LANES
    half = MLA_ROPE // 2
    if reps > 1:
        c = jnp.concatenate([c] * reps, axis=1)
        s1 = jnp.concatenate([s1] * reps, axis=1)
        s2 = jnp.concatenate([s2] * reps, axis=1)
    n = acc.shape[1]
    out = acc * c + pltpu.roll(acc, n - half, 1) * s1 + pltpu.roll(acc, half, 1) * s2
    return out * scale


def _epi_residual(acc, x, g):
    return x + g * acc


def _ffn_kernel(*refs, final):
    if final:
        x_ref, g_ref, sc_ref, sh_ref, g2_ref, w1_ref, w3_ref, w2_ref, fn_ref, o_ref, hn_ref = refs
    else:
        x_ref, g_ref, sc_ref, sh_ref, g2_ref, w1_ref, w3_ref, w2_ref, o_ref, hn_ref = refs
    f = pl.program_id(1)
    tm = x_ref.shape[0]
    halves = [slice(r * (tm // 2), (r + 1) * (tm // 2)) for r in range(2)]

    @pl.when(f == 0)
    def _():
        for rows in halves:
            x = x_ref[rows, :]
            hn_ref[rows, :] = ((_rms(x) * g_ref[...]) * (1.0 + sc_ref[...]) + sh_ref[...]).astype(BF)
            o_ref[rows, :] = x

    w1 = w1_ref[...].astype(BF)
    w3 = w3_ref[...].astype(BF)
    w2 = w2_ref[...].astype(BF)
    for rows in halves:
        h = hn_ref[rows, :]
        h1 = jnp.dot(h, w1, preferred_element_type=F32)
        h3 = jnp.dot(h, w3, preferred_element_type=F32)
        a = (h1 * jax.nn.sigmoid(h1)) * h3
        o_ref[rows, :] += g2_ref[...] * jnp.dot(a.astype(BF), w2, preferred_element_type=F32)

    if final:
        @pl.when(f == pl.num_programs(1) - 1)
        def _():
            for rows in halves:
                o_ref[rows, :] = _rms(o_ref[rows, :]) * fn_ref[...]


def _ffn(x, g, sc, sh, g2, w1, w3, w2, layer, final_gain=None):
    s, d = x.shape
    fdim = w1.shape[2]
    tm = min(1024, s)
    tf = 256
    row = pl.BlockSpec((1, d), lambda i, f: (0, 0))
    in_specs = [pl.BlockSpec((tm, d), lambda i, f: (i, 0), pipeline_mode=pl.Buffered(1)), row, row, row, row,
                pl.BlockSpec((None, d, tf), lambda i, f: (layer, 0, f)),
                pl.BlockSpec((None, d, tf), lambda i, f: (layer, 0, f)),
                pl.BlockSpec((None, tf, d), lambda i, f: (layer, f, 0))]
    arrays = [x, g, sc, sh, g2, w1, w3, w2]
    if final_gain is not None:
        in_specs.append(row)
        arrays.append(final_gain)
    return pl.pallas_call(
        functools.partial(_ffn_kernel, final=final_gain is not None),
        grid=(s // tm, fdim // tf),
        in_specs=in_specs,
        out_specs=pl.BlockSpec((tm, d), lambda i, f: (i, 0)),
        out_shape=jax.ShapeDtypeStruct((s, d), F32),
        scratch_shapes=[pltpu.VMEM((tm, d), BF)],
        compiler_params=_cparams("arbitrary", "arbitrary"),
        name="ffn",
    )(*arrays)


def _flash_update(s, vp, m_ref, acc_ref, row_shift=None):
    m_prev = m_ref[...]
    m_tile = jnp.max(s[...], axis=1, keepdims=True)
    if row_shift is not None:
        m_tile = m_tile + row_shift
    m_new = jnp.maximum(m_prev, m_tile)
    alpha = jnp.exp2(m_prev - m_new)
    sub = m_new if row_shift is None else m_new - row_shift
    p = jnp.exp2(s[...] - sub).astype(BF)
    acc_ref[...] = alpha * acc_ref[...] + jnp.dot(p, vp, preferred_element_type=F32)
    m_ref[...] = m_new


def _init_state(m_ref, acc_ref):
    m_ref[...] = jnp.full_like(m_ref, NEG_INF)
    acc_ref[...] = jnp.zeros_like(acc_ref)


def _with_ones(v):
    return jnp.concatenate([v, jnp.ones(v.shape, v.dtype)], axis=1)


def _normalized(acc):
    return acc[:, :LANES] / acc[:, LANES:]


def _causal_mask(t):
    rows = lax.broadcasted_iota(jnp.int32, (t, t), 0)
    cols = lax.broadcasted_iota(jnp.int32, (t, t), 1)
    return cols <= rows


def _causal_sweep(qi, tc, s_bufs, m_ref, acc_ref, kside_fn, logits_fn, vp_fn, row_shifts=None):
    s0, s1 = s_bufs
    causal = _causal_mask(tc)

    def pre(t, s_ref, chains, masked_chain=None):
        kside = kside_fn(pl.multiple_of(t * tc, tc))
        for c in chains:
            s = logits_fn(c, kside)
            if c == masked_chain:
                s = jnp.where(causal, s, NEG_INF)
            s_ref[c] = s

    def process(t, s_ref, chains):
        vp = vp_fn(pl.multiple_of(t * tc, tc))
        for c in chains:
            _flash_update(s_ref.at[c], vp, m_ref.at[c], acc_ref.at[c],
                          row_shift=None if row_shifts is None else row_shifts[c])

    d0 = 2 * qi
    pre(d0, s0, (0, 1), 0)
    pre(d0 + 1, s1, (1,), 1)
    process(d0, s0, (0, 1))
    pre(0, s0, (0, 1))
    process(d0 + 1, s1, (1,))

    def body(u, carry):
        t = 2 * u
        pre(t + 1, s1, (0, 1))
        process(t, s0, (0, 1))
        pre(t + 2, s0, (0, 1))
        process(t + 1, s1, (0, 1))
        return carry

    lax.fori_loop(0, qi, body, 0)


def _sweep_scratch(tc):
    return [pltpu.VMEM((2, tc, tc), F32), pltpu.VMEM((2, tc, tc), F32),
            pltpu.VMEM((2, tc, 1), F32), pltpu.VMEM((2, tc, 2 * LANES), F32)]


def _mla_kernel(qn_ref, qr_ref, kn_ref, kr_ref, v_ref, o_ref, s0_ref, s1_ref, m_ref, acc_ref, *, tc):
    qi = pl.program_id(1)
    qs = [jnp.concatenate([qn_ref[0, c * tc:(c + 1) * tc, :], qr_ref[0, c * tc:(c + 1) * tc, :]], axis=1)
          for c in range(2)]
    _init_state(m_ref, acc_ref)

    def kside(off):
        return jnp.concatenate([kn_ref[0, pl.ds(off, tc), :], kr_ref[pl.ds(off, tc), :]], axis=1)

    def logits(c, k):
        return lax.dot_general(qs[c], k, _NT, preferred_element_type=F32)

    def vp(off):
        return _with_ones(v_ref[0, pl.ds(off, tc), :])

    _causal_sweep(qi, tc, (s0_ref, s1_ref), m_ref, acc_ref, kside, logits, vp)
    for c in range(2):
        o_ref[c * tc:(c + 1) * tc, :] = _normalized(acc_ref[c]).astype(o_ref.dtype)


def _mla_attention(qn, qr, kvh, kr):
    h, s, _ = qn.shape
    tc = min(512, s // 2)
    w = 2 * tc
    return pl.pallas_call(
        functools.partial(_mla_kernel, tc=tc),
        grid=(h, s // w),
        in_specs=[pl.BlockSpec((1, w, LANES), lambda hh, i: (hh, i, 0)),
                  pl.BlockSpec((1, w, LANES), lambda hh, i: (hh, i, 0)),
                  pl.BlockSpec((1, s, LANES), lambda hh, i: (2 * hh, 0, 0)),
                  pl.BlockSpec((s, LANES), lambda hh, i: (0, 0)),
                  pl.BlockSpec((1, s, LANES), lambda hh, i: (2 * hh + 1, 0, 0))],
        out_specs=pl.BlockSpec((w, LANES), lambda hh, i: (i, hh)),
        out_shape=jax.ShapeDtypeStruct((s, h * LANES), BF),
        scratch_shapes=_sweep_scratch(tc),
        compiler_params=_cparams("arbitrary", "arbitrary"),
        name="mla_attn",
    )(qn, qr, kvh, kr, kvh)


def _fox_kernel(q_ref, k_ref, v_ref, ck_ref, cum_ref, og_ref, o_ref, s0_ref, s1_ref, m_ref, acc_ref, *, tc):
    hh = pl.program_id(0)
    qi = pl.program_id(1)
    qs = [q_ref[0, c * tc:(c + 1) * tc, :] for c in range(2)]
    lane = lax.broadcasted_iota(jnp.int32, (tc, LANES), 1)
    cqs = [jnp.sum(jnp.where(lane == hh, cum_ref[c * tc:(c + 1) * tc, :], 0.0), axis=1, keepdims=True)
           for c in range(2)]
    _init_state(m_ref, acc_ref)

    def kside(off):
        return k_ref[0, pl.ds(off, tc), :], ck_ref[0, :, pl.ds(off, tc)]

    def logits(c, kc):
        return lax.dot_general(qs[c], kc[0], _NT, preferred_element_type=F32) - kc[1]

    def vp(off):
        return _with_ones(v_ref[0, pl.ds(off, tc), :])

    _causal_sweep(qi, tc, (s0_ref, s1_ref), m_ref, acc_ref, kside, logits, vp, row_shifts=cqs)
    for c in range(2):
        rows = slice(c * tc, (c + 1) * tc)
        o_ref[rows, :] = (_normalized(acc_ref[c]) * og_ref[rows, :]).astype(o_ref.dtype)


def _fox_attention(q, k, v, cum_t, cum, sig_og):
    h, s, _ = q.shape
    tc = min(512, s // 2)
    w = 2 * tc
    hm = pl.BlockSpec((1, s, LANES), lambda hh, i: (hh, 0, 0))
    return pl.pallas_call(
        functools.partial(_fox_kernel, tc=tc),
        grid=(h, s // w),
        in_specs=[pl.BlockSpec((1, w, LANES), lambda hh, i: (hh, i, 0)), hm, hm,
                  pl.BlockSpec((1, 1, s), lambda hh, i: (hh, 0, 0)),
                  pl.BlockSpec((w, LANES), lambda hh, i: (i, 0)),
                  pl.BlockSpec((w, LANES), lambda hh, i: (i, hh))],
        out_specs=pl.BlockSpec((w, LANES), lambda hh, i: (i, hh)),
        out_shape=jax.ShapeDtypeStruct((s, h * LANES), BF),
        scratch_shapes=_sweep_scratch(tc),
        compiler_params=_cparams("arbitrary", "arbitrary"),
        name="fox_attn",
    )(q, k, v, cum_t, cum, sig_og)


def _cumsum_kernel(x_ref, o_ref, ot_ref, carry_ref, *, t, out_scale):
    @pl.when(pl.program_id(0) == 0)
    def _():
        carry_ref[...] = jnp.zeros_like(carry_ref)

    x = x_ref[...]
    rows = lax.broadcasted_iota(jnp.int32, (t, t), 0)
    cols = lax.broadcasted_iota(jnp.int32, (t, t), 1)
    tri = jnp.where(cols <= rows, 1.0, 0.0).astype(BF)
    hi = x.astype(BF)
    r1 = x - hi.astype(F32)
    mid = r1.astype(BF)
    lo = (r1 - mid.astype(F32)).astype(BF)
    cum = (jnp.dot(tri, hi, preferred_element_type=F32) + jnp.dot(tri, mid, preferred_element_type=F32)
           + jnp.dot(tri, lo, preferred_element_type=F32)) + carry_ref[...]
    scaled = cum * out_scale
    o_ref[...] = scaled
    ot_ref[...] = scaled.T
    carry_ref[...] = cum[t - 1:t, :]


def _cumsum_tokens(x, out_scale):
    s, n = x.shape
    t = min(256, s)
    return pl.pallas_call(
        functools.partial(_cumsum_kernel, t=t, out_scale=out_scale),
        grid=(s // t,),
        in_specs=[pl.BlockSpec((t, n), lambda i: (i, 0))],
        out_specs=[pl.BlockSpec((t, n), lambda i: (i, 0)), pl.BlockSpec((n, t), lambda i: (0, i))],
        out_shape=[jax.ShapeDtypeStruct((s, n), F32), jax.ShapeDtypeStruct((n, s), F32)],
        scratch_shapes=[pltpu.VMEM((1, n), F32)],
        compiler_params=_cparams("arbitrary"),
        name="cumsum",
    )(x)


def _compress_kernel(a_ref, pos_ref, w1_ref, w2_ref, o_ref):
    a = a_ref[0]
    nc = a.shape[0]
    p1 = jnp.dot((a + pos_ref[0, 0:1, :]).astype(BF), w1_ref[0, 0], preferred_element_type=F32)
    p2 = jnp.dot((a + pos_ref[0, 1:2, :]).astype(BF), w1_ref[0, 1], preferred_element_type=F32)
    h = p1 + pltpu.roll(p2, nc - 1, 0)
    act = h * jax.nn.sigmoid(h)
    o_ref[0] = jnp.dot(act.astype(BF), w2_ref[0], preferred_element_type=F32).astype(o_ref.dtype)


def _nsa_compress(kv_cmp, pos, w1, w2):
    c, nc, gw = kv_cmp.shape
    return pl.pallas_call(
        _compress_kernel,
        grid=(c,),
        in_specs=[pl.BlockSpec((1, nc, gw), lambda i: (i, 0, 0)),
                  pl.BlockSpec((1, 2, gw), lambda i: (i // 2, 0, 0)),
                  pl.BlockSpec((1, 2, gw, CMP_HIDDEN), lambda i: (i // 2, 0, 0, 0)),
                  pl.BlockSpec((1, CMP_HIDDEN, NSA_HEAD_DIM), lambda i: (i // 2, 0, 0))],
        out_specs=pl.BlockSpec((1, nc, NSA_HEAD_DIM), lambda i: (i, 0, 0)),
        out_shape=jax.ShapeDtypeStruct((c, nc, NSA_HEAD_DIM), BF),
        compiler_params=_cparams("arbitrary"),
        name="nsa_compress",
    )(kv_cmp, pos, w1, w2)


def _bias_table_kernel(brev_ref, basc_ref, tbl_ref, tb_ref, asc_ref, *, s, mc):
    tbl = tbl_ref[0] * LOG2E

    def lookup(bkt):
        out = jnp.zeros(bkt.shape, F32)
        for b in range(REL_BUCKETS):
            out = jnp.where(bkt == b, tbl[:, b:b + 1], out)
        return out

    rev = lookup(brev_ref[...])
    asc_ref[0] = lookup(basc_ref[...])
    for m in range(mc + 1):
        win = rev[:, s - Q_BLOCK * m:s - Q_BLOCK * m + 2 * Q_BLOCK]
        rolled = pltpu.roll(jnp.broadcast_to(win, (Q_BLOCK, 2 * Q_BLOCK)), Q_BLOCK + 1, 1,
                            stride=1, stride_axis=0)
        tb_ref[0, m] = rolled[:, :Q_BLOCK]


def _t5_bucket(dist):
    max_exact = REL_BUCKETS // 2
    d = jnp.maximum(dist, 0)
    ratio = jnp.log(jnp.maximum(d, max_exact).astype(F32) / max_exact) / math.log(REL_MAX_DIST / max_exact)
    large = jnp.minimum(max_exact + (ratio * (REL_BUCKETS - max_exact)).astype(jnp.int32), REL_BUCKETS - 1)
    return jnp.where(d < max_exact, d, large)


def _bias_tables(rel_bias, s):
    assert Q_BLOCK == SUBLANES * CMP_STRIDE
    h = rel_bias.shape[1]
    qblocks = s // Q_BLOCK
    max_exact = REL_BUCKETS // 2
    d_const = int(math.ceil(max_exact * (REL_MAX_DIST / max_exact)
                            ** ((REL_BUCKETS - max_exact - 1) / (REL_BUCKETS - max_exact)))) + 1
    mc = min(-(-(d_const + Q_BLOCK - 1) // Q_BLOCK), qblocks - 1)
    nrev = s + 2 * Q_BLOCK
    nasc = s + 3 * Q_BLOCK
    brev = _t5_bucket(s + Q_BLOCK - 1 - jnp.arange(nrev, dtype=jnp.int32))[None, :]
    basc = _t5_bucket(jnp.arange(nasc, dtype=jnp.int32) - 2 * Q_BLOCK)[None, :]
    tb, asc = pl.pallas_call(
        functools.partial(_bias_table_kernel, s=s, mc=mc),
        grid=(h,),
        in_specs=[pl.BlockSpec((1, nrev), lambda i: (0, 0)),
                  pl.BlockSpec((1, nasc), lambda i: (0, 0)),
                  pl.BlockSpec((1, 1, REL_BUCKETS), lambda i: (i, 0, 0))],
        out_specs=[pl.BlockSpec((1, mc + 1, Q_BLOCK, Q_BLOCK), lambda i: (i, 0, 0, 0)),
                   pl.BlockSpec((1, 1, nasc), lambda i: (i, 0, 0))],
        out_shape=[jax.ShapeDtypeStruct((h, mc + 1, Q_BLOCK, Q_BLOCK), F32),
                   jax.ShapeDtypeStruct((h, 1, nasc), F32)],
        compiler_params=_cparams("arbitrary"),
        name="bias_tables",
    )(brev, basc, rel_bias.T.reshape(h, 1, REL_BUCKETS))
    asc = asc[:, 0, :]
    base = 2 * Q_BLOCK - (CMP_LEN - 1)
    segs = [asc[:, base - CMP_STRIDE * nn:base - CMP_STRIDE * nn + s].reshape(h, qblocks, Q_BLOCK)
            for nn in range(SUBLANES)]
    trc = jnp.flip(jnp.stack(segs, axis=2), axis=1).reshape(h, qblocks * SUBLANES, Q_BLOCK)
    return tb, jnp.pad(trc, ((0, 0), (0, qblocks * SUBLANES), (0, 0)))


def _nsa_cmp_kernel(q_ref, kc_ref, vct_ref, trc_ref, smt_ref, gate_ref, oc_ref, sel_ref, *, qblocks, topn):
    qb = pl.program_id(1)
    rq = NSA_GROUP * Q_BLOCK
    q = q_ref[...].reshape(rq, NSA_HEAD_DIM)
    kc = kc_ref[0]
    nc = kc.shape[0]
    nslc = smt_ref.shape[0]
    s = lax.dot_general(kc, q, _NT, preferred_element_type=F32)
    n_io = lax.broadcasted_iota(jnp.int32, (nc, Q_BLOCK), 0)
    i_io = lax.broadcasted_iota(jnp.int32, (nc, Q_BLOCK), 1)
    mask = (n_io * CMP_STRIDE + (CMP_LEN - 1)) <= (qb * Q_BLOCK + i_io)
    off = pl.multiple_of((qblocks - 1 - qb) * SUBLANES, SUBLANES)
    ps = []
    for r in range(NSA_GROUP):
        l = jnp.where(mask, s[:, r * Q_BLOCK:(r + 1) * Q_BLOCK] + trc_ref[r, pl.ds(off, nc), :], NEG_INF)
        m = jnp.max(l, axis=0, keepdims=True)
        p = jnp.where(mask, jnp.exp2(l - m), 0.0)
        den = jnp.maximum(jnp.sum(p, axis=0, keepdims=True), 1e-30)
        ps.append(p / den)
    p_all = jnp.concatenate(ps, axis=1).astype(BF)
    oc_t = jnp.dot(vct_ref[0], p_all, preferred_element_type=F32)
    imp4 = jnp.dot(smt_ref[...], p_all, preferred_element_type=F32)
    imp = imp4[:, 0:Q_BLOCK]
    for r in range(1, NSA_GROUP):
        imp = imp + imp4[:, r * Q_BLOCK:(r + 1) * Q_BLOCK]

    j_io = lax.broadcasted_iota(jnp.int32, (nslc, Q_BLOCK), 0)
    t_io = qb * Q_BLOCK + lax.broadcasted_iota(jnp.int32, (nslc, Q_BLOCK), 1)
    cur = t_io >> int(math.log2(SLC_LEN))
    forced = jnp.logical_or(j_io == 0, jnp.logical_or(j_io == cur, j_io == cur - 1))
    val = jnp.where(forced, 1e9, jnp.where(j_io <= cur, imp, -1e9))
    sel = jnp.zeros((nslc, Q_BLOCK), F32)
    for _ in range(topn):
        mx = jnp.max(val, axis=0, keepdims=True)
        cand = jnp.where(val == mx, j_io, nslc)
        jmin = jnp.min(cand, axis=0, keepdims=True)
        pick = j_io == jmin
        sel = jnp.where(pick, 1.0, sel)
        val = jnp.where(pick, -3e38, val)
    sel_ref[0] = sel.T.astype(sel_ref.dtype)

    gates = gate_ref[0]
    for r in range(NSA_GROUP):
        o_r = oc_t[:, r * Q_BLOCK:(r + 1) * Q_BLOCK].T
        oc_ref[:, r * NSA_HEAD_DIM:(r + 1) * NSA_HEAD_DIM] = o_r * gates[:, 3 * r:3 * r + 1]


def _nsa_cmp(q, kc, vct, trc, smt, gates):
    h, s, d = q.shape
    g = NSA_KV_HEADS
    qblocks = s // Q_BLOCK
    nc = kc.shape[1]
    nslc = smt.shape[0]
    topn = min(SLC_TOPN, nslc)
    return pl.pallas_call(
        functools.partial(_nsa_cmp_kernel, qblocks=qblocks, topn=topn),
        grid=(g, qblocks),
        in_specs=[pl.BlockSpec((NSA_GROUP, Q_BLOCK, d), lambda gg, i: (gg, i, 0)),
                  pl.BlockSpec((1, nc, d), lambda gg, i: (gg, 0, 0)),
                  pl.BlockSpec((1, d, nc), lambda gg, i: (gg, 0, 0)),
                  pl.BlockSpec((NSA_GROUP, trc.shape[1], Q_BLOCK), lambda gg, i: (gg, 0, 0)),
                  pl.BlockSpec(smt.shape, lambda gg, i: (0, 0)),
                  pl.BlockSpec((1, Q_BLOCK, LANES), lambda gg, i: (gg, i, 0))],
        out_specs=[pl.BlockSpec((Q_BLOCK, NSA_GROUP * d), lambda gg, i: (i, gg)),
                   pl.BlockSpec((1, Q_BLOCK, nslc), lambda gg, i: (gg, i, 0))],
        out_shape=[jax.ShapeDtypeStruct((s, h * d), F32),
                   jax.ShapeDtypeStruct((g, s, nslc), BF)],
        compiler_params=_cparams("arbitrary", "arbitrary"),
        name="nsa_cmp",
    )(q, kc, vct, trc, smt, gates)


def _nsa_sw_kernel(q_ref, ks_ref, vs_ref, kw_ref, vw_ref, tb_ref, sel_ref, e_ref, gate_ref, oc_ref,
                   o_ref, s0_ref, s1_ref, m_ref, acc_ref, *, tk, mc, nch):
    i = pl.program_id(1)
    rq = NSA_GROUP * Q_BLOCK
    nsub = tk // Q_BLOCK
    qbs = [i * nch + c for c in range(nch)]
    qs = [q_ref[:, c * Q_BLOCK:(c + 1) * Q_BLOCK, :].reshape(rq, NSA_HEAD_DIM) for c in range(nch)]
    sels = [sel_ref[0, c * Q_BLOCK:(c + 1) * Q_BLOCK, :] for c in range(nch)]

    def biased(c, s, msk, kb0, nblk):
        parts = []
        for r in range(NSA_GROUP):
            bias = jnp.concatenate(
                [tb_ref[r, jnp.clip(qbs[c] - (kb0 + b), 0, mc)] for b in range(nblk)], axis=1)
            parts.append(jnp.where(msk, s[r * Q_BLOCK:(r + 1) * Q_BLOCK, :] + bias, NEG_INF))
        return jnp.concatenate(parts, axis=0)

    def gated(c, o, branch, base):
        gates = gate_ref[0, c * Q_BLOCK:(c + 1) * Q_BLOCK, :]
        outs = []
        for r in range(NSA_GROUP):
            o_r = o[r * Q_BLOCK:(r + 1) * Q_BLOCK, :] * gates[:, 3 * r + branch:3 * r + branch + 1]
            outs.append(base[:, r * NSA_HEAD_DIM:(r + 1) * NSA_HEAD_DIM] + o_r)
        return jnp.concatenate(outs, axis=1)

    _init_state(m_ref, acc_ref)
    last_tile = e_ref.shape[0] - 1

    def pre(j, s_ref):
        jl = jnp.minimum(j, last_tile)
        k = ks_ref[0, pl.ds(pl.multiple_of(jl * tk, tk), tk), :]
        e = e_ref[jl]
        kk = j * tk + lax.broadcasted_iota(jnp.int32, (Q_BLOCK, tk), 1)
        for c in range(nch):
            s = lax.dot_general(qs[c], k, _NT, preferred_element_type=F32)
            ii = qbs[c] * Q_BLOCK + lax.broadcasted_iota(jnp.int32, (Q_BLOCK, tk), 0)
            msk = jnp.logical_and(jnp.dot(sels[c], e, preferred_element_type=F32) > 0.5, kk <= ii)
            s_ref[c] = biased(c, s, msk, j * nsub, nsub)

    def process(j, s_ref):
        vp = _with_ones(vs_ref[0, pl.ds(pl.multiple_of(j * tk, tk), tk), :])
        for c in range(nch):
            _flash_update(s_ref.at[c], vp, m_ref.at[c], acc_ref.at[c])

    n_tiles = (i * nch) // nsub + 1

    def body(u, carry):
        t = 2 * u
        pre(t + 1, s1_ref)
        process(t, s0_ref)
        pre(t + 2, s0_ref)
        process(t + 1, s1_ref)
        return carry

    pre(0, s0_ref)
    lax.fori_loop(0, n_tiles // 2, body, 0)

    @pl.when(n_tiles % 2 == 1)
    def _():
        process(n_tiles - 1, s0_ref)

    outs = [gated(c, _normalized(acc_ref[c]), 1, oc_ref[c * Q_BLOCK:(c + 1) * Q_BLOCK, :]) for c in range(nch)]

    wk = WINDOW + Q_BLOCK
    wblk = wk // Q_BLOCK
    for c in range(nch):
        kb0 = jnp.maximum(qbs[c] - WINDOW // Q_BLOCK, 0)
        off = pl.multiple_of(kb0 * Q_BLOCK, Q_BLOCK)
        s = lax.dot_general(qs[c], kw_ref[0, pl.ds(off, wk), :], _NT, preferred_element_type=F32)
        rel = (qbs[c] * Q_BLOCK + lax.broadcasted_iota(jnp.int32, (Q_BLOCK, wk), 0)) - (
            off + lax.broadcasted_iota(jnp.int32, (Q_BLOCK, wk), 1))
        msk = jnp.logical_and(rel >= 0, rel < WINDOW)
        l = biased(c, s, msk, kb0, wblk)
        p = jnp.exp2(l - jnp.max(l, axis=1, keepdims=True)).astype(BF)
        ow = jnp.dot(p, _with_ones(vw_ref[0, pl.ds(off, wk), :]), preferred_element_type=F32)
        o_ref[c * Q_BLOCK:(c + 1) * Q_BLOCK, :] = gated(c, _normalized(ow), 2, outs[c]).astype(o_ref.dtype)


def _nsa_sw(q, kvsw, tb, sel, emat, gates, oc):
    h, s, d = q.shape
    g = NSA_KV_HEADS
    nch = 2
    qrows = nch * Q_BLOCK
    tk = emat.shape[2]
    mc = tb.shape[1] - 1
    nslc = sel.shape[2]
    rq = NSA_GROUP * Q_BLOCK
    assert tk % qrows == 0 and WINDOW + Q_BLOCK <= s

    def kv(slot):
        return pl.BlockSpec((1, s, d), lambda gg, i, slot=slot: (slot + gg, 0, 0))

    return pl.pallas_call(
        functools.partial(_nsa_sw_kernel, tk=tk, mc=mc, nch=nch),
        grid=(g, s // qrows),
        in_specs=[pl.BlockSpec((NSA_GROUP, qrows, d), lambda gg, i: (gg, i, 0)),
                  kv(0), kv(2), kv(4), kv(6),
                  pl.BlockSpec((NSA_GROUP,) + tb.shape[1:], lambda gg, i: (gg, 0, 0, 0)),
                  pl.BlockSpec((1, qrows, nslc), lambda gg, i: (gg, i, 0)),
                  pl.BlockSpec(emat.shape, lambda gg, i: (0, 0, 0)),
                  pl.BlockSpec((1, qrows, LANES), lambda gg, i: (gg, i, 0)),
                  pl.BlockSpec((qrows, NSA_GROUP * d), lambda gg, i: (i, gg))],
        out_specs=pl.BlockSpec((qrows, NSA_GROUP * d), lambda gg, i: (i, gg)),
        out_shape=jax.ShapeDtypeStruct((s, h * d), BF),
        scratch_shapes=[pltpu.VMEM((nch, rq, tk), F32), pltpu.VMEM((nch, rq, tk), F32),
                        pltpu.VMEM((nch, rq, 1), F32), pltpu.VMEM((nch, rq, 2 * LANES), F32)],
        compiler_params=_cparams("arbitrary", "arbitrary"),
        name="nsa_sel_win",
    )(q, kvsw, kvsw, kvsw, kvsw, tb, sel, emat, gates, oc)


def _selection_map_t(nc, nslc):
    n = np.arange(nc)[None, :] * CMP_STRIDE
    j0 = np.arange(nslc)[:, None] * SLC_LEN
    valid = np.arange(nc)[None, :] < nc - 1
    return jnp.asarray(((n < j0 + SLC_LEN) & (n + CMP_LEN > j0) & valid).astype(np.float32), dtype=BF)


def _expand_matrix(s, nslc, tk):
    tok = np.arange(s).reshape(s // tk, 1, tk)
    j = np.arange(nslc).reshape(1, nslc, 1)
    return jnp.asarray((tok // SLC_LEN == j).astype(np.float32), dtype=BF)


def _rope_tables(positions):
    half = MLA_ROPE // 2
    inv = ROPE_THETA ** (-jnp.arange(half, dtype=F32) / half)
    ang = positions.astype(F32)[:, None] * inv
    cos, sin = jnp.cos(ang), jnp.sin(ang)
    z = jnp.zeros_like(cos)
    zpad = jnp.zeros((positions.shape[0], LANES - MLA_ROPE), F32)
    c = jnp.concatenate([cos, cos, zpad], axis=1)
    s1 = jnp.concatenate([-sin, z, zpad], axis=1)
    s2 = jnp.concatenate([z, sin, zpad], axis=1)
    return c, s1, s2


def _pad_cols(w, n):
    return jnp.pad(w, ((0, 0), (0, n - w.shape[1])))


def _nsa_mla_mixer(hn, pos, rel_bias, w_in, w_out, gate_b, pos_k, w1_k, w2_k, pos_v, w1_v, w2_v,
                   q_norm, w_uq, kv_norm, w_ukv, x, g1):
    s = hn.shape[0]
    d = NSA_HEAD_DIM
    nq = NSA_HEADS * d
    nkv = 2 * NSA_KV_HEADS * d
    o0 = 0
    w_q = w_in[:, o0:o0 + nq]; o0 += nq
    w_kvc = w_in[:, o0:o0 + nkv]; o0 += nkv
    w_kvsw = w_in[:, o0:o0 + 2 * nkv]; o0 += 2 * nkv
    w_g = w_in[:, o0:o0 + 3 * NSA_HEADS]; o0 += 3 * NSA_HEADS
    w_lat = w_in[:, o0:o0 + MLA_Q_RANK + MLA_KV_RANK]; o0 += MLA_Q_RANK + MLA_KV_RANK
    w_kr = w_in[:, o0:o0 + MLA_ROPE]

    dm = w_in.shape[0]
    q_nsa = _proj([hn], [(w_in, dm, 0, 0)], functools.partial(_epi_scale, d ** -0.5 * LOG2E), n=nq,
                  out_dtype=BF, head_major=True, tn=1024, name="proj_q_nsa")
    kv_cmp = _proj([hn], [(w_in, dm, 0, nq)], _epi_id, n=nkv, out_dtype=F32, head_major=True,
                   name="proj_kv_cmp")
    kv_sw = _proj([hn], [(w_in, dm, 0, nq + nkv)], _epi_id, n=2 * nkv, out_dtype=BF, head_major=True,
                  name="proj_kv_sw")
    per_g = 3 * NSA_GROUP
    w_gp = jnp.concatenate([_pad_cols(w_g[:, g * per_g:(g + 1) * per_g], LANES) for g in range(NSA_KV_HEADS)], 1)
    b_gp = jnp.concatenate([_pad_cols(gate_b[None, g * per_g:(g + 1) * per_g], LANES)
                            for g in range(NSA_KV_HEADS)], 1)
    gates = _proj([hn], [(w_gp, dm, 0, 0)], _epi_sigmoid_bias, [(b_gp, "row")], n=NSA_KV_HEADS * LANES,
                  out_dtype=F32, head_major=True, tn=LANES, name="proj_gates")

    nc = s // CMP_STRIDE
    gw = CMP_STRIDE * d
    pos_kv = jnp.stack([pos_k.reshape(2, gw), pos_v.reshape(2, gw)])
    w1_kv = jnp.stack([w1_k.reshape(2, gw, CMP_HIDDEN), w1_v.reshape(2, gw, CMP_HIDDEN)]).astype(BF)
    w2_kv = jnp.stack([w2_k, w2_v]).astype(BF)
    kvc = _nsa_compress(kv_cmp.reshape(2 * NSA_KV_HEADS, nc, gw), pos_kv, w1_kv, w2_kv)
    kc = kvc[:NSA_KV_HEADS]
    vct = jnp.swapaxes(kvc[NSA_KV_HEADS:], 1, 2)

    nslc = s // SLC_LEN
    tb, trc = _bias_tables(rel_bias, s)
    smt = _selection_map_t(nc, nslc)
    oc, sel = _nsa_cmp(q_nsa, kc, vct, trc, smt, gates)
    tk = min(512, s)
    o_nsa = _nsa_sw(q_nsa, kv_sw, tb, sel, _expand_matrix(s, nslc, tk), gates, oc)

    nlat = MLA_Q_RANK + MLA_KV_RANK
    lat = _proj([hn], [(w_lat, dm, 0, 0)], _epi_rmsnorm,
                [(jnp.concatenate([q_norm, kv_norm])[None, :], "row")], n=nlat, out_dtype=BF, tn=MLA_Q_RANK,
                name="proj_mla_latent")
    c, s1, s2 = _rope_tables(pos)
    kr = _proj([hn], [(_pad_cols(w_kr, LANES), dm, 0, 0)], functools.partial(_epi_rope, 1.0),
               [(c, "rowtile"), (s1, "rowtile"), (s2, "rowtile")], n=LANES, out_dtype=BF, name="proj_mla_kr")
    scale = (MLA_NOPE + MLA_ROPE) ** -0.5 * LOG2E
    w_uq3 = w_uq.reshape(MLA_Q_RANK, MLA_HEADS, MLA_NOPE + MLA_ROPE)
    w_uqn = w_uq3[:, :, :MLA_NOPE].reshape(MLA_Q_RANK, MLA_HEADS * MLA_NOPE)
    w_uqr = jnp.pad(w_uq3[:, :, MLA_NOPE:], ((0, 0), (0, 0), (0, LANES - MLA_ROPE))).reshape(
        MLA_Q_RANK, MLA_HEADS * LANES)
    qn = _proj([lat], [(w_uqn, MLA_Q_RANK, 0, 0)], functools.partial(_epi_scale, scale),
               n=MLA_HEADS * MLA_NOPE, out_dtype=BF, head_major=True, tn=1024, lhs_col_block=[0],
               name="proj_mla_qn")
    qr = _proj([lat], [(w_uqr, MLA_Q_RANK, 0, 0)], functools.partial(_epi_rope, scale),
               [(c, "rowtile"), (s1, "rowtile"), (s2, "rowtile")], n=MLA_HEADS * LANES, out_dtype=BF,
               head_major=True, lhs_col_block=[0], name="proj_mla_qr")
    kvh = _proj([lat], [(w_ukv, MLA_KV_RANK, 0, 0)], _epi_id, n=MLA_HEADS * (MLA_NOPE + MLA_V), out_dtype=BF,
                head_major=True, tn=1024, lhs_col_block=[1], name="proj_mla_kv")
    o_mla = _mla_attention(qn, qr, kvh, kr)

    return _proj([o_nsa, o_mla], [(w_out, nq, 0, 0), (w_out, MLA_HEADS * MLA_V, 1, 0)], _epi_residual,
                 [(x, "tile"), (g1, "row")], n=w_out.shape[1], out_dtype=F32, tn=1024, name="proj_even_out")


def _fox_mixer(hn, w_in, w_out, f_b, q_norm, k_norm, x, g1):
    d = D_MODEL
    dh = FOX_HEAD_DIM
    w_f = w_in[:, 3 * d:3 * d + FOX_HEADS]
    w_og = w_in[:, 3 * d + FOX_HEADS:]
    q = _proj([hn], [(w_in, d, 0, 0)], functools.partial(_epi_headnorm, dh ** -0.5 * LOG2E),
              [(q_norm[None, :], "const")], n=d, out_dtype=BF, head_major=True, tn=1024, name="proj_fox_q")
    k = _proj([hn], [(w_in, d, 0, d)], functools.partial(_epi_headnorm, 1.0), [(k_norm[None, :], "const")],
              n=d, out_dtype=BF, head_major=True, tn=1024, name="proj_fox_k")
    v = _proj([hn], [(w_in, d, 0, 2 * d)], _epi_id, n=d, out_dtype=BF, head_major=True, tn=1024,
              name="proj_fox_v")
    lf = _proj([hn], [(_pad_cols(w_f, LANES), d, 0, 0)], _epi_logsigmoid_bias,
               [(_pad_cols(f_b[None, :], LANES), "row")], n=LANES, out_dtype=F32, name="proj_fox_f")
    sig_og = _proj([hn], [(w_og, d, 0, 0)], _epi_sigmoid, n=d, out_dtype=F32, tn=1024, name="proj_fox_og")
    cum, cum_t = _cumsum_tokens(lf, LOG2E)
    cum_t = cum_t[:FOX_HEADS].reshape(FOX_HEADS, 1, -1)
    o = _fox_attention(q, k, v, cum_t, cum, sig_og)
    return _proj([o], [(w_out, d, 0, 0)], _epi_residual, [(x, "tile"), (g1, "row")], n=d, out_dtype=F32,
                 tn=1024, name="proj_fox_out")


def kernel(x, c, positions, rel_bias, ada_w, ada_b, norm_mix, norm_ffn, ffn_w1, ffn_w3, ffn_w2, even_w_in, even_w_out, nsa_gate_b, nsa_cmp_pos_k, nsa_cmp_w1_k, nsa_cmp_w2_k, nsa_cmp_pos_v, nsa_cmp_w1_v, nsa_cmp_w2_v, mla_q_norm, mla_w_uq, mla_kv_norm, mla_w_ukv, fox_w_in, fox_w_out, fox_f_b, fox_q_norm, fox_k_norm, final_norm):
    b, s, d = x.shape
    assert b == 1 and d == D_MODEL and s % 1024 == 0
    xs = x[0]
    pos = positions[0]
    mod = _adaln(c, ada_w, ada_b)
    depth = ada_w.shape[0]
    for i in range(depth):
        sh1, sc1, g1, sh2, sc2, g2 = [mod[i:i + 1, k * d:(k + 1) * d] for k in range(6)]
        hn = _normmod(xs, norm_mix[i][None, :], sc1, sh1)
        if i % 2 == 0:
            e = i // 2
            xs = _nsa_mla_mixer(hn, pos, rel_bias, even_w_in[e], even_w_out[e], nsa_gate_b[e],
                                nsa_cmp_pos_k[e], nsa_cmp_w1_k[e], nsa_cmp_w2_k[e],
                                nsa_cmp_pos_v[e], nsa_cmp_w1_v[e], nsa_cmp_w2_v[e],
                                mla_q_norm[e], mla_w_uq[e], mla_kv_norm[e], mla_w_ukv[e], xs, g1)
        else:
            o = i // 2
            xs = _fox_mixer(hn, fox_w_in[o], fox_w_out[o], fox_f_b[o], fox_q_norm[o], fox_k_norm[o], xs, g1)
        fin = final_norm[None, :] if i == depth - 1 else None
        xs = _ffn(xs, norm_ffn[i][None, :], sc2, sh2, g2, ffn_w1, ffn_w3, ffn_w2, i, fin)
    return xs[None]
```

```python
import functools
import math

import numpy as np
import jax
import jax.numpy as jnp
from jax import lax
from jax.experimental import pallas as pl
from jax.experimental.pallas import tpu as pltpu

D_MODEL = 2048
DEPTH = 2
EPS = 1e-6
NEG_INF = -1e30

NSA_HEADS = 8
NSA_KV_HEADS = 2
NSA_GROUP = NSA_HEADS // NSA_KV_HEADS
NSA_HEAD_DIM = 128
CMP_LEN = 32
CMP_STRIDE = 16
CMP_HIDDEN = 256
SLC_LEN = 64
SLC_TOPN = 16
WINDOW = 512
Q_BLOCK = 128

MLA_HEADS = 8
MLA_Q_RANK = 512
MLA_KV_RANK = 512
MLA_NOPE = 128
MLA_ROPE = 64
MLA_V = 128
ROPE_THETA = 10000.0

FOX_HEADS = 16
FOX_HEAD_DIM = D_MODEL // FOX_HEADS

REL_BUCKETS = 32
REL_MAX_DIST = 4096

FFN_HIDDEN = ((8 * D_MODEL + 2) // 3 + 255) // 256 * 256

LANES = 128
SUBLANES = 8
VMEM_LIMIT_BYTES = 56 * 1024 * 1024
LOG2E = math.log2(math.e)

BF = jnp.bfloat16
F32 = jnp.float32
_NT = (((1,), (1,)), ((), ()))


def _cparams(*sem):
    return pltpu.CompilerParams(dimension_semantics=sem, vmem_limit_bytes=VMEM_LIMIT_BYTES)


def _rms(x):
    return x * lax.rsqrt(jnp.mean(x * x, axis=-1, keepdims=True) + EPS)


def _adaln_kernel(c_ref, w_ref, b_ref, o_ref):
    c = c_ref[...]
    cond = c * jax.nn.sigmoid(c)
    acc = jnp.dot(cond.astype(BF), w_ref[0].astype(BF), preferred_element_type=F32)
    o_ref[0] = acc + b_ref[0]


def _adaln(c, ada_w, ada_b):
    depth, d, n = ada_w.shape
    tn = 1024
    c8 = jnp.broadcast_to(c, (SUBLANES, d))
    out = pl.pallas_call(
        _adaln_kernel,
        grid=(depth, n // tn),
        in_specs=[pl.BlockSpec((SUBLANES, d), lambda i, j: (0, 0)),
                  pl.BlockSpec((1, d, tn), lambda i, j: (i, 0, j)),
                  pl.BlockSpec((1, 1, tn), lambda i, j: (i, 0, j))],
        out_specs=pl.BlockSpec((1, SUBLANES, tn), lambda i, j: (i, 0, j)),
        out_shape=jax.ShapeDtypeStruct((depth, SUBLANES, n), F32),
        compiler_params=_cparams("arbitrary", "arbitrary"),
        name="adaln",
    )(c8, ada_w, ada_b.reshape(depth, 1, n))
    return out[:, 0, :]


def _normmod_kernel(x_ref, g_ref, sc_ref, sh_ref, o_ref):
    y = _rms(x_ref[...])
    o_ref[...] = ((y * g_ref[...]) * (1.0 + sc_ref[...]) + sh_ref[...]).astype(o_ref.dtype)


def _normmod(x, g, sc, sh):
    s, d = x.shape
    tm = 512
    row = pl.BlockSpec((1, d), lambda i: (0, 0))
    return pl.pallas_call(
        _normmod_kernel,
        grid=(s // tm,),
        in_specs=[pl.BlockSpec((tm, d), lambda i: (i, 0)), row, row, row],
        out_specs=pl.BlockSpec((tm, d), lambda i: (i, 0)),
        out_shape=jax.ShapeDtypeStruct((s, d), BF),
        compiler_params=_cparams("arbitrary"),
        name="normmod",
    )(x, g, sc, sh)


def _proj_kernel(*refs, n_lhs, n_epi, epi, head_major):
    lhs = refs[:n_lhs]
    ws = refs[n_lhs:2 * n_lhs]
    epis = refs[2 * n_lhs:2 * n_lhs + n_epi]
    o_ref = refs[2 * n_lhs + n_epi]
    wbf = refs[2 * n_lhs + n_epi + 1:]

    @pl.when(pl.program_id(1) == 0)
    def _():
        for w, wb in zip(ws, wbf):
            wb[...] = w[...].astype(BF)

    acc = None
    for a, wb in zip(lhs, wbf):
        d = jnp.dot(a[...], wb[...], preferred_element_type=F32)
        acc = d if acc is None else acc + d
    res = epi(acc, *[e[...] for e in epis])
    if head_major:
        for r in range(o_ref.shape[0]):
            o_ref[r] = res[:, r * LANES:(r + 1) * LANES].astype(o_ref.dtype)
    else:
        o_ref[...] = res.astype(o_ref.dtype)


def _proj(lhs, ws, epi, epi_in=(), *, n, out_dtype, head_major=False, tm=1024, tn=512,
          lhs_col_block=None, name="proj"):
    m = lhs[0].shape[0]
    tm = min(tm, m)
    tn = min(tn, n)
    if lhs_col_block is None:
        lhs_col_block = [0] * len(lhs)
    in_specs = []
    for (_, k, _, _), cb in zip(ws, lhs_col_block):
        in_specs.append(pl.BlockSpec((tm, k), lambda j, i, cb=cb: (i, cb)))
    for arr, k, rb, col0 in ws:
        assert col0 % tn == 0
        if arr.ndim == 3:
            in_specs.append(pl.BlockSpec((None, k, tn),
                                         lambda j, i, rb=rb, cb0=col0 // tn: (rb[0], rb[1], cb0 + j)))
        else:
            in_specs.append(pl.BlockSpec((k, tn), lambda j, i, rb=rb, cb0=col0 // tn: (rb, cb0 + j)))
    arrays = list(lhs) + [w[0] for w in ws]
    for arr, kind in epi_in:
        if kind == "row":
            in_specs.append(pl.BlockSpec((1, tn), lambda j, i: (0, j)))
        elif kind == "const":
            in_specs.append(pl.BlockSpec(arr.shape, lambda j, i: (0, 0)))
        elif kind == "tile":
            in_specs.append(pl.BlockSpec((tm, tn), lambda j, i: (i, j)))
        elif kind == "rowtile":
            in_specs.append(pl.BlockSpec((tm, arr.shape[1]), lambda j, i: (i, 0)))
        else:
            raise ValueError(kind)
        arrays.append(arr)
    if head_major:
        hpt = tn // LANES
        out_spec = pl.BlockSpec((hpt, tm, LANES), lambda j, i: (j, i, 0))
        out_shape = jax.ShapeDtypeStruct((n // LANES, m, LANES), out_dtype)
    else:
        out_spec = pl.BlockSpec((tm, tn), lambda j, i: (i, j))
        out_shape = jax.ShapeDtypeStruct((m, n), out_dtype)
    kern = functools.partial(_proj_kernel, n_lhs=len(lhs), n_epi=len(epi_in), epi=epi,
                             head_major=head_major)
    return pl.pallas_call(
        kern,
        grid=(n // tn, m // tm),
        in_specs=in_specs,
        out_specs=out_spec,
        out_shape=out_shape,
        scratch_shapes=[pltpu.VMEM((k, tn), BF) for _, k, _, _ in ws],
        compiler_params=_cparams("arbitrary", "arbitrary"),
        name=name,
    )(*arrays)


def _epi_id(acc):
    return acc


def _epi_scale(scale, acc):
    return acc * scale


def _epi_sigmoid_bias(acc, b):
    return jax.nn.sigmoid(acc + b)


def _epi_logsigmoid_bias(acc, b):
    return jax.nn.log_sigmoid(acc + b)


def _epi_sigmoid(acc):
    return jax.nn.sigmoid(acc)


def _epi_rmsnorm(acc, g):
    return _rms(acc) * g


def _epi_headnorm(scale, acc, g):
    outs = []
    for r in range(acc.shape[1] // LANES):
        outs.append(_rms(acc[:, r * LANES:(r + 1) * LANES]) * g * scale)
    return jnp.concatenate(outs, axis=1)


def _epi_rope(scale, acc, c, s1, s2):
    reps = acc.shape[1] // LANES
    half = MLA_ROPE // 2
    if reps > 1:
        c = jnp.concatenate([c] * reps, axis=1)
        s1 = jnp.concatenate([s1] * reps, axis=1)
        s2 = jnp.concatenate([s2] * reps, axis=1)
    n = acc.shape[1]
    out = acc * c + pltpu.roll(acc, n - half, 1) * s1 + pltpu.roll(acc, half, 1) * s2
    return out * scale


def _epi_residual(acc, x, g):
    return x + g * acc


def _ffn_kernel(*refs, final):
    if final:
        x_ref, g_ref, sc_ref, sh_ref, g2_ref, w1_ref, w3_ref, w2_ref, fn_ref, o_ref, hn_ref = refs
    else:
        x_ref, g_ref, sc_ref, sh_ref, g2_ref, w1_ref, w3_ref, w2_ref, o_ref, hn_ref = refs
    f = pl.program_id(1)
    tm = x_ref.shape[0]
    halves = [slice(r * (tm // 2), (r + 1) * (tm // 2)) for r in range(2)]

    @pl.when(f == 0)
    def _():
        for rows in halves:
            x = x_ref[rows, :]
            hn_ref[rows, :] = ((_rms(x) * g_ref[...]) * (1.0 + sc_ref[...]) + sh_ref[...]).astype(BF)
            o_ref[rows, :] = x

    w1 = w1_ref[...].astype(BF)
    w3 = w3_ref[...].astype(BF)
    w2 = w2_ref[...].astype(BF)
    for rows in halves:
        h = hn_ref[rows, :]
        h1 = jnp.dot(h, w1, preferred_element_type=F32)
        h3 = jnp.dot(h, w3, preferred_element_type=F32)
        a = (h1 * jax.nn.sigmoid(h1)) * h3
        o_ref[rows, :] += g2_ref[...] * jnp.dot(a.astype(BF), w2, preferred_element_type=F32)

    if final:
        @pl.when(f == pl.num_programs(1) - 1)
        def _():
            for rows in halves:
                o_ref[rows, :] = _rms(o_ref[rows, :]) * fn_ref[...]


def _ffn(x, g, sc, sh, g2, w1, w3, w2, layer, final_gain=None):
    s, d = x.shape
    fdim = w1.shape[2]
    tm = min(1024, s)
    tf = 256
    row = pl.BlockSpec((1, d), lambda i, f: (0, 0))
    in_specs = [pl.BlockSpec((tm, d), lambda i, f: (i, 0), pipeline_mode=pl.Buffered(1)), row, row, row, row,
                pl.BlockSpec((None, d, tf), lambda i, f: (layer, 0, f)),
                pl.BlockSpec((None, d, tf), lambda i, f: (layer, 0, f)),
                pl.BlockSpec((None, tf, d), lambda i, f: (layer, f, 0))]
    arrays = [x, g, sc, sh, g2, w1, w3, w2]
    if final_gain is not None:
        in_specs.append(row)
        arrays.append(final_gain)
    return pl.pallas_call(
        functools.partial(_ffn_kernel, final=final_gain is not None),
        grid=(s // tm, fdim // tf),
        in_specs=in_specs,
        out_specs=pl.BlockSpec((tm, d), lambda i, f: (i, 0)),
        out_shape=jax.ShapeDtypeStruct((s, d), F32),
        scratch_shapes=[pltpu.VMEM((tm, d), BF)],
        compiler_params=_cparams("arbitrary", "arbitrary"),
        name="ffn",
    )(*arrays)


def _flash_update(s, vp, m_ref, acc_ref, row_shift=None):
    m_prev = m_ref[...]
    m_tile = jnp.max(s[...], axis=1, keepdims=True)
    if row_shift is not None:
        m_tile = m_tile + row_shift
    m_new = jnp.maximum(m_prev, m_tile)
    alpha = jnp.exp2(m_prev - m_new)
    sub = m_new if row_shift is None else m_new - row_shift
    p = jnp.exp2(s[...] - sub).astype(BF)
    acc_ref[...] = alpha * acc_ref[...] + jnp.dot(p, vp, preferred_element_type=F32)
    m_ref[...] = m_new


def _init_state(m_ref, acc_ref):
    m_ref[...] = jnp.full_like(m_ref, NEG_INF)
    acc_ref[...] = jnp.zeros_like(acc_ref)


def _with_ones(v):
    return jnp.concatenate([v, jnp.ones(v.shape, v.dtype)], axis=1)


def _normalized(acc):
    return acc[:, :LANES] / acc[:, LANES:]


def _causal_mask(t):
    rows = lax.broadcasted_iota(jnp.int32, (t, t), 0)
    cols = lax.broadcasted_iota(jnp.int32, (t, t), 1)
    return cols <= rows


def _causal_sweep(qi, tc, s_bufs, m_ref, acc_ref, kside_fn, logits_fn, vp_fn, row_shifts=None):
    s0, s1 = s_bufs
    causal = _causal_mask(tc)

    def pre(t, s_ref, chains, masked_chain=None):
        kside = kside_fn(pl.multiple_of(t * tc, tc))
        for c in chains:
            s = logits_fn(c, kside)
            if c == masked_chain:
                s = jnp.where(causal, s, NEG_INF)
            s_ref[c] = s

    def process(t, s_ref, chains):
        vp = vp_fn(pl.multiple_of(t * tc, tc))
        for c in chains:
            _flash_update(s_ref.at[c], vp, m_ref.at[c], acc_ref.at[c],
                          row_shift=None if row_shifts is None else row_shifts[c])

    d0 = 2 * qi
    pre(d0, s0, (0, 1), 0)
    pre(d0 + 1, s1, (1,), 1)
    process(d0, s0, (0, 1))
    pre(0, s0, (0, 1))
    process(d0 + 1, s1, (1,))

    def body(u, carry):
        t = 2 * u
        pre(t + 1, s1, (0, 1))
        process(t, s0, (0, 1))
        pre(t + 2, s0, (0, 1))
        process(t + 1, s1, (0, 1))
        return carry

    lax.fori_loop(0, qi, body, 0)


def _sweep_scratch(tc):
    return [pltpu.VMEM((2, tc, tc), F32), pltpu.VMEM((2, tc, tc), F32),
            pltpu.VMEM((2, tc, 1), F32), pltpu.VMEM((2, tc, 2 * LANES), F32)]


def _mla_kernel(qn_ref, qr_ref, kn_ref, kr_ref, v_ref, o_ref, s0_ref, s1_ref, m_ref, acc_ref, *, tc):
    qi = pl.program_id(1)
    qs = [jnp.concatenate([qn_ref[0, c * tc:(c + 1) * tc, :], qr_ref[0, c * tc:(c + 1) * tc, :]], axis=1)
          for c in range(2)]
    _init_state(m_ref, acc_ref)

    def kside(off):
        return jnp.concatenate([kn_ref[0, pl.ds(off, tc), :], kr_ref[pl.ds(off, tc), :]], axis=1)

    def logits(c, k):
        return lax.dot_general(qs[c], k, _NT, preferred_element_type=F32)

    def vp(off):
        return _with_ones(v_ref[0, pl.ds(off, tc), :])

    _causal_sweep(qi, tc, (s0_ref, s1_ref), m_ref, acc_ref, kside, logits, vp)
    for c in range(2):
        o_ref[c * tc:(c + 1) * tc, :] = _normalized(acc_ref[c]).astype(o_ref.dtype)


def _mla_attention(qn, qr, kvh, kr):
    h, s, _ = qn.shape
    tc = min(512, s // 2)
    w = 2 * tc
    return pl.pallas_call(
        functools.partial(_mla_kernel, tc=tc),
        grid=(h, s // w),
        in_specs=[pl.BlockSpec((1, w, LANES), lambda hh, i: (hh, i, 0)),
                  pl.BlockSpec((1, w, LANES), lambda hh, i: (hh, i, 0)),
                  pl.BlockSpec((1, s, LANES), lambda hh, i: (2 * hh, 0, 0)),
                  pl.BlockSpec((s, LANES), lambda hh, i: (0, 0)),
                  pl.BlockSpec((1, s, LANES), lambda hh, i: (2 * hh + 1, 0, 0))],
        out_specs=pl.BlockSpec((w, LANES), lambda hh, i: (i, hh)),
        out_shape=jax.ShapeDtypeStruct((s, h * LANES), BF),
        scratch_shapes=_sweep_scratch(tc),
        compiler_params=_cparams("arbitrary", "arbitrary"),
        name="mla_attn",
    )(qn, qr, kvh, kr, kvh)


def _fox_kernel(q_ref, k_ref, v_ref, ck_ref, cum_ref, og_ref, o_ref, s0_ref, s1_ref, m_ref, acc_ref, *, tc):
    hh = pl.program_id(0)
    qi = pl.program_id(1)
    qs = [q_ref[0, c * tc:(c + 1) * tc, :] for c in range(2)]
    lane = lax.broadcasted_iota(jnp.int32, (tc, LANES), 1)
    cqs = [jnp.sum(jnp.where(lane == hh, cum_ref[c * tc:(c + 1) * tc, :], 0.0), axis=1, keepdims=True)
           for c in range(2)]
    _init_state(m_ref, acc_ref)

    def kside(off):
        return k_ref[0, pl.ds(off, tc), :], ck_ref[0, :, pl.ds(off, tc)]

    def logits(c, kc):
        return lax.dot_general(qs[c], kc[0], _NT, preferred_element_type=F32) - kc[1]

    def vp(off):
        return _with_ones(v_ref[0, pl.ds(off, tc), :])

    _causal_sweep(qi, tc, (s0_ref, s1_ref), m_ref, acc_ref, kside, logits, vp, row_shifts=cqs)
    for c in range(2):
        rows = slice(c * tc, (c + 1) * tc)
        o_ref[rows, :] = (_normalized(acc_ref[c]) * og_ref[rows, :]).astype(o_ref.dtype)


def _fox_attention(q, k, v, cum_t, cum, sig_og):
    h, s, _ = q.shape
    tc = min(512, s // 2)
    w = 2 * tc
    hm = pl.BlockSpec((1, s, LANES), lambda hh, i: (hh, 0, 0))
    return pl.pallas_call(
        functools.partial(_fox_kernel, tc=tc),
        grid=(h, s // w),
        in_specs=[pl.BlockSpec((1, w, LANES), lambda hh, i: (hh, i, 0)), hm, hm,
                  pl.BlockSpec((1, 1, s), lambda hh, i: (hh, 0, 0)),
                  pl.BlockSpec((w, LANES), lambda hh, i: (i, 0)),
                  pl.BlockSpec((w, LANES), lambda hh, i: (i, hh))],
        out_specs=pl.BlockSpec((w, LANES), lambda hh, i: (i, hh)),
        out_shape=jax.ShapeDtypeStruct((s, h * LANES), BF),
        scratch_shapes=_sweep_scratch(tc),
        compiler_params=_cparams("arbitrary", "arbitrary"),
        name="fox_attn",
    )(q, k, v, cum_t, cum, sig_og)


def _cumsum_kernel(x_ref, o_ref, ot_ref, carry_ref, *, t, out_scale):
    @pl.when(pl.program_id(0) == 0)
    def _():
        carry_ref[...] = jnp.zeros_like(carry_ref)

    x = x_ref[...]
    rows = lax.broadcasted_iota(jnp.int32, (t, t), 0)
    cols = lax.broadcasted_iota(jnp.int32, (t, t), 1)
    tri = jnp.where(cols <= rows, 1.0, 0.0).astype(BF)
    hi = x.astype(BF)
    r1 = x - hi.astype(F32)
    mid = r1.astype(BF)
    lo = (r1 - mid.astype(F32)).astype(BF)
    cum = (jnp.dot(tri, hi, preferred_element_type=F32) + jnp.dot(tri, mid, preferred_element_type=F32)
           + jnp.dot(tri, lo, preferred_element_type=F32)) + carry_ref[...]
    scaled = cum * out_scale
    o_ref[...] = scaled
    ot_ref[...] = scaled.T
    carry_ref[...] = cum[t - 1:t, :]


def _cumsum_tokens(x, out_scale):
    s, n = x.shape
    t = min(256, s)
    return pl.pallas_call(
        functools.partial(_cumsum_kernel, t=t, out_scale=out_scale),
        grid=(s // t,),
        in_specs=[pl.BlockSpec((t, n), lambda i: (i, 0))],
        out_specs=[pl.BlockSpec((t, n), lambda i: (i, 0)), pl.BlockSpec((n, t), lambda i: (0, i))],
        out_shape=[jax.ShapeDtypeStruct((s, n), F32), jax.ShapeDtypeStruct((n, s), F32)],
        scratch_shapes=[pltpu.VMEM((1, n), F32)],
        compiler_params=_cparams("arbitrary"),
        name="cumsum",
    )(x)


def _compress_kernel(a_ref, pos_ref, w1_ref, w2_ref, o_ref):
    a = a_ref[0]
    nc = a.shape[0]
    p1 = jnp.dot((a + pos_ref[0, 0:1, :]).astype(BF), w1_ref[0, 0], preferred_element_type=F32)
    p2 = jnp.dot((a + pos_ref[0, 1:2, :]).astype(BF), w1_ref[0, 1], preferred_element_type=F32)
    h = p1 + pltpu.roll(p2, nc - 1, 0)
    act = h * jax.nn.sigmoid(h)
    o_ref[0] = jnp.dot(act.astype(BF), w2_ref[0], preferred_element_type=F32).astype(o_ref.dtype)


def _nsa_compress(kv_cmp, pos, w1, w2):
    c, nc, gw = kv_cmp.shape
    return pl.pallas_call(
        _compress_kernel,
        grid=(c,),
        in_specs=[pl.BlockSpec((1, nc, gw), lambda i: (i, 0, 0)),
                  pl.BlockSpec((1, 2, gw), lambda i: (i // 2, 0, 0)),
                  pl.BlockSpec((1, 2, gw, CMP_HIDDEN), lambda i: (i // 2, 0, 0, 0)),
                  pl.BlockSpec((1, CMP_HIDDEN, NSA_HEAD_DIM), lambda i: (i // 2, 0, 0))],
        out_specs=pl.BlockSpec((1, nc, NSA_HEAD_DIM), lambda i: (i, 0, 0)),
        out_shape=jax.ShapeDtypeStruct((c, nc, NSA_HEAD_DIM), BF),
        compiler_params=_cparams("arbitrary"),
        name="nsa_compress",
    )(kv_cmp, pos, w1, w2)


def _bias_table_kernel(brev_ref, basc_ref, tbl_ref, tb_ref, asc_ref, *, s, mc):
    tbl = tbl_ref[0] * LOG2E

    def lookup(bkt):
        out = jnp.zeros(bkt.shape, F32)
        for b in range(REL_BUCKETS):
            out = jnp.where(bkt == b, tbl[:, b:b + 1], out)
        return out

    rev = lookup(brev_ref[...])
    asc_ref[0] = lookup(basc_ref[...])
    for m in range(mc + 1):
        win = rev[:, s - Q_BLOCK * m:s - Q_BLOCK * m + 2 * Q_BLOCK]
        rolled = pltpu.roll(jnp.broadcast_to(win, (Q_BLOCK, 2 * Q_BLOCK)), Q_BLOCK + 1, 1,
                            stride=1, stride_axis=0)
        tb_ref[0, m] = rolled[:, :Q_BLOCK]


def _t5_bucket(dist):
    max_exact = REL_BUCKETS // 2
    d = jnp.maximum(dist, 0)
    ratio = jnp.log(jnp.maximum(d, max_exact).astype(F32) / max_exact) / math.log(REL_MAX_DIST / max_exact)
    large = jnp.minimum(max_exact + (ratio * (REL_BUCKETS - max_exact)).astype(jnp.int32), REL_BUCKETS - 1)
    return jnp.where(d < max_exact, d, large)


def _bias_tables(rel_bias, s):
    assert Q_BLOCK == SUBLANES * CMP_STRIDE
    h = rel_bias.shape[1]
    qblocks = s // Q_BLOCK
    max_exact = REL_BUCKETS // 2
    d_const = int(math.ceil(max_exact * (REL_MAX_DIST / max_exact)
                            ** ((REL_BUCKETS - max_exact - 1) / (REL_BUCKETS - max_exact)))) + 1
    mc = min(-(-(d_const + Q_BLOCK - 1) // Q_BLOCK), qblocks - 1)
    nrev = s + 2 * Q_BLOCK
    nasc = s + 3 * Q_BLOCK
    brev = _t5_bucket(s + Q_BLOCK - 1 - jnp.arange(nrev, dtype=jnp.int32))[None, :]
    basc = _t5_bucket(jnp.arange(nasc, dtype=jnp.int32) - 2 * Q_BLOCK)[None, :]
    tb, asc = pl.pallas_call(
        functools.partial(_bias_table_kernel, s=s, mc=mc),
        grid=(h,),
        in_specs=[pl.BlockSpec((1, nrev), lambda i: (0, 0)),
                  pl.BlockSpec((1, nasc), lambda i: (0, 0)),
                  pl.BlockSpec((1, 1, REL_BUCKETS), lambda i: (i, 0, 0))],
        out_specs=[pl.BlockSpec((1, mc + 1, Q_BLOCK, Q_BLOCK), lambda i: (i, 0, 0, 0)),
                   pl.BlockSpec((1, 1, nasc), lambda i: (i, 0, 0))],
        out_shape=[jax.ShapeDtypeStruct((h, mc + 1, Q_BLOCK, Q_BLOCK), F32),
                   jax.ShapeDtypeStruct((h, 1, nasc), F32)],
        compiler_params=_cparams("arbitrary"),
        name="bias_tables",
    )(brev, basc, rel_bias.T.reshape(h, 1, REL_BUCKETS))
    asc = asc[:, 0, :]
    base = 2 * Q_BLOCK - (CMP_LEN - 1)
    segs = [asc[:, base - CMP_STRIDE * nn:base - CMP_STRIDE * nn + s].reshape(h, qblocks, Q_BLOCK)
            for nn in range(SUBLANES)]
    trc = jnp.flip(jnp.stack(segs, axis=2), axis=1).reshape(h, qblocks * SUBLANES, Q_BLOCK)
    return tb, jnp.pad(trc, ((0, 0), (0, qblocks * SUBLANES), (0, 0)))


def _nsa_cmp_kernel(q_ref, kc_ref, vct_ref, trc_ref, smt_ref, gate_ref, oc_ref, sel_ref, *, qblocks, topn):
    qb = pl.program_id(1)
    rq = NSA_GROUP * Q_BLOCK
    q = q_ref[...].reshape(rq, NSA_HEAD_DIM)
    kc = kc_ref[0]
    nc = kc.shape[0]
    nslc = smt_ref.shape[0]
    s = lax.dot_general(kc, q, _NT, preferred_element_type=F32)
    n_io = lax.broadcasted_iota(jnp.int32, (nc, Q_BLOCK), 0)
    i_io = lax.broadcasted_iota(jnp.int32, (nc, Q_BLOCK), 1)
    mask = (n_io * CMP_STRIDE + (CMP_LEN - 1)) <= (qb * Q_BLOCK + i_io)
    off = pl.multiple_of((qblocks - 1 - qb) * SUBLANES, SUBLANES)
    ps = []
    for r in range(NSA_GROUP):
        l = jnp.where(mask, s[:, r * Q_BLOCK:(r + 1) * Q_BLOCK] + trc_ref[r, pl.ds(off, nc), :], NEG_INF)
        m = jnp.max(l, axis=0, keepdims=True)
        p = jnp.where(mask, jnp.exp2(l - m), 0.0)
        den = jnp.maximum(jnp.sum(p, axis=0, keepdims=True), 1e-30)
        ps.append(p / den)
    p_all = jnp.concatenate(ps, axis=1).astype(BF)
    oc_t = jnp.dot(vct_ref[0], p_all, preferred_element_type=F32)
    imp4 = jnp.dot(smt_ref[...], p_all, preferred_element_type=F32)
    imp = imp4[:, 0:Q_BLOCK]
    for r in range(1, NSA_GROUP):
        imp = imp + imp4[:, r * Q_BLOCK:(r + 1) * Q_BLOCK]

    j_io = lax.broadcasted_iota(jnp.int32, (nslc, Q_BLOCK), 0)
    t_io = qb * Q_BLOCK + lax.broadcasted_iota(jnp.int32, (nslc, Q_BLOCK), 1)
    cur = t_io >> int(math.log2(SLC_LEN))
    forced = jnp.logical_or(j_io == 0, jnp.logical_or(j_io == cur, j_io == cur - 1))
    val = jnp.where(forced, 1e9, jnp.where(j_io <= cur, imp, -1e9))
    sel = jnp.zeros((nslc, Q_BLOCK), F32)
    for _ in range(topn):
        mx = jnp.max(val, axis=0, keepdims=True)
        cand = jnp.where(val == mx, j_io, nslc)
        jmin = jnp.min(cand, axis=0, keepdims=True)
        pick = j_io == jmin
        sel = jnp.where(pick, 1.0, sel)
        val = jnp.where(pick, -3e38, val)
    sel_ref[0] = jnp.where(sel.T > 0.5, 0.0, NEG_INF).astype(sel_ref.dtype)

    gates = gate_ref[0]
    for r in range(NSA_GROUP):
        o_r = oc_t[:, r * Q_BLOCK:(r + 1) * Q_BLOCK].T
        oc_ref[:, r * NSA_HEAD_DIM:(r + 1) * NSA_HEAD_DIM] = o_r * gates[:, 3 * r:3 * r + 1]


def _nsa_cmp(q, kc, vct, trc, smt, gates):
    h, s, d = q.shape
    g = NSA_KV_HEADS
    qblocks = s // Q_BLOCK
    nc = kc.shape[1]
    nslc = smt.shape[0]
    topn = min(SLC_TOPN, nslc)
    return pl.pallas_call(
        functools.partial(_nsa_cmp_kernel, qblocks=qblocks, topn=topn),
        grid=(g, qblocks),
        in_specs=[pl.BlockSpec((NSA_GROUP, Q_BLOCK, d), lambda gg, i: (gg, i, 0)),
                  pl.BlockSpec((1, nc, d), lambda gg, i: (gg, 0, 0)),
                  pl.BlockSpec((1, d, nc), lambda gg, i: (gg, 0, 0)),
                  pl.BlockSpec((NSA_GROUP, trc.shape[1], Q_BLOCK), lambda gg, i: (gg, 0, 0)),
                  pl.BlockSpec(smt.shape, lambda gg, i: (0, 0)),
                  pl.BlockSpec((1, Q_BLOCK, LANES), lambda gg, i: (gg, i, 0))],
        out_specs=[pl.BlockSpec((Q_BLOCK, NSA_GROUP * d), lambda gg, i: (i, gg)),
                   pl.BlockSpec((1, Q_BLOCK, nslc), lambda gg, i: (gg, i, 0))],
        out_shape=[jax.ShapeDtypeStruct((s, h * d), F32),
                   jax.ShapeDtypeStruct((g, s, nslc), BF)],
        compiler_params=_cparams("arbitrary", "arbitrary"),
        name="nsa_cmp",
    )(q, kc, vct, trc, smt, gates)


def _nsa_sw_kernel(q_ref, ks_ref, vs_ref, kw_ref, vw_ref, tb_ref, sel_ref, et_ref, gate_ref, oc_ref,
                   o_ref, s0_ref, s1_ref, m_ref, acc_ref, *, tk, mc, nch):
    i = pl.program_id(1)
    rq = NSA_GROUP * Q_BLOCK
    nsub = tk // Q_BLOCK
    qbs = [i * nch + c for c in range(nch)]
    qs = [q_ref[:, c * Q_BLOCK:(c + 1) * Q_BLOCK, :].reshape(rq, NSA_HEAD_DIM) for c in range(nch)]
    sels = [sel_ref[0, c * Q_BLOCK:(c + 1) * Q_BLOCK, :] for c in range(nch)]

    def biased(c, s, msk, kb0, nblk):
        parts = []
        for r in range(NSA_GROUP):
            bias = jnp.concatenate(
                [tb_ref[r, jnp.clip(qbs[c] - (kb0 + b), 0, mc)] for b in range(nblk)], axis=1)
            parts.append(jnp.where(msk, s[r * Q_BLOCK:(r + 1) * Q_BLOCK, :] + bias, NEG_INF))
        return jnp.concatenate(parts, axis=0)

    def gated(c, o, branch, base):
        gates = gate_ref[0, c * Q_BLOCK:(c + 1) * Q_BLOCK, :]
        outs = []
        for r in range(NSA_GROUP):
            o_r = o[r * Q_BLOCK:(r + 1) * Q_BLOCK, :] * gates[:, 3 * r + branch:3 * r + branch + 1]
            outs.append(base[:, r * NSA_HEAD_DIM:(r + 1) * NSA_HEAD_DIM] + o_r)
        return jnp.concatenate(outs, axis=1)

    _init_state(m_ref, acc_ref)
    qa = [jnp.concatenate([qs[c], jnp.concatenate([sels[c]] * NSA_GROUP, axis=0)], axis=1) for c in range(nch)]

    def pre(j, s_ref, causal):
        off = pl.multiple_of(j * tk, tk)
        ka = jnp.concatenate([ks_ref[0, pl.ds(off, tk), :], et_ref[pl.ds(off, tk), :]], axis=1)
        for c in range(nch):
            s = lax.dot_general(qa[c], ka, _NT, preferred_element_type=F32)
            if causal:
                kk = off + lax.broadcasted_iota(jnp.int32, (Q_BLOCK, tk), 1)
                ii = qbs[c] * Q_BLOCK + lax.broadcasted_iota(jnp.int32, (Q_BLOCK, tk), 0)
                future = jnp.where(kk <= ii, 0.0, NEG_INF)
            for r in range(NSA_GROUP):
                bias = jnp.concatenate(
                    [tb_ref[r, jnp.clip(qbs[c] - (j * nsub + b), 0, mc)] for b in range(nsub)], axis=1)
                if causal:
                    bias = bias + future
                s_ref[c, r * Q_BLOCK:(r + 1) * Q_BLOCK, :] = s[r * Q_BLOCK:(r + 1) * Q_BLOCK, :] + bias

    def process(j, s_ref):
        vp = _with_ones(vs_ref[0, pl.ds(pl.multiple_of(j * tk, tk), tk), :])
        for c in range(nch):
            _flash_update(s_ref.at[c], vp, m_ref.at[c], acc_ref.at[c])

    n_past = (i * nch) // nsub

    def body(u, carry):
        t = 2 * u
        pre(t + 1, s0_ref, False)
        process(t, s1_ref)
        pre(t + 2, s1_ref, False)
        process(t + 1, s0_ref)
        return carry

    pre(n_past, s0_ref, True)
    pre(0, s1_ref, False)
    process(n_past, s0_ref)
    lax.fori_loop(0, n_past // 2, body, 0)

    @pl.when(n_past % 2 == 1)
    def _():
        process(n_past - 1, s1_ref)

    outs = [gated(c, _normalized(acc_ref[c]), 1, oc_ref[c * Q_BLOCK:(c + 1) * Q_BLOCK, :]) for c in range(nch)]

    wk = WINDOW + Q_BLOCK
    wblk = wk // Q_BLOCK
    for c in range(nch):
        kb0 = jnp.maximum(qbs[c] - WINDOW // Q_BLOCK, 0)
        off = pl.multiple_of(kb0 * Q_BLOCK, Q_BLOCK)
        s = lax.dot_general(qs[c], kw_ref[0, pl.ds(off, wk), :], _NT, preferred_element_type=F32)
        rel = (qbs[c] * Q_BLOCK + lax.broadcasted_iota(jnp.int32, (Q_BLOCK, wk), 0)) - (
            off + lax.broadcasted_iota(jnp.int32, (Q_BLOCK, wk), 1))
        msk = jnp.logical_and(rel >= 0, rel < WINDOW)
        l = biased(c, s, msk, kb0, wblk)
        p = jnp.exp2(l - jnp.max(l, axis=1, keepdims=True)).astype(BF)
        ow = jnp.dot(p, _with_ones(vw_ref[0, pl.ds(off, wk), :]), preferred_element_type=F32)
        o_ref[c * Q_BLOCK:(c + 1) * Q_BLOCK, :] = gated(c, _normalized(ow), 2, outs[c]).astype(o_ref.dtype)


def _nsa_sw(q, kvsw, tb, sel, et, gates, oc):
    h, s, d = q.shape
    g = NSA_KV_HEADS
    nch = 2
    qrows = nch * Q_BLOCK
    tk = min(512, s)
    mc = tb.shape[1] - 1
    nslc = sel.shape[2]
    rq = NSA_GROUP * Q_BLOCK
    assert tk % qrows == 0 and WINDOW + Q_BLOCK <= s

    def kv(slot):
        return pl.BlockSpec((1, s, d), lambda gg, i, slot=slot: (slot + gg, 0, 0))

    return pl.pallas_call(
        functools.partial(_nsa_sw_kernel, tk=tk, mc=mc, nch=nch),
        grid=(g, s // qrows),
        in_specs=[pl.BlockSpec((NSA_GROUP, qrows, d), lambda gg, i: (gg, i, 0)),
                  kv(0), kv(2), kv(4), kv(6),
                  pl.BlockSpec((NSA_GROUP,) + tb.shape[1:], lambda gg, i: (gg, 0, 0, 0)),
                  pl.BlockSpec((1, qrows, nslc), lambda gg, i: (gg, i, 0)),
                  pl.BlockSpec(et.shape, lambda gg, i: (0, 0)),
                  pl.BlockSpec((1, qrows, LANES), lambda gg, i: (gg, i, 0)),
                  pl.BlockSpec((qrows, NSA_GROUP * d), lambda gg, i: (i, gg))],
        out_specs=pl.BlockSpec((qrows, NSA_GROUP * d), lambda gg, i: (i, gg)),
        out_shape=jax.ShapeDtypeStruct((s, h * d), BF),
        scratch_shapes=[pltpu.VMEM((nch, rq, tk), F32), pltpu.VMEM((nch, rq, tk), F32),
                        pltpu.VMEM((nch, rq, 1), F32), pltpu.VMEM((nch, rq, 2 * LANES), F32)],
        compiler_params=_cparams("arbitrary", "arbitrary"),
        name="nsa_sel_win",
    )(q, kvsw, kvsw, kvsw, kvsw, tb, sel, et, gates, oc)


def _selection_map_t(nc, nslc):
    n = np.arange(nc)[None, :] * CMP_STRIDE
    j0 = np.arange(nslc)[:, None] * SLC_LEN
    valid = np.arange(nc)[None, :] < nc - 1
    return jnp.asarray(((n < j0 + SLC_LEN) & (n + CMP_LEN > j0) & valid).astype(np.float32), dtype=BF)


def _block_onehot(s, nslc):
    tok = np.arange(s)[:, None]
    j = np.arange(nslc)[None, :]
    return jnp.asarray((tok // SLC_LEN == j).astype(np.float32), dtype=BF)


def _rope_tables(positions):
    half = MLA_ROPE // 2
    inv = ROPE_THETA ** (-jnp.arange(half, dtype=F32) / half)
    ang = positions.astype(F32)[:, None] * inv
    cos, sin = jnp.cos(ang), jnp.sin(ang)
    z = jnp.zeros_like(cos)
    zpad = jnp.zeros((positions.shape[0], LANES - MLA_ROPE), F32)
    c = jnp.concatenate([cos, cos, zpad], axis=1)
    s1 = jnp.concatenate([-sin, z, zpad], axis=1)
    s2 = jnp.concatenate([z, sin, zpad], axis=1)
    return c, s1, s2


def _pad_cols(w, n):
    return jnp.pad(w, ((0, 0), (0, n - w.shape[1])))


def _nsa_mla_mixer(hn, pos, rel_bias, w_in_all, w_out_all, e, gate_b, pos_k, w1_k, w2_k, pos_v, w1_v, w2_v,
                   q_norm, w_uq, kv_norm, w_ukv, x, g1):
    s = hn.shape[0]
    d = NSA_HEAD_DIM
    nq = NSA_HEADS * d
    nkv = 2 * NSA_KV_HEADS * d
    w_in = w_in_all[e]
    o0 = nq + 3 * nkv
    w_g = w_in[:, o0:o0 + 3 * NSA_HEADS]; o0 += 3 * NSA_HEADS
    w_lat = w_in[:, o0:o0 + MLA_Q_RANK + MLA_KV_RANK]; o0 += MLA_Q_RANK + MLA_KV_RANK
    w_kr = w_in[:, o0:o0 + MLA_ROPE]

    dm = w_in.shape[0]
    q_nsa = _proj([hn], [(w_in_all, dm, (e, 0), 0)], functools.partial(_epi_scale, d ** -0.5 * LOG2E), n=nq,
                  out_dtype=BF, head_major=True, tn=1024, name="proj_q_nsa")
    kv_cmp = _proj([hn], [(w_in_all, dm, (e, 0), nq)], _epi_id, n=nkv, out_dtype=F32, head_major=True,
                   name="proj_kv_cmp")
    kv_sw = _proj([hn], [(w_in_all, dm, (e, 0), nq + nkv)], _epi_id, n=2 * nkv, out_dtype=BF, head_major=True,
                  name="proj_kv_sw")
    per_g = 3 * NSA_GROUP
    w_gp = jnp.concatenate([_pad_cols(w_g[:, g * per_g:(g + 1) * per_g], LANES) for g in range(NSA_KV_HEADS)], 1)
    b_gp = jnp.concatenate([_pad_cols(gate_b[None, g * per_g:(g + 1) * per_g], LANES)
                            for g in range(NSA_KV_HEADS)], 1)
    gates = _proj([hn], [(w_gp, dm, 0, 0)], _epi_sigmoid_bias, [(b_gp, "row")], n=NSA_KV_HEADS * LANES,
                  out_dtype=F32, head_major=True, tn=LANES, name="proj_gates")

    nc = s // CMP_STRIDE
    gw = CMP_STRIDE * d
    pos_kv = jnp.stack([pos_k.reshape(2, gw), pos_v.reshape(2, gw)])
    w1_kv = jnp.stack([w1_k.reshape(2, gw, CMP_HIDDEN), w1_v.reshape(2, gw, CMP_HIDDEN)]).astype(BF)
    w2_kv = jnp.stack([w2_k, w2_v]).astype(BF)
    kvc = _nsa_compress(kv_cmp.reshape(2 * NSA_KV_HEADS, nc, gw), pos_kv, w1_kv, w2_kv)
    kc = kvc[:NSA_KV_HEADS]
    vct = jnp.swapaxes(kvc[NSA_KV_HEADS:], 1, 2)

    nslc = s // SLC_LEN
    tb, trc = _bias_tables(rel_bias, s)
    smt = _selection_map_t(nc, nslc)
    oc, sel = _nsa_cmp(q_nsa, kc, vct, trc, smt, gates)
    o_nsa = _nsa_sw(q_nsa, kv_sw, tb, sel, _block_onehot(s, nslc), gates, oc)

    nlat = MLA_Q_RANK + MLA_KV_RANK
    lat = _proj([hn], [(w_lat, dm, 0, 0)], _epi_rmsnorm,
                [(jnp.concatenate([q_norm, kv_norm])[None, :], "row")], n=nlat, out_dtype=BF, tn=MLA_Q_RANK,
                name="proj_mla_latent")
    c, s1, s2 = _rope_tables(pos)
    kr = _proj([hn], [(_pad_cols(w_kr, LANES), dm, 0, 0)], functools.partial(_epi_rope, 1.0),
               [(c, "rowtile"), (s1, "rowtile"), (s2, "rowtile")], n=LANES, out_dtype=BF, name="proj_mla_kr")
    scale = (MLA_NOPE + MLA_ROPE) ** -0.5 * LOG2E
    w_uq3 = w_uq.reshape(MLA_Q_RANK, MLA_HEADS, MLA_NOPE + MLA_ROPE)
    w_uqn = w_uq3[:, :, :MLA_NOPE].reshape(MLA_Q_RANK, MLA_HEADS * MLA_NOPE)
    w_uqr = jnp.pad(w_uq3[:, :, MLA_NOPE:], ((0, 0), (0, 0), (0, LANES - MLA_ROPE))).reshape(
        MLA_Q_RANK, MLA_HEADS * LANES)
    qn = _proj([lat], [(w_uqn, MLA_Q_RANK, 0, 0)], functools.partial(_epi_scale, scale),
               n=MLA_HEADS * MLA_NOPE, out_dtype=BF, head_major=True, tn=1024, lhs_col_block=[0],
               name="proj_mla_qn")
    qr = _proj([lat], [(w_uqr, MLA_Q_RANK, 0, 0)], functools.partial(_epi_rope, scale),
               [(c, "rowtile"), (s1, "rowtile"), (s2, "rowtile")], n=MLA_HEADS * LANES, out_dtype=BF,
               head_major=True, lhs_col_block=[0], name="proj_mla_qr")
    kvh = _proj([lat], [(w_ukv, MLA_KV_RANK, 0, 0)], _epi_id, n=MLA_HEADS * (MLA_NOPE + MLA_V), out_dtype=BF,
                head_major=True, tn=1024, lhs_col_block=[1], name="proj_mla_kv")
    o_mla = _mla_attention(qn, qr, kvh, kr)

    return _proj([o_nsa, o_mla], [(w_out_all, nq, (e, 0), 0), (w_out_all, MLA_HEADS * MLA_V, (e, 1), 0)],
                 _epi_residual, [(x, "tile"), (g1, "row")], n=w_out_all.shape[2], out_dtype=F32, tn=1024,
                 name="proj_even_out")


def _fox_mixer(hn, w_in_all, w_out_all, o, f_b, q_norm, k_norm, x, g1):
    d = D_MODEL
    dh = FOX_HEAD_DIM
    w_in = w_in_all[o]
    w_f = w_in[:, 3 * d:3 * d + FOX_HEADS]
    w_og = w_in[:, 3 * d + FOX_HEADS:]
    q = _proj([hn], [(w_in_all, d, (o, 0), 0)], functools.partial(_epi_headnorm, dh ** -0.5 * LOG2E),
              [(q_norm[None, :], "const")], n=d, out_dtype=BF, head_major=True, tn=1024, name="proj_fox_q")
    k = _proj([hn], [(w_in_all, d, (o, 0), d)], functools.partial(_epi_headnorm, 1.0),
              [(k_norm[None, :], "const")], n=d, out_dtype=BF, head_major=True, tn=1024, name="proj_fox_k")
    v = _proj([hn], [(w_in_all, d, (o, 0), 2 * d)], _epi_id, n=d, out_dtype=BF, head_major=True, tn=1024,
              name="proj_fox_v")
    lf = _proj([hn], [(_pad_cols(w_f, LANES), d, 0, 0)], _epi_logsigmoid_bias,
               [(_pad_cols(f_b[None, :], LANES), "row")], n=LANES, out_dtype=F32, name="proj_fox_f")
    sig_og = _proj([hn], [(w_og, d, 0, 0)], _epi_sigmoid, n=d, out_dtype=F32, tn=1024, name="proj_fox_og")
    cum, cum_t = _cumsum_tokens(lf, LOG2E)
    cum_t = cum_t[:FOX_HEADS].reshape(FOX_HEADS, 1, -1)
    att = _fox_attention(q, k, v, cum_t, cum, sig_og)
    return _proj([att], [(w_out_all, d, (o, 0), 0)], _epi_residual, [(x, "tile"), (g1, "row")], n=d,
                 out_dtype=F32, tn=1024, name="proj_fox_out")


def kernel(x, c, positions, rel_bias, ada_w, ada_b, norm_mix, norm_ffn, ffn_w1, ffn_w3, ffn_w2, even_w_in, even_w_out, nsa_gate_b, nsa_cmp_pos_k, nsa_cmp_w1_k, nsa_cmp_w2_k, nsa_cmp_pos_v, nsa_cmp_w1_v, nsa_cmp_w2_v, mla_q_norm, mla_w_uq, mla_kv_norm, mla_w_ukv, fox_w_in, fox_w_out, fox_f_b, fox_q_norm, fox_k_norm, final_norm):
    b, s, d = x.shape
    assert b == 1 and d == D_MODEL and s % 1024 == 0
    xs = x[0]
    pos = positions[0]
    mod = _adaln(c, ada_w, ada_b)
    depth = ada_w.shape[0]
    for i in range(depth):
        sh1, sc1, g1, sh2, sc2, g2 = [mod[i:i + 1, k * d:(k + 1) * d] for k in range(6)]
        hn = _normmod(xs, norm_mix[i][None, :], sc1, sh1)
        if i % 2 == 0:
            e = i // 2
            xs = _nsa_mla_mixer(hn, pos, rel_bias, even_w_in, even_w_out, e, nsa_gate_b[e],
                                nsa_cmp_pos_k[e], nsa_cmp_w1_k[e], nsa_cmp_w2_k[e],
                                nsa_cmp_pos_v[e], nsa_cmp_w1_v[e], nsa_cmp_w2_v[e],
                                mla_q_norm[e], mla_w_uq[e], mla_kv_norm[e], mla_w_ukv[e], xs, g1)
        else:
            o = i // 2
            xs = _fox_mixer(hn, fox_w_in, fox_w_out, o, fox_f_b[o], fox_q_norm[o], fox_k_norm[o], xs, g1)
        fin = final_norm[None, :] if i == depth - 1 else None
        xs = _ffn(xs, norm_ffn[i][None, :], sc2, sh2, g2, ffn_w1, ffn_w3, ffn_w2, i, fin)
    return xs[None]
```

```python
import functools
import math

import numpy as np
import jax
import jax.numpy as jnp
from jax import lax
from jax.experimental import pallas as pl
from jax.experimental.pallas import tpu as pltpu

D_MODEL = 2048
DEPTH = 2
EPS = 1e-6
NEG_INF = -1e30

NSA_HEADS = 8
NSA_KV_HEADS = 2
NSA_GROUP = NSA_HEADS // NSA_KV_HEADS
NSA_HEAD_DIM = 128
CMP_LEN = 32
CMP_STRIDE = 16
CMP_HIDDEN = 256
SLC_LEN = 64
SLC_TOPN = 16
WINDOW = 512
Q_BLOCK = 128

MLA_HEADS = 8
MLA_Q_RANK = 512
MLA_KV_RANK = 512
MLA_NOPE = 128
MLA_ROPE = 64
MLA_V = 128
ROPE_THETA = 10000.0

FOX_HEADS = 16
FOX_HEAD_DIM = D_MODEL // FOX_HEADS

REL_BUCKETS = 32
REL_MAX_DIST = 4096

FFN_HIDDEN = ((8 * D_MODEL + 2) // 3 + 255) // 256 * 256

LANES = 128
SUBLANES = 8
VMEM_LIMIT_BYTES = 56 * 1024 * 1024
LOG2E = math.log2(math.e)

BF = jnp.bfloat16
F32 = jnp.float32
_NT = (((1,), (1,)), ((), ()))


def _cparams(*sem):
    return pltpu.CompilerParams(dimension_semantics=sem, vmem_limit_bytes=VMEM_LIMIT_BYTES)


def _rms(x):
    return x * lax.rsqrt(jnp.mean(x * x, axis=-1, keepdims=True) + EPS)


def _adaln_kernel(c_ref, w_ref, b_ref, o_ref):
    c = c_ref[...]
    cond = c * jax.nn.sigmoid(c)
    acc = jnp.dot(cond.astype(BF), w_ref[0].astype(BF), preferred_element_type=F32)
    o_ref[0] = acc + b_ref[0]


def _adaln(c, ada_w, ada_b):
    depth, d, n = ada_w.shape
    tn = 1024
    c8 = jnp.broadcast_to(c, (SUBLANES, d))
    out = pl.pallas_call(
        _adaln_kernel,
        grid=(depth, n // tn),
        in_specs=[pl.BlockSpec((SUBLANES, d), lambda i, j: (0, 0)),
                  pl.BlockSpec((1, d, tn), lambda i, j: (i, 0, j)),
                  pl.BlockSpec((1, 1, tn), lambda i, j: (i, 0, j))],
        out_specs=pl.BlockSpec((1, SUBLANES, tn), lambda i, j: (i, 0, j)),
        out_shape=jax.ShapeDtypeStruct((depth, SUBLANES, n), F32),
        compiler_params=_cparams("arbitrary", "arbitrary"),
        name="adaln",
    )(c8, ada_w, ada_b.reshape(depth, 1, n))
    return out[:, 0, :]


def _normmod_kernel(x_ref, g_ref, sc_ref, sh_ref, o_ref):
    y = _rms(x_ref[...])
    o_ref[...] = ((y * g_ref[...]) * (1.0 + sc_ref[...]) + sh_ref[...]).astype(o_ref.dtype)


def _normmod(x, g, sc, sh):
    s, d = x.shape
    tm = 512
    row = pl.BlockSpec((1, d), lambda i: (0, 0))
    return pl.pallas_call(
        _normmod_kernel,
        grid=(s // tm,),
        in_specs=[pl.BlockSpec((tm, d), lambda i: (i, 0)), row, row, row],
        out_specs=pl.BlockSpec((tm, d), lambda i: (i, 0)),
        out_shape=jax.ShapeDtypeStruct((s, d), BF),
        compiler_params=_cparams("arbitrary"),
        name="normmod",
    )(x, g, sc, sh)


def _proj_kernel(*refs, n_lhs, n_epi, epi, head_major, w_transposed):
    lhs = refs[:n_lhs]
    ws = refs[n_lhs:2 * n_lhs]
    epis = refs[2 * n_lhs:2 * n_lhs + n_epi]
    o_ref = refs[2 * n_lhs + n_epi]
    wbf = refs[2 * n_lhs + n_epi + 1:]

    @pl.when(pl.program_id(1) == 0)
    def _():
        for w, wb in zip(ws, wbf):
            wb[...] = w[...].astype(BF)

    acc = None
    for a, wb in zip(lhs, wbf):
        if w_transposed:
            d = lax.dot_general(a[...], wb[...], _NT, preferred_element_type=F32)
        else:
            d = jnp.dot(a[...], wb[...], preferred_element_type=F32)
        acc = d if acc is None else acc + d
    res = epi(acc, *[e[...] for e in epis])
    if head_major:
        for r in range(o_ref.shape[0]):
            o_ref[r] = res[:, r * LANES:(r + 1) * LANES].astype(o_ref.dtype)
    else:
        o_ref[...] = res.astype(o_ref.dtype)


def _proj(lhs, ws, epi, epi_in=(), *, n, out_dtype, head_major=False, tm=1024, tn=512,
          lhs_col_block=None, w_transposed=False, name="proj"):
    m = lhs[0].shape[0]
    tm = min(tm, m)
    tn = min(tn, n)
    if lhs_col_block is None:
        lhs_col_block = [0] * len(lhs)
    in_specs = []
    for (_, k, _, _), cb in zip(ws, lhs_col_block):
        in_specs.append(pl.BlockSpec((tm, k), lambda j, i, cb=cb: (i, cb)))
    for arr, k, rb, col0 in ws:
        assert col0 % tn == 0
        if w_transposed:
            in_specs.append(pl.BlockSpec((None, tn, k),
                                         lambda j, i, rb=rb, cb0=col0 // tn: (rb[0], cb0 + j, rb[1])))
        elif arr.ndim == 3:
            in_specs.append(pl.BlockSpec((None, k, tn),
                                         lambda j, i, rb=rb, cb0=col0 // tn: (rb[0], rb[1], cb0 + j)))
        else:
            in_specs.append(pl.BlockSpec((k, tn), lambda j, i, rb=rb, cb0=col0 // tn: (rb, cb0 + j)))
    arrays = list(lhs) + [w[0] for w in ws]
    for arr, kind in epi_in:
        if kind == "row":
            in_specs.append(pl.BlockSpec((1, tn), lambda j, i: (0, j)))
        elif kind == "const":
            in_specs.append(pl.BlockSpec(arr.shape, lambda j, i: (0, 0)))
        elif kind == "tile":
            in_specs.append(pl.BlockSpec((tm, tn), lambda j, i: (i, j)))
        elif kind == "rowtile":
            in_specs.append(pl.BlockSpec((tm, arr.shape[1]), lambda j, i: (i, 0)))
        else:
            raise ValueError(kind)
        arrays.append(arr)
    if head_major:
        hpt = tn // LANES
        out_spec = pl.BlockSpec((hpt, tm, LANES), lambda j, i: (j, i, 0))
        out_shape = jax.ShapeDtypeStruct((n // LANES, m, LANES), out_dtype)
    else:
        out_spec = pl.BlockSpec((tm, tn), lambda j, i: (i, j))
        out_shape = jax.ShapeDtypeStruct((m, n), out_dtype)
    kern = functools.partial(_proj_kernel, n_lhs=len(lhs), n_epi=len(epi_in), epi=epi,
                             head_major=head_major, w_transposed=w_transposed)
    return pl.pallas_call(
        kern,
        grid=(n // tn, m // tm),
        in_specs=in_specs,
        out_specs=out_spec,
        out_shape=out_shape,
        scratch_shapes=[pltpu.VMEM((tn, k) if w_transposed else (k, tn), BF) for _, k, _, _ in ws],
        compiler_params=_cparams("arbitrary", "arbitrary"),
        name=name,
    )(*arrays)


def _epi_id(acc):
    return acc


def _epi_scale(scale, acc):
    return acc * scale


def _epi_sigmoid_bias(acc, b):
    return jax.nn.sigmoid(acc + b)


def _epi_logsigmoid_bias(acc, b):
    return jax.nn.log_sigmoid(acc + b)


def _epi_sigmoid(acc):
    return jax.nn.sigmoid(acc)


def _epi_rmsnorm(acc, g):
    return _rms(acc) * g


def _epi_headnorm(scale, acc, g):
    outs = []
    for r in range(acc.shape[1] // LANES):
        outs.append(_rms(acc[:, r * LANES:(r + 1) * LANES]) * g * scale)
    return jnp.concatenate(outs, axis=1)


def _epi_rope(scale, acc, c, s1, s2):
    reps = acc.shape[1] // LANES
    half = MLA_ROPE // 2
    if reps > 1:
        c = jnp.concatenate([c] * reps, axis=1)
        s1 = jnp.concatenate([s1] * reps, axis=1)
        s2 = jnp.concatenate([s2] * reps, axis=1)
    n = acc.shape[1]
    out = acc * c + pltpu.roll(acc, n - half, 1) * s1 + pltpu.roll(acc, half, 1) * s2
    return out * scale


def _epi_residual(acc, x, g):
    return x + g * acc


def _ffn_kernel(*refs, final):
    if final:
        x_ref, g_ref, sc_ref, sh_ref, g2_ref, w1_ref, w3_ref, w2_ref, fn_ref, o_ref, hn_ref = refs
    else:
        x_ref, g_ref, sc_ref, sh_ref, g2_ref, w1_ref, w3_ref, w2_ref, o_ref, hn_ref = refs
    f = pl.program_id(1)
    tm = x_ref.shape[0]
    halves = [slice(r * (tm // 2), (r + 1) * (tm // 2)) for r in range(2)]

    @pl.when(f == 0)
    def _():
        for rows in halves:
            x = x_ref[rows, :]
            hn_ref[rows, :] = ((_rms(x) * g_ref[...]) * (1.0 + sc_ref[...]) + sh_ref[...]).astype(BF)
            o_ref[rows, :] = x

    w1 = w1_ref[...].astype(BF)
    w3 = w3_ref[...].astype(BF)
    w2 = w2_ref[...].astype(BF)
    for rows in halves:
        h = hn_ref[rows, :]
        h1 = jnp.dot(h, w1, preferred_element_type=F32)
        h3 = jnp.dot(h, w3, preferred_element_type=F32)
        a = (h1 * jax.nn.sigmoid(h1)) * h3
        o_ref[rows, :] += g2_ref[...] * jnp.dot(a.astype(BF), w2, preferred_element_type=F32)

    if final:
        @pl.when(f == pl.num_programs(1) - 1)
        def _():
            for rows in halves:
                o_ref[rows, :] = _rms(o_ref[rows, :]) * fn_ref[...]


def _ffn(x, g, sc, sh, g2, w1, w3, w2, layer, final_gain=None):
    s, d = x.shape
    fdim = w1.shape[2]
    tm = min(1024, s)
    tf = 256
    row = pl.BlockSpec((1, d), lambda i, f: (0, 0))
    in_specs = [pl.BlockSpec((tm, d), lambda i, f: (i, 0), pipeline_mode=pl.Buffered(1)), row, row, row, row,
                pl.BlockSpec((None, d, tf), lambda i, f: (layer, 0, f)),
                pl.BlockSpec((None, d, tf), lambda i, f: (layer, 0, f)),
                pl.BlockSpec((None, tf, d), lambda i, f: (layer, f, 0))]
    arrays = [x, g, sc, sh, g2, w1, w3, w2]
    if final_gain is not None:
        in_specs.append(row)
        arrays.append(final_gain)
    return pl.pallas_call(
        functools.partial(_ffn_kernel, final=final_gain is not None),
        grid=(s // tm, fdim // tf),
        in_specs=in_specs,
        out_specs=pl.BlockSpec((tm, d), lambda i, f: (i, 0)),
        out_shape=jax.ShapeDtypeStruct((s, d), F32),
        scratch_shapes=[pltpu.VMEM((tm, d), BF)],
        compiler_params=_cparams("arbitrary", "arbitrary"),
        name="ffn",
    )(*arrays)


def _flash_update(s, vp, m_ref, acc_ref, row_shift=None):
    m_prev = m_ref[...]
    m_tile = jnp.max(s[...], axis=1, keepdims=True)
    if row_shift is not None:
        m_tile = m_tile + row_shift
    m_new = jnp.maximum(m_prev, m_tile)
    alpha = jnp.exp2(m_prev - m_new)
    sub = m_new if row_shift is None else m_new - row_shift
    p = jnp.exp2(s[...] - sub).astype(BF)
    acc_ref[...] = alpha * acc_ref[...] + jnp.dot(p, vp, preferred_element_type=F32)
    m_ref[...] = m_new


def _init_state(m_ref, acc_ref):
    m_ref[...] = jnp.full_like(m_ref, NEG_INF)
    acc_ref[...] = jnp.zeros_like(acc_ref)


def _with_ones(v):
    return jnp.concatenate([v, jnp.ones(v.shape, v.dtype)], axis=1)


def _normalized(acc):
    return acc[:, :LANES] / acc[:, LANES:]


def _causal_mask(t):
    rows = lax.broadcasted_iota(jnp.int32, (t, t), 0)
    cols = lax.broadcasted_iota(jnp.int32, (t, t), 1)
    return cols <= rows


def _causal_sweep(qi, tc, s_bufs, m_ref, acc_ref, kside_fn, logits_fn, vp_fn, row_shifts=None):
    s0, s1 = s_bufs
    tw = 2 * tc
    shifts = (None, None) if row_shifts is None else row_shifts

    def pre(t, s_ref):
        kside = kside_fn(pl.multiple_of(t * tw, tw), tw)
        for c in range(2):
            s_ref[c] = logits_fn(c, kside)

    def process(t, s_ref):
        vp = vp_fn(pl.multiple_of(t * tw, tw), tw)
        for c in range(2):
            _flash_update(s_ref.at[c], vp, m_ref.at[c], acc_ref.at[c], row_shift=shifts[c])

    doff = pl.multiple_of(qi * tw, tw)
    s0[0, :, 0:tc] = jnp.where(_causal_mask(tc), logits_fn(0, kside_fn(doff, tc)), NEG_INF)
    rows = tc + lax.broadcasted_iota(jnp.int32, (tc, tw), 0)
    cols = lax.broadcasted_iota(jnp.int32, (tc, tw), 1)
    s0[1] = jnp.where(cols <= rows, logits_fn(1, kside_fn(doff, tw)), NEG_INF)
    pre(0, s1)
    _flash_update(s0.at[0, :, 0:tc], vp_fn(doff, tc), m_ref.at[0], acc_ref.at[0], row_shift=shifts[0])
    _flash_update(s0.at[1], vp_fn(doff, tw), m_ref.at[1], acc_ref.at[1], row_shift=shifts[1])

    def body(u, carry):
        t = 2 * u
        pre(t + 1, s0)
        process(t, s1)
        pre(t + 2, s1)
        process(t + 1, s0)
        return carry

    lax.fori_loop(0, qi // 2, body, 0)

    @pl.when(qi % 2 == 1)
    def _():
        process(qi - 1, s1)


def _sweep_scratch(tc):
    return [pltpu.VMEM((2, tc, 2 * tc), F32), pltpu.VMEM((2, tc, 2 * tc), F32),
            pltpu.VMEM((2, tc, 1), F32), pltpu.VMEM((2, tc, 2 * LANES), F32)]


def _mla_kernel(qn_ref, qr_ref, kn_ref, kr_ref, v_ref, o_ref, s0_ref, s1_ref, m_ref, acc_ref, *, tc):
    qi = pl.program_id(1)
    qs = [jnp.concatenate([qn_ref[0, c * tc:(c + 1) * tc, :], qr_ref[0, c * tc:(c + 1) * tc, :]], axis=1)
          for c in range(2)]
    _init_state(m_ref, acc_ref)

    def kside(off, width):
        return jnp.concatenate([kn_ref[0, pl.ds(off, width), :], kr_ref[pl.ds(off, width), :]], axis=1)

    def logits(c, k):
        return lax.dot_general(qs[c], k, _NT, preferred_element_type=F32)

    def vp(off, width):
        return _with_ones(v_ref[0, pl.ds(off, width), :])

    _causal_sweep(qi, tc, (s0_ref, s1_ref), m_ref, acc_ref, kside, logits, vp)
    for c in range(2):
        o_ref[c * tc:(c + 1) * tc, :] = _normalized(acc_ref[c]).astype(o_ref.dtype)


def _mla_attention(qn, qr, kvh, kr):
    h, s, _ = qn.shape
    tc = min(512, s // 2)
    w = 2 * tc
    return pl.pallas_call(
        functools.partial(_mla_kernel, tc=tc),
        grid=(h, s // w),
        in_specs=[pl.BlockSpec((1, w, LANES), lambda hh, i: (hh, i, 0)),
                  pl.BlockSpec((1, w, LANES), lambda hh, i: (hh, i, 0)),
                  pl.BlockSpec((1, s, LANES), lambda hh, i: (2 * hh, 0, 0)),
                  pl.BlockSpec((s, LANES), lambda hh, i: (0, 0)),
                  pl.BlockSpec((1, s, LANES), lambda hh, i: (2 * hh + 1, 0, 0))],
        out_specs=pl.BlockSpec((w, LANES), lambda hh, i: (i, hh)),
        out_shape=jax.ShapeDtypeStruct((s, h * LANES), BF),
        scratch_shapes=_sweep_scratch(tc),
        compiler_params=_cparams("arbitrary", "arbitrary"),
        name="mla_attn",
    )(qn, qr, kvh, kr, kvh)


def _fox_kernel(q_ref, k_ref, v_ref, ck_ref, cum_ref, og_ref, o_ref, s0_ref, s1_ref, m_ref, acc_ref, *, tc):
    hh = pl.program_id(0)
    qi = pl.program_id(1)
    qs = [q_ref[0, c * tc:(c + 1) * tc, :] for c in range(2)]
    lane = lax.broadcasted_iota(jnp.int32, (tc, LANES), 1)
    cqs = [jnp.sum(jnp.where(lane == hh, cum_ref[c * tc:(c + 1) * tc, :], 0.0), axis=1, keepdims=True)
           for c in range(2)]
    _init_state(m_ref, acc_ref)

    def kside(off, width):
        return k_ref[0, pl.ds(off, width), :], ck_ref[0, :, pl.ds(off, width)]

    def logits(c, kc):
        return lax.dot_general(qs[c], kc[0], _NT, preferred_element_type=F32) - kc[1]

    def vp(off, width):
        return _with_ones(v_ref[0, pl.ds(off, width), :])

    _causal_sweep(qi, tc, (s0_ref, s1_ref), m_ref, acc_ref, kside, logits, vp, row_shifts=cqs)
    for c in range(2):
        rows = slice(c * tc, (c + 1) * tc)
        o_ref[rows, :] = (_normalized(acc_ref[c]) * og_ref[rows, :]).astype(o_ref.dtype)


def _fox_attention(q, k, v, cum_t, cum, sig_og):
    h, s, _ = q.shape
    tc = min(512, s // 2)
    w = 2 * tc
    hm = pl.BlockSpec((1, s, LANES), lambda hh, i: (hh, 0, 0))
    return pl.pallas_call(
        functools.partial(_fox_kernel, tc=tc),
        grid=(h, s // w),
        in_specs=[pl.BlockSpec((1, w, LANES), lambda hh, i: (hh, i, 0)), hm, hm,
                  pl.BlockSpec((1, 1, s), lambda hh, i: (hh, 0, 0)),
                  pl.BlockSpec((w, LANES), lambda hh, i: (i, 0)),
                  pl.BlockSpec((w, LANES), lambda hh, i: (i, hh))],
        out_specs=pl.BlockSpec((w, LANES), lambda hh, i: (i, hh)),
        out_shape=jax.ShapeDtypeStruct((s, h * LANES), BF),
        scratch_shapes=_sweep_scratch(tc),
        compiler_params=_cparams("arbitrary", "arbitrary"),
        name="fox_attn",
    )(q, k, v, cum_t, cum, sig_og)


def _cumsum_kernel(x_ref, o_ref, ot_ref, carry_ref, *, t, out_scale):
    @pl.when(pl.program_id(0) == 0)
    def _():
        carry_ref[...] = jnp.zeros_like(carry_ref)

    x = x_ref[...]
    rows = lax.broadcasted_iota(jnp.int32, (t, t), 0)
    cols = lax.broadcasted_iota(jnp.int32, (t, t), 1)
    tri = jnp.where(cols <= rows, 1.0, 0.0).astype(BF)
    hi = x.astype(BF)
    r1 = x - hi.astype(F32)
    mid = r1.astype(BF)
    lo = (r1 - mid.astype(F32)).astype(BF)
    cum = (jnp.dot(tri, hi, preferred_element_type=F32) + jnp.dot(tri, mid, preferred_element_type=F32)
           + jnp.dot(tri, lo, preferred_element_type=F32)) + carry_ref[...]
    scaled = cum * out_scale
    o_ref[...] = scaled
    ot_ref[...] = scaled.T
    carry_ref[...] = cum[t - 1:t, :]


def _cumsum_tokens(x, out_scale):
    s, n = x.shape
    t = min(256, s)
    return pl.pallas_call(
        functools.partial(_cumsum_kernel, t=t, out_scale=out_scale),
        grid=(s // t,),
        in_specs=[pl.BlockSpec((t, n), lambda i: (i, 0))],
        out_specs=[pl.BlockSpec((t, n), lambda i: (i, 0)), pl.BlockSpec((n, t), lambda i: (0, i))],
        out_shape=[jax.ShapeDtypeStruct((s, n), F32), jax.ShapeDtypeStruct((n, s), F32)],
        scratch_shapes=[pltpu.VMEM((1, n), F32)],
        compiler_params=_cparams("arbitrary"),
        name="cumsum",
    )(x)


def _compress_kernel(a_ref, pos_ref, w1_ref, w2_ref, o_ref):
    a = a_ref[0]
    nc = a.shape[0]
    p1 = jnp.dot((a + pos_ref[0, 0:1, :]).astype(BF), w1_ref[0, 0], preferred_element_type=F32)
    p2 = jnp.dot((a + pos_ref[0, 1:2, :]).astype(BF), w1_ref[0, 1], preferred_element_type=F32)
    h = p1 + pltpu.roll(p2, nc - 1, 0)
    act = h * jax.nn.sigmoid(h)
    o_ref[0] = jnp.dot(act.astype(BF), w2_ref[0], preferred_element_type=F32).astype(o_ref.dtype)


def _nsa_compress(kv_cmp, pos, w1, w2):
    c, nc, gw = kv_cmp.shape
    return pl.pallas_call(
        _compress_kernel,
        grid=(c,),
        in_specs=[pl.BlockSpec((1, nc, gw), lambda i: (i, 0, 0)),
                  pl.BlockSpec((1, 2, gw), lambda i: (i // 2, 0, 0)),
                  pl.BlockSpec((1, 2, gw, CMP_HIDDEN), lambda i: (i // 2, 0, 0, 0)),
                  pl.BlockSpec((1, CMP_HIDDEN, NSA_HEAD_DIM), lambda i: (i // 2, 0, 0))],
        out_specs=pl.BlockSpec((1, nc, NSA_HEAD_DIM), lambda i: (i, 0, 0)),
        out_shape=jax.ShapeDtypeStruct((c, nc, NSA_HEAD_DIM), BF),
        compiler_params=_cparams("arbitrary"),
        name="nsa_compress",
    )(kv_cmp, pos, w1, w2)


def _bias_table_kernel(brev_ref, basc_ref, tbl_ref, tb_ref, asc_ref, *, s, mc):
    tbl = tbl_ref[0] * LOG2E

    def lookup(bkt):
        out = jnp.zeros(bkt.shape, F32)
        for b in range(REL_BUCKETS):
            out = jnp.where(bkt == b, tbl[:, b:b + 1], out)
        return out

    rev = lookup(brev_ref[...])
    asc_ref[0] = lookup(basc_ref[...])
    for m in range(mc + 1):
        win = rev[:, s - Q_BLOCK * m:s - Q_BLOCK * m + 2 * Q_BLOCK]
        rolled = pltpu.roll(jnp.broadcast_to(win, (Q_BLOCK, 2 * Q_BLOCK)), Q_BLOCK + 1, 1,
                            stride=1, stride_axis=0)
        tb_ref[0, m] = rolled[:, :Q_BLOCK]


def _t5_bucket(dist):
    max_exact = REL_BUCKETS // 2
    d = jnp.maximum(dist, 0)
    ratio = jnp.log(jnp.maximum(d, max_exact).astype(F32) / max_exact) / math.log(REL_MAX_DIST / max_exact)
    large = jnp.minimum(max_exact + (ratio * (REL_BUCKETS - max_exact)).astype(jnp.int32), REL_BUCKETS - 1)
    return jnp.where(d < max_exact, d, large)


def _bias_tables(rel_bias, s):
    assert Q_BLOCK == SUBLANES * CMP_STRIDE
    h = rel_bias.shape[1]
    qblocks = s // Q_BLOCK
    max_exact = REL_BUCKETS // 2
    d_const = int(math.ceil(max_exact * (REL_MAX_DIST / max_exact)
                            ** ((REL_BUCKETS - max_exact - 1) / (REL_BUCKETS - max_exact)))) + 1
    mc = min(-(-(d_const + Q_BLOCK - 1) // Q_BLOCK), qblocks - 1)
    nrev = s + 2 * Q_BLOCK
    nasc = s + 3 * Q_BLOCK
    brev = _t5_bucket(s + Q_BLOCK - 1 - jnp.arange(nrev, dtype=jnp.int32))[None, :]
    basc = _t5_bucket(jnp.arange(nasc, dtype=jnp.int32) - 2 * Q_BLOCK)[None, :]
    tb, asc = pl.pallas_call(
        functools.partial(_bias_table_kernel, s=s, mc=mc),
        grid=(h,),
        in_specs=[pl.BlockSpec((1, nrev), lambda i: (0, 0)),
                  pl.BlockSpec((1, nasc), lambda i: (0, 0)),
                  pl.BlockSpec((1, 1, REL_BUCKETS), lambda i: (i, 0, 0))],
        out_specs=[pl.BlockSpec((1, mc + 1, Q_BLOCK, Q_BLOCK), lambda i: (i, 0, 0, 0)),
                   pl.BlockSpec((1, 1, nasc), lambda i: (i, 0, 0))],
        out_shape=[jax.ShapeDtypeStruct((h, mc + 1, Q_BLOCK, Q_BLOCK), F32),
                   jax.ShapeDtypeStruct((h, 1, nasc), F32)],
        compiler_params=_cparams("arbitrary"),
        name="bias_tables",
    )(brev, basc, rel_bias.T.reshape(h, 1, REL_BUCKETS))
    asc = asc[:, 0, :]
    base = 2 * Q_BLOCK - (CMP_LEN - 1)
    segs = [asc[:, base - CMP_STRIDE * nn:base - CMP_STRIDE * nn + s].reshape(h, qblocks, Q_BLOCK)
            for nn in range(SUBLANES)]
    trc = jnp.flip(jnp.stack(segs, axis=2), axis=1).reshape(h, qblocks * SUBLANES, Q_BLOCK)
    return tb, jnp.pad(trc, ((0, 0), (0, qblocks * SUBLANES), (0, 0)))


def _nsa_cmp_kernel(q_ref, kc_ref, vct_ref, trc_ref, smt_ref, gate_ref, oc_ref, sel_ref, *, qblocks, topn, nb):
    for c in range(nb):
        _nsa_cmp_block(pl.program_id(1) * nb + c, slice(c * Q_BLOCK, (c + 1) * Q_BLOCK), q_ref, kc_ref, vct_ref,
                       trc_ref, smt_ref, gate_ref, oc_ref, sel_ref, qblocks=qblocks, topn=topn)


def _nsa_cmp_block(qb, rows, q_ref, kc_ref, vct_ref, trc_ref, smt_ref, gate_ref, oc_ref, sel_ref, *, qblocks, topn):
    rq = NSA_GROUP * Q_BLOCK
    q = q_ref[:, rows, :].reshape(rq, NSA_HEAD_DIM)
    kc = kc_ref[0]
    nc = kc.shape[0]
    nslc = smt_ref.shape[0]
    s = lax.dot_general(kc, q, _NT, preferred_element_type=F32)
    n_io = lax.broadcasted_iota(jnp.int32, (nc, Q_BLOCK), 0)
    i_io = lax.broadcasted_iota(jnp.int32, (nc, Q_BLOCK), 1)
    mask = (n_io * CMP_STRIDE + (CMP_LEN - 1)) <= (qb * Q_BLOCK + i_io)
    off = pl.multiple_of((qblocks - 1 - qb) * SUBLANES, SUBLANES)
    ps = []
    for r in range(NSA_GROUP):
        l = jnp.where(mask, s[:, r * Q_BLOCK:(r + 1) * Q_BLOCK] + trc_ref[r, pl.ds(off, nc), :], NEG_INF)
        m = jnp.max(l, axis=0, keepdims=True)
        p = jnp.where(mask, jnp.exp2(l - m), 0.0)
        den = jnp.maximum(jnp.sum(p, axis=0, keepdims=True), 1e-30)
        ps.append(p / den)
    p_all = jnp.concatenate(ps, axis=1).astype(BF)
    oc_t = jnp.dot(vct_ref[0], p_all, preferred_element_type=F32)
    imp4 = jnp.dot(smt_ref[...], p_all, preferred_element_type=F32)
    imp = imp4[:, 0:Q_BLOCK]
    for r in range(1, NSA_GROUP):
        imp = imp + imp4[:, r * Q_BLOCK:(r + 1) * Q_BLOCK]

    j_io = lax.broadcasted_iota(jnp.int32, (nslc, Q_BLOCK), 0)
    t_io = qb * Q_BLOCK + lax.broadcasted_iota(jnp.int32, (nslc, Q_BLOCK), 1)
    cur = t_io >> int(math.log2(SLC_LEN))
    forced = jnp.logical_or(j_io == 0, jnp.logical_or(j_io == cur, j_io == cur - 1))
    val = jnp.where(forced, 1e9, jnp.where(j_io <= cur, imp, -1e9))
    sel = jnp.zeros((nslc, Q_BLOCK), F32)
    for _ in range(topn):
        mx = jnp.max(val, axis=0, keepdims=True)
        cand = jnp.where(val == mx, j_io, nslc)
        jmin = jnp.min(cand, axis=0, keepdims=True)
        pick = j_io == jmin
        sel = jnp.where(pick, 1.0, sel)
        val = jnp.where(pick, -3e38, val)
    sel_ref[0, rows, :] = jnp.where(sel.T > 0.5, 0.0, NEG_INF).astype(sel_ref.dtype)

    gates = gate_ref[0, rows, :]
    for r in range(NSA_GROUP):
        o_r = oc_t[:, r * Q_BLOCK:(r + 1) * Q_BLOCK].T
        oc_ref[rows, r * NSA_HEAD_DIM:(r + 1) * NSA_HEAD_DIM] = o_r * gates[:, 3 * r:3 * r + 1]


def _nsa_cmp(q, kc, vct, trc, smt, gates):
    h, s, d = q.shape
    g = NSA_KV_HEADS
    qblocks = s // Q_BLOCK
    nc = kc.shape[1]
    nslc = smt.shape[0]
    topn = min(SLC_TOPN, nslc)
    nb = 2
    qrows = nb * Q_BLOCK
    return pl.pallas_call(
        functools.partial(_nsa_cmp_kernel, qblocks=qblocks, topn=topn, nb=nb),
        grid=(g, qblocks // nb),
        in_specs=[pl.BlockSpec((NSA_GROUP, qrows, d), lambda gg, i: (gg, i, 0)),
                  pl.BlockSpec((1, nc, d), lambda gg, i: (gg, 0, 0)),
                  pl.BlockSpec((1, d, nc), lambda gg, i: (gg, 0, 0)),
                  pl.BlockSpec((NSA_GROUP, trc.shape[1], Q_BLOCK), lambda gg, i: (gg, 0, 0)),
                  pl.BlockSpec(smt.shape, lambda gg, i: (0, 0)),
                  pl.BlockSpec((1, qrows, LANES), lambda gg, i: (gg, i, 0))],
        out_specs=[pl.BlockSpec((qrows, NSA_GROUP * d), lambda gg, i: (i, gg)),
                   pl.BlockSpec((1, qrows, nslc), lambda gg, i: (gg, i, 0))],
        out_shape=[jax.ShapeDtypeStruct((s, h * d), F32),
                   jax.ShapeDtypeStruct((g, s, nslc), BF)],
        compiler_params=_cparams("arbitrary", "arbitrary"),
        name="nsa_cmp",
    )(q, kc, vct, trc, smt, gates)


def _nsa_sw_kernel(q_ref, ks_ref, vs_ref, kw_ref, vw_ref, tb_ref, sel_ref, et_ref, gate_ref, oc_ref,
                   o_ref, s0_ref, s1_ref, m_ref, acc_ref, *, tk, mc, nch):
    i = pl.program_id(1)
    rq = NSA_GROUP * Q_BLOCK
    nsub = tk // Q_BLOCK
    qbs = [i * nch + c for c in range(nch)]
    qs = [q_ref[:, c * Q_BLOCK:(c + 1) * Q_BLOCK, :].reshape(rq, NSA_HEAD_DIM) for c in range(nch)]
    sels = [sel_ref[0, c * Q_BLOCK:(c + 1) * Q_BLOCK, :] for c in range(nch)]

    def biased(c, s, msk, kb0, nblk):
        parts = []
        for r in range(NSA_GROUP):
            bias = jnp.concatenate(
                [tb_ref[r, jnp.clip(qbs[c] - (kb0 + b), 0, mc)] for b in range(nblk)], axis=1)
            parts.append(jnp.where(msk, s[r * Q_BLOCK:(r + 1) * Q_BLOCK, :] + bias, NEG_INF))
        return jnp.concatenate(parts, axis=0)

    def gated(c, o, branch, base):
        gates = gate_ref[0, c * Q_BLOCK:(c + 1) * Q_BLOCK, :]
        outs = []
        for r in range(NSA_GROUP):
            o_r = o[r * Q_BLOCK:(r + 1) * Q_BLOCK, :] * gates[:, 3 * r + branch:3 * r + branch + 1]
            outs.append(base[:, r * NSA_HEAD_DIM:(r + 1) * NSA_HEAD_DIM] + o_r)
        return jnp.concatenate(outs, axis=1)

    _init_state(m_ref, acc_ref)
    qa = [jnp.concatenate([qs[c], jnp.concatenate([sels[c]] * NSA_GROUP, axis=0)], axis=1) for c in range(nch)]

    def pre(j, s_ref, causal):
        off = pl.multiple_of(j * tk, tk)
        ka = jnp.concatenate([ks_ref[0, pl.ds(off, tk), :], et_ref[pl.ds(off, tk), :]], axis=1)
        for c in range(nch):
            s = lax.dot_general(qa[c], ka, _NT, preferred_element_type=F32)
            if causal:
                kk = off + lax.broadcasted_iota(jnp.int32, (Q_BLOCK, tk), 1)
                ii = qbs[c] * Q_BLOCK + lax.broadcasted_iota(jnp.int32, (Q_BLOCK, tk), 0)
                future = jnp.where(kk <= ii, 0.0, NEG_INF)
            for r in range(NSA_GROUP):
                bias = jnp.concatenate(
                    [tb_ref[r, jnp.clip(qbs[c] - (j * nsub + b), 0, mc)] for b in range(nsub)], axis=1)
                if causal:
                    bias = bias + future
                s_ref[c, r * Q_BLOCK:(r + 1) * Q_BLOCK, :] = s[r * Q_BLOCK:(r + 1) * Q_BLOCK, :] + bias

    def process(j, s_ref):
        vp = _with_ones(vs_ref[0, pl.ds(pl.multiple_of(j * tk, tk), tk), :])
        for c in range(nch):
            _flash_update(s_ref.at[c], vp, m_ref.at[c], acc_ref.at[c])

    n_past = (i * nch) // nsub

    def body(u, carry):
        t = 2 * u
        pre(t + 1, s0_ref, False)
        process(t, s1_ref)
        pre(t + 2, s1_ref, False)
        process(t + 1, s0_ref)
        return carry

    pre(n_past, s0_ref, True)
    pre(0, s1_ref, False)
    process(n_past, s0_ref)
    lax.fori_loop(0, n_past // 2, body, 0)

    @pl.when(n_past % 2 == 1)
    def _():
        process(n_past - 1, s1_ref)

    outs = [gated(c, _normalized(acc_ref[c]), 1, oc_ref[c * Q_BLOCK:(c + 1) * Q_BLOCK, :]) for c in range(nch)]

    wk = WINDOW + Q_BLOCK
    wblk = wk // Q_BLOCK
    for c in range(nch):
        kb0 = jnp.maximum(qbs[c] - WINDOW // Q_BLOCK, 0)
        off = pl.multiple_of(kb0 * Q_BLOCK, Q_BLOCK)
        s = lax.dot_general(qs[c], kw_ref[0, pl.ds(off, wk), :], _NT, preferred_element_type=F32)
        rel = (qbs[c] * Q_BLOCK + lax.broadcasted_iota(jnp.int32, (Q_BLOCK, wk), 0)) - (
            off + lax.broadcasted_iota(jnp.int32, (Q_BLOCK, wk), 1))
        msk = jnp.logical_and(rel >= 0, rel < WINDOW)
        l = biased(c, s, msk, kb0, wblk)
        p = jnp.exp2(l - jnp.max(l, axis=1, keepdims=True)).astype(BF)
        ow = jnp.dot(p, _with_ones(vw_ref[0, pl.ds(off, wk), :]), preferred_element_type=F32)
        o_ref[c * Q_BLOCK:(c + 1) * Q_BLOCK, :] = gated(c, _normalized(ow), 2, outs[c]).astype(o_ref.dtype)


def _nsa_sw(q, kvsw, tb, sel, et, gates, oc):
    h, s, d = q.shape
    g = NSA_KV_HEADS
    nch = 2
    qrows = nch * Q_BLOCK
    tk = min(512, s)
    mc = tb.shape[1] - 1
    nslc = sel.shape[2]
    rq = NSA_GROUP * Q_BLOCK
    assert tk % qrows == 0 and WINDOW + Q_BLOCK <= s

    def kv(slot):
        return pl.BlockSpec((1, s, d), lambda gg, i, slot=slot: (slot + gg, 0, 0))

    return pl.pallas_call(
        functools.partial(_nsa_sw_kernel, tk=tk, mc=mc, nch=nch),
        grid=(g, s // qrows),
        in_specs=[pl.BlockSpec((NSA_GROUP, qrows, d), lambda gg, i: (gg, i, 0)),
                  kv(0), kv(2), kv(4), kv(6),
                  pl.BlockSpec((NSA_GROUP,) + tb.shape[1:], lambda gg, i: (gg, 0, 0, 0)),
                  pl.BlockSpec((1, qrows, nslc), lambda gg, i: (gg, i, 0)),
                  pl.BlockSpec(et.shape, lambda gg, i: (0, 0)),
                  pl.BlockSpec((1, qrows, LANES), lambda gg, i: (gg, i, 0)),
                  pl.BlockSpec((qrows, NSA_GROUP * d), lambda gg, i: (i, gg))],
        out_specs=pl.BlockSpec((qrows, NSA_GROUP * d), lambda gg, i: (i, gg)),
        out_shape=jax.ShapeDtypeStruct((s, h * d), BF),
        scratch_shapes=[pltpu.VMEM((nch, rq, tk), F32), pltpu.VMEM((nch, rq, tk), F32),
                        pltpu.VMEM((nch, rq, 1), F32), pltpu.VMEM((nch, rq, 2 * LANES), F32)],
        compiler_params=_cparams("arbitrary", "arbitrary"),
        name="nsa_sel_win",
    )(q, kvsw, kvsw, kvsw, kvsw, tb, sel, et, gates, oc)


def _selection_map_t(nc, nslc):
    n = np.arange(nc)[None, :] * CMP_STRIDE
    j0 = np.arange(nslc)[:, None] * SLC_LEN
    valid = np.arange(nc)[None, :] < nc - 1
    return jnp.asarray(((n < j0 + SLC_LEN) & (n + CMP_LEN > j0) & valid).astype(np.float32), dtype=BF)


def _block_onehot(s, nslc):
    tok = np.arange(s)[:, None]
    j = np.arange(nslc)[None, :]
    return jnp.asarray((tok // SLC_LEN == j).astype(np.float32), dtype=BF)


def _rope_tables(positions):
    half = MLA_ROPE // 2
    inv = ROPE_THETA ** (-jnp.arange(half, dtype=F32) / half)
    ang = positions.astype(F32)[:, None] * inv
    cos, sin = jnp.cos(ang), jnp.sin(ang)
    z = jnp.zeros_like(cos)
    zpad = jnp.zeros((positions.shape[0], LANES - MLA_ROPE), F32)
    c = jnp.concatenate([cos, cos, zpad], axis=1)
    s1 = jnp.concatenate([-sin, z, zpad], axis=1)
    s2 = jnp.concatenate([z, sin, zpad], axis=1)
    return c, s1, s2


def _pad_cols(w, n):
    return jnp.pad(w, ((0, 0), (0, n - w.shape[1])))


def _pad_rows(w, n):
    return jnp.pad(w, ((0, n - w.shape[0]), (0, 0)))


def _nsa_mla_mixer(hn, pos, rel_bias, w_in_all, w_out_all, e, gate_b, pos_k, w1_k, w2_k, pos_v, w1_v, w2_v,
                   q_norm, w_uq, kv_norm, w_ukv, x, g1):
    s = hn.shape[0]
    d = NSA_HEAD_DIM
    nq = NSA_HEADS * d
    nkv = 2 * NSA_KV_HEADS * d
    w_in_t = jnp.swapaxes(w_in_all, 1, 2)
    dm = w_in_t.shape[2]
    o0 = nq + 3 * nkv
    w_g_t = w_in_t[e, o0:o0 + 3 * NSA_HEADS]; o0 += 3 * NSA_HEADS
    w_lat_t = w_in_t[e, o0:o0 + MLA_Q_RANK + MLA_KV_RANK]; o0 += MLA_Q_RANK + MLA_KV_RANK
    w_kr_t = w_in_t[e, o0:o0 + MLA_ROPE]

    q_nsa = _proj([hn], [(w_in_t, dm, (e, 0), 0)], functools.partial(_epi_scale, d ** -0.5 * LOG2E), n=nq,
                  out_dtype=BF, head_major=True, tn=1024, w_transposed=True, name="proj_q_nsa")
    kv_cmp = _proj([hn], [(w_in_t, dm, (e, 0), nq)], _epi_id, n=nkv, out_dtype=F32, head_major=True,
                   w_transposed=True, name="proj_kv_cmp")
    kv_sw = _proj([hn], [(w_in_t, dm, (e, 0), nq + nkv)], _epi_id, n=2 * nkv, out_dtype=BF, head_major=True,
                  w_transposed=True, name="proj_kv_sw")
    per_g = 3 * NSA_GROUP
    w_gp_t = jnp.concatenate([_pad_rows(w_g_t[g * per_g:(g + 1) * per_g], LANES) for g in range(NSA_KV_HEADS)], 0)
    b_gp = jnp.concatenate([_pad_cols(gate_b[None, g * per_g:(g + 1) * per_g], LANES)
                            for g in range(NSA_KV_HEADS)], 1)
    gates = _proj([hn], [(w_gp_t[None], dm, (0, 0), 0)], _epi_sigmoid_bias, [(b_gp, "row")],
                  n=NSA_KV_HEADS * LANES, out_dtype=F32, head_major=True, tn=LANES, w_transposed=True,
                  name="proj_gates")

    nc = s // CMP_STRIDE
    gw = CMP_STRIDE * d
    pos_kv = jnp.stack([pos_k.reshape(2, gw), pos_v.reshape(2, gw)])
    w1_kv = jnp.stack([w1_k.reshape(2, gw, CMP_HIDDEN), w1_v.reshape(2, gw, CMP_HIDDEN)]).astype(BF)
    w2_kv = jnp.stack([w2_k, w2_v]).astype(BF)
    kvc = _nsa_compress(kv_cmp.reshape(2 * NSA_KV_HEADS, nc, gw), pos_kv, w1_kv, w2_kv)
    kc = kvc[:NSA_KV_HEADS]
    vct = jnp.swapaxes(kvc[NSA_KV_HEADS:], 1, 2)

    nslc = s // SLC_LEN
    tb, trc = _bias_tables(rel_bias, s)
    smt = _selection_map_t(nc, nslc)
    oc, sel = _nsa_cmp(q_nsa, kc, vct, trc, smt, gates)
    o_nsa = _nsa_sw(q_nsa, kv_sw, tb, sel, _block_onehot(s, nslc), gates, oc)

    nlat = MLA_Q_RANK + MLA_KV_RANK
    lat = _proj([hn], [(w_lat_t[None], dm, (0, 0), 0)], _epi_rmsnorm,
                [(jnp.concatenate([q_norm, kv_norm])[None, :], "row")], n=nlat, out_dtype=BF, tn=MLA_Q_RANK,
                w_transposed=True, name="proj_mla_latent")
    c, s1, s2 = _rope_tables(pos)
    kr = _proj([hn], [(_pad_rows(w_kr_t, LANES)[None], dm, (0, 0), 0)], functools.partial(_epi_rope, 1.0),
               [(c, "rowtile"), (s1, "rowtile"), (s2, "rowtile")], n=LANES, out_dtype=BF, w_transposed=True,
               name="proj_mla_kr")
    scale = (MLA_NOPE + MLA_ROPE) ** -0.5 * LOG2E
    w_uq3 = w_uq.reshape(MLA_Q_RANK, MLA_HEADS, MLA_NOPE + MLA_ROPE)
    w_uqn = w_uq3[:, :, :MLA_NOPE].reshape(MLA_Q_RANK, MLA_HEADS * MLA_NOPE)
    w_uqr = jnp.pad(w_uq3[:, :, MLA_NOPE:], ((0, 0), (0, 0), (0, LANES - MLA_ROPE))).reshape(
        MLA_Q_RANK, MLA_HEADS * LANES)
    qn = _proj([lat], [(w_uqn, MLA_Q_RANK, 0, 0)], functools.partial(_epi_scale, scale),
               n=MLA_HEADS * MLA_NOPE, out_dtype=BF, head_major=True, tn=1024, lhs_col_block=[0],
               name="proj_mla_qn")
    qr = _proj([lat], [(w_uqr, MLA_Q_RANK, 0, 0)], functools.partial(_epi_rope, scale),
               [(c, "rowtile"), (s1, "rowtile"), (s2, "rowtile")], n=MLA_HEADS * LANES, out_dtype=BF,
               head_major=True, lhs_col_block=[0], name="proj_mla_qr")
    kvh = _proj([lat], [(w_ukv, MLA_KV_RANK, 0, 0)], _epi_id, n=MLA_HEADS * (MLA_NOPE + MLA_V), out_dtype=BF,
                head_major=True, tn=1024, lhs_col_block=[1], name="proj_mla_kv")
    o_mla = _mla_attention(qn, qr, kvh, kr)

    return _proj([o_nsa, o_mla], [(w_out_all, nq, (e, 0), 0), (w_out_all, MLA_HEADS * MLA_V, (e, 1), 0)],
                 _epi_residual, [(x, "tile"), (g1, "row")], n=w_out_all.shape[2], out_dtype=F32, tn=1024,
                 name="proj_even_out")


def _fox_mixer(hn, w_in_all, w_out_all, o, f_b, q_norm, k_norm, x, g1):
    d = D_MODEL
    dh = FOX_HEAD_DIM
    w_in_t = jnp.swapaxes(w_in_all, 1, 2)
    w_f_t = w_in_t[o, 3 * d:3 * d + FOX_HEADS]
    w_og_t = w_in_t[o, 3 * d + FOX_HEADS:]
    q = _proj([hn], [(w_in_t, d, (o, 0), 0)], functools.partial(_epi_headnorm, dh ** -0.5 * LOG2E),
              [(q_norm[None, :], "const")], n=d, out_dtype=BF, head_major=True, tn=1024, w_transposed=True,
              name="proj_fox_q")
    k = _proj([hn], [(w_in_t, d, (o, 0), d)], functools.partial(_epi_headnorm, 1.0),
              [(k_norm[None, :], "const")], n=d, out_dtype=BF, head_major=True, tn=1024, w_transposed=True,
              name="proj_fox_k")
    v = _proj([hn], [(w_in_t, d, (o, 0), 2 * d)], _epi_id, n=d, out_dtype=BF, head_major=True, tn=1024,
              w_transposed=True, name="proj_fox_v")
    lf = _proj([hn], [(_pad_rows(w_f_t, LANES)[None], d, (0, 0), 0)], _epi_logsigmoid_bias,
               [(_pad_cols(f_b[None, :], LANES), "row")], n=LANES, out_dtype=F32, w_transposed=True,
               name="proj_fox_f")
    sig_og = _proj([hn], [(w_og_t[None], d, (0, 0), 0)], _epi_sigmoid, n=d, out_dtype=F32, tn=1024,
                   w_transposed=True, name="proj_fox_og")
    cum, cum_t = _cumsum_tokens(lf, LOG2E)
    cum_t = cum_t[:FOX_HEADS].reshape(FOX_HEADS, 1, -1)
    att = _fox_attention(q, k, v, cum_t, cum, sig_og)
    return _proj([att], [(w_out_all, d, (o, 0), 0)], _epi_residual, [(x, "tile"), (g1, "row")], n=d,
                 out_dtype=F32, tn=1024, name="proj_fox_out")


def kernel(x, c, positions, rel_bias, ada_w, ada_b, norm_mix, norm_ffn, ffn_w1, ffn_w3, ffn_w2, even_w_in, even_w_out, nsa_gate_b, nsa_cmp_pos_k, nsa_cmp_w1_k, nsa_cmp_w2_k, nsa_cmp_pos_v, nsa_cmp_w1_v, nsa_cmp_w2_v, mla_q_norm, mla_w_uq, mla_kv_norm, mla_w_ukv, fox_w_in, fox_w_out, fox_f_b, fox_q_norm, fox_k_norm, final_norm):
    b, s, d = x.shape
    assert b == 1 and d == D_MODEL and s % 1024 == 0
    xs = x[0]
    pos = positions[0]
    mod = _adaln(c, ada_w, ada_b)
    depth = ada_w.shape[0]
    for i in range(depth):
        sh1, sc1, g1, sh2, sc2, g2 = [mod[i:i + 1, k * d:(k + 1) * d] for k in range(6)]
        hn = _normmod(xs, norm_mix[i][None, :], sc1, sh1)
        if i % 2 == 0:
            e = i // 2
            xs = _nsa_mla_mixer(hn, pos, rel_bias, even_w_in, even_w_out, e, nsa_gate_b[e],
                                nsa_cmp_pos_k[e], nsa_cmp_w1_k[e], nsa_cmp_w2_k[e],
                                nsa_cmp_pos_v[e], nsa_cmp_w1_v[e], nsa_cmp_w2_v[e],
                                mla_q_norm[e], mla_w_uq[e], mla_kv_norm[e], mla_w_ukv[e], xs, g1)
        else:
            o = i // 2
            xs = _fox_mixer(hn, fox_w_in, fox_w_out, o, fox_f_b[o], fox_q_norm[o], fox_k_norm[o], xs, g1)
        fin = final_norm[None, :] if i == depth - 1 else None
        xs = _ffn(xs, norm_ffn[i][None, :], sc2, sh2, g2, ffn_w1, ffn_w3, ffn_w2, i, fin)
    return xs[None]
```

```python
import functools
import math

import numpy as np
import jax
import jax.numpy as jnp
from jax import lax
from jax.experimental import pallas as pl
from jax.experimental.pallas import tpu as pltpu

D_MODEL = 2048
DEPTH = 2
EPS = 1e-6
NEG_INF = -1e30

NSA_HEADS = 8
NSA_KV_HEADS = 2
NSA_GROUP = NSA_HEADS // NSA_KV_HEADS
NSA_HEAD_DIM = 128
CMP_LEN = 32
CMP_STRIDE = 16
CMP_HIDDEN = 256
SLC_LEN = 64
SLC_TOPN = 16
WINDOW = 512
Q_BLOCK = 128

MLA_HEADS = 8
MLA_Q_RANK = 512
MLA_KV_RANK = 512
MLA_NOPE = 128
MLA_ROPE = 64
MLA_V = 128
ROPE_THETA = 10000.0

FOX_HEADS = 16
FOX_HEAD_DIM = D_MODEL // FOX_HEADS

REL_BUCKETS = 32
REL_MAX_DIST = 4096

FFN_HIDDEN = ((8 * D_MODEL + 2) // 3 + 255) // 256 * 256

LANES = 128
SUBLANES = 8
VMEM_LIMIT_BYTES = 56 * 1024 * 1024
LOG2E = math.log2(math.e)

BF = jnp.bfloat16
F32 = jnp.float32
_NT = (((1,), (1,)), ((), ()))


def _cparams(*sem):
    return pltpu.CompilerParams(dimension_semantics=sem, vmem_limit_bytes=VMEM_LIMIT_BYTES)


def _rms(x):
    return x * lax.rsqrt(jnp.mean(x * x, axis=-1, keepdims=True) + EPS)


def _adaln_kernel(c_ref, w_ref, b_ref, o_ref):
    c = c_ref[...]
    cond = c * jax.nn.sigmoid(c)
    acc = jnp.dot(cond.astype(BF), w_ref[0].astype(BF), preferred_element_type=F32)
    o_ref[0] = acc + b_ref[0]


def _adaln(c, ada_w, ada_b):
    depth, d, n = ada_w.shape
    tn = 1024
    c8 = jnp.broadcast_to(c, (SUBLANES, d))
    out = pl.pallas_call(
        _adaln_kernel,
        grid=(depth, n // tn),
        in_specs=[pl.BlockSpec((SUBLANES, d), lambda i, j: (0, 0)),
                  pl.BlockSpec((1, d, tn), lambda i, j: (i, 0, j)),
                  pl.BlockSpec((1, 1, tn), lambda i, j: (i, 0, j))],
        out_specs=pl.BlockSpec((1, SUBLANES, tn), lambda i, j: (i, 0, j)),
        out_shape=jax.ShapeDtypeStruct((depth, SUBLANES, n), F32),
        compiler_params=_cparams("arbitrary", "arbitrary"),
        name="adaln",
    )(c8, ada_w, ada_b.reshape(depth, 1, n))
    return out[:, 0, :]


def _normmod_kernel(x_ref, g_ref, sc_ref, sh_ref, o_ref):
    y = _rms(x_ref[...])
    o_ref[...] = ((y * g_ref[...]) * (1.0 + sc_ref[...]) + sh_ref[...]).astype(o_ref.dtype)


def _normmod(x, g, sc, sh):
    s, d = x.shape
    tm = 512
    row = pl.BlockSpec((1, d), lambda i: (0, 0))
    return pl.pallas_call(
        _normmod_kernel,
        grid=(s // tm,),
        in_specs=[pl.BlockSpec((tm, d), lambda i: (i, 0)), row, row, row],
        out_specs=pl.BlockSpec((tm, d), lambda i: (i, 0)),
        out_shape=jax.ShapeDtypeStruct((s, d), BF),
        compiler_params=_cparams("arbitrary"),
        name="normmod",
    )(x, g, sc, sh)


def _proj_kernel(*refs, n_lhs, n_epi, epi, head_major, w_transposed):
    lhs = refs[:n_lhs]
    ws = refs[n_lhs:2 * n_lhs]
    epis = refs[2 * n_lhs:2 * n_lhs + n_epi]
    o_ref = refs[2 * n_lhs + n_epi]
    wbf = refs[2 * n_lhs + n_epi + 1:]

    @pl.when(pl.program_id(1) == 0)
    def _():
        for w, wb in zip(ws, wbf):
            wb[...] = w[...].astype(BF)

    acc = None
    for a, wb in zip(lhs, wbf):
        if w_transposed:
            d = lax.dot_general(a[...], wb[...], _NT, preferred_element_type=F32)
        else:
            d = jnp.dot(a[...], wb[...], preferred_element_type=F32)
        acc = d if acc is None else acc + d
    res = epi(acc, *[e[...] for e in epis])
    if head_major:
        for r in range(o_ref.shape[0]):
            o_ref[r] = res[:, r * LANES:(r + 1) * LANES].astype(o_ref.dtype)
    else:
        o_ref[...] = res.astype(o_ref.dtype)


def _proj(lhs, ws, epi, epi_in=(), *, n, out_dtype, head_major=False, tm=1024, tn=512,
          lhs_col_block=None, w_transposed=False, name="proj"):
    m = lhs[0].shape[0]
    tm = min(tm, m)
    tn = min(tn, n)
    if lhs_col_block is None:
        lhs_col_block = [0] * len(lhs)
    in_specs = []
    for (_, k, _, _), cb in zip(ws, lhs_col_block):
        in_specs.append(pl.BlockSpec((tm, k), lambda j, i, cb=cb: (i, cb)))
    for arr, k, rb, col0 in ws:
        assert col0 % tn == 0
        if w_transposed:
            in_specs.append(pl.BlockSpec((None, tn, k),
                                         lambda j, i, rb=rb, cb0=col0 // tn: (rb[0], cb0 + j, rb[1])))
        elif arr.ndim == 3:
            in_specs.append(pl.BlockSpec((None, k, tn),
                                         lambda j, i, rb=rb, cb0=col0 // tn: (rb[0], rb[1], cb0 + j)))
        else:
            in_specs.append(pl.BlockSpec((k, tn), lambda j, i, rb=rb, cb0=col0 // tn: (rb, cb0 + j)))
    arrays = list(lhs) + [w[0] for w in ws]
    for arr, kind in epi_in:
        if kind == "row":
            in_specs.append(pl.BlockSpec((1, tn), lambda j, i: (0, j)))
        elif kind == "const":
            in_specs.append(pl.BlockSpec(arr.shape, lambda j, i: (0, 0)))
        elif kind == "tile":
            in_specs.append(pl.BlockSpec((tm, tn), lambda j, i: (i, j)))
        elif kind == "rowtile":
            in_specs.append(pl.BlockSpec((tm, arr.shape[1]), lambda j, i: (i, 0)))
        else:
            raise ValueError(kind)
        arrays.append(arr)
    if head_major:
        hpt = tn // LANES
        out_spec = pl.BlockSpec((hpt, tm, LANES), lambda j, i: (j, i, 0))
        out_shape = jax.ShapeDtypeStruct((n // LANES, m, LANES), out_dtype)
    else:
        out_spec = pl.BlockSpec((tm, tn), lambda j, i: (i, j))
        out_shape = jax.ShapeDtypeStruct((m, n), out_dtype)
    kern = functools.partial(_proj_kernel, n_lhs=len(lhs), n_epi=len(epi_in), epi=epi,
                             head_major=head_major, w_transposed=w_transposed)
    return pl.pallas_call(
        kern,
        grid=(n // tn, m // tm),
        in_specs=in_specs,
        out_specs=out_spec,
        out_shape=out_shape,
        scratch_shapes=[pltpu.VMEM((tn, k) if w_transposed else (k, tn), BF) for _, k, _, _ in ws],
        compiler_params=_cparams("arbitrary", "arbitrary"),
        name=name,
    )(*arrays)


def _epi_id(acc):
    return acc


def _epi_scale(scale, acc):
    return acc * scale


def _epi_sigmoid_bias(acc, b):
    return jax.nn.sigmoid(acc + b)


def _epi_logsigmoid_bias(acc, b):
    return jax.nn.log_sigmoid(acc + b)


def _epi_sigmoid(acc):
    return jax.nn.sigmoid(acc)


def _epi_rmsnorm(acc, g):
    return _rms(acc) * g


def _epi_headnorm(scale, acc, g):
    outs = []
    for r in range(acc.shape[1] // LANES):
        outs.append(_rms(acc[:, r * LANES:(r + 1) * LANES]) * g * scale)
    return jnp.concatenate(outs, axis=1)


def _epi_rope(scale, acc, c, s1, s2):
    reps = acc.shape[1] // LANES
    half = MLA_ROPE // 2
    if reps > 1:
        c = jnp.concatenate([c] * reps, axis=1)
        s1 = jnp.concatenate([s1] * reps, axis=1)
        s2 = jnp.concatenate([s2] * reps, axis=1)
    n = acc.shape[1]
    out = acc * c + pltpu.roll(acc, n - half, 1) * s1 + pltpu.roll(acc, half, 1) * s2
    return out * scale


def _epi_residual(acc, x, g):
    return x + g * acc


def _ffn_kernel(*refs, final):
    if final:
        x_ref, g_ref, sc_ref, sh_ref, g2_ref, w1_ref, w3_ref, w2_ref, fn_ref, o_ref, hn_ref = refs
    else:
        x_ref, g_ref, sc_ref, sh_ref, g2_ref, w1_ref, w3_ref, w2_ref, o_ref, hn_ref = refs
    f = pl.program_id(1)
    tm = x_ref.shape[0]
    halves = [slice(r * (tm // 2), (r + 1) * (tm // 2)) for r in range(2)]

    @pl.when(f == 0)
    def _():
        for rows in halves:
            x = x_ref[rows, :]
            hn_ref[rows, :] = ((_rms(x) * g_ref[...]) * (1.0 + sc_ref[...]) + sh_ref[...]).astype(BF)
            o_ref[rows, :] = x

    w1 = w1_ref[...].astype(BF)
    w3 = w3_ref[...].astype(BF)
    w2 = w2_ref[...].astype(BF)
    for rows in halves:
        h = hn_ref[rows, :]
        h1 = jnp.dot(h, w1, preferred_element_type=F32)
        h3 = jnp.dot(h, w3, preferred_element_type=F32)
        a = (h1 * jax.nn.sigmoid(h1)) * h3
        o_ref[rows, :] += g2_ref[...] * jnp.dot(a.astype(BF), w2, preferred_element_type=F32)

    if final:
        @pl.when(f == pl.num_programs(1) - 1)
        def _():
            for rows in halves:
                o_ref[rows, :] = _rms(o_ref[rows, :]) * fn_ref[...]


def _ffn(x, g, sc, sh, g2, w1, w3, w2, layer, final_gain=None):
    s, d = x.shape
    fdim = w1.shape[2]
    tm = min(1024, s)
    tf = 256
    row = pl.BlockSpec((1, d), lambda i, f: (0, 0))
    in_specs = [pl.BlockSpec((tm, d), lambda i, f: (i, 0), pipeline_mode=pl.Buffered(1)), row, row, row, row,
                pl.BlockSpec((None, d, tf), lambda i, f: (layer, 0, f)),
                pl.BlockSpec((None, d, tf), lambda i, f: (layer, 0, f)),
                pl.BlockSpec((None, tf, d), lambda i, f: (layer, f, 0))]
    arrays = [x, g, sc, sh, g2, w1, w3, w2]
    if final_gain is not None:
        in_specs.append(row)
        arrays.append(final_gain)
    return pl.pallas_call(
        functools.partial(_ffn_kernel, final=final_gain is not None),
        grid=(s // tm, fdim // tf),
        in_specs=in_specs,
        out_specs=pl.BlockSpec((tm, d), lambda i, f: (i, 0)),
        out_shape=jax.ShapeDtypeStruct((s, d), F32),
        scratch_shapes=[pltpu.VMEM((tm, d), BF)],
        compiler_params=_cparams("arbitrary", "arbitrary"),
        name="ffn",
    )(*arrays)


def _flash_update(s, vp, m_ref, acc_ref, row_shift=None):
    m_prev = m_ref[...]
    m_tile = jnp.max(s[...], axis=1, keepdims=True)
    if row_shift is not None:
        m_tile = m_tile + row_shift
    m_new = jnp.maximum(m_prev, m_tile)
    alpha = jnp.exp2(m_prev - m_new)
    sub = m_new if row_shift is None else m_new - row_shift
    p = jnp.exp2(s[...] - sub).astype(BF)
    acc_ref[...] = alpha * acc_ref[...] + jnp.dot(p, vp, preferred_element_type=F32)
    m_ref[...] = m_new


def _init_state(m_ref, acc_ref):
    m_ref[...] = jnp.full_like(m_ref, NEG_INF)
    acc_ref[...] = jnp.zeros_like(acc_ref)


def _with_ones(v):
    return jnp.concatenate([v, jnp.ones(v.shape, v.dtype)], axis=1)


def _normalized(acc):
    return acc[:, :LANES] / acc[:, LANES:]


def _causal_mask(t):
    rows = lax.broadcasted_iota(jnp.int32, (t, t), 0)
    cols = lax.broadcasted_iota(jnp.int32, (t, t), 1)
    return cols <= rows


def _causal_sweep(qi, tc, s_bufs, m_ref, acc_ref, streams):
    tw = 2 * tc

    def pre(si, t):
        kside_fn, logits_fn, _, _ = streams[si]
        kside = kside_fn(pl.multiple_of(t * tw, tw), tw)
        for c in range(2):
            s_bufs[si][c] = logits_fn(c, kside)

    def process(si, t):
        _, _, vp_fn, shifts = streams[si]
        vp = vp_fn(pl.multiple_of(t * tw, tw), tw)
        for c in range(2):
            _flash_update(s_bufs[si].at[c], vp, m_ref.at[2 * si + c], acc_ref.at[2 * si + c], row_shift=shifts[c])

    doff = pl.multiple_of(qi * tw, tw)

    def pre_diag(si):
        kside_fn, logits_fn, _, _ = streams[si]
        s_bufs[si][0, :, 0:tc] = jnp.where(_causal_mask(tc), logits_fn(0, kside_fn(doff, tc)), NEG_INF)
        rows = tc + lax.broadcasted_iota(jnp.int32, (tc, tw), 0)
        cols = lax.broadcasted_iota(jnp.int32, (tc, tw), 1)
        s_bufs[si][1] = jnp.where(cols <= rows, logits_fn(1, kside_fn(doff, tw)), NEG_INF)

    def process_diag(si):
        _, _, vp_fn, shifts = streams[si]
        _flash_update(s_bufs[si].at[0, :, 0:tc], vp_fn(doff, tc), m_ref.at[2 * si], acc_ref.at[2 * si],
                      row_shift=shifts[0])
        _flash_update(s_bufs[si].at[1], vp_fn(doff, tw), m_ref.at[2 * si + 1], acc_ref.at[2 * si + 1],
                      row_shift=shifts[1])

    pre_diag(0)
    pre_diag(1)
    process_diag(0)
    pre(0, 0)
    process_diag(1)

    def body(t, carry):
        pre(1, t)
        process(0, t)
        pre(0, t + 1)
        process(1, t)
        return carry

    lax.fori_loop(0, qi, body, 0)


def _sweep_scratch(tc):
    return [pltpu.VMEM((2, tc, 2 * tc), F32), pltpu.VMEM((2, tc, 2 * tc), F32),
            pltpu.VMEM((4, tc, 1), F32), pltpu.VMEM((4, tc, 2 * LANES), F32)]


def _mla_kernel(qn_ref, qr_ref, kv_ref, kr_ref, o_ref, s0_ref, s1_ref, m_ref, acc_ref, *, tc):
    qi = pl.program_id(1)
    _init_state(m_ref, acc_ref)

    def stream(si):
        qs = [jnp.concatenate([qn_ref[si, c * tc:(c + 1) * tc, :], qr_ref[si, c * tc:(c + 1) * tc, :]], axis=1)
              for c in range(2)]

        def kside(off, width):
            return jnp.concatenate([kv_ref[2 * si, pl.ds(off, width), :], kr_ref[pl.ds(off, width), :]], axis=1)

        def logits(c, k):
            return lax.dot_general(qs[c], k, _NT, preferred_element_type=F32)

        def vp(off, width):
            return _with_ones(kv_ref[2 * si + 1, pl.ds(off, width), :])

        return kside, logits, vp, (None, None)

    _causal_sweep(qi, tc, (s0_ref, s1_ref), m_ref, acc_ref, [stream(0), stream(1)])
    for si in range(2):
        for c in range(2):
            o_ref[c * tc:(c + 1) * tc, si * LANES:(si + 1) * LANES] = _normalized(
                acc_ref[2 * si + c]).astype(o_ref.dtype)


def _mla_attention(qn, qr, kvh, kr):
    h, s, _ = qn.shape
    tc = min(512, s // 2)
    w = 2 * tc
    return pl.pallas_call(
        functools.partial(_mla_kernel, tc=tc),
        grid=(h // 2, s // w),
        in_specs=[pl.BlockSpec((2, w, LANES), lambda p, i: (p, i, 0)),
                  pl.BlockSpec((2, w, LANES), lambda p, i: (p, i, 0)),
                  pl.BlockSpec((4, s, LANES), lambda p, i: (p, 0, 0)),
                  pl.BlockSpec((s, LANES), lambda p, i: (0, 0))],
        out_specs=pl.BlockSpec((w, 2 * LANES), lambda p, i: (i, p)),
        out_shape=jax.ShapeDtypeStruct((s, h * LANES), BF),
        scratch_shapes=_sweep_scratch(tc),
        compiler_params=_cparams("arbitrary", "arbitrary"),
        name="mla_attn",
    )(qn, qr, kvh, kr)


def _fox_kernel(q_ref, k_ref, v_ref, ck_ref, cum_ref, og_ref, o_ref, s0_ref, s1_ref, m_ref, acc_ref, *, tc):
    pair = pl.program_id(0)
    qi = pl.program_id(1)
    lane = lax.broadcasted_iota(jnp.int32, (tc, LANES), 1)
    _init_state(m_ref, acc_ref)

    def stream(si):
        qs = [q_ref[si, c * tc:(c + 1) * tc, :] for c in range(2)]
        cqs = [jnp.sum(jnp.where(lane == 2 * pair + si, cum_ref[c * tc:(c + 1) * tc, :], 0.0), axis=1,
                       keepdims=True) for c in range(2)]

        def kside(off, width):
            return k_ref[si, pl.ds(off, width), :], ck_ref[si, :, pl.ds(off, width)]

        def logits(c, kc):
            return lax.dot_general(qs[c], kc[0], _NT, preferred_element_type=F32) - kc[1]

        def vp(off, width):
            return _with_ones(v_ref[si, pl.ds(off, width), :])

        return kside, logits, vp, cqs

    _causal_sweep(qi, tc, (s0_ref, s1_ref), m_ref, acc_ref, [stream(0), stream(1)])
    for si in range(2):
        cols = slice(si * LANES, (si + 1) * LANES)
        for c in range(2):
            rows = slice(c * tc, (c + 1) * tc)
            o_ref[rows, cols] = (_normalized(acc_ref[2 * si + c]) * og_ref[rows, cols]).astype(o_ref.dtype)


def _fox_attention(q, k, v, cum_t, cum, sig_og):
    h, s, _ = q.shape
    tc = min(512, s // 2)
    w = 2 * tc
    hm = pl.BlockSpec((2, s, LANES), lambda p, i: (p, 0, 0))
    return pl.pallas_call(
        functools.partial(_fox_kernel, tc=tc),
        grid=(h // 2, s // w),
        in_specs=[pl.BlockSpec((2, w, LANES), lambda p, i: (p, i, 0)), hm, hm,
                  pl.BlockSpec((2, 1, s), lambda p, i: (p, 0, 0)),
                  pl.BlockSpec((w, LANES), lambda p, i: (i, 0)),
                  pl.BlockSpec((w, 2 * LANES), lambda p, i: (i, p))],
        out_specs=pl.BlockSpec((w, 2 * LANES), lambda p, i: (i, p)),
        out_shape=jax.ShapeDtypeStruct((s, h * LANES), BF),
        scratch_shapes=_sweep_scratch(tc),
        compiler_params=_cparams("arbitrary", "arbitrary"),
        name="fox_attn",
    )(q, k, v, cum_t, cum, sig_og)


def _cumsum_kernel(x_ref, o_ref, ot_ref, carry_ref, *, t, out_scale):
    @pl.when(pl.program_id(0) == 0)
    def _():
        carry_ref[...] = jnp.zeros_like(carry_ref)

    x = x_ref[...]
    rows = lax.broadcasted_iota(jnp.int32, (t, t), 0)
    cols = lax.broadcasted_iota(jnp.int32, (t, t), 1)
    tri = jnp.where(cols <= rows, 1.0, 0.0).astype(BF)
    hi = x.astype(BF)
    r1 = x - hi.astype(F32)
    mid = r1.astype(BF)
    lo = (r1 - mid.astype(F32)).astype(BF)
    cum = (jnp.dot(tri, hi, preferred_element_type=F32) + jnp.dot(tri, mid, preferred_element_type=F32)
           + jnp.dot(tri, lo, preferred_element_type=F32)) + carry_ref[...]
    scaled = cum * out_scale
    o_ref[...] = scaled
    ot_ref[...] = scaled.T
    carry_ref[...] = cum[t - 1:t, :]


def _cumsum_tokens(x, out_scale):
    s, n = x.shape
    t = min(256, s)
    return pl.pallas_call(
        functools.partial(_cumsum_kernel, t=t, out_scale=out_scale),
        grid=(s // t,),
        in_specs=[pl.BlockSpec((t, n), lambda i: (i, 0))],
        out_specs=[pl.BlockSpec((t, n), lambda i: (i, 0)), pl.BlockSpec((n, t), lambda i: (0, i))],
        out_shape=[jax.ShapeDtypeStruct((s, n), F32), jax.ShapeDtypeStruct((n, s), F32)],
        scratch_shapes=[pltpu.VMEM((1, n), F32)],
        compiler_params=_cparams("arbitrary"),
        name="cumsum",
    )(x)


def _compress_kernel(a_ref, pos_ref, w1_ref, w2_ref, o_ref):
    a = a_ref[0]
    nc = a.shape[0]
    p1 = jnp.dot((a + pos_ref[0, 0:1, :]).astype(BF), w1_ref[0, 0], preferred_element_type=F32)
    p2 = jnp.dot((a + pos_ref[0, 1:2, :]).astype(BF), w1_ref[0, 1], preferred_element_type=F32)
    h = p1 + pltpu.roll(p2, nc - 1, 0)
    act = h * jax.nn.sigmoid(h)
    o_ref[0] = jnp.dot(act.astype(BF), w2_ref[0], preferred_element_type=F32).astype(o_ref.dtype)


def _nsa_compress(kv_cmp, pos, w1, w2):
    c, nc, gw = kv_cmp.shape
    return pl.pallas_call(
        _compress_kernel,
        grid=(c,),
        in_specs=[pl.BlockSpec((1, nc, gw), lambda i: (i, 0, 0)),
                  pl.BlockSpec((1, 2, gw), lambda i: (i // 2, 0, 0)),
                  pl.BlockSpec((1, 2, gw, CMP_HIDDEN), lambda i: (i // 2, 0, 0, 0)),
                  pl.BlockSpec((1, CMP_HIDDEN, NSA_HEAD_DIM), lambda i: (i // 2, 0, 0))],
        out_specs=pl.BlockSpec((1, nc, NSA_HEAD_DIM), lambda i: (i, 0, 0)),
        out_shape=jax.ShapeDtypeStruct((c, nc, NSA_HEAD_DIM), BF),
        compiler_params=_cparams("arbitrary"),
        name="nsa_compress",
    )(kv_cmp, pos, w1, w2)


def _bias_table_kernel(brev_ref, basc_ref, tbl_ref, tb_ref, asc_ref, *, s, mc):
    tbl = tbl_ref[0] * LOG2E

    def lookup(bkt):
        out = jnp.zeros(bkt.shape, F32)
        for b in range(REL_BUCKETS):
            out = jnp.where(bkt == b, tbl[:, b:b + 1], out)
        return out

    rev = lookup(brev_ref[...])
    asc_ref[0] = lookup(basc_ref[...])
    for m in range(mc + 1):
        win = rev[:, s - Q_BLOCK * m:s - Q_BLOCK * m + 2 * Q_BLOCK]
        rolled = pltpu.roll(jnp.broadcast_to(win, (Q_BLOCK, 2 * Q_BLOCK)), Q_BLOCK + 1, 1,
                            stride=1, stride_axis=0)
        tb_ref[0, m] = rolled[:, :Q_BLOCK]


def _t5_bucket(dist):
    max_exact = REL_BUCKETS // 2
    d = jnp.maximum(dist, 0)
    ratio = jnp.log(jnp.maximum(d, max_exact).astype(F32) / max_exact) / math.log(REL_MAX_DIST / max_exact)
    large = jnp.minimum(max_exact + (ratio * (REL_BUCKETS - max_exact)).astype(jnp.int32), REL_BUCKETS - 1)
    return jnp.where(d < max_exact, d, large)


def _bias_tables(rel_bias, s):
    assert Q_BLOCK == SUBLANES * CMP_STRIDE
    h = rel_bias.shape[1]
    qblocks = s // Q_BLOCK
    max_exact = REL_BUCKETS // 2
    d_const = int(math.ceil(max_exact * (REL_MAX_DIST / max_exact)
                            ** ((REL_BUCKETS - max_exact - 1) / (REL_BUCKETS - max_exact)))) + 1
    mc = min(-(-(d_const + Q_BLOCK - 1) // Q_BLOCK), qblocks - 1)
    nrev = s + 2 * Q_BLOCK
    nasc = s + 3 * Q_BLOCK
    brev = _t5_bucket(s + Q_BLOCK - 1 - jnp.arange(nrev, dtype=jnp.int32))[None, :]
    basc = _t5_bucket(jnp.arange(nasc, dtype=jnp.int32) - 2 * Q_BLOCK)[None, :]
    tb, asc = pl.pallas_call(
        functools.partial(_bias_table_kernel, s=s, mc=mc),
        grid=(h,),
        in_specs=[pl.BlockSpec((1, nrev), lambda i: (0, 0)),
                  pl.BlockSpec((1, nasc), lambda i: (0, 0)),
                  pl.BlockSpec((1, 1, REL_BUCKETS), lambda i: (i, 0, 0))],
        out_specs=[pl.BlockSpec((1, mc + 1, Q_BLOCK, Q_BLOCK), lambda i: (i, 0, 0, 0)),
                   pl.BlockSpec((1, 1, nasc), lambda i: (i, 0, 0))],
        out_shape=[jax.ShapeDtypeStruct((h, mc + 1, Q_BLOCK, Q_BLOCK), F32),
                   jax.ShapeDtypeStruct((h, 1, nasc), F32)],
        compiler_params=_cparams("arbitrary"),
        name="bias_tables",
    )(brev, basc, rel_bias.T.reshape(h, 1, REL_BUCKETS))
    asc = asc[:, 0, :]
    base = 2 * Q_BLOCK - (CMP_LEN - 1)
    segs = [asc[:, base - CMP_STRIDE * nn:base - CMP_STRIDE * nn + s].reshape(h, qblocks, Q_BLOCK)
            for nn in range(SUBLANES)]
    trc = jnp.flip(jnp.stack(segs, axis=2), axis=1).reshape(h, qblocks * SUBLANES, Q_BLOCK)
    return tb, jnp.pad(trc, ((0, 0), (0, qblocks * SUBLANES), (0, 0)))


def _nsa_cmp_kernel(q_ref, kc_ref, vct_ref, trc_ref, smt_ref, gate_ref, oc_ref, sel_ref, *, qblocks, topn, nb):
    for c in range(nb):
        _nsa_cmp_block(pl.program_id(1) * nb + c, slice(c * Q_BLOCK, (c + 1) * Q_BLOCK), q_ref, kc_ref, vct_ref,
                       trc_ref, smt_ref, gate_ref, oc_ref, sel_ref, qblocks=qblocks, topn=topn)


def _nsa_cmp_block(qb, rows, q_ref, kc_ref, vct_ref, trc_ref, smt_ref, gate_ref, oc_ref, sel_ref, *, qblocks, topn):
    rq = NSA_GROUP * Q_BLOCK
    q = q_ref[:, rows, :].reshape(rq, NSA_HEAD_DIM)
    kc = kc_ref[0]
    nc = kc.shape[0]
    nslc = smt_ref.shape[0]
    s = lax.dot_general(kc, q, _NT, preferred_element_type=F32)
    n_io = lax.broadcasted_iota(jnp.int32, (nc, Q_BLOCK), 0)
    i_io = lax.broadcasted_iota(jnp.int32, (nc, Q_BLOCK), 1)
    mask = (n_io * CMP_STRIDE + (CMP_LEN - 1)) <= (qb * Q_BLOCK + i_io)
    off = pl.multiple_of((qblocks - 1 - qb) * SUBLANES, SUBLANES)
    ps = []
    for r in range(NSA_GROUP):
        l = jnp.where(mask, s[:, r * Q_BLOCK:(r + 1) * Q_BLOCK] + trc_ref[r, pl.ds(off, nc), :], NEG_INF)
        m = jnp.max(l, axis=0, keepdims=True)
        p = jnp.where(mask, jnp.exp2(l - m), 0.0)
        den = jnp.maximum(jnp.sum(p, axis=0, keepdims=True), 1e-30)
        ps.append(p / den)
    p_all = jnp.concatenate(ps, axis=1).astype(BF)
    oc_t = jnp.dot(vct_ref[0], p_all, preferred_element_type=F32)
    imp4 = jnp.dot(smt_ref[...], p_all, preferred_element_type=F32)
    imp = imp4[:, 0:Q_BLOCK]
    for r in range(1, NSA_GROUP):
        imp = imp + imp4[:, r * Q_BLOCK:(r + 1) * Q_BLOCK]

    j_io = lax.broadcasted_iota(jnp.int32, (nslc, Q_BLOCK), 0)
    t_io = qb * Q_BLOCK + lax.broadcasted_iota(jnp.int32, (nslc, Q_BLOCK), 1)
    cur = t_io >> int(math.log2(SLC_LEN))
    forced = jnp.logical_or(j_io == 0, jnp.logical_or(j_io == cur, j_io == cur - 1))
    val = jnp.where(forced, 1e9, jnp.where(j_io <= cur, imp, -1e9))
    sel = jnp.zeros((nslc, Q_BLOCK), F32)
    for _ in range(topn):
        mx = jnp.max(val, axis=0, keepdims=True)
        cand = jnp.where(val == mx, j_io, nslc)
        jmin = jnp.min(cand, axis=0, keepdims=True)
        pick = j_io == jmin
        sel = jnp.where(pick, 1.0, sel)
        val = jnp.where(pick, -3e38, val)
    sel_ref[0, rows, :] = jnp.where(sel.T > 0.5, 0.0, NEG_INF).astype(sel_ref.dtype)

    gates = gate_ref[0, rows, :]
    for r in range(NSA_GROUP):
        o_r = oc_t[:, r * Q_BLOCK:(r + 1) * Q_BLOCK].T
        oc_ref[rows, r * NSA_HEAD_DIM:(r + 1) * NSA_HEAD_DIM] = o_r * gates[:, 3 * r:3 * r + 1]


def _nsa_cmp(q, kc, vct, trc, smt, gates):
    h, s, d = q.shape
    g = NSA_KV_HEADS
    qblocks = s // Q_BLOCK
    nc = kc.shape[1]
    nslc = smt.shape[0]
    topn = min(SLC_TOPN, nslc)
    nb = 2
    qrows = nb * Q_BLOCK
    return pl.pallas_call(
        functools.partial(_nsa_cmp_kernel, qblocks=qblocks, topn=topn, nb=nb),
        grid=(g, qblocks // nb),
        in_specs=[pl.BlockSpec((NSA_GROUP, qrows, d), lambda gg, i: (gg, i, 0)),
                  pl.BlockSpec((1, nc, d), lambda gg, i: (gg, 0, 0)),
                  pl.BlockSpec((1, d, nc), lambda gg, i: (gg, 0, 0)),
                  pl.BlockSpec((NSA_GROUP, trc.shape[1], Q_BLOCK), lambda gg, i: (gg, 0, 0)),
                  pl.BlockSpec(smt.shape, lambda gg, i: (0, 0)),
                  pl.BlockSpec((1, qrows, LANES), lambda gg, i: (gg, i, 0))],
        out_specs=[pl.BlockSpec((qrows, NSA_GROUP * d), lambda gg, i: (i, gg)),
                   pl.BlockSpec((1, qrows, nslc), lambda gg, i: (gg, i, 0))],
        out_shape=[jax.ShapeDtypeStruct((s, h * d), F32),
                   jax.ShapeDtypeStruct((g, s, nslc), BF)],
        compiler_params=_cparams("arbitrary", "arbitrary"),
        name="nsa_cmp",
    )(q, kc, vct, trc, smt, gates)


def _nsa_sw_kernel(q_ref, kv_ref, tb_ref, sel_ref, et_ref, gate_ref, oc_ref,
                   o_ref, s0_ref, s1_ref, m_ref, acc_ref, *, tk, mc, nch):
    i = pl.program_id(0)
    ngrp = NSA_KV_HEADS
    rq = NSA_GROUP * Q_BLOCK
    nsub = tk // Q_BLOCK
    s_bufs = (s0_ref, s1_ref)
    qbs = [i * nch + c for c in range(nch)]
    qs = [[q_ref[g * NSA_GROUP:(g + 1) * NSA_GROUP, c * Q_BLOCK:(c + 1) * Q_BLOCK, :].reshape(rq, NSA_HEAD_DIM)
           for c in range(nch)] for g in range(ngrp)]

    def bias_tile(g, r, c, kb0, nblk):
        return jnp.concatenate(
            [tb_ref[g * NSA_GROUP + r, jnp.clip(qbs[c] - (kb0 + b), 0, mc)] for b in range(nblk)], axis=1)

    def gated(g, c, o, branch, base):
        gates = gate_ref[g, c * Q_BLOCK:(c + 1) * Q_BLOCK, :]
        outs = []
        for r in range(NSA_GROUP):
            o_r = o[r * Q_BLOCK:(r + 1) * Q_BLOCK, :] * gates[:, 3 * r + branch:3 * r + branch + 1]
            outs.append(base[:, r * NSA_HEAD_DIM:(r + 1) * NSA_HEAD_DIM] + o_r)
        return jnp.concatenate(outs, axis=1)

    _init_state(m_ref, acc_ref)
    qa = [[jnp.concatenate([qs[g][c], jnp.concatenate(
        [sel_ref[g, c * Q_BLOCK:(c + 1) * Q_BLOCK, :]] * NSA_GROUP, axis=0)], axis=1) for c in range(nch)]
        for g in range(ngrp)]

    def pre(g, j, causal):
        off = pl.multiple_of(j * tk, tk)
        ka = jnp.concatenate([kv_ref[g, pl.ds(off, tk), :], et_ref[pl.ds(off, tk), :]], axis=1)
        for c in range(nch):
            s = lax.dot_general(qa[g][c], ka, _NT, preferred_element_type=F32)
            if causal:
                kk = off + lax.broadcasted_iota(jnp.int32, (Q_BLOCK, tk), 1)
                ii = qbs[c] * Q_BLOCK + lax.broadcasted_iota(jnp.int32, (Q_BLOCK, tk), 0)
                future = jnp.where(kk <= ii, 0.0, NEG_INF)
            for r in range(NSA_GROUP):
                bias = bias_tile(g, r, c, j * nsub, nsub)
                if causal:
                    bias = bias + future
                s_bufs[g][c, r * Q_BLOCK:(r + 1) * Q_BLOCK, :] = s[r * Q_BLOCK:(r + 1) * Q_BLOCK, :] + bias

    def process(g, j):
        vp = _with_ones(kv_ref[ngrp + g, pl.ds(pl.multiple_of(j * tk, tk), tk), :])
        for c in range(nch):
            _flash_update(s_bufs[g].at[c], vp, m_ref.at[g * nch + c], acc_ref.at[g * nch + c])

    n_past = (i * nch) // nsub

    def body(t, carry):
        pre(1, t, False)
        process(0, t)
        pre(0, t + 1, False)
        process(1, t)
        return carry

    pre(0, n_past, True)
    pre(1, n_past, True)
    process(0, n_past)
    pre(0, 0, False)
    process(1, n_past)
    lax.fori_loop(0, n_past, body, 0)

    wk = WINDOW + Q_BLOCK
    wblk = wk // Q_BLOCK
    for g in range(ngrp):
        cols = slice(g * rq, (g + 1) * rq)
        for c in range(nch):
            rows = slice(c * Q_BLOCK, (c + 1) * Q_BLOCK)
            o_sel = gated(g, c, _normalized(acc_ref[g * nch + c]), 1, oc_ref[rows, cols])
            kb0 = jnp.maximum(qbs[c] - WINDOW // Q_BLOCK, 0)
            off = pl.multiple_of(kb0 * Q_BLOCK, Q_BLOCK)
            s = lax.dot_general(qs[g][c], kv_ref[2 * ngrp + g, pl.ds(off, wk), :], _NT, preferred_element_type=F32)
            rel = (qbs[c] * Q_BLOCK + lax.broadcasted_iota(jnp.int32, (Q_BLOCK, wk), 0)) - (
                off + lax.broadcasted_iota(jnp.int32, (Q_BLOCK, wk), 1))
            outside = jnp.where(jnp.logical_and(rel >= 0, rel < WINDOW), 0.0, NEG_INF)
            l = jnp.concatenate([s[r * Q_BLOCK:(r + 1) * Q_BLOCK, :] + (bias_tile(g, r, c, kb0, wblk) + outside)
                                 for r in range(NSA_GROUP)], axis=0)
            p = jnp.exp2(l - jnp.max(l, axis=1, keepdims=True)).astype(BF)
            ow = jnp.dot(p, _with_ones(kv_ref[3 * ngrp + g, pl.ds(off, wk), :]), preferred_element_type=F32)
            o_ref[rows, cols] = gated(g, c, _normalized(ow), 2, o_sel).astype(o_ref.dtype)


def _nsa_sw(q, kvsw, tb, sel, et, gates, oc):
    h, s, d = q.shape
    g = NSA_KV_HEADS
    nch = 2
    qrows = nch * Q_BLOCK
    tk = min(512, s)
    mc = tb.shape[1] - 1
    nslc = sel.shape[2]
    rq = NSA_GROUP * Q_BLOCK
    assert tk % qrows == 0 and WINDOW + Q_BLOCK <= s and NSA_GROUP * d == rq

    def resident(arr):
        return pl.BlockSpec(arr.shape, lambda i, nd=arr.ndim: (0,) * nd, pipeline_mode=pl.Buffered(1))

    return pl.pallas_call(
        functools.partial(_nsa_sw_kernel, tk=tk, mc=mc, nch=nch),
        grid=(s // qrows,),
        in_specs=[pl.BlockSpec((h, qrows, d), lambda i: (0, i, 0)),
                  resident(kvsw), resident(tb),
                  pl.BlockSpec((g, qrows, nslc), lambda i: (0, i, 0)),
                  resident(et),
                  pl.BlockSpec((g, qrows, LANES), lambda i: (0, i, 0)),
                  pl.BlockSpec((qrows, h * d), lambda i: (i, 0))],
        out_specs=pl.BlockSpec((qrows, h * d), lambda i: (i, 0)),
        out_shape=jax.ShapeDtypeStruct((s, h * d), BF),
        scratch_shapes=[pltpu.VMEM((nch, rq, tk), F32), pltpu.VMEM((nch, rq, tk), F32),
                        pltpu.VMEM((g * nch, rq, 1), F32), pltpu.VMEM((g * nch, rq, 2 * LANES), F32)],
        compiler_params=_cparams("arbitrary"),
        name="nsa_sel_win",
    )(q, kvsw, tb, sel, et, gates, oc)


def _selection_map_t(nc, nslc):
    n = np.arange(nc)[None, :] * CMP_STRIDE
    j0 = np.arange(nslc)[:, None] * SLC_LEN
    valid = np.arange(nc)[None, :] < nc - 1
    return jnp.asarray(((n < j0 + SLC_LEN) & (n + CMP_LEN > j0) & valid).astype(np.float32), dtype=BF)


def _block_onehot(s, nslc):
    tok = np.arange(s)[:, None]
    j = np.arange(nslc)[None, :]
    return jnp.asarray((tok // SLC_LEN == j).astype(np.float32), dtype=BF)


def _rope_tables(positions):
    half = MLA_ROPE // 2
    inv = ROPE_THETA ** (-jnp.arange(half, dtype=F32) / half)
    ang = positions.astype(F32)[:, None] * inv
    cos, sin = jnp.cos(ang), jnp.sin(ang)
    z = jnp.zeros_like(cos)
    zpad = jnp.zeros((positions.shape[0], LANES - MLA_ROPE), F32)
    c = jnp.concatenate([cos, cos, zpad], axis=1)
    s1 = jnp.concatenate([-sin, z, zpad], axis=1)
    s2 = jnp.concatenate([z, sin, zpad], axis=1)
    return c, s1, s2


def _pad_cols(w, n):
    return jnp.pad(w, ((0, 0), (0, n - w.shape[1])))


def _pad_rows(w, n):
    return jnp.pad(w, ((0, n - w.shape[0]), (0, 0)))


def _nsa_mla_mixer(hn, pos, rel_bias, w_in_all, w_out_all, e, gate_b, pos_k, w1_k, w2_k, pos_v, w1_v, w2_v,
                   q_norm, w_uq, kv_norm, w_ukv, x, g1):
    s = hn.shape[0]
    d = NSA_HEAD_DIM
    nq = NSA_HEADS * d
    nkv = 2 * NSA_KV_HEADS * d
    w_in_t = jnp.swapaxes(w_in_all, 1, 2)
    dm = w_in_t.shape[2]
    o0 = nq + 3 * nkv
    w_g_t = w_in_t[e, o0:o0 + 3 * NSA_HEADS]; o0 += 3 * NSA_HEADS
    w_lat_t = w_in_t[e, o0:o0 + MLA_Q_RANK + MLA_KV_RANK]; o0 += MLA_Q_RANK + MLA_KV_RANK
    w_kr_t = w_in_t[e, o0:o0 + MLA_ROPE]

    q_nsa = _proj([hn], [(w_in_t, dm, (e, 0), 0)], functools.partial(_epi_scale, d ** -0.5 * LOG2E), n=nq,
                  out_dtype=BF, head_major=True, tn=1024, w_transposed=True, name="proj_q_nsa")
    kv_cmp = _proj([hn], [(w_in_t, dm, (e, 0), nq)], _epi_id, n=nkv, out_dtype=F32, head_major=True,
                   w_transposed=True, name="proj_kv_cmp")
    kv_sw = _proj([hn], [(w_in_t, dm, (e, 0), nq + nkv)], _epi_id, n=2 * nkv, out_dtype=BF, head_major=True,
                  w_transposed=True, name="proj_kv_sw")
    per_g = 3 * NSA_GROUP
    w_gp_t = jnp.concatenate([_pad_rows(w_g_t[g * per_g:(g + 1) * per_g], LANES) for g in range(NSA_KV_HEADS)], 0)
    b_gp = jnp.concatenate([_pad_cols(gate_b[None, g * per_g:(g + 1) * per_g], LANES)
                            for g in range(NSA_KV_HEADS)], 1)
    gates = _proj([hn], [(w_gp_t[None], dm, (0, 0), 0)], _epi_sigmoid_bias, [(b_gp, "row")],
                  n=NSA_KV_HEADS * LANES, out_dtype=F32, head_major=True, tn=LANES, w_transposed=True,
                  name="proj_gates")

    nc = s // CMP_STRIDE
    gw = CMP_STRIDE * d
    pos_kv = jnp.stack([pos_k.reshape(2, gw), pos_v.reshape(2, gw)])
    w1_kv = jnp.stack([w1_k.reshape(2, gw, CMP_HIDDEN), w1_v.reshape(2, gw, CMP_HIDDEN)]).astype(BF)
    w2_kv = jnp.stack([w2_k, w2_v]).astype(BF)
    kvc = _nsa_compress(kv_cmp.reshape(2 * NSA_KV_HEADS, nc, gw), pos_kv, w1_kv, w2_kv)
    kc = kvc[:NSA_KV_HEADS]
    vct = jnp.swapaxes(kvc[NSA_KV_HEADS:], 1, 2)

    nslc = s // SLC_LEN
    tb, trc = _bias_tables(rel_bias, s)
    smt = _selection_map_t(nc, nslc)
    oc, sel = _nsa_cmp(q_nsa, kc, vct, trc, smt, gates)
    o_nsa = _nsa_sw(q_nsa, kv_sw, tb, sel, _block_onehot(s, nslc), gates, oc)

    nlat = MLA_Q_RANK + MLA_KV_RANK
    lat = _proj([hn], [(w_lat_t[None], dm, (0, 0), 0)], _epi_rmsnorm,
                [(jnp.concatenate([q_norm, kv_norm])[None, :], "row")], n=nlat, out_dtype=BF, tn=MLA_Q_RANK,
                w_transposed=True, name="proj_mla_latent")
    c, s1, s2 = _rope_tables(pos)
    kr = _proj([hn], [(_pad_rows(w_kr_t, LANES)[None], dm, (0, 0), 0)], functools.partial(_epi_rope, 1.0),
               [(c, "rowtile"), (s1, "rowtile"), (s2, "rowtile")], n=LANES, out_dtype=BF, w_transposed=True,
               name="proj_mla_kr")
    scale = (MLA_NOPE + MLA_ROPE) ** -0.5 * LOG2E
    w_uq3 = w_uq.reshape(MLA_Q_RANK, MLA_HEADS, MLA_NOPE + MLA_ROPE)
    w_uqn = w_uq3[:, :, :MLA_NOPE].reshape(MLA_Q_RANK, MLA_HEADS * MLA_NOPE)
    w_uqr = jnp.pad(w_uq3[:, :, MLA_NOPE:], ((0, 0), (0, 0), (0, LANES - MLA_ROPE))).reshape(
        MLA_Q_RANK, MLA_HEADS * LANES)
    qn = _proj([lat], [(w_uqn, MLA_Q_RANK, 0, 0)], functools.partial(_epi_scale, scale),
               n=MLA_HEADS * MLA_NOPE, out_dtype=BF, head_major=True, tn=1024, lhs_col_block=[0],
               name="proj_mla_qn")
    qr = _proj([lat], [(w_uqr, MLA_Q_RANK, 0, 0)], functools.partial(_epi_rope, scale),
               [(c, "rowtile"), (s1, "rowtile"), (s2, "rowtile")], n=MLA_HEADS * LANES, out_dtype=BF,
               head_major=True, lhs_col_block=[0], name="proj_mla_qr")
    kvh = _proj([lat], [(w_ukv, MLA_KV_RANK, 0, 0)], _epi_id, n=MLA_HEADS * (MLA_NOPE + MLA_V), out_dtype=BF,
                head_major=True, tn=1024, lhs_col_block=[1], name="proj_mla_kv")
    o_mla = _mla_attention(qn, qr, kvh, kr)

    return _proj([o_nsa, o_mla], [(w_out_all, nq, (e, 0), 0), (w_out_all, MLA_HEADS * MLA_V, (e, 1), 0)],
                 _epi_residual, [(x, "tile"), (g1, "row")], n=w_out_all.shape[2], out_dtype=F32, tn=1024,
                 name="proj_even_out")


def _fox_mixer(hn, w_in_all, w_out_all, o, f_b, q_norm, k_norm, x, g1):
    d = D_MODEL
    dh = FOX_HEAD_DIM
    w_in_t = jnp.swapaxes(w_in_all, 1, 2)
    w_f_t = w_in_t[o, 3 * d:3 * d + FOX_HEADS]
    w_og_t = w_in_t[o, 3 * d + FOX_HEADS:]
    q = _proj([hn], [(w_in_t, d, (o, 0), 0)], functools.partial(_epi_headnorm, dh ** -0.5 * LOG2E),
              [(q_norm[None, :], "const")], n=d, out_dtype=BF, head_major=True, tn=1024, w_transposed=True,
              name="proj_fox_q")
    k = _proj([hn], [(w_in_t, d, (o, 0), d)], functools.partial(_epi_headnorm, 1.0),
              [(k_norm[None, :], "const")], n=d, out_dtype=BF, head_major=True, tn=1024, w_transposed=True,
              name="proj_fox_k")
    v = _proj([hn], [(w_in_t, d, (o, 0), 2 * d)], _epi_id, n=d, out_dtype=BF, head_major=True, tn=1024,
              w_transposed=True, name="proj_fox_v")
    lf = _proj([hn], [(_pad_rows(w_f_t, LANES)[None], d, (0, 0), 0)], _epi_logsigmoid_bias,
               [(_pad_cols(f_b[None, :], LANES), "row")], n=LANES, out_dtype=F32, w_transposed=True,
               name="proj_fox_f")
    sig_og = _proj([hn], [(w_og_t[None], d, (0, 0), 0)], _epi_sigmoid, n=d, out_dtype=F32, tn=1024,
                   w_transposed=True, name="proj_fox_og")
    cum, cum_t = _cumsum_tokens(lf, LOG2E)
    cum_t = cum_t[:FOX_HEADS].reshape(FOX_HEADS, 1, -1)
    att = _fox_attention(q, k, v, cum_t, cum, sig_og)
    return _proj([att], [(w_out_all, d, (o, 0), 0)], _epi_residual, [(x, "tile"), (g1, "row")], n=d,
                 out_dtype=F32, tn=1024, name="proj_fox_out")


def kernel(x, c, positions, rel_bias, ada_w, ada_b, norm_mix, norm_ffn, ffn_w1, ffn_w3, ffn_w2, even_w_in, even_w_out, nsa_gate_b, nsa_cmp_pos_k, nsa_cmp_w1_k, nsa_cmp_w2_k, nsa_cmp_pos_v, nsa_cmp_w1_v, nsa_cmp_w2_v, mla_q_norm, mla_w_uq, mla_kv_norm, mla_w_ukv, fox_w_in, fox_w_out, fox_f_b, fox_q_norm, fox_k_norm, final_norm):
    b, s, d = x.shape
    assert b == 1 and d == D_MODEL and s % 1024 == 0
    xs = x[0]
    pos = positions[0]
    mod = _adaln(c, ada_w, ada_b)
    depth = ada_w.shape[0]
    for i in range(depth):
        sh1, sc1, g1, sh2, sc2, g2 = [mod[i:i + 1, k * d:(k + 1) * d] for k in range(6)]
        hn = _normmod(xs, norm_mix[i][None, :], sc1, sh1)
        if i % 2 == 0:
            e = i // 2
            xs = _nsa_mla_mixer(hn, pos, rel_bias, even_w_in, even_w_out, e, nsa_gate_b[e],
                                nsa_cmp_pos_k[e], nsa_cmp_w1_k[e], nsa_cmp_w2_k[e],
                                nsa_cmp_pos_v[e], nsa_cmp_w1_v[e], nsa_cmp_w2_v[e],
                                mla_q_norm[e], mla_w_uq[e], mla_kv_norm[e], mla_w_ukv[e], xs, g1)
        else:
            o = i // 2
            xs = _fox_mixer(hn, fox_w_in, fox_w_out, o, fox_f_b[o], fox_q_norm[o], fox_k_norm[o], xs, g1)
        fin = final_norm[None, :] if i == depth - 1 else None
        xs = _ffn(xs, norm_ffn[i][None, :], sc2, sh2, g2, ffn_w1, ffn_w3, ffn_w2, i, fin)
    return xs[None]
```

```python
import functools
import math

import numpy as np
import jax
import jax.numpy as jnp
from jax import lax
from jax.experimental import pallas as pl
from jax.experimental.pallas import tpu as pltpu

D_MODEL = 2048
DEPTH = 2
EPS = 1e-6
NEG_INF = -1e30

NSA_HEADS = 8
NSA_KV_HEADS = 2
NSA_GROUP = NSA_HEADS // NSA_KV_HEADS
NSA_HEAD_DIM = 128
CMP_LEN = 32
CMP_STRIDE = 16
CMP_HIDDEN = 256
SLC_LEN = 64
SLC_TOPN = 16
WINDOW = 512
Q_BLOCK = 128

MLA_HEADS = 8
MLA_Q_RANK = 512
MLA_KV_RANK = 512
MLA_NOPE = 128
MLA_ROPE = 64
MLA_V = 128
ROPE_THETA = 10000.0

FOX_HEADS = 16
FOX_HEAD_DIM = D_MODEL // FOX_HEADS

REL_BUCKETS = 32
REL_MAX_DIST = 4096

FFN_HIDDEN = ((8 * D_MODEL + 2) // 3 + 255) // 256 * 256

LANES = 128
SUBLANES = 8
VMEM_LIMIT_BYTES = 56 * 1024 * 1024
LOG2E = math.log2(math.e)

BF = jnp.bfloat16
F32 = jnp.float32
_NT = (((1,), (1,)), ((), ()))


def _cparams(*sem):
    return pltpu.CompilerParams(dimension_semantics=sem, vmem_limit_bytes=VMEM_LIMIT_BYTES)


def _rms(x):
    return x * lax.rsqrt(jnp.mean(x * x, axis=-1, keepdims=True) + EPS)


def _adaln_kernel(c_ref, w_ref, b_ref, o_ref):
    c = c_ref[...]
    cond = c * jax.nn.sigmoid(c)
    acc = jnp.dot(cond.astype(BF), w_ref[0].astype(BF), preferred_element_type=F32)
    o_ref[0] = acc + b_ref[0]


def _adaln(c, ada_w, ada_b):
    depth, d, n = ada_w.shape
    tn = 1024
    c8 = jnp.broadcast_to(c, (SUBLANES, d))
    out = pl.pallas_call(
        _adaln_kernel,
        grid=(depth, n // tn),
        in_specs=[pl.BlockSpec((SUBLANES, d), lambda i, j: (0, 0)),
                  pl.BlockSpec((1, d, tn), lambda i, j: (i, 0, j)),
                  pl.BlockSpec((1, 1, tn), lambda i, j: (i, 0, j))],
        out_specs=pl.BlockSpec((1, SUBLANES, tn), lambda i, j: (i, 0, j)),
        out_shape=jax.ShapeDtypeStruct((depth, SUBLANES, n), F32),
        compiler_params=_cparams("arbitrary", "arbitrary"),
        name="adaln",
    )(c8, ada_w, ada_b.reshape(depth, 1, n))
    return out[:, 0, :]


def _normmod_kernel(x_ref, g_ref, sc_ref, sh_ref, o_ref):
    y = _rms(x_ref[...])
    o_ref[...] = ((y * g_ref[...]) * (1.0 + sc_ref[...]) + sh_ref[...]).astype(o_ref.dtype)


def _normmod(x, g, sc, sh):
    s, d = x.shape
    tm = 512
    row = pl.BlockSpec((1, d), lambda i: (0, 0))
    return pl.pallas_call(
        _normmod_kernel,
        grid=(s // tm,),
        in_specs=[pl.BlockSpec((tm, d), lambda i: (i, 0)), row, row, row],
        out_specs=pl.BlockSpec((tm, d), lambda i: (i, 0)),
        out_shape=jax.ShapeDtypeStruct((s, d), BF),
        compiler_params=_cparams("arbitrary"),
        name="normmod",
    )(x, g, sc, sh)


def _proj_kernel(*refs, n_lhs, n_epi, epi, head_major, w_transposed):
    lhs = refs[:n_lhs]
    ws = refs[n_lhs:2 * n_lhs]
    epis = refs[2 * n_lhs:2 * n_lhs + n_epi]
    o_ref = refs[2 * n_lhs + n_epi]
    wbf = refs[2 * n_lhs + n_epi + 1:]

    @pl.when(pl.program_id(1) == 0)
    def _():
        for w, wb in zip(ws, wbf):
            wb[...] = w[...].astype(BF)

    acc = None
    for a, wb in zip(lhs, wbf):
        if w_transposed:
            d = lax.dot_general(a[...], wb[...], _NT, preferred_element_type=F32)
        else:
            d = jnp.dot(a[...], wb[...], preferred_element_type=F32)
        acc = d if acc is None else acc + d
    res = epi(acc, *[e[...] for e in epis])
    if head_major:
        for r in range(o_ref.shape[0]):
            o_ref[r] = res[:, r * LANES:(r + 1) * LANES].astype(o_ref.dtype)
    else:
        o_ref[...] = res.astype(o_ref.dtype)


def _proj(lhs, ws, epi, epi_in=(), *, n, out_dtype, head_major=False, tm=1024, tn=512,
          lhs_col_block=None, w_transposed=False, name="proj"):
    m = lhs[0].shape[0]
    tm = min(tm, m)
    tn = min(tn, n)
    if lhs_col_block is None:
        lhs_col_block = [0] * len(lhs)
    in_specs = []
    for (_, k, _, _), cb in zip(ws, lhs_col_block):
        in_specs.append(pl.BlockSpec((tm, k), lambda j, i, cb=cb: (i, cb)))
    for arr, k, rb, col0 in ws:
        assert col0 % tn == 0
        if w_transposed:
            in_specs.append(pl.BlockSpec((None, tn, k),
                                         lambda j, i, rb=rb, cb0=col0 // tn: (rb[0], cb0 + j, rb[1])))
        elif arr.ndim == 3:
            in_specs.append(pl.BlockSpec((None, k, tn),
                                         lambda j, i, rb=rb, cb0=col0 // tn: (rb[0], rb[1], cb0 + j)))
        else:
            in_specs.append(pl.BlockSpec((k, tn), lambda j, i, rb=rb, cb0=col0 // tn: (rb, cb0 + j)))
    arrays = list(lhs) + [w[0] for w in ws]
    for arr, kind in epi_in:
        if kind == "row":
            in_specs.append(pl.BlockSpec((1, tn), lambda j, i: (0, j)))
        elif kind == "const":
            in_specs.append(pl.BlockSpec(arr.shape, lambda j, i: (0, 0)))
        elif kind == "tile":
            in_specs.append(pl.BlockSpec((tm, tn), lambda j, i: (i, j)))
        elif kind == "rowtile":
            in_specs.append(pl.BlockSpec((tm, arr.shape[1]), lambda j, i: (i, 0)))
        else:
            raise ValueError(kind)
        arrays.append(arr)
    if head_major:
        hpt = tn // LANES
        out_spec = pl.BlockSpec((hpt, tm, LANES), lambda j, i: (j, i, 0))
        out_shape = jax.ShapeDtypeStruct((n // LANES, m, LANES), out_dtype)
    else:
        out_spec = pl.BlockSpec((tm, tn), lambda j, i: (i, j))
        out_shape = jax.ShapeDtypeStruct((m, n), out_dtype)
    kern = functools.partial(_proj_kernel, n_lhs=len(lhs), n_epi=len(epi_in), epi=epi,
                             head_major=head_major, w_transposed=w_transposed)
    return pl.pallas_call(
        kern,
        grid=(n // tn, m // tm),
        in_specs=in_specs,
        out_specs=out_spec,
        out_shape=out_shape,
        scratch_shapes=[pltpu.VMEM((tn, k) if w_transposed else (k, tn), BF) for _, k, _, _ in ws],
        compiler_params=_cparams("arbitrary", "arbitrary"),
        name=name,
    )(*arrays)


def _epi_id(acc):
    return acc


def _epi_scale(scale, acc):
    return acc * scale


def _epi_sigmoid_bias(acc, b):
    return jax.nn.sigmoid(acc + b)


def _epi_logsigmoid_bias(acc, b):
    return jax.nn.log_sigmoid(acc + b)


def _epi_sigmoid(acc):
    return jax.nn.sigmoid(acc)


def _epi_rmsnorm(acc, g):
    return _rms(acc) * g


def _epi_headnorm(scale, acc, g):
    outs = []
    for r in range(acc.shape[1] // LANES):
        outs.append(_rms(acc[:, r * LANES:(r + 1) * LANES]) * g * scale)
    return jnp.concatenate(outs, axis=1)


def _epi_rope(scale, acc, c, s1, s2):
    reps = acc.shape[1] // LANES
    half = MLA_ROPE // 2
    if reps > 1:
        c = jnp.concatenate([c] * reps, axis=1)
        s1 = jnp.concatenate([s1] * reps, axis=1)
        s2 = jnp.concatenate([s2] * reps, axis=1)
    n = acc.shape[1]
    out = acc * c + pltpu.roll(acc, n - half, 1) * s1 + pltpu.roll(acc, half, 1) * s2
    return out * scale


def _epi_residual(acc, x, g):
    return x + g * acc


def _ffn_kernel(*refs, mode):
    x_ref, g_ref, sc_ref, sh_ref, g2_ref, w1_ref, w3_ref, w2_ref = refs[:8]
    if mode == "final":
        fn_ref, o_ref, hn_ref = refs[8:]
    elif mode == "next":
        gn_ref, scn_ref, shn_ref, o_ref, hnext_ref, hn_ref = refs[8:]
    else:
        o_ref, hn_ref = refs[8:]
    f = pl.program_id(1)
    tm = x_ref.shape[0]
    halves = [slice(r * (tm // 2), (r + 1) * (tm // 2)) for r in range(2)]

    @pl.when(f == 0)
    def _():
        for rows in halves:
            x = x_ref[rows, :]
            hn_ref[rows, :] = ((_rms(x) * g_ref[...]) * (1.0 + sc_ref[...]) + sh_ref[...]).astype(BF)
            o_ref[rows, :] = x

    w1 = w1_ref[...].astype(BF)
    w3 = w3_ref[...].astype(BF)
    w2 = w2_ref[...].astype(BF)
    for rows in halves:
        h = hn_ref[rows, :]
        h1 = jnp.dot(h, w1, preferred_element_type=F32)
        h3 = jnp.dot(h, w3, preferred_element_type=F32)
        a = (h1 * jax.nn.sigmoid(h1)) * h3
        o_ref[rows, :] += g2_ref[...] * jnp.dot(a.astype(BF), w2, preferred_element_type=F32)

    if mode != "plain":
        @pl.when(f == pl.num_programs(1) - 1)
        def _():
            for rows in halves:
                y = _rms(o_ref[rows, :])
                if mode == "final":
                    o_ref[rows, :] = y * fn_ref[...]
                else:
                    hnext_ref[rows, :] = ((y * gn_ref[...]) * (1.0 + scn_ref[...]) + shn_ref[...]).astype(BF)


def _ffn(x, g, sc, sh, g2, w1, w3, w2, layer, final_gain=None, next_mod=None):
    assert final_gain is None or next_mod is None
    s, d = x.shape
    fdim = w1.shape[2]
    tm = min(1024, s)
    tf = 256
    row = pl.BlockSpec((1, d), lambda i, f: (0, 0))
    tile = pl.BlockSpec((tm, d), lambda i, f: (i, 0))
    in_specs = [pl.BlockSpec((tm, d), lambda i, f: (i, 0), pipeline_mode=pl.Buffered(1)), row, row, row, row,
                pl.BlockSpec((None, d, tf), lambda i, f: (layer, 0, f)),
                pl.BlockSpec((None, d, tf), lambda i, f: (layer, 0, f)),
                pl.BlockSpec((None, tf, d), lambda i, f: (layer, f, 0))]
    arrays = [x, g, sc, sh, g2, w1, w3, w2]
    out_specs, out_shape, mode = tile, jax.ShapeDtypeStruct((s, d), F32), "plain"
    if final_gain is not None:
        in_specs.append(row)
        arrays.append(final_gain)
        mode = "final"
    elif next_mod is not None:
        in_specs += [row, row, row]
        arrays += list(next_mod)
        out_specs = [tile, tile]
        out_shape = [out_shape, jax.ShapeDtypeStruct((s, d), BF)]
        mode = "next"
    return pl.pallas_call(
        functools.partial(_ffn_kernel, mode=mode),
        grid=(s // tm, fdim // tf),
        in_specs=in_specs,
        out_specs=out_specs,
        out_shape=out_shape,
        scratch_shapes=[pltpu.VMEM((tm, d), BF)],
        compiler_params=_cparams("arbitrary", "arbitrary"),
        name="ffn",
    )(*arrays)


def _flash_update(s, vp, m_ref, acc_ref, row_shift=None):
    m_prev = m_ref[...]
    m_tile = jnp.max(s[...], axis=1, keepdims=True)
    if row_shift is not None:
        m_tile = m_tile + row_shift
    m_new = jnp.maximum(m_prev, m_tile)
    alpha = jnp.exp2(m_prev - m_new)
    sub = m_new if row_shift is None else m_new - row_shift
    p = jnp.exp2(s[...] - sub).astype(BF)
    acc_ref[...] = alpha * acc_ref[...] + jnp.dot(p, vp, preferred_element_type=F32)
    m_ref[...] = m_new


def _init_state(m_ref, acc_ref):
    m_ref[...] = jnp.full_like(m_ref, NEG_INF)
    acc_ref[...] = jnp.zeros_like(acc_ref)


def _with_ones(v):
    return jnp.concatenate([v, jnp.ones(v.shape, v.dtype)], axis=1)


def _normalized(acc):
    return acc[:, :LANES] / acc[:, LANES:]


def _causal_mask(t):
    rows = lax.broadcasted_iota(jnp.int32, (t, t), 0)
    cols = lax.broadcasted_iota(jnp.int32, (t, t), 1)
    return cols <= rows


def _causal_sweep(qi, tc, s_bufs, m_ref, acc_ref, streams):
    tw = 2 * tc

    def pre(si, t):
        kside_fn, logits_fn, _, _ = streams[si]
        kside = kside_fn(pl.multiple_of(t * tw, tw), tw)
        for c in range(2):
            s_bufs[si][c] = logits_fn(c, kside)

    def process(si, t):
        _, _, vp_fn, shifts = streams[si]
        vp = vp_fn(pl.multiple_of(t * tw, tw), tw)
        for c in range(2):
            _flash_update(s_bufs[si].at[c], vp, m_ref.at[2 * si + c], acc_ref.at[2 * si + c], row_shift=shifts[c])

    doff = pl.multiple_of(qi * tw, tw)

    def pre_diag(si):
        kside_fn, logits_fn, _, _ = streams[si]
        s_bufs[si][0, :, 0:tc] = jnp.where(_causal_mask(tc), logits_fn(0, kside_fn(doff, tc)), NEG_INF)
        rows = tc + lax.broadcasted_iota(jnp.int32, (tc, tw), 0)
        cols = lax.broadcasted_iota(jnp.int32, (tc, tw), 1)
        s_bufs[si][1] = jnp.where(cols <= rows, logits_fn(1, kside_fn(doff, tw)), NEG_INF)

    def process_diag(si):
        _, _, vp_fn, shifts = streams[si]
        _flash_update(s_bufs[si].at[0, :, 0:tc], vp_fn(doff, tc), m_ref.at[2 * si], acc_ref.at[2 * si],
                      row_shift=shifts[0])
        _flash_update(s_bufs[si].at[1], vp_fn(doff, tw), m_ref.at[2 * si + 1], acc_ref.at[2 * si + 1],
                      row_shift=shifts[1])

    pre_diag(0)
    pre_diag(1)
    process_diag(0)
    pre(0, 0)
    process_diag(1)

    def body(t, carry):
        pre(1, t)
        process(0, t)
        pre(0, t + 1)
        process(1, t)
        return carry

    lax.fori_loop(0, qi, body, 0)


def _sweep_scratch(tc):
    return [pltpu.VMEM((2, tc, 2 * tc), F32), pltpu.VMEM((2, tc, 2 * tc), F32),
            pltpu.VMEM((4, tc, 1), F32), pltpu.VMEM((4, tc, 2 * LANES), F32)]


def _mla_kernel(qn_ref, qr_ref, kv_ref, kr_ref, o_ref, s0_ref, s1_ref, m_ref, acc_ref, *, tc):
    qi = pl.program_id(1)
    _init_state(m_ref, acc_ref)

    def stream(si):
        qs = [jnp.concatenate([qn_ref[si, c * tc:(c + 1) * tc, :], qr_ref[si, c * tc:(c + 1) * tc, :]], axis=1)
              for c in range(2)]

        def kside(off, width):
            return jnp.concatenate([kv_ref[2 * si, pl.ds(off, width), :], kr_ref[pl.ds(off, width), :]], axis=1)

        def logits(c, k):
            return lax.dot_general(qs[c], k, _NT, preferred_element_type=F32)

        def vp(off, width):
            return _with_ones(kv_ref[2 * si + 1, pl.ds(off, width), :])

        return kside, logits, vp, (None, None)

    _causal_sweep(qi, tc, (s0_ref, s1_ref), m_ref, acc_ref, [stream(0), stream(1)])
    for si in range(2):
        for c in range(2):
            o_ref[c * tc:(c + 1) * tc, si * LANES:(si + 1) * LANES] = _normalized(
                acc_ref[2 * si + c]).astype(o_ref.dtype)


def _mla_attention(qn, qr, kvh, kr):
    h, s, _ = qn.shape
    tc = min(512, s // 2)
    w = 2 * tc
    return pl.pallas_call(
        functools.partial(_mla_kernel, tc=tc),
        grid=(h // 2, s // w),
        in_specs=[pl.BlockSpec((2, w, LANES), lambda p, i: (p, i, 0)),
                  pl.BlockSpec((2, w, LANES), lambda p, i: (p, i, 0)),
                  pl.BlockSpec((4, s, LANES), lambda p, i: (p, 0, 0)),
                  pl.BlockSpec((s, LANES), lambda p, i: (0, 0))],
        out_specs=pl.BlockSpec((w, 2 * LANES), lambda p, i: (i, p)),
        out_shape=jax.ShapeDtypeStruct((s, h * LANES), BF),
        scratch_shapes=_sweep_scratch(tc),
        compiler_params=_cparams("arbitrary", "arbitrary"),
        name="mla_attn",
    )(qn, qr, kvh, kr)


def _fox_kernel(q_ref, k_ref, v_ref, ck_ref, cum_ref, og_ref, o_ref, s0_ref, s1_ref, m_ref, acc_ref, *, tc):
    pair = pl.program_id(0)
    qi = pl.program_id(1)
    lane = lax.broadcasted_iota(jnp.int32, (tc, LANES), 1)
    _init_state(m_ref, acc_ref)

    def stream(si):
        qs = [q_ref[si, c * tc:(c + 1) * tc, :] for c in range(2)]
        cqs = [jnp.sum(jnp.where(lane == 2 * pair + si, cum_ref[c * tc:(c + 1) * tc, :], 0.0), axis=1,
                       keepdims=True) for c in range(2)]

        def kside(off, width):
            return k_ref[si, pl.ds(off, width), :], ck_ref[si, :, pl.ds(off, width)]

        def logits(c, kc):
            return lax.dot_general(qs[c], kc[0], _NT, preferred_element_type=F32) - kc[1]

        def vp(off, width):
            return _with_ones(v_ref[si, pl.ds(off, width), :])

        return kside, logits, vp, cqs

    _causal_sweep(qi, tc, (s0_ref, s1_ref), m_ref, acc_ref, [stream(0), stream(1)])
    for si in range(2):
        cols = slice(si * LANES, (si + 1) * LANES)
        for c in range(2):
            rows = slice(c * tc, (c + 1) * tc)
            o_ref[rows, cols] = (_normalized(acc_ref[2 * si + c]) * og_ref[rows, cols]).astype(o_ref.dtype)


def _fox_attention(q, k, v, cum_t, cum, sig_og):
    h, s, _ = q.shape
    tc = min(512, s // 2)
    w = 2 * tc
    hm = pl.BlockSpec((2, s, LANES), lambda p, i: (p, 0, 0))
    return pl.pallas_call(
        functools.partial(_fox_kernel, tc=tc),
        grid=(h // 2, s // w),
        in_specs=[pl.BlockSpec((2, w, LANES), lambda p, i: (p, i, 0)), hm, hm,
                  pl.BlockSpec((2, 1, s), lambda p, i: (p, 0, 0)),
                  pl.BlockSpec((w, LANES), lambda p, i: (i, 0)),
                  pl.BlockSpec((w, 2 * LANES), lambda p, i: (i, p))],
        out_specs=pl.BlockSpec((w, 2 * LANES), lambda p, i: (i, p)),
        out_shape=jax.ShapeDtypeStruct((s, h * LANES), BF),
        scratch_shapes=_sweep_scratch(tc),
        compiler_params=_cparams("arbitrary", "arbitrary"),
        name="fox_attn",
    )(q, k, v, cum_t, cum, sig_og)


def _cumsum_kernel(x_ref, o_ref, ot_ref, carry_ref, *, t, out_scale):
    @pl.when(pl.program_id(0) == 0)
    def _():
        carry_ref[...] = jnp.zeros_like(carry_ref)

    x = x_ref[...]
    rows = lax.broadcasted_iota(jnp.int32, (t, t), 0)
    cols = lax.broadcasted_iota(jnp.int32, (t, t), 1)
    tri = jnp.where(cols <= rows, 1.0, 0.0).astype(BF)
    hi = x.astype(BF)
    r1 = x - hi.astype(F32)
    mid = r1.astype(BF)
    lo = (r1 - mid.astype(F32)).astype(BF)
    cum = (jnp.dot(tri, hi, preferred_element_type=F32) + jnp.dot(tri, mid, preferred_element_type=F32)
           + jnp.dot(tri, lo, preferred_element_type=F32)) + carry_ref[...]
    scaled = cum * out_scale
    o_ref[...] = scaled
    ot_ref[...] = scaled.T
    carry_ref[...] = cum[t - 1:t, :]


def _cumsum_tokens(x, out_scale):
    s, n = x.shape
    t = min(256, s)
    return pl.pallas_call(
        functools.partial(_cumsum_kernel, t=t, out_scale=out_scale),
        grid=(s // t,),
        in_specs=[pl.BlockSpec((t, n), lambda i: (i, 0))],
        out_specs=[pl.BlockSpec((t, n), lambda i: (i, 0)), pl.BlockSpec((n, t), lambda i: (0, i))],
        out_shape=[jax.ShapeDtypeStruct((s, n), F32), jax.ShapeDtypeStruct((n, s), F32)],
        scratch_shapes=[pltpu.VMEM((1, n), F32)],
        compiler_params=_cparams("arbitrary"),
        name="cumsum",
    )(x)


def _compress_kernel(a_ref, pos_ref, w1_ref, w2_ref, o_ref):
    a = a_ref[0]
    nc = a.shape[0]
    p1 = jnp.dot((a + pos_ref[0, 0:1, :]).astype(BF), w1_ref[0, 0], preferred_element_type=F32)
    p2 = jnp.dot((a + pos_ref[0, 1:2, :]).astype(BF), w1_ref[0, 1], preferred_element_type=F32)
    h = p1 + pltpu.roll(p2, nc - 1, 0)
    act = h * jax.nn.sigmoid(h)
    o_ref[0] = jnp.dot(act.astype(BF), w2_ref[0], preferred_element_type=F32).astype(o_ref.dtype)


def _nsa_compress(kv_cmp, pos, w1, w2):
    c, nc, gw = kv_cmp.shape
    return pl.pallas_call(
        _compress_kernel,
        grid=(c,),
        in_specs=[pl.BlockSpec((1, nc, gw), lambda i: (i, 0, 0)),
                  pl.BlockSpec((1, 2, gw), lambda i: (i // 2, 0, 0)),
                  pl.BlockSpec((1, 2, gw, CMP_HIDDEN), lambda i: (i // 2, 0, 0, 0)),
                  pl.BlockSpec((1, CMP_HIDDEN, NSA_HEAD_DIM), lambda i: (i // 2, 0, 0))],
        out_specs=pl.BlockSpec((1, nc, NSA_HEAD_DIM), lambda i: (i, 0, 0)),
        out_shape=jax.ShapeDtypeStruct((c, nc, NSA_HEAD_DIM), BF),
        compiler_params=_cparams("arbitrary"),
        name="nsa_compress",
    )(kv_cmp, pos, w1, w2)


def _bias_table_kernel(brev_ref, basc_ref, tbl_ref, tb_ref, asc_ref, *, s, mc):
    tbl = tbl_ref[0] * LOG2E

    def lookup(bkt):
        out = jnp.zeros(bkt.shape, F32)
        for b in range(REL_BUCKETS):
            out = jnp.where(bkt == b, tbl[:, b:b + 1], out)
        return out

    rev = lookup(brev_ref[...])
    asc_ref[0] = lookup(basc_ref[...])
    for m in range(mc + 1):
        win = rev[:, s - Q_BLOCK * m:s - Q_BLOCK * m + 2 * Q_BLOCK]
        rolled = pltpu.roll(jnp.broadcast_to(win, (Q_BLOCK, 2 * Q_BLOCK)), Q_BLOCK + 1, 1,
                            stride=1, stride_axis=0)
        tb_ref[0, m] = rolled[:, :Q_BLOCK]


def _t5_bucket(dist):
    max_exact = REL_BUCKETS // 2
    d = jnp.maximum(dist, 0)
    ratio = jnp.log(jnp.maximum(d, max_exact).astype(F32) / max_exact) / math.log(REL_MAX_DIST / max_exact)
    large = jnp.minimum(max_exact + (ratio * (REL_BUCKETS - max_exact)).astype(jnp.int32), REL_BUCKETS - 1)
    return jnp.where(d < max_exact, d, large)


def _bias_tables(rel_bias, s):
    assert Q_BLOCK == SUBLANES * CMP_STRIDE
    h = rel_bias.shape[1]
    qblocks = s // Q_BLOCK
    max_exact = REL_BUCKETS // 2
    d_const = int(math.ceil(max_exact * (REL_MAX_DIST / max_exact)
                            ** ((REL_BUCKETS - max_exact - 1) / (REL_BUCKETS - max_exact)))) + 1
    mc = min(-(-(d_const + Q_BLOCK - 1) // Q_BLOCK), qblocks - 1)
    nrev = s + 2 * Q_BLOCK
    nasc = s + 3 * Q_BLOCK
    brev = _t5_bucket(s + Q_BLOCK - 1 - jnp.arange(nrev, dtype=jnp.int32))[None, :]
    basc = _t5_bucket(jnp.arange(nasc, dtype=jnp.int32) - 2 * Q_BLOCK)[None, :]
    tb, asc = pl.pallas_call(
        functools.partial(_bias_table_kernel, s=s, mc=mc),
        grid=(h,),
        in_specs=[pl.BlockSpec((1, nrev), lambda i: (0, 0)),
                  pl.BlockSpec((1, nasc), lambda i: (0, 0)),
                  pl.BlockSpec((1, 1, REL_BUCKETS), lambda i: (i, 0, 0))],
        out_specs=[pl.BlockSpec((1, mc + 1, Q_BLOCK, Q_BLOCK), lambda i: (i, 0, 0, 0)),
                   pl.BlockSpec((1, 1, nasc), lambda i: (i, 0, 0))],
        out_shape=[jax.ShapeDtypeStruct((h, mc + 1, Q_BLOCK, Q_BLOCK), F32),
                   jax.ShapeDtypeStruct((h, 1, nasc), F32)],
        compiler_params=_cparams("arbitrary"),
        name="bias_tables",
    )(brev, basc, rel_bias.T.reshape(h, 1, REL_BUCKETS))
    asc = asc[:, 0, :]
    base = 2 * Q_BLOCK - (CMP_LEN - 1)
    segs = [asc[:, base - CMP_STRIDE * nn:base - CMP_STRIDE * nn + s].reshape(h, qblocks, Q_BLOCK)
            for nn in range(SUBLANES)]
    trc = jnp.flip(jnp.stack(segs, axis=2), axis=1).reshape(h, qblocks * SUBLANES, Q_BLOCK)
    return tb, jnp.pad(trc, ((0, 0), (0, qblocks * SUBLANES), (0, 0)))


def _nsa_cmp_kernel(q_ref, kc_ref, vct_ref, trc_ref, smt_ref, gate_ref, oc_ref, sel_ref, *, qblocks, topn, nb):
    for c in range(nb):
        _nsa_cmp_block(pl.program_id(1) * nb + c, slice(c * Q_BLOCK, (c + 1) * Q_BLOCK), q_ref, kc_ref, vct_ref,
                       trc_ref, smt_ref, gate_ref, oc_ref, sel_ref, qblocks=qblocks, topn=topn)


def _nsa_cmp_block(qb, rows, q_ref, kc_ref, vct_ref, trc_ref, smt_ref, gate_ref, oc_ref, sel_ref, *, qblocks, topn):
    rq = NSA_GROUP * Q_BLOCK
    q = q_ref[:, rows, :].reshape(rq, NSA_HEAD_DIM)
    kc = kc_ref[0]
    nc = kc.shape[0]
    nslc = smt_ref.shape[0]
    s = lax.dot_general(kc, q, _NT, preferred_element_type=F32)
    n_io = lax.broadcasted_iota(jnp.int32, (nc, Q_BLOCK), 0)
    i_io = lax.broadcasted_iota(jnp.int32, (nc, Q_BLOCK), 1)
    mask = (n_io * CMP_STRIDE + (CMP_LEN - 1)) <= (qb * Q_BLOCK + i_io)
    off = pl.multiple_of((qblocks - 1 - qb) * SUBLANES, SUBLANES)
    ps = []
    for r in range(NSA_GROUP):
        l = jnp.where(mask, s[:, r * Q_BLOCK:(r + 1) * Q_BLOCK] + trc_ref[r, pl.ds(off, nc), :], NEG_INF)
        m = jnp.max(l, axis=0, keepdims=True)
        p = jnp.where(mask, jnp.exp2(l - m), 0.0)
        den = jnp.maximum(jnp.sum(p, axis=0, keepdims=True), 1e-30)
        ps.append(p / den)
    p_all = jnp.concatenate(ps, axis=1).astype(BF)
    oc_t = jnp.dot(vct_ref[0], p_all, preferred_element_type=F32)
    imp4 = jnp.dot(smt_ref[...], p_all, preferred_element_type=F32)
    imp = imp4[:, 0:Q_BLOCK]
    for r in range(1, NSA_GROUP):
        imp = imp + imp4[:, r * Q_BLOCK:(r + 1) * Q_BLOCK]

    j_io = lax.broadcasted_iota(jnp.int32, (nslc, Q_BLOCK), 0)
    t_io = qb * Q_BLOCK + lax.broadcasted_iota(jnp.int32, (nslc, Q_BLOCK), 1)
    cur = t_io >> int(math.log2(SLC_LEN))
    forced = jnp.logical_or(j_io == 0, jnp.logical_or(j_io == cur, j_io == cur - 1))
    n_forced = 3
    val = jnp.where(forced, -3e38, jnp.where(j_io <= cur, imp, -1e9))
    sel = jnp.where(forced, 1.0, 0.0)
    for _ in range(max(topn - n_forced, 0)):
        mx = jnp.max(val, axis=0, keepdims=True)
        cand = jnp.where(val == mx, j_io, nslc)
        jmin = jnp.min(cand, axis=0, keepdims=True)
        pick = j_io == jmin
        sel = jnp.where(pick, 1.0, sel)
        val = jnp.where(pick, -3e38, val)
    sel_ref[0, rows, :] = jnp.where(sel.T > 0.5, 0.0, NEG_INF).astype(sel_ref.dtype)

    gates = gate_ref[0, rows, :]
    for r in range(NSA_GROUP):
        o_r = oc_t[:, r * Q_BLOCK:(r + 1) * Q_BLOCK].T
        oc_ref[rows, r * NSA_HEAD_DIM:(r + 1) * NSA_HEAD_DIM] = o_r * gates[:, 3 * r:3 * r + 1]


def _nsa_cmp(q, kc, vct, trc, smt, gates):
    h, s, d = q.shape
    g = NSA_KV_HEADS
    qblocks = s // Q_BLOCK
    nc = kc.shape[1]
    nslc = smt.shape[0]
    topn = min(SLC_TOPN, nslc)
    nb = 2
    qrows = nb * Q_BLOCK
    return pl.pallas_call(
        functools.partial(_nsa_cmp_kernel, qblocks=qblocks, topn=topn, nb=nb),
        grid=(g, qblocks // nb),
        in_specs=[pl.BlockSpec((NSA_GROUP, qrows, d), lambda gg, i: (gg, i, 0)),
                  pl.BlockSpec((1, nc, d), lambda gg, i: (gg, 0, 0)),
                  pl.BlockSpec((1, d, nc), lambda gg, i: (gg, 0, 0)),
                  pl.BlockSpec((NSA_GROUP, trc.shape[1], Q_BLOCK), lambda gg, i: (gg, 0, 0)),
                  pl.BlockSpec(smt.shape, lambda gg, i: (0, 0)),
                  pl.BlockSpec((1, qrows, LANES), lambda gg, i: (gg, i, 0))],
        out_specs=[pl.BlockSpec((qrows, NSA_GROUP * d), lambda gg, i: (i, gg)),
                   pl.BlockSpec((1, qrows, nslc), lambda gg, i: (gg, i, 0))],
        out_shape=[jax.ShapeDtypeStruct((s, h * d), F32),
                   jax.ShapeDtypeStruct((g, s, nslc), BF)],
        compiler_params=_cparams("arbitrary", "arbitrary"),
        name="nsa_cmp",
    )(q, kc, vct, trc, smt, gates)


def _nsa_sw_kernel(q_ref, kv_ref, tb_ref, sel_ref, et_ref, gate_ref, oc_ref,
                   o_ref, s0_ref, s1_ref, m_ref, acc_ref, *, tk, mc, nch):
    i = pl.program_id(0)
    ngrp = NSA_KV_HEADS
    rq = NSA_GROUP * Q_BLOCK
    nsub = tk // Q_BLOCK
    s_bufs = (s0_ref, s1_ref)
    qbs = [i * nch + c for c in range(nch)]
    qs = [[q_ref[g * NSA_GROUP:(g + 1) * NSA_GROUP, c * Q_BLOCK:(c + 1) * Q_BLOCK, :].reshape(rq, NSA_HEAD_DIM)
           for c in range(nch)] for g in range(ngrp)]

    def bias_tile(g, r, c, kb0, nblk):
        return jnp.concatenate(
            [tb_ref[g * NSA_GROUP + r, jnp.clip(qbs[c] - (kb0 + b), 0, mc)] for b in range(nblk)], axis=1)

    def gated(g, c, o, branch, base):
        gates = gate_ref[g, c * Q_BLOCK:(c + 1) * Q_BLOCK, :]
        outs = []
        for r in range(NSA_GROUP):
            o_r = o[r * Q_BLOCK:(r + 1) * Q_BLOCK, :] * gates[:, 3 * r + branch:3 * r + branch + 1]
            outs.append(base[:, r * NSA_HEAD_DIM:(r + 1) * NSA_HEAD_DIM] + o_r)
        return jnp.concatenate(outs, axis=1)

    _init_state(m_ref, acc_ref)
    qa = [[jnp.concatenate([qs[g][c], jnp.concatenate(
        [sel_ref[g, c * Q_BLOCK:(c + 1) * Q_BLOCK, :]] * NSA_GROUP, axis=0)], axis=1) for c in range(nch)]
        for g in range(ngrp)]

    def pre(g, j, causal):
        off = pl.multiple_of(j * tk, tk)
        ka = jnp.concatenate([kv_ref[g, pl.ds(off, tk), :], et_ref[pl.ds(off, tk), :]], axis=1)
        for c in range(nch):
            s = lax.dot_general(qa[g][c], ka, _NT, preferred_element_type=F32)
            if causal:
                kk = off + lax.broadcasted_iota(jnp.int32, (Q_BLOCK, tk), 1)
                ii = qbs[c] * Q_BLOCK + lax.broadcasted_iota(jnp.int32, (Q_BLOCK, tk), 0)
                future = jnp.where(kk <= ii, 0.0, NEG_INF)
            for r in range(NSA_GROUP):
                bias = bias_tile(g, r, c, j * nsub, nsub)
                if causal:
                    bias = bias + future
                s_bufs[g][c, r * Q_BLOCK:(r + 1) * Q_BLOCK, :] = s[r * Q_BLOCK:(r + 1) * Q_BLOCK, :] + bias

    def process(g, j):
        vp = _with_ones(kv_ref[ngrp + g, pl.ds(pl.multiple_of(j * tk, tk), tk), :])
        for c in range(nch):
            _flash_update(s_bufs[g].at[c], vp, m_ref.at[g * nch + c], acc_ref.at[g * nch + c])

    n_past = (i * nch) // nsub

    def body(t, carry):
        pre(1, t, False)
        process(0, t)
        pre(0, t + 1, False)
        process(1, t)
        return carry

    pre(0, n_past, True)
    pre(1, n_past, True)
    process(0, n_past)
    pre(0, 0, False)
    process(1, n_past)
    lax.fori_loop(0, n_past, body, 0)

    wk = WINDOW + Q_BLOCK
    wblk = wk // Q_BLOCK
    for g in range(ngrp):
        cols = slice(g * rq, (g + 1) * rq)
        for c in range(nch):
            rows = slice(c * Q_BLOCK, (c + 1) * Q_BLOCK)
            o_sel = gated(g, c, _normalized(acc_ref[g * nch + c]), 1, oc_ref[rows, cols])
            kb0 = jnp.maximum(qbs[c] - WINDOW // Q_BLOCK, 0)
            off = pl.multiple_of(kb0 * Q_BLOCK, Q_BLOCK)
            s = lax.dot_general(qs[g][c], kv_ref[2 * ngrp + g, pl.ds(off, wk), :], _NT, preferred_element_type=F32)
            rel = (qbs[c] * Q_BLOCK + lax.broadcasted_iota(jnp.int32, (Q_BLOCK, wk), 0)) - (
                off + lax.broadcasted_iota(jnp.int32, (Q_BLOCK, wk), 1))
            outside = jnp.where(jnp.logical_and(rel >= 0, rel < WINDOW), 0.0, NEG_INF)
            l = jnp.concatenate([s[r * Q_BLOCK:(r + 1) * Q_BLOCK, :] + (bias_tile(g, r, c, kb0, wblk) + outside)
                                 for r in range(NSA_GROUP)], axis=0)
            p = jnp.exp2(l - jnp.max(l, axis=1, keepdims=True)).astype(BF)
            ow = jnp.dot(p, _with_ones(kv_ref[3 * ngrp + g, pl.ds(off, wk), :]), preferred_element_type=F32)
            o_ref[rows, cols] = gated(g, c, _normalized(ow), 2, o_sel).astype(o_ref.dtype)


def _nsa_sw(q, kvsw, tb, sel, et, gates, oc):
    h, s, d = q.shape
    g = NSA_KV_HEADS
    nch = 2
    qrows = nch * Q_BLOCK
    tk = min(512, s)
    mc = tb.shape[1] - 1
    nslc = sel.shape[2]
    rq = NSA_GROUP * Q_BLOCK
    assert tk % qrows == 0 and WINDOW + Q_BLOCK <= s and NSA_GROUP * d == rq

    def resident(arr):
        return pl.BlockSpec(arr.shape, lambda i, nd=arr.ndim: (0,) * nd, pipeline_mode=pl.Buffered(1))

    return pl.pallas_call(
        functools.partial(_nsa_sw_kernel, tk=tk, mc=mc, nch=nch),
        grid=(s // qrows,),
        in_specs=[pl.BlockSpec((h, qrows, d), lambda i: (0, i, 0)),
                  resident(kvsw), resident(tb),
                  pl.BlockSpec((g, qrows, nslc), lambda i: (0, i, 0)),
                  resident(et),
                  pl.BlockSpec((g, qrows, LANES), lambda i: (0, i, 0)),
                  pl.BlockSpec((qrows, h * d), lambda i: (i, 0))],
        out_specs=pl.BlockSpec((qrows, h * d), lambda i: (i, 0)),
        out_shape=jax.ShapeDtypeStruct((s, h * d), BF),
        scratch_shapes=[pltpu.VMEM((nch, rq, tk), F32), pltpu.VMEM((nch, rq, tk), F32),
                        pltpu.VMEM((g * nch, rq, 1), F32), pltpu.VMEM((g * nch, rq, 2 * LANES), F32)],
        compiler_params=_cparams("arbitrary"),
        name="nsa_sel_win",
    )(q, kvsw, tb, sel, et, gates, oc)


def _selection_map_t(nc, nslc):
    n = np.arange(nc)[None, :] * CMP_STRIDE
    j0 = np.arange(nslc)[:, None] * SLC_LEN
    valid = np.arange(nc)[None, :] < nc - 1
    return jnp.asarray(((n < j0 + SLC_LEN) & (n + CMP_LEN > j0) & valid).astype(np.float32), dtype=BF)


def _block_onehot(s, nslc):
    tok = np.arange(s)[:, None]
    j = np.arange(nslc)[None, :]
    return jnp.asarray((tok // SLC_LEN == j).astype(np.float32), dtype=BF)


def _rope_tables(positions):
    half = MLA_ROPE // 2
    inv = ROPE_THETA ** (-jnp.arange(half, dtype=F32) / half)
    ang = positions.astype(F32)[:, None] * inv
    cos, sin = jnp.cos(ang), jnp.sin(ang)
    z = jnp.zeros_like(cos)
    zpad = jnp.zeros((positions.shape[0], LANES - MLA_ROPE), F32)
    c = jnp.concatenate([cos, cos, zpad], axis=1)
    s1 = jnp.concatenate([-sin, z, zpad], axis=1)
    s2 = jnp.concatenate([z, sin, zpad], axis=1)
    return c, s1, s2


def _pad_cols(w, n):
    return jnp.pad(w, ((0, 0), (0, n - w.shape[1])))


def _pad_rows(w, n):
    return jnp.pad(w, ((0, n - w.shape[0]), (0, 0)))


def _nsa_mla_mixer(hn, pos, rel_bias, w_in_all, w_out_all, e, gate_b, pos_k, w1_k, w2_k, pos_v, w1_v, w2_v,
                   q_norm, w_uq, kv_norm, w_ukv, x, g1):
    s = hn.shape[0]
    d = NSA_HEAD_DIM
    nq = NSA_HEADS * d
    nkv = 2 * NSA_KV_HEADS * d
    w_in_t = jnp.swapaxes(w_in_all, 1, 2)
    dm = w_in_t.shape[2]
    o0 = nq + 3 * nkv
    w_g_t = w_in_t[e, o0:o0 + 3 * NSA_HEADS]; o0 += 3 * NSA_HEADS
    w_lat_t = w_in_t[e, o0:o0 + MLA_Q_RANK + MLA_KV_RANK]; o0 += MLA_Q_RANK + MLA_KV_RANK
    w_kr_t = w_in_t[e, o0:o0 + MLA_ROPE]

    q_nsa = _proj([hn], [(w_in_t, dm, (e, 0), 0)], functools.partial(_epi_scale, d ** -0.5 * LOG2E), n=nq,
                  out_dtype=BF, head_major=True, tn=1024, w_transposed=True, name="proj_q_nsa")
    kv_cmp = _proj([hn], [(w_in_t, dm, (e, 0), nq)], _epi_id, n=nkv, out_dtype=F32, head_major=True,
                   w_transposed=True, name="proj_kv_cmp")
    kv_sw = _proj([hn], [(w_in_t, dm, (e, 0), nq + nkv)], _epi_id, n=2 * nkv, out_dtype=BF, head_major=True,
                  w_transposed=True, name="proj_kv_sw")
    per_g = 3 * NSA_GROUP
    w_gp_t = jnp.concatenate([_pad_rows(w_g_t[g * per_g:(g + 1) * per_g], LANES) for g in range(NSA_KV_HEADS)], 0)
    b_gp = jnp.concatenate([_pad_cols(gate_b[None, g * per_g:(g + 1) * per_g], LANES)
                            for g in range(NSA_KV_HEADS)], 1)
    gates = _proj([hn], [(w_gp_t[None], dm, (0, 0), 0)], _epi_sigmoid_bias, [(b_gp, "row")],
                  n=NSA_KV_HEADS * LANES, out_dtype=F32, head_major=True, tn=LANES, w_transposed=True,
                  name="proj_gates")

    nc = s // CMP_STRIDE
    gw = CMP_STRIDE * d
    pos_kv = jnp.stack([pos_k.reshape(2, gw), pos_v.reshape(2, gw)])
    w1_kv = jnp.stack([w1_k.reshape(2, gw, CMP_HIDDEN), w1_v.reshape(2, gw, CMP_HIDDEN)]).astype(BF)
    w2_kv = jnp.stack([w2_k, w2_v]).astype(BF)
    kvc = _nsa_compress(kv_cmp.reshape(2 * NSA_KV_HEADS, nc, gw), pos_kv, w1_kv, w2_kv)
    kc = kvc[:NSA_KV_HEADS]
    vct = jnp.swapaxes(kvc[NSA_KV_HEADS:], 1, 2)

    nslc = s // SLC_LEN
    tb, trc = _bias_tables(rel_bias, s)
    smt = _selection_map_t(nc, nslc)
    oc, sel = _nsa_cmp(q_nsa, kc, vct, trc, smt, gates)
    o_nsa = _nsa_sw(q_nsa, kv_sw, tb, sel, _block_onehot(s, nslc), gates, oc)

    nlat = MLA_Q_RANK + MLA_KV_RANK
    lat = _proj([hn], [(w_lat_t[None], dm, (0, 0), 0)], _epi_rmsnorm,
                [(jnp.concatenate([q_norm, kv_norm])[None, :], "row")], n=nlat, out_dtype=BF, tn=MLA_Q_RANK,
                w_transposed=True, name="proj_mla_latent")
    c, s1, s2 = _rope_tables(pos)
    kr = _proj([hn], [(_pad_rows(w_kr_t, LANES)[None], dm, (0, 0), 0)], functools.partial(_epi_rope, 1.0),
               [(c, "rowtile"), (s1, "rowtile"), (s2, "rowtile")], n=LANES, out_dtype=BF, w_transposed=True,
               name="proj_mla_kr")
    scale = (MLA_NOPE + MLA_ROPE) ** -0.5 * LOG2E
    w_uq3 = w_uq.reshape(MLA_Q_RANK, MLA_HEADS, MLA_NOPE + MLA_ROPE)
    w_uqn = w_uq3[:, :, :MLA_NOPE].reshape(MLA_Q_RANK, MLA_HEADS * MLA_NOPE)
    w_uqr = jnp.pad(w_uq3[:, :, MLA_NOPE:], ((0, 0), (0, 0), (0, LANES - MLA_ROPE))).reshape(
        MLA_Q_RANK, MLA_HEADS * LANES)
    qn = _proj([lat], [(w_uqn, MLA_Q_RANK, 0, 0)], functools.partial(_epi_scale, scale),
               n=MLA_HEADS * MLA_NOPE, out_dtype=BF, head_major=True, tn=1024, lhs_col_block=[0],
               name="proj_mla_qn")
    qr = _proj([lat], [(w_uqr, MLA_Q_RANK, 0, 0)], functools.partial(_epi_rope, scale),
               [(c, "rowtile"), (s1, "rowtile"), (s2, "rowtile")], n=MLA_HEADS * LANES, out_dtype=BF,
               head_major=True, lhs_col_block=[0], name="proj_mla_qr")
    kvh = _proj([lat], [(w_ukv, MLA_KV_RANK, 0, 0)], _epi_id, n=MLA_HEADS * (MLA_NOPE + MLA_V), out_dtype=BF,
                head_major=True, tn=1024, lhs_col_block=[1], name="proj_mla_kv")
    o_mla = _mla_attention(qn, qr, kvh, kr)

    return _proj([o_nsa, o_mla], [(w_out_all, nq, (e, 0), 0), (w_out_all, MLA_HEADS * MLA_V, (e, 1), 0)],
                 _epi_residual, [(x, "tile"), (g1, "row")], n=w_out_all.shape[2], out_dtype=F32, tn=1024,
                 name="proj_even_out")


def _fox_mixer(hn, w_in_all, w_out_all, o, f_b, q_norm, k_norm, x, g1):
    d = D_MODEL
    dh = FOX_HEAD_DIM
    w_in_t = jnp.swapaxes(w_in_all, 1, 2)
    w_f_t = w_in_t[o, 3 * d:3 * d + FOX_HEADS]
    w_og_t = w_in_t[o, 3 * d + FOX_HEADS:]
    q = _proj([hn], [(w_in_t, d, (o, 0), 0)], functools.partial(_epi_headnorm, dh ** -0.5 * LOG2E),
              [(q_norm[None, :], "const")], n=d, out_dtype=BF, head_major=True, tn=1024, w_transposed=True,
              name="proj_fox_q")
    k = _proj([hn], [(w_in_t, d, (o, 0), d)], functools.partial(_epi_headnorm, 1.0),
              [(k_norm[None, :], "const")], n=d, out_dtype=BF, head_major=True, tn=1024, w_transposed=True,
              name="proj_fox_k")
    v = _proj([hn], [(w_in_t, d, (o, 0), 2 * d)], _epi_id, n=d, out_dtype=BF, head_major=True, tn=1024,
              w_transposed=True, name="proj_fox_v")
    lf = _proj([hn], [(_pad_rows(w_f_t, LANES)[None], d, (0, 0), 0)], _epi_logsigmoid_bias,
               [(_pad_cols(f_b[None, :], LANES), "row")], n=LANES, out_dtype=F32, w_transposed=True,
               name="proj_fox_f")
    sig_og = _proj([hn], [(w_og_t[None], d, (0, 0), 0)], _epi_sigmoid, n=d, out_dtype=F32, tn=1024,
                   w_transposed=True, name="proj_fox_og")
    cum, cum_t = _cumsum_tokens(lf, LOG2E)
    cum_t = cum_t[:FOX_HEADS].reshape(FOX_HEADS, 1, -1)
    att = _fox_attention(q, k, v, cum_t, cum, sig_og)
    return _proj([att], [(w_out_all, d, (o, 0), 0)], _epi_residual, [(x, "tile"), (g1, "row")], n=d,
                 out_dtype=F32, tn=1024, name="proj_fox_out")


def kernel(x, c, positions, rel_bias, ada_w, ada_b, norm_mix, norm_ffn, ffn_w1, ffn_w3, ffn_w2, even_w_in, even_w_out, nsa_gate_b, nsa_cmp_pos_k, nsa_cmp_w1_k, nsa_cmp_w2_k, nsa_cmp_pos_v, nsa_cmp_w1_v, nsa_cmp_w2_v, mla_q_norm, mla_w_uq, mla_kv_norm, mla_w_ukv, fox_w_in, fox_w_out, fox_f_b, fox_q_norm, fox_k_norm, final_norm):
    b, s, d = x.shape
    assert b == 1 and d == D_MODEL and s % 1024 == 0
    xs = x[0]
    pos = positions[0]
    mod = _adaln(c, ada_w, ada_b)
    depth = ada_w.shape[0]
    hn = None
    for i in range(depth):
        sh1, sc1, g1, sh2, sc2, g2 = [mod[i:i + 1, k * d:(k + 1) * d] for k in range(6)]
        if hn is None:
            hn = _normmod(xs, norm_mix[i][None, :], sc1, sh1)
        if i % 2 == 0:
            e = i // 2
            xs = _nsa_mla_mixer(hn, pos, rel_bias, even_w_in, even_w_out, e, nsa_gate_b[e],
                                nsa_cmp_pos_k[e], nsa_cmp_w1_k[e], nsa_cmp_w2_k[e],
                                nsa_cmp_pos_v[e], nsa_cmp_w1_v[e], nsa_cmp_w2_v[e],
                                mla_q_norm[e], mla_w_uq[e], mla_kv_norm[e], mla_w_ukv[e], xs, g1)
        else:
            o = i // 2
            xs = _fox_mixer(hn, fox_w_in, fox_w_out, o, fox_f_b[o], fox_q_norm[o], fox_k_norm[o], xs, g1)
        if i == depth - 1:
            xs = _ffn(xs, norm_ffn[i][None, :], sc2, sh2, g2, ffn_w1, ffn_w3, ffn_w2, i,
                      final_gain=final_norm[None, :])
        else:
            nxt = (norm_mix[i + 1][None, :], mod[i + 1:i + 2, d:2 * d], mod[i + 1:i + 2, 0:d])
            xs, hn = _ffn(xs, norm_ffn[i][None, :], sc2, sh2, g2, ffn_w1, ffn_w3, ffn_w2, i, next_mod=nxt)
    return xs[None]
```

```python
import functools
import math

import numpy as np
import jax
import jax.numpy as jnp
from jax import lax
from jax.experimental import pallas as pl
from jax.experimental.pallas import tpu as pltpu

D_MODEL = 2048
DEPTH = 2
EPS = 1e-6
NEG_INF = -1e30

NSA_HEADS = 8
NSA_KV_HEADS = 2
NSA_GROUP = NSA_HEADS // NSA_KV_HEADS
NSA_HEAD_DIM = 128
CMP_LEN = 32
CMP_STRIDE = 16
CMP_HIDDEN = 256
SLC_LEN = 64
SLC_TOPN = 16
WINDOW = 512
Q_BLOCK = 128

MLA_HEADS = 8
MLA_Q_RANK = 512
MLA_KV_RANK = 512
MLA_NOPE = 128
MLA_ROPE = 64
MLA_V = 128
ROPE_THETA = 10000.0

FOX_HEADS = 16
FOX_HEAD_DIM = D_MODEL // FOX_HEADS

REL_BUCKETS = 32
REL_MAX_DIST = 4096

FFN_HIDDEN = ((8 * D_MODEL + 2) // 3 + 255) // 256 * 256

LANES = 128
SUBLANES = 8
VMEM_LIMIT_BYTES = 56 * 1024 * 1024
LOG2E = math.log2(math.e)

BF = jnp.bfloat16
F32 = jnp.float32
_NT = (((1,), (1,)), ((), ()))


def _cparams(*sem):
    return pltpu.CompilerParams(dimension_semantics=sem, vmem_limit_bytes=VMEM_LIMIT_BYTES)


def _rms(x):
    return x * lax.rsqrt(jnp.mean(x * x, axis=-1, keepdims=True) + EPS)


def _adaln_kernel(c_ref, w_ref, b_ref, o_ref):
    c = c_ref[...]
    cond = c * jax.nn.sigmoid(c)
    acc = jnp.dot(cond.astype(BF), w_ref[0].astype(BF), preferred_element_type=F32)
    o_ref[0] = acc + b_ref[0]


def _adaln(c, ada_w, ada_b):
    depth, d, n = ada_w.shape
    tn = 1024
    c8 = jnp.broadcast_to(c, (SUBLANES, d))
    out = pl.pallas_call(
        _adaln_kernel,
        grid=(depth, n // tn),
        in_specs=[pl.BlockSpec((SUBLANES, d), lambda i, j: (0, 0)),
                  pl.BlockSpec((1, d, tn), lambda i, j: (i, 0, j)),
                  pl.BlockSpec((1, 1, tn), lambda i, j: (i, 0, j))],
        out_specs=pl.BlockSpec((1, SUBLANES, tn), lambda i, j: (i, 0, j)),
        out_shape=jax.ShapeDtypeStruct((depth, SUBLANES, n), F32),
        compiler_params=_cparams("arbitrary", "arbitrary"),
        name="adaln",
    )(c8, ada_w, ada_b.reshape(depth, 1, n))
    return out[:, 0, :]


def _normmod_kernel(x_ref, g_ref, sc_ref, sh_ref, o_ref):
    y = _rms(x_ref[...])
    o_ref[...] = ((y * g_ref[...]) * (1.0 + sc_ref[...]) + sh_ref[...]).astype(o_ref.dtype)


def _normmod(x, g, sc, sh):
    s, d = x.shape
    tm = 512
    row = pl.BlockSpec((1, d), lambda i: (0, 0))
    return pl.pallas_call(
        _normmod_kernel,
        grid=(s // tm,),
        in_specs=[pl.BlockSpec((tm, d), lambda i: (i, 0)), row, row, row],
        out_specs=pl.BlockSpec((tm, d), lambda i: (i, 0)),
        out_shape=jax.ShapeDtypeStruct((s, d), BF),
        compiler_params=_cparams("arbitrary"),
        name="normmod",
    )(x, g, sc, sh)


def _proj_kernel(*refs, n_lhs, n_epi, epi, head_major, w_transposed):
    lhs = refs[:n_lhs]
    ws = refs[n_lhs:2 * n_lhs]
    epis = refs[2 * n_lhs:2 * n_lhs + n_epi]
    o_ref = refs[2 * n_lhs + n_epi]
    wbf = refs[2 * n_lhs + n_epi + 1:]

    @pl.when(pl.program_id(1) == 0)
    def _():
        for w, wb in zip(ws, wbf):
            wb[...] = w[...].astype(BF)

    acc = None
    for a, wb in zip(lhs, wbf):
        if w_transposed:
            d = lax.dot_general(a[...], wb[...], _NT, preferred_element_type=F32)
        else:
            d = jnp.dot(a[...], wb[...], preferred_element_type=F32)
        acc = d if acc is None else acc + d
    res = epi(acc, *[e[...] for e in epis])
    if head_major:
        for r in range(o_ref.shape[0]):
            o_ref[r] = res[:, r * LANES:(r + 1) * LANES].astype(o_ref.dtype)
    else:
        o_ref[...] = res.astype(o_ref.dtype)


def _proj(lhs, ws, epi, epi_in=(), *, n, out_dtype, head_major=False, tm=1024, tn=512,
          lhs_col_block=None, w_transposed=False, name="proj"):
    m = lhs[0].shape[0]
    tm = min(tm, m)
    tn = min(tn, n)
    if lhs_col_block is None:
        lhs_col_block = [0] * len(lhs)
    in_specs = []
    for (_, k, _, _), cb in zip(ws, lhs_col_block):
        in_specs.append(pl.BlockSpec((tm, k), lambda j, i, cb=cb: (i, cb)))
    for arr, k, rb, col0 in ws:
        assert col0 % tn == 0
        if w_transposed:
            in_specs.append(pl.BlockSpec((None, tn, k),
                                         lambda j, i, rb=rb, cb0=col0 // tn: (rb[0], cb0 + j, rb[1])))
        elif arr.ndim == 3:
            in_specs.append(pl.BlockSpec((None, k, tn),
                                         lambda j, i, rb=rb, cb0=col0 // tn: (rb[0], rb[1], cb0 + j)))
        else:
            in_specs.append(pl.BlockSpec((k, tn), lambda j, i, rb=rb, cb0=col0 // tn: (rb, cb0 + j)))
    arrays = list(lhs) + [w[0] for w in ws]
    for arr, kind in epi_in:
        if kind == "row":
            in_specs.append(pl.BlockSpec((1, tn), lambda j, i: (0, j)))
        elif kind == "const":
            in_specs.append(pl.BlockSpec(arr.shape, lambda j, i: (0, 0)))
        elif kind == "tile":
            in_specs.append(pl.BlockSpec((tm, tn), lambda j, i: (i, j)))
        elif kind == "rowtile":
            in_specs.append(pl.BlockSpec((tm, arr.shape[1]), lambda j, i: (i, 0)))
        else:
            raise ValueError(kind)
        arrays.append(arr)
    if head_major:
        hpt = tn // LANES
        out_spec = pl.BlockSpec((hpt, tm, LANES), lambda j, i: (j, i, 0))
        out_shape = jax.ShapeDtypeStruct((n // LANES, m, LANES), out_dtype)
    else:
        out_spec = pl.BlockSpec((tm, tn), lambda j, i: (i, j))
        out_shape = jax.ShapeDtypeStruct((m, n), out_dtype)
    kern = functools.partial(_proj_kernel, n_lhs=len(lhs), n_epi=len(epi_in), epi=epi,
                             head_major=head_major, w_transposed=w_transposed)
    return pl.pallas_call(
        kern,
        grid=(n // tn, m // tm),
        in_specs=in_specs,
        out_specs=out_spec,
        out_shape=out_shape,
        scratch_shapes=[pltpu.VMEM((tn, k) if w_transposed else (k, tn), BF) for _, k, _, _ in ws],
        compiler_params=_cparams("arbitrary", "arbitrary"),
        name=name,
    )(*arrays)


def _epi_id(acc):
    return acc


def _epi_scale(scale, acc):
    return acc * scale


def _epi_sigmoid_bias(acc, b):
    return jax.nn.sigmoid(acc + b)


def _epi_logsigmoid_bias(acc, b):
    return jax.nn.log_sigmoid(acc + b)


def _epi_sigmoid(acc):
    return jax.nn.sigmoid(acc)


def _epi_rmsnorm(acc, g):
    return _rms(acc) * g


def _epi_headnorm(scale, acc, g):
    outs = []
    for r in range(acc.shape[1] // LANES):
        outs.append(_rms(acc[:, r * LANES:(r + 1) * LANES]) * g * scale)
    return jnp.concatenate(outs, axis=1)


def _epi_rope(scale, acc, c, s1, s2):
    reps = acc.shape[1] // LANES
    half = MLA_ROPE // 2
    if reps > 1:
        c = jnp.concatenate([c] * reps, axis=1)
        s1 = jnp.concatenate([s1] * reps, axis=1)
        s2 = jnp.concatenate([s2] * reps, axis=1)
    n = acc.shape[1]
    out = acc * c + pltpu.roll(acc, n - half, 1) * s1 + pltpu.roll(acc, half, 1) * s2
    return out * scale


def _epi_residual(acc, x, g):
    return x + g * acc


def _ffn_kernel(*refs, mode):
    x_ref, g_ref, sc_ref, sh_ref, g2_ref, w1_ref, w3_ref, w2_ref = refs[:8]
    if mode == "final":
        fn_ref, o_ref, hn_ref = refs[8:]
    elif mode == "next":
        gn_ref, scn_ref, shn_ref, o_ref, hnext_ref, hn_ref = refs[8:]
    else:
        o_ref, hn_ref = refs[8:]
    f = pl.program_id(1)
    tm = x_ref.shape[0]
    halves = [slice(r * (tm // 2), (r + 1) * (tm // 2)) for r in range(2)]

    @pl.when(f == 0)
    def _():
        for rows in halves:
            x = x_ref[rows, :]
            hn_ref[rows, :] = ((_rms(x) * g_ref[...]) * (1.0 + sc_ref[...]) + sh_ref[...]).astype(BF)
            o_ref[rows, :] = x

    w1 = w1_ref[...].astype(BF)
    w3 = w3_ref[...].astype(BF)
    w2 = w2_ref[...].astype(BF)
    for rows in halves:
        h = hn_ref[rows, :]
        h1 = jnp.dot(h, w1, preferred_element_type=F32)
        h3 = jnp.dot(h, w3, preferred_element_type=F32)
        a = (h1 * jax.nn.sigmoid(h1)) * h3
        o_ref[rows, :] += g2_ref[...] * jnp.dot(a.astype(BF), w2, preferred_element_type=F32)

    if mode != "plain":
        @pl.when(f == pl.num_programs(1) - 1)
        def _():
            for rows in halves:
                y = _rms(o_ref[rows, :])
                if mode == "final":
                    o_ref[rows, :] = y * fn_ref[...]
                else:
                    hnext_ref[rows, :] = ((y * gn_ref[...]) * (1.0 + scn_ref[...]) + shn_ref[...]).astype(BF)


def _ffn(x, g, sc, sh, g2, w1, w3, w2, layer, final_gain=None, next_mod=None):
    assert final_gain is None or next_mod is None
    s, d = x.shape
    fdim = w1.shape[2]
    tm = min(1024, s)
    tf = 256
    row = pl.BlockSpec((1, d), lambda i, f: (0, 0))
    tile = pl.BlockSpec((tm, d), lambda i, f: (i, 0))
    in_specs = [pl.BlockSpec((tm, d), lambda i, f: (i, 0), pipeline_mode=pl.Buffered(1)), row, row, row, row,
                pl.BlockSpec((None, d, tf), lambda i, f: (layer, 0, f)),
                pl.BlockSpec((None, d, tf), lambda i, f: (layer, 0, f)),
                pl.BlockSpec((None, tf, d), lambda i, f: (layer, f, 0))]
    arrays = [x, g, sc, sh, g2, w1, w3, w2]
    out_specs, out_shape, mode = tile, jax.ShapeDtypeStruct((s, d), F32), "plain"
    if final_gain is not None:
        in_specs.append(row)
        arrays.append(final_gain)
        mode = "final"
    elif next_mod is not None:
        in_specs += [row, row, row]
        arrays += list(next_mod)
        out_specs = [tile, tile]
        out_shape = [out_shape, jax.ShapeDtypeStruct((s, d), BF)]
        mode = "next"
    return pl.pallas_call(
        functools.partial(_ffn_kernel, mode=mode),
        grid=(s // tm, fdim // tf),
        in_specs=in_specs,
        out_specs=out_specs,
        out_shape=out_shape,
        scratch_shapes=[pltpu.VMEM((tm, d), BF)],
        compiler_params=_cparams("arbitrary", "arbitrary"),
        name="ffn",
    )(*arrays)


def _flash_update(s, vp, m_ref, acc_ref, row_shift=None):
    m_prev = m_ref[...]
    m_tile = jnp.max(s[...], axis=1, keepdims=True)
    if row_shift is not None:
        m_tile = m_tile + row_shift
    m_new = jnp.maximum(m_prev, m_tile)
    alpha = jnp.exp2(m_prev - m_new)
    sub = m_new if row_shift is None else m_new - row_shift
    p = jnp.exp2(s[...] - sub).astype(BF)
    acc_ref[...] = alpha * acc_ref[...] + jnp.dot(p, vp, preferred_element_type=F32)
    m_ref[...] = m_new


def _init_state(m_ref, acc_ref):
    m_ref[...] = jnp.full_like(m_ref, NEG_INF)
    acc_ref[...] = jnp.zeros_like(acc_ref)


def _with_ones(v):
    return jnp.concatenate([v, jnp.ones(v.shape, v.dtype)], axis=1)


def _normalized(acc):
    return acc[:, :LANES] / acc[:, LANES:]


def _causal_mask(t):
    rows = lax.broadcasted_iota(jnp.int32, (t, t), 0)
    cols = lax.broadcasted_iota(jnp.int32, (t, t), 1)
    return cols <= rows


def _causal_sweep(qi, tc, s_bufs, m_ref, acc_ref, streams):
    tw = 2 * tc

    def pre(si, t):
        kside_fn, logits_fn, _, _ = streams[si]
        kside = kside_fn(pl.multiple_of(t * tw, tw), tw)
        for c in range(2):
            s_bufs[si][c] = logits_fn(c, kside)

    def process(si, t):
        _, _, vp_fn, shifts = streams[si]
        vp = vp_fn(pl.multiple_of(t * tw, tw), tw)
        for c in range(2):
            _flash_update(s_bufs[si].at[c], vp, m_ref.at[2 * si + c], acc_ref.at[2 * si + c], row_shift=shifts[c])

    doff = pl.multiple_of(qi * tw, tw)

    def pre_diag(si):
        kside_fn, logits_fn, _, _ = streams[si]
        s_bufs[si][0, :, 0:tc] = jnp.where(_causal_mask(tc), logits_fn(0, kside_fn(doff, tc)), NEG_INF)
        rows = tc + lax.broadcasted_iota(jnp.int32, (tc, tw), 0)
        cols = lax.broadcasted_iota(jnp.int32, (tc, tw), 1)
        s_bufs[si][1] = jnp.where(cols <= rows, logits_fn(1, kside_fn(doff, tw)), NEG_INF)

    def process_diag(si):
        _, _, vp_fn, shifts = streams[si]
        _flash_update(s_bufs[si].at[0, :, 0:tc], vp_fn(doff, tc), m_ref.at[2 * si], acc_ref.at[2 * si],
                      row_shift=shifts[0])
        _flash_update(s_bufs[si].at[1], vp_fn(doff, tw), m_ref.at[2 * si + 1], acc_ref.at[2 * si + 1],
                      row_shift=shifts[1])

    pre_diag(0)
    pre_diag(1)
    process_diag(0)
    pre(0, 0)
    process_diag(1)

    def body(t, carry):
        pre(1, t)
        process(0, t)
        pre(0, t + 1)
        process(1, t)
        return carry

    lax.fori_loop(0, qi, body, 0)


def _sweep_scratch(tc):
    return [pltpu.VMEM((2, tc, 2 * tc), F32), pltpu.VMEM((2, tc, 2 * tc), F32),
            pltpu.VMEM((4, tc, 1), F32), pltpu.VMEM((4, tc, 2 * LANES), F32)]


def _mla_kernel(qn_ref, qr_ref, kv_ref, kr_ref, o_ref, s0_ref, s1_ref, m_ref, acc_ref, *, tc):
    qi = pl.program_id(1)
    _init_state(m_ref, acc_ref)

    def stream(si):
        qs = [jnp.concatenate([qn_ref[si, c * tc:(c + 1) * tc, :], qr_ref[si, c * tc:(c + 1) * tc, :]], axis=1)
              for c in range(2)]

        def kside(off, width):
            return jnp.concatenate([kv_ref[2 * si, pl.ds(off, width), :], kr_ref[pl.ds(off, width), :]], axis=1)

        def logits(c, k):
            return lax.dot_general(qs[c], k, _NT, preferred_element_type=F32)

        def vp(off, width):
            return _with_ones(kv_ref[2 * si + 1, pl.ds(off, width), :])

        return kside, logits, vp, (None, None)

    _causal_sweep(qi, tc, (s0_ref, s1_ref), m_ref, acc_ref, [stream(0), stream(1)])
    for si in range(2):
        for c in range(2):
            o_ref[c * tc:(c + 1) * tc, si * LANES:(si + 1) * LANES] = _normalized(
                acc_ref[2 * si + c]).astype(o_ref.dtype)


def _mla_attention(qn, qr, kvh, kr):
    h, s, _ = qn.shape
    tc = min(512, s // 2)
    w = 2 * tc
    return pl.pallas_call(
        functools.partial(_mla_kernel, tc=tc),
        grid=(h // 2, s // w),
        in_specs=[pl.BlockSpec((2, w, LANES), lambda p, i: (p, i, 0)),
                  pl.BlockSpec((2, w, LANES), lambda p, i: (p, i, 0)),
                  pl.BlockSpec((4, s, LANES), lambda p, i: (p, 0, 0)),
                  pl.BlockSpec((s, LANES), lambda p, i: (0, 0))],
        out_specs=pl.BlockSpec((w, 2 * LANES), lambda p, i: (i, p)),
        out_shape=jax.ShapeDtypeStruct((s, h * LANES), BF),
        scratch_shapes=_sweep_scratch(tc),
        compiler_params=_cparams("arbitrary", "arbitrary"),
        name="mla_attn",
    )(qn, qr, kvh, kr)


def _fox_kernel(q_ref, k_ref, v_ref, ck_ref, cum_ref, og_ref, o_ref, s0_ref, s1_ref, m_ref, acc_ref, *, tc):
    pair = pl.program_id(0)
    qi = pl.program_id(1)
    lane = lax.broadcasted_iota(jnp.int32, (tc, LANES), 1)
    _init_state(m_ref, acc_ref)

    def stream(si):
        qs = [q_ref[si, c * tc:(c + 1) * tc, :] for c in range(2)]
        cqs = [jnp.sum(jnp.where(lane == 2 * pair + si, cum_ref[c * tc:(c + 1) * tc, :], 0.0), axis=1,
                       keepdims=True) for c in range(2)]

        def kside(off, width):
            return k_ref[si, pl.ds(off, width), :], ck_ref[si, :, pl.ds(off, width)]

        def logits(c, kc):
            return lax.dot_general(qs[c], kc[0], _NT, preferred_element_type=F32) - kc[1]

        def vp(off, width):
            return _with_ones(v_ref[si, pl.ds(off, width), :])

        return kside, logits, vp, cqs

    _causal_sweep(qi, tc, (s0_ref, s1_ref), m_ref, acc_ref, [stream(0), stream(1)])
    for si in range(2):
        cols = slice(si * LANES, (si + 1) * LANES)
        for c in range(2):
            rows = slice(c * tc, (c + 1) * tc)
            o_ref[rows, cols] = (_normalized(acc_ref[2 * si + c]) * og_ref[rows, cols]).astype(o_ref.dtype)


def _fox_attention(q, k, v, cum_t, cum, sig_og):
    h, s, _ = q.shape
    tc = min(512, s // 2)
    w = 2 * tc
    hm = pl.BlockSpec((2, s, LANES), lambda p, i: (p, 0, 0))
    return pl.pallas_call(
        functools.partial(_fox_kernel, tc=tc),
        grid=(h // 2, s // w),
        in_specs=[pl.BlockSpec((2, w, LANES), lambda p, i: (p, i, 0)), hm, hm,
                  pl.BlockSpec((2, 1, s), lambda p, i: (p, 0, 0)),
                  pl.BlockSpec((w, LANES), lambda p, i: (i, 0)),
                  pl.BlockSpec((w, 2 * LANES), lambda p, i: (i, p))],
        out_specs=pl.BlockSpec((w, 2 * LANES), lambda p, i: (i, p)),
        out_shape=jax.ShapeDtypeStruct((s, h * LANES), BF),
        scratch_shapes=_sweep_scratch(tc),
        compiler_params=_cparams("arbitrary", "arbitrary"),
        name="fox_attn",
    )(q, k, v, cum_t, cum, sig_og)


def _cumsum_kernel(x_ref, o_ref, ot_ref, carry_ref, *, t, out_scale):
    @pl.when(pl.program_id(0) == 0)
    def _():
        carry_ref[...] = jnp.zeros_like(carry_ref)

    x = x_ref[...]
    rows = lax.broadcasted_iota(jnp.int32, (t, t), 0)
    cols = lax.broadcasted_iota(jnp.int32, (t, t), 1)
    tri = jnp.where(cols <= rows, 1.0, 0.0).astype(BF)
    hi = x.astype(BF)
    r1 = x - hi.astype(F32)
    mid = r1.astype(BF)
    lo = (r1 - mid.astype(F32)).astype(BF)
    cum = (jnp.dot(tri, hi, preferred_element_type=F32) + jnp.dot(tri, mid, preferred_element_type=F32)
           + jnp.dot(tri, lo, preferred_element_type=F32)) + carry_ref[...]
    scaled = cum * out_scale
    o_ref[...] = scaled
    ot_ref[...] = scaled.T
    carry_ref[...] = cum[t - 1:t, :]


def _cumsum_tokens(x, out_scale):
    s, n = x.shape
    t = min(256, s)
    return pl.pallas_call(
        functools.partial(_cumsum_kernel, t=t, out_scale=out_scale),
        grid=(s // t,),
        in_specs=[pl.BlockSpec((t, n), lambda i: (i, 0))],
        out_specs=[pl.BlockSpec((t, n), lambda i: (i, 0)), pl.BlockSpec((n, t), lambda i: (0, i))],
        out_shape=[jax.ShapeDtypeStruct((s, n), F32), jax.ShapeDtypeStruct((n, s), F32)],
        scratch_shapes=[pltpu.VMEM((1, n), F32)],
        compiler_params=_cparams("arbitrary"),
        name="cumsum",
    )(x)


def _compress_kernel(a_ref, pos_ref, w1_ref, w2_ref, o_ref):
    a = a_ref[0]
    nc = a.shape[0]
    p1 = jnp.dot((a + pos_ref[0, 0:1, :]).astype(BF), w1_ref[0, 0], preferred_element_type=F32)
    p2 = jnp.dot((a + pos_ref[0, 1:2, :]).astype(BF), w1_ref[0, 1], preferred_element_type=F32)
    h = p1 + pltpu.roll(p2, nc - 1, 0)
    act = h * jax.nn.sigmoid(h)
    o_ref[0] = jnp.dot(act.astype(BF), w2_ref[0], preferred_element_type=F32).astype(o_ref.dtype)


def _nsa_compress(kv_cmp, pos, w1, w2):
    c, nc, gw = kv_cmp.shape
    return pl.pallas_call(
        _compress_kernel,
        grid=(c,),
        in_specs=[pl.BlockSpec((1, nc, gw), lambda i: (i, 0, 0)),
                  pl.BlockSpec((1, 2, gw), lambda i: (i // 2, 0, 0)),
                  pl.BlockSpec((1, 2, gw, CMP_HIDDEN), lambda i: (i // 2, 0, 0, 0)),
                  pl.BlockSpec((1, CMP_HIDDEN, NSA_HEAD_DIM), lambda i: (i // 2, 0, 0))],
        out_specs=pl.BlockSpec((1, nc, NSA_HEAD_DIM), lambda i: (i, 0, 0)),
        out_shape=jax.ShapeDtypeStruct((c, nc, NSA_HEAD_DIM), BF),
        compiler_params=_cparams("arbitrary"),
        name="nsa_compress",
    )(kv_cmp, pos, w1, w2)


def _bias_table_kernel(brev_ref, basc_ref, tbl_ref, tb_ref, asc_ref, *, s, mc):
    tbl = tbl_ref[0] * LOG2E

    def lookup(bkt):
        out = jnp.zeros(bkt.shape, F32)
        for b in range(REL_BUCKETS):
            out = jnp.where(bkt == b, tbl[:, b:b + 1], out)
        return out

    rev = lookup(brev_ref[...])
    asc_ref[0] = lookup(basc_ref[...])
    for m in range(mc + 1):
        win = rev[:, s - Q_BLOCK * m:s - Q_BLOCK * m + 2 * Q_BLOCK]
        rolled = pltpu.roll(jnp.broadcast_to(win, (Q_BLOCK, 2 * Q_BLOCK)), Q_BLOCK + 1, 1,
                            stride=1, stride_axis=0)
        tb_ref[0, m] = rolled[:, :Q_BLOCK]


def _t5_bucket(dist):
    max_exact = REL_BUCKETS // 2
    d = jnp.maximum(dist, 0)
    ratio = jnp.log(jnp.maximum(d, max_exact).astype(F32) / max_exact) / math.log(REL_MAX_DIST / max_exact)
    large = jnp.minimum(max_exact + (ratio * (REL_BUCKETS - max_exact)).astype(jnp.int32), REL_BUCKETS - 1)
    return jnp.where(d < max_exact, d, large)


def _bias_tables(rel_bias, s):
    assert Q_BLOCK == SUBLANES * CMP_STRIDE
    h = rel_bias.shape[1]
    qblocks = s // Q_BLOCK
    max_exact = REL_BUCKETS // 2
    d_const = int(math.ceil(max_exact * (REL_MAX_DIST / max_exact)
                            ** ((REL_BUCKETS - max_exact - 1) / (REL_BUCKETS - max_exact)))) + 1
    mc = min(-(-(d_const + Q_BLOCK - 1) // Q_BLOCK), qblocks - 1)
    nrev = s + 2 * Q_BLOCK
    nasc = s + 3 * Q_BLOCK
    brev = _t5_bucket(s + Q_BLOCK - 1 - jnp.arange(nrev, dtype=jnp.int32))[None, :]
    basc = _t5_bucket(jnp.arange(nasc, dtype=jnp.int32) - 2 * Q_BLOCK)[None, :]
    tb, asc = pl.pallas_call(
        functools.partial(_bias_table_kernel, s=s, mc=mc),
        grid=(h,),
        in_specs=[pl.BlockSpec((1, nrev), lambda i: (0, 0)),
                  pl.BlockSpec((1, nasc), lambda i: (0, 0)),
                  pl.BlockSpec((1, 1, REL_BUCKETS), lambda i: (i, 0, 0))],
        out_specs=[pl.BlockSpec((1, mc + 1, Q_BLOCK, Q_BLOCK), lambda i: (i, 0, 0, 0)),
                   pl.BlockSpec((1, 1, nasc), lambda i: (i, 0, 0))],
        out_shape=[jax.ShapeDtypeStruct((h, mc + 1, Q_BLOCK, Q_BLOCK), F32),
                   jax.ShapeDtypeStruct((h, 1, nasc), F32)],
        compiler_params=_cparams("arbitrary"),
        name="bias_tables",
    )(brev, basc, rel_bias.T.reshape(h, 1, REL_BUCKETS))
    asc = asc[:, 0, :]
    base = 2 * Q_BLOCK - (CMP_LEN - 1)
    segs = [asc[:, base - CMP_STRIDE * nn:base - CMP_STRIDE * nn + s].reshape(h, qblocks, Q_BLOCK)
            for nn in range(SUBLANES)]
    trc = jnp.flip(jnp.stack(segs, axis=2), axis=1).reshape(h, qblocks * SUBLANES, Q_BLOCK)
    return tb, jnp.pad(trc, ((0, 0), (0, qblocks * SUBLANES), (0, 0)))


def _nsa_cmp_kernel(q_ref, kc_ref, vct_ref, trc_ref, smt_ref, gate_ref, oc_ref, sel_ref, *, qblocks, topn, nb):
    for c in range(nb):
        _nsa_cmp_block(pl.program_id(1) * nb + c, slice(c * Q_BLOCK, (c + 1) * Q_BLOCK), q_ref, kc_ref, vct_ref,
                       trc_ref, smt_ref, gate_ref, oc_ref, sel_ref, qblocks=qblocks, topn=topn)


def _nsa_cmp_block(qb, rows, q_ref, kc_ref, vct_ref, trc_ref, smt_ref, gate_ref, oc_ref, sel_ref, *, qblocks, topn):
    rq = NSA_GROUP * Q_BLOCK
    q = q_ref[:, rows, :].reshape(rq, NSA_HEAD_DIM)
    kc = kc_ref[0]
    nc = kc.shape[0]
    nslc = smt_ref.shape[0]
    s = lax.dot_general(kc, q, _NT, preferred_element_type=F32)
    n_io = lax.broadcasted_iota(jnp.int32, (nc, Q_BLOCK), 0)
    i_io = lax.broadcasted_iota(jnp.int32, (nc, Q_BLOCK), 1)
    mask = (n_io * CMP_STRIDE + (CMP_LEN - 1)) <= (qb * Q_BLOCK + i_io)
    off = pl.multiple_of((qblocks - 1 - qb) * SUBLANES, SUBLANES)
    ps = []
    for r in range(NSA_GROUP):
        l = jnp.where(mask, s[:, r * Q_BLOCK:(r + 1) * Q_BLOCK] + trc_ref[r, pl.ds(off, nc), :], NEG_INF)
        m = jnp.max(l, axis=0, keepdims=True)
        p = jnp.where(mask, jnp.exp2(l - m), 0.0)
        den = jnp.maximum(jnp.sum(p, axis=0, keepdims=True), 1e-30)
        ps.append(p / den)
    p_all = jnp.concatenate(ps, axis=1).astype(BF)
    oc_t = jnp.dot(vct_ref[0], p_all, preferred_element_type=F32)
    imp4 = jnp.dot(smt_ref[...], p_all, preferred_element_type=F32)
    imp = imp4[:, 0:Q_BLOCK]
    for r in range(1, NSA_GROUP):
        imp = imp + imp4[:, r * Q_BLOCK:(r + 1) * Q_BLOCK]

    j_io = lax.broadcasted_iota(jnp.int32, (nslc, Q_BLOCK), 0)
    t_io = qb * Q_BLOCK + lax.broadcasted_iota(jnp.int32, (nslc, Q_BLOCK), 1)
    cur = t_io >> int(math.log2(SLC_LEN))
    forced = jnp.logical_or(j_io == 0, jnp.logical_or(j_io == cur, j_io == cur - 1))
    n_forced = 3
    val = jnp.where(forced, -3e38, jnp.where(j_io <= cur, imp, -1e9))
    sel = jnp.where(forced, 1.0, 0.0)
    for _ in range(max(topn - n_forced, 0)):
        mx = jnp.max(val, axis=0, keepdims=True)
        cand = jnp.where(val == mx, j_io, nslc)
        jmin = jnp.min(cand, axis=0, keepdims=True)
        pick = j_io == jmin
        sel = jnp.where(pick, 1.0, sel)
        val = jnp.where(pick, -3e38, val)
    sel_ref[0, rows, :] = jnp.where(sel.T > 0.5, 0.0, NEG_INF).astype(sel_ref.dtype)

    gates = gate_ref[0, rows, :]
    for r in range(NSA_GROUP):
        o_r = oc_t[:, r * Q_BLOCK:(r + 1) * Q_BLOCK].T
        oc_ref[rows, r * NSA_HEAD_DIM:(r + 1) * NSA_HEAD_DIM] = o_r * gates[:, 3 * r:3 * r + 1]


def _nsa_cmp(q, kc, vct, trc, smt, gates):
    h, s, d = q.shape
    g = NSA_KV_HEADS
    qblocks = s // Q_BLOCK
    nc = kc.shape[1]
    nslc = smt.shape[0]
    topn = min(SLC_TOPN, nslc)
    nb = 2
    qrows = nb * Q_BLOCK
    return pl.pallas_call(
        functools.partial(_nsa_cmp_kernel, qblocks=qblocks, topn=topn, nb=nb),
        grid=(g, qblocks // nb),
        in_specs=[pl.BlockSpec((NSA_GROUP, qrows, d), lambda gg, i: (gg, i, 0)),
                  pl.BlockSpec((1, nc, d), lambda gg, i: (gg, 0, 0)),
                  pl.BlockSpec((1, d, nc), lambda gg, i: (gg, 0, 0)),
                  pl.BlockSpec((NSA_GROUP, trc.shape[1], Q_BLOCK), lambda gg, i: (gg, 0, 0)),
                  pl.BlockSpec(smt.shape, lambda gg, i: (0, 0)),
                  pl.BlockSpec((1, qrows, LANES), lambda gg, i: (gg, i, 0))],
        out_specs=[pl.BlockSpec((qrows, NSA_GROUP * d), lambda gg, i: (i, gg)),
                   pl.BlockSpec((1, qrows, nslc), lambda gg, i: (gg, i, 0))],
        out_shape=[jax.ShapeDtypeStruct((s, h * d), F32),
                   jax.ShapeDtypeStruct((g, s, nslc), BF)],
        compiler_params=_cparams("arbitrary", "arbitrary"),
        name="nsa_cmp",
    )(q, kc, vct, trc, smt, gates)


def _nsa_sw_kernel(q_ref, kv_ref, tb_ref, sel_ref, et_ref, gate_ref, oc_ref,
                   o_ref, s0_ref, s1_ref, m_ref, acc_ref, part_ref, *, tk, mc, nch):
    i = pl.program_id(0)
    ngrp = NSA_KV_HEADS
    rq = NSA_GROUP * Q_BLOCK
    nsub = tk // Q_BLOCK
    s_bufs = (s0_ref, s1_ref)
    qbs = [i * nch + c for c in range(nch)]
    qs = [[q_ref[g * NSA_GROUP:(g + 1) * NSA_GROUP, c * Q_BLOCK:(c + 1) * Q_BLOCK, :].reshape(rq, NSA_HEAD_DIM)
           for c in range(nch)] for g in range(ngrp)]

    def bias_tile(g, r, c, kb0, nblk):
        return jnp.concatenate(
            [tb_ref[g * NSA_GROUP + r, jnp.clip(qbs[c] - (kb0 + b), 0, mc)] for b in range(nblk)], axis=1)

    def gated(g, c, o, branch, base):
        gates = gate_ref[g, c * Q_BLOCK:(c + 1) * Q_BLOCK, :]
        outs = []
        for r in range(NSA_GROUP):
            o_r = o[r * Q_BLOCK:(r + 1) * Q_BLOCK, :] * gates[:, 3 * r + branch:3 * r + branch + 1]
            outs.append(base[:, r * NSA_HEAD_DIM:(r + 1) * NSA_HEAD_DIM] + o_r)
        return jnp.concatenate(outs, axis=1)

    _init_state(m_ref, acc_ref)
    qa = [[jnp.concatenate([qs[g][c], jnp.concatenate(
        [sel_ref[g, c * Q_BLOCK:(c + 1) * Q_BLOCK, :]] * NSA_GROUP, axis=0)], axis=1) for c in range(nch)]
        for g in range(ngrp)]

    def pre(g, j, causal):
        off = pl.multiple_of(j * tk, tk)
        ka = jnp.concatenate([kv_ref[g, pl.ds(off, tk), :], et_ref[pl.ds(off, tk), :]], axis=1)
        for c in range(nch):
            s = lax.dot_general(qa[g][c], ka, _NT, preferred_element_type=F32)
            if causal:
                kk = off + lax.broadcasted_iota(jnp.int32, (Q_BLOCK, tk), 1)
                ii = qbs[c] * Q_BLOCK + lax.broadcasted_iota(jnp.int32, (Q_BLOCK, tk), 0)
                future = jnp.where(kk <= ii, 0.0, NEG_INF)
            for r in range(NSA_GROUP):
                bias = bias_tile(g, r, c, j * nsub, nsub)
                if causal:
                    bias = bias + future
                s_bufs[g][c, r * Q_BLOCK:(r + 1) * Q_BLOCK, :] = s[r * Q_BLOCK:(r + 1) * Q_BLOCK, :] + bias

    def process(g, j):
        vp = _with_ones(kv_ref[ngrp + g, pl.ds(pl.multiple_of(j * tk, tk), tk), :])
        for c in range(nch):
            _flash_update(s_bufs[g].at[c], vp, m_ref.at[g * nch + c], acc_ref.at[g * nch + c])

    n_past = (i * nch) // nsub

    def body(t, carry):
        pre(1, t, False)
        process(0, t)
        pre(0, t + 1, False)
        process(1, t)
        return carry

    pre(0, n_past, True)
    pre(1, n_past, True)
    process(0, n_past)
    pre(0, 0, False)
    process(1, n_past)

    wk = WINDOW + Q_BLOCK
    wblk = wk // Q_BLOCK
    for g in range(ngrp):
        cols = slice(g * rq, (g + 1) * rq)
        for c in range(nch):
            rows = slice(c * Q_BLOCK, (c + 1) * Q_BLOCK)
            kb0 = jnp.maximum(qbs[c] - WINDOW // Q_BLOCK, 0)
            off = pl.multiple_of(kb0 * Q_BLOCK, Q_BLOCK)
            s = lax.dot_general(qs[g][c], kv_ref[2 * ngrp + g, pl.ds(off, wk), :], _NT, preferred_element_type=F32)
            rel = (qbs[c] * Q_BLOCK + lax.broadcasted_iota(jnp.int32, (Q_BLOCK, wk), 0)) - (
                off + lax.broadcasted_iota(jnp.int32, (Q_BLOCK, wk), 1))
            outside = jnp.where(jnp.logical_and(rel >= 0, rel < WINDOW), 0.0, NEG_INF)
            l = jnp.concatenate([s[r * Q_BLOCK:(r + 1) * Q_BLOCK, :] + (bias_tile(g, r, c, kb0, wblk) + outside)
                                 for r in range(NSA_GROUP)], axis=0)
            p = jnp.exp2(l - jnp.max(l, axis=1, keepdims=True)).astype(BF)
            ow = jnp.dot(p, _with_ones(kv_ref[3 * ngrp + g, pl.ds(off, wk), :]), preferred_element_type=F32)
            part_ref[rows, cols] = gated(g, c, _normalized(ow), 2, oc_ref[rows, cols])

    lax.fori_loop(0, n_past, body, 0)

    for g in range(ngrp):
        cols = slice(g * rq, (g + 1) * rq)
        for c in range(nch):
            rows = slice(c * Q_BLOCK, (c + 1) * Q_BLOCK)
            o_ref[rows, cols] = gated(g, c, _normalized(acc_ref[g * nch + c]), 1,
                                      part_ref[rows, cols]).astype(o_ref.dtype)


def _nsa_sw(q, kvsw, tb, sel, et, gates, oc):
    h, s, d = q.shape
    g = NSA_KV_HEADS
    nch = 2
    qrows = nch * Q_BLOCK
    tk = min(512, s)
    mc = tb.shape[1] - 1
    nslc = sel.shape[2]
    rq = NSA_GROUP * Q_BLOCK
    assert tk % qrows == 0 and WINDOW + Q_BLOCK <= s and NSA_GROUP * d == rq

    def resident(arr):
        return pl.BlockSpec(arr.shape, lambda i, nd=arr.ndim: (0,) * nd, pipeline_mode=pl.Buffered(1))

    return pl.pallas_call(
        functools.partial(_nsa_sw_kernel, tk=tk, mc=mc, nch=nch),
        grid=(s // qrows,),
        in_specs=[pl.BlockSpec((h, qrows, d), lambda i: (0, i, 0)),
                  resident(kvsw), resident(tb),
                  pl.BlockSpec((g, qrows, nslc), lambda i: (0, i, 0)),
                  resident(et),
                  pl.BlockSpec((g, qrows, LANES), lambda i: (0, i, 0)),
                  pl.BlockSpec((qrows, h * d), lambda i: (i, 0))],
        out_specs=pl.BlockSpec((qrows, h * d), lambda i: (i, 0)),
        out_shape=jax.ShapeDtypeStruct((s, h * d), BF),
        scratch_shapes=[pltpu.VMEM((nch, rq, tk), F32), pltpu.VMEM((nch, rq, tk), F32),
                        pltpu.VMEM((g * nch, rq, 1), F32), pltpu.VMEM((g * nch, rq, 2 * LANES), F32),
                        pltpu.VMEM((qrows, h * d), F32)],
        compiler_params=_cparams("arbitrary"),
        name="nsa_sel_win",
    )(q, kvsw, tb, sel, et, gates, oc)


def _selection_map_t(nc, nslc):
    n = np.arange(nc)[None, :] * CMP_STRIDE
    j0 = np.arange(nslc)[:, None] * SLC_LEN
    valid = np.arange(nc)[None, :] < nc - 1
    return jnp.asarray(((n < j0 + SLC_LEN) & (n + CMP_LEN > j0) & valid).astype(np.float32), dtype=BF)


def _block_onehot(s, nslc):
    tok = np.arange(s)[:, None]
    j = np.arange(nslc)[None, :]
    return jnp.asarray((tok // SLC_LEN == j).astype(np.float32), dtype=BF)


def _rope_tables(positions):
    half = MLA_ROPE // 2
    inv = ROPE_THETA ** (-jnp.arange(half, dtype=F32) / half)
    ang = positions.astype(F32)[:, None] * inv
    cos, sin = jnp.cos(ang), jnp.sin(ang)
    z = jnp.zeros_like(cos)
    zpad = jnp.zeros((positions.shape[0], LANES - MLA_ROPE), F32)
    c = jnp.concatenate([cos, cos, zpad], axis=1)
    s1 = jnp.concatenate([-sin, z, zpad], axis=1)
    s2 = jnp.concatenate([z, sin, zpad], axis=1)
    return c, s1, s2


def _pad_cols(w, n):
    return jnp.pad(w, ((0, 0), (0, n - w.shape[1])))


def _pad_rows(w, n):
    return jnp.pad(w, ((0, n - w.shape[0]), (0, 0)))


def _nsa_mla_mixer(hn, pos, rel_bias, w_in_all, w_out_all, e, gate_b, pos_k, w1_k, w2_k, pos_v, w1_v, w2_v,
                   q_norm, w_uq, kv_norm, w_ukv, x, g1):
    s = hn.shape[0]
    d = NSA_HEAD_DIM
    nq = NSA_HEADS * d
    nkv = 2 * NSA_KV_HEADS * d
    w_in_t = jnp.swapaxes(w_in_all, 1, 2)
    dm = w_in_t.shape[2]
    o0 = nq + 3 * nkv
    w_g_t = w_in_t[e, o0:o0 + 3 * NSA_HEADS]; o0 += 3 * NSA_HEADS
    w_lat_t = w_in_t[e, o0:o0 + MLA_Q_RANK + MLA_KV_RANK]; o0 += MLA_Q_RANK + MLA_KV_RANK
    w_kr_t = w_in_t[e, o0:o0 + MLA_ROPE]

    q_nsa = _proj([hn], [(w_in_t, dm, (e, 0), 0)], functools.partial(_epi_scale, d ** -0.5 * LOG2E), n=nq,
                  out_dtype=BF, head_major=True, tn=1024, w_transposed=True, name="proj_q_nsa")
    kv_cmp = _proj([hn], [(w_in_t, dm, (e, 0), nq)], _epi_id, n=nkv, out_dtype=F32, head_major=True,
                   w_transposed=True, name="proj_kv_cmp")
    kv_sw = _proj([hn], [(w_in_t, dm, (e, 0), nq + nkv)], _epi_id, n=2 * nkv, out_dtype=BF, head_major=True,
                  w_transposed=True, name="proj_kv_sw")
    per_g = 3 * NSA_GROUP
    w_gp_t = jnp.concatenate([_pad_rows(w_g_t[g * per_g:(g + 1) * per_g], LANES) for g in range(NSA_KV_HEADS)], 0)
    b_gp = jnp.concatenate([_pad_cols(gate_b[None, g * per_g:(g + 1) * per_g], LANES)
                            for g in range(NSA_KV_HEADS)], 1)
    gates = _proj([hn], [(w_gp_t[None], dm, (0, 0), 0)], _epi_sigmoid_bias, [(b_gp, "row")],
                  n=NSA_KV_HEADS * LANES, out_dtype=F32, head_major=True, w_transposed=True,
                  name="proj_gates")

    nc = s // CMP_STRIDE
    gw = CMP_STRIDE * d
    pos_kv = jnp.stack([pos_k.reshape(2, gw), pos_v.reshape(2, gw)])
    w1_kv = jnp.stack([w1_k.reshape(2, gw, CMP_HIDDEN), w1_v.reshape(2, gw, CMP_HIDDEN)]).astype(BF)
    w2_kv = jnp.stack([w2_k, w2_v]).astype(BF)
    kvc = _nsa_compress(kv_cmp.reshape(2 * NSA_KV_HEADS, nc, gw), pos_kv, w1_kv, w2_kv)
    kc = kvc[:NSA_KV_HEADS]
    vct = jnp.swapaxes(kvc[NSA_KV_HEADS:], 1, 2)

    nslc = s // SLC_LEN
    tb, trc = _bias_tables(rel_bias, s)
    smt = _selection_map_t(nc, nslc)
    oc, sel = _nsa_cmp(q_nsa, kc, vct, trc, smt, gates)
    o_nsa = _nsa_sw(q_nsa, kv_sw, tb, sel, _block_onehot(s, nslc), gates, oc)

    nlat = MLA_Q_RANK + MLA_KV_RANK
    lat = _proj([hn], [(w_lat_t[None], dm, (0, 0), 0)], _epi_rmsnorm,
                [(jnp.concatenate([q_norm, kv_norm])[None, :], "row")], n=nlat, out_dtype=BF, tn=MLA_Q_RANK,
                w_transposed=True, name="proj_mla_latent")
    c, s1, s2 = _rope_tables(pos)
    kr = _proj([hn], [(_pad_rows(w_kr_t, LANES)[None], dm, (0, 0), 0)], functools.partial(_epi_rope, 1.0),
               [(c, "rowtile"), (s1, "rowtile"), (s2, "rowtile")], n=LANES, out_dtype=BF, w_transposed=True,
               name="proj_mla_kr")
    scale = (MLA_NOPE + MLA_ROPE) ** -0.5 * LOG2E
    w_uq3 = w_uq.reshape(MLA_Q_RANK, MLA_HEADS, MLA_NOPE + MLA_ROPE)
    w_uqn = w_uq3[:, :, :MLA_NOPE].reshape(MLA_Q_RANK, MLA_HEADS * MLA_NOPE)
    w_uqr = jnp.pad(w_uq3[:, :, MLA_NOPE:], ((0, 0), (0, 0), (0, LANES - MLA_ROPE))).reshape(
        MLA_Q_RANK, MLA_HEADS * LANES)
    qn = _proj([lat], [(w_uqn, MLA_Q_RANK, 0, 0)], functools.partial(_epi_scale, scale),
               n=MLA_HEADS * MLA_NOPE, out_dtype=BF, head_major=True, tn=1024, lhs_col_block=[0],
               name="proj_mla_qn")
    qr = _proj([lat], [(w_uqr, MLA_Q_RANK, 0, 0)], functools.partial(_epi_rope, scale),
               [(c, "rowtile"), (s1, "rowtile"), (s2, "rowtile")], n=MLA_HEADS * LANES, out_dtype=BF,
               head_major=True, lhs_col_block=[0], name="proj_mla_qr")
    kvh = _proj([lat], [(w_ukv, MLA_KV_RANK, 0, 0)], _epi_id, n=MLA_HEADS * (MLA_NOPE + MLA_V), out_dtype=BF,
                head_major=True, tn=1024, lhs_col_block=[1], name="proj_mla_kv")
    o_mla = _mla_attention(qn, qr, kvh, kr)

    return _proj([o_nsa, o_mla], [(w_out_all, nq, (e, 0), 0), (w_out_all, MLA_HEADS * MLA_V, (e, 1), 0)],
                 _epi_residual, [(x, "tile"), (g1, "row")], n=w_out_all.shape[2], out_dtype=F32, tn=1024,
                 name="proj_even_out")


def _fox_mixer(hn, w_in_all, w_out_all, o, f_b, q_norm, k_norm, x, g1):
    d = D_MODEL
    dh = FOX_HEAD_DIM
    w_in_t = jnp.swapaxes(w_in_all, 1, 2)
    w_f_t = w_in_t[o, 3 * d:3 * d + FOX_HEADS]
    w_og_t = w_in_t[o, 3 * d + FOX_HEADS:]
    q = _proj([hn], [(w_in_t, d, (o, 0), 0)], functools.partial(_epi_headnorm, dh ** -0.5 * LOG2E),
              [(q_norm[None, :], "const")], n=d, out_dtype=BF, head_major=True, tn=1024, w_transposed=True,
              name="proj_fox_q")
    k = _proj([hn], [(w_in_t, d, (o, 0), d)], functools.partial(_epi_headnorm, 1.0),
              [(k_norm[None, :], "const")], n=d, out_dtype=BF, head_major=True, tn=1024, w_transposed=True,
              name="proj_fox_k")
    v = _proj([hn], [(w_in_t, d, (o, 0), 2 * d)], _epi_id, n=d, out_dtype=BF, head_major=True, tn=1024,
              w_transposed=True, name="proj_fox_v")
    lf = _proj([hn], [(_pad_rows(w_f_t, LANES)[None], d, (0, 0), 0)], _epi_logsigmoid_bias,
               [(_pad_cols(f_b[None, :], LANES), "row")], n=LANES, out_dtype=F32, w_transposed=True,
               name="proj_fox_f")
    sig_og = _proj([hn], [(w_og_t[None], d, (0, 0), 0)], _epi_sigmoid, n=d, out_dtype=F32, tn=1024,
                   w_transposed=True, name="proj_fox_og")
    cum, cum_t = _cumsum_tokens(lf, LOG2E)
    cum_t = cum_t[:FOX_HEADS].reshape(FOX_HEADS, 1, -1)
    att = _fox_attention(q, k, v, cum_t, cum, sig_og)
    return _proj([att], [(w_out_all, d, (o, 0), 0)], _epi_residual, [(x, "tile"), (g1, "row")], n=d,
                 out_dtype=F32, tn=1024, name="proj_fox_out")


def kernel(x, c, positions, rel_bias, ada_w, ada_b, norm_mix, norm_ffn, ffn_w1, ffn_w3, ffn_w2, even_w_in, even_w_out, nsa_gate_b, nsa_cmp_pos_k, nsa_cmp_w1_k, nsa_cmp_w2_k, nsa_cmp_pos_v, nsa_cmp_w1_v, nsa_cmp_w2_v, mla_q_norm, mla_w_uq, mla_kv_norm, mla_w_ukv, fox_w_in, fox_w_out, fox_f_b, fox_q_norm, fox_k_norm, final_norm):
    b, s, d = x.shape
    assert b == 1 and d == D_MODEL and s % 1024 == 0
    xs = x[0]
    pos = positions[0]
    mod = _adaln(c, ada_w, ada_b)
    depth = ada_w.shape[0]
    hn = None
    for i in range(depth):
        sh1, sc1, g1, sh2, sc2, g2 = [mod[i:i + 1, k * d:(k + 1) * d] for k in range(6)]
        if hn is None:
            hn = _normmod(xs, norm_mix[i][None, :], sc1, sh1)
        if i % 2 == 0:
            e = i // 2
            xs = _nsa_mla_mixer(hn, pos, rel_bias, even_w_in, even_w_out, e, nsa_gate_b[e],
                                nsa_cmp_pos_k[e], nsa_cmp_w1_k[e], nsa_cmp_w2_k[e],
                                nsa_cmp_pos_v[e], nsa_cmp_w1_v[e], nsa_cmp_w2_v[e],
                                mla_q_norm[e], mla_w_uq[e], mla_kv_norm[e], mla_w_ukv[e], xs, g1)
        else:
            o = i // 2
            xs = _fox_mixer(hn, fox_w_in, fox_w_out, o, fox_f_b[o], fox_q_norm[o], fox_k_norm[o], xs, g1)
        if i == depth - 1:
            xs = _ffn(xs, norm_ffn[i][None, :], sc2, sh2, g2, ffn_w1, ffn_w3, ffn_w2, i,
                      final_gain=final_norm[None, :])
        else:
            nxt = (norm_mix[i + 1][None, :], mod[i + 1:i + 2, d:2 * d], mod[i + 1:i + 2, 0:d])
            xs, hn = _ffn(xs, norm_ffn[i][None, :], sc2, sh2, g2, ffn_w1, ffn_w3, ffn_w2, i, next_mod=nxt)
    return xs[None]
```

```python
import functools
import math

import numpy as np
import jax
import jax.numpy as jnp
from jax import lax
from jax.experimental import pallas as pl
from jax.experimental.pallas import tpu as pltpu

D_MODEL = 2048
DEPTH = 2
EPS = 1e-6
NEG_INF = -1e30

NSA_HEADS = 8
NSA_KV_HEADS = 2
NSA_GROUP = NSA_HEADS // NSA_KV_HEADS
NSA_HEAD_DIM = 128
CMP_LEN = 32
CMP_STRIDE = 16
CMP_HIDDEN = 256
SLC_LEN = 64
SLC_TOPN = 16
WINDOW = 512
Q_BLOCK = 128

MLA_HEADS = 8
MLA_Q_RANK = 512
MLA_KV_RANK = 512
MLA_NOPE = 128
MLA_ROPE = 64
MLA_V = 128
ROPE_THETA = 10000.0

FOX_HEADS = 16
FOX_HEAD_DIM = D_MODEL // FOX_HEADS

REL_BUCKETS = 32
REL_MAX_DIST = 4096

FFN_HIDDEN = ((8 * D_MODEL + 2) // 3 + 255) // 256 * 256

LANES = 128
SUBLANES = 8
VMEM_LIMIT_BYTES = 56 * 1024 * 1024
LOG2E = math.log2(math.e)

BF = jnp.bfloat16
F32 = jnp.float32
_NT = (((1,), (1,)), ((), ()))


def _cparams(*sem):
    return pltpu.CompilerParams(dimension_semantics=sem, vmem_limit_bytes=VMEM_LIMIT_BYTES)


def _rms(x):
    return x * lax.rsqrt(jnp.mean(x * x, axis=-1, keepdims=True) + EPS)


def _adaln_kernel(c_ref, w_ref, b_ref, o_ref):
    c = c_ref[...]
    cond = c * jax.nn.sigmoid(c)
    acc = jnp.dot(cond.astype(BF), w_ref[0].astype(BF), preferred_element_type=F32)
    o_ref[0] = acc + b_ref[0]


def _adaln(c, ada_w, ada_b):
    depth, d, n = ada_w.shape
    tn = 1024
    c8 = jnp.broadcast_to(c, (SUBLANES, d))
    out = pl.pallas_call(
        _adaln_kernel,
        grid=(depth, n // tn),
        in_specs=[pl.BlockSpec((SUBLANES, d), lambda i, j: (0, 0)),
                  pl.BlockSpec((1, d, tn), lambda i, j: (i, 0, j)),
                  pl.BlockSpec((1, 1, tn), lambda i, j: (i, 0, j))],
        out_specs=pl.BlockSpec((1, SUBLANES, tn), lambda i, j: (i, 0, j)),
        out_shape=jax.ShapeDtypeStruct((depth, SUBLANES, n), F32),
        compiler_params=_cparams("arbitrary", "arbitrary"),
        name="adaln",
    )(c8, ada_w, ada_b.reshape(depth, 1, n))
    return out[:, 0, :]


def _normmod_kernel(x_ref, g_ref, sc_ref, sh_ref, o_ref):
    y = _rms(x_ref[...])
    o_ref[...] = ((y * g_ref[...]) * (1.0 + sc_ref[...]) + sh_ref[...]).astype(o_ref.dtype)


def _normmod(x, g, sc, sh):
    s, d = x.shape
    tm = 512
    row = pl.BlockSpec((1, d), lambda i: (0, 0))
    return pl.pallas_call(
        _normmod_kernel,
        grid=(s // tm,),
        in_specs=[pl.BlockSpec((tm, d), lambda i: (i, 0)), row, row, row],
        out_specs=pl.BlockSpec((tm, d), lambda i: (i, 0)),
        out_shape=jax.ShapeDtypeStruct((s, d), BF),
        compiler_params=_cparams("arbitrary"),
        name="normmod",
    )(x, g, sc, sh)


def _proj_kernel(*refs, n_lhs, n_epi, epi, head_major, w_transposed):
    lhs = refs[:n_lhs]
    ws = refs[n_lhs:2 * n_lhs]
    epis = refs[2 * n_lhs:2 * n_lhs + n_epi]
    o_ref = refs[2 * n_lhs + n_epi]
    wbf = refs[2 * n_lhs + n_epi + 1:]

    @pl.when(pl.program_id(1) == 0)
    def _():
        for w, wb in zip(ws, wbf):
            wb[...] = w[...].astype(BF)

    acc = None
    for a, wb in zip(lhs, wbf):
        if w_transposed:
            d = lax.dot_general(a[...], wb[...], _NT, preferred_element_type=F32)
        else:
            d = jnp.dot(a[...], wb[...], preferred_element_type=F32)
        acc = d if acc is None else acc + d
    res = epi(acc, *[e[...] for e in epis])
    if head_major:
        for r in range(o_ref.shape[0]):
            o_ref[r] = res[:, r * LANES:(r + 1) * LANES].astype(o_ref.dtype)
    else:
        o_ref[...] = res.astype(o_ref.dtype)


def _proj(lhs, ws, epi, epi_in=(), *, n, out_dtype, head_major=False, tm=1024, tn=512,
          lhs_col_block=None, w_transposed=False, name="proj"):
    m = lhs[0].shape[0]
    tm = min(tm, m)
    tn = min(tn, n)
    if lhs_col_block is None:
        lhs_col_block = [0] * len(lhs)
    in_specs = []
    for (_, k, _, _), cb in zip(ws, lhs_col_block):
        in_specs.append(pl.BlockSpec((tm, k), lambda j, i, cb=cb: (i, cb)))
    for arr, k, rb, col0 in ws:
        assert col0 % tn == 0
        if w_transposed:
            in_specs.append(pl.BlockSpec((None, tn, k),
                                         lambda j, i, rb=rb, cb0=col0 // tn: (rb[0], cb0 + j, rb[1])))
        elif arr.ndim == 3:
            in_specs.append(pl.BlockSpec((None, k, tn),
                                         lambda j, i, rb=rb, cb0=col0 // tn: (rb[0], rb[1], cb0 + j)))
        else:
            in_specs.append(pl.BlockSpec((k, tn), lambda j, i, rb=rb, cb0=col0 // tn: (rb, cb0 + j)))
    arrays = list(lhs) + [w[0] for w in ws]
    for arr, kind in epi_in:
        if kind == "row":
            in_specs.append(pl.BlockSpec((1, tn), lambda j, i: (0, j)))
        elif kind == "const":
            in_specs.append(pl.BlockSpec(arr.shape, lambda j, i: (0, 0)))
        elif kind == "tile":
            in_specs.append(pl.BlockSpec((tm, tn), lambda j, i: (i, j)))
        elif kind == "rowtile":
            in_specs.append(pl.BlockSpec((tm, arr.shape[1]), lambda j, i: (i, 0)))
        else:
            raise ValueError(kind)
        arrays.append(arr)
    if head_major:
        hpt = tn // LANES
        out_spec = pl.BlockSpec((hpt, tm, LANES), lambda j, i: (j, i, 0))
        out_shape = jax.ShapeDtypeStruct((n // LANES, m, LANES), out_dtype)
    else:
        out_spec = pl.BlockSpec((tm, tn), lambda j, i: (i, j))
        out_shape = jax.ShapeDtypeStruct((m, n), out_dtype)
    kern = functools.partial(_proj_kernel, n_lhs=len(lhs), n_epi=len(epi_in), epi=epi,
                             head_major=head_major, w_transposed=w_transposed)
    return pl.pallas_call(
        kern,
        grid=(n // tn, m // tm),
        in_specs=in_specs,
        out_specs=out_spec,
        out_shape=out_shape,
        scratch_shapes=[pltpu.VMEM((tn, k) if w_transposed else (k, tn), BF) for _, k, _, _ in ws],
        compiler_params=_cparams("arbitrary", "arbitrary"),
        name=name,
    )(*arrays)


def _epi_id(acc):
    return acc


def _epi_scale(scale, acc):
    return acc * scale


def _epi_sigmoid_bias(acc, b):
    return jax.nn.sigmoid(acc + b)


def _epi_logsigmoid_bias(acc, b):
    return jax.nn.log_sigmoid(acc + b)


def _epi_sigmoid(acc):
    return jax.nn.sigmoid(acc)


def _epi_rmsnorm(acc, g):
    return _rms(acc) * g


def _epi_headnorm(scale, acc, g):
    outs = []
    for r in range(acc.shape[1] // LANES):
        outs.append(_rms(acc[:, r * LANES:(r + 1) * LANES]) * g * scale)
    return jnp.concatenate(outs, axis=1)


def _epi_rope(scale, acc, c, s1, s2):
    reps = acc.shape[1] // LANES
    half = MLA_ROPE // 2
    if reps > 1:
        c = jnp.concatenate([c] * reps, axis=1)
        s1 = jnp.concatenate([s1] * reps, axis=1)
        s2 = jnp.concatenate([s2] * reps, axis=1)
    n = acc.shape[1]
    out = acc * c + pltpu.roll(acc, n - half, 1) * s1 + pltpu.roll(acc, half, 1) * s2
    return out * scale


def _epi_residual(acc, x, g):
    return x + g * acc


def _ffn_kernel(*refs, mode):
    x_ref, g_ref, sc_ref, sh_ref, g2_ref, w1_ref, w3_ref, w2_ref = refs[:8]
    if mode == "final":
        fn_ref, o_ref, hn_ref = refs[8:]
    elif mode == "next":
        gn_ref, scn_ref, shn_ref, o_ref, hnext_ref, hn_ref = refs[8:]
    else:
        o_ref, hn_ref = refs[8:]
    f = pl.program_id(1)
    tm = x_ref.shape[0]
    halves = [slice(r * (tm // 2), (r + 1) * (tm // 2)) for r in range(2)]

    @pl.when(f == 0)
    def _():
        for rows in halves:
            x = x_ref[rows, :]
            hn_ref[rows, :] = ((_rms(x) * g_ref[...]) * (1.0 + sc_ref[...]) + sh_ref[...]).astype(BF)
            o_ref[rows, :] = x

    w1 = w1_ref[...].astype(BF)
    w3 = w3_ref[...].astype(BF)
    w2 = w2_ref[...].astype(BF)
    for rows in halves:
        h = hn_ref[rows, :]
        h1 = jnp.dot(h, w1, preferred_element_type=F32)
        h3 = jnp.dot(h, w3, preferred_element_type=F32)
        a = (h1 * jax.nn.sigmoid(h1)) * h3
        o_ref[rows, :] += g2_ref[...] * jnp.dot(a.astype(BF), w2, preferred_element_type=F32)

    if mode != "plain":
        @pl.when(f == pl.num_programs(1) - 1)
        def _():
            for rows in halves:
                y = _rms(o_ref[rows, :])
                if mode == "final":
                    o_ref[rows, :] = y * fn_ref[...]
                else:
                    hnext_ref[rows, :] = ((y * gn_ref[...]) * (1.0 + scn_ref[...]) + shn_ref[...]).astype(BF)


def _ffn(x, g, sc, sh, g2, w1, w3, w2, layer, final_gain=None, next_mod=None):
    assert final_gain is None or next_mod is None
    s, d = x.shape
    fdim = w1.shape[2]
    tm = min(1024, s)
    tf = 256
    row = pl.BlockSpec((1, d), lambda i, f: (0, 0))
    tile = pl.BlockSpec((tm, d), lambda i, f: (i, 0))
    in_specs = [pl.BlockSpec((tm, d), lambda i, f: (i, 0), pipeline_mode=pl.Buffered(1)), row, row, row, row,
                pl.BlockSpec((None, d, tf), lambda i, f: (layer, 0, f)),
                pl.BlockSpec((None, d, tf), lambda i, f: (layer, 0, f)),
                pl.BlockSpec((None, tf, d), lambda i, f: (layer, f, 0))]
    arrays = [x, g, sc, sh, g2, w1, w3, w2]
    out_specs, out_shape, mode = tile, jax.ShapeDtypeStruct((s, d), F32), "plain"
    if final_gain is not None:
        in_specs.append(row)
        arrays.append(final_gain)
        mode = "final"
    elif next_mod is not None:
        in_specs += [row, row, row]
        arrays += list(next_mod)
        out_specs = [tile, tile]
        out_shape = [out_shape, jax.ShapeDtypeStruct((s, d), BF)]
        mode = "next"
    return pl.pallas_call(
        functools.partial(_ffn_kernel, mode=mode),
        grid=(s // tm, fdim // tf),
        in_specs=in_specs,
        out_specs=out_specs,
        out_shape=out_shape,
        scratch_shapes=[pltpu.VMEM((tm, d), BF)],
        compiler_params=_cparams("arbitrary", "arbitrary"),
        name="ffn",
    )(*arrays)


def _flash_update(s, vp, m_ref, acc_ref, row_shift=None):
    m_prev = m_ref[...]
    m_tile = jnp.max(s[...], axis=1, keepdims=True)
    if row_shift is not None:
        m_tile = m_tile + row_shift
    m_new = jnp.maximum(m_prev, m_tile)
    alpha = jnp.exp2(m_prev - m_new)
    sub = m_new if row_shift is None else m_new - row_shift
    p = jnp.exp2(s[...] - sub).astype(BF)
    acc_ref[...] = alpha * acc_ref[...] + jnp.dot(p, vp, preferred_element_type=F32)
    m_ref[...] = m_new


def _init_state(m_ref, acc_ref):
    m_ref[...] = jnp.full_like(m_ref, NEG_INF)
    acc_ref[...] = jnp.zeros_like(acc_ref)


def _with_ones(v):
    return jnp.concatenate([v, jnp.ones(v.shape, v.dtype)], axis=1)


def _normalized(acc):
    return acc[:, :LANES] / acc[:, LANES:]


def _causal_mask(t):
    rows = lax.broadcasted_iota(jnp.int32, (t, t), 0)
    cols = lax.broadcasted_iota(jnp.int32, (t, t), 1)
    return cols <= rows


def _causal_sweep(qi, tc, s_bufs, m_ref, acc_ref, streams):
    tw = 2 * tc

    doff = pl.multiple_of(qi * tw, tw)

    def pre_diag(si, c):
        kside_fn, logits_fn, _, _ = streams[si]
        if c == 0:
            s_bufs[si][0, :, 0:tc] = jnp.where(_causal_mask(tc), logits_fn(0, kside_fn(doff, tc)), NEG_INF)
        else:
            rows = tc + lax.broadcasted_iota(jnp.int32, (tc, tw), 0)
            cols = lax.broadcasted_iota(jnp.int32, (tc, tw), 1)
            s_bufs[si][1] = jnp.where(cols <= rows, logits_fn(1, kside_fn(doff, tw)), NEG_INF)

    def process_diag(si, c):
        _, _, vp_fn, shifts = streams[si]
        if c == 0:
            _flash_update(s_bufs[si].at[0, :, 0:tc], vp_fn(doff, tc), m_ref.at[2 * si], acc_ref.at[2 * si],
                          row_shift=shifts[0])
        else:
            _flash_update(s_bufs[si].at[1], vp_fn(doff, tw), m_ref.at[2 * si + 1], acc_ref.at[2 * si + 1],
                          row_shift=shifts[1])

    pre_diag(0, 0)
    pre_diag(0, 1)
    kside_fn0, logits_fn0, _, _ = streams[0]
    kside0 = kside_fn0(0, tw)
    for c in range(2):
        pre_diag(1, c)
        process_diag(0, c)
    for c in range(2):
        s_bufs[0][c] = logits_fn0(c, kside0)
        process_diag(1, c)

    def half(sp, tp, sq, tq_):
        kside_fn, logits_fn, _, _ = streams[sp]
        _, _, vp_fn, shifts = streams[sq]
        kside = kside_fn(pl.multiple_of(tp * tw, tw), tw)
        vp = vp_fn(pl.multiple_of(tq_ * tw, tw), tw)
        for c in range(2):
            s_bufs[sp][c] = logits_fn(c, kside)
            _flash_update(s_bufs[sq].at[c], vp, m_ref.at[2 * sq + c], acc_ref.at[2 * sq + c], row_shift=shifts[c])

    def body(t, carry):
        half(1, t, 0, t)
        half(0, t + 1, 1, t)
        return carry

    lax.fori_loop(0, qi, body, 0)


def _sweep_scratch(tc):
    return [pltpu.VMEM((2, tc, 2 * tc), F32), pltpu.VMEM((2, tc, 2 * tc), F32),
            pltpu.VMEM((4, tc, 1), F32), pltpu.VMEM((4, tc, 2 * LANES), F32)]


def _mla_kernel(qn_ref, qr_ref, kv_ref, kr_ref, o_ref, s0_ref, s1_ref, m_ref, acc_ref, *, tc):
    qi = pl.program_id(1)
    _init_state(m_ref, acc_ref)

    def stream(si):
        qs = [jnp.concatenate([qn_ref[si, c * tc:(c + 1) * tc, :], qr_ref[si, c * tc:(c + 1) * tc, :]], axis=1)
              for c in range(2)]

        def kside(off, width):
            return jnp.concatenate([kv_ref[2 * si, pl.ds(off, width), :], kr_ref[pl.ds(off, width), :]], axis=1)

        def logits(c, k):
            return lax.dot_general(qs[c], k, _NT, preferred_element_type=F32)

        def vp(off, width):
            return _with_ones(kv_ref[2 * si + 1, pl.ds(off, width), :])

        return kside, logits, vp, (None, None)

    _causal_sweep(qi, tc, (s0_ref, s1_ref), m_ref, acc_ref, [stream(0), stream(1)])
    for si in range(2):
        for c in range(2):
            o_ref[c * tc:(c + 1) * tc, si * LANES:(si + 1) * LANES] = _normalized(
                acc_ref[2 * si + c]).astype(o_ref.dtype)


def _mla_attention(qn, qr, kvh, kr):
    h, s, _ = qn.shape
    tc = min(512, s // 2)
    w = 2 * tc
    return pl.pallas_call(
        functools.partial(_mla_kernel, tc=tc),
        grid=(h // 2, s // w),
        in_specs=[pl.BlockSpec((2, w, LANES), lambda p, i: (p, i, 0)),
                  pl.BlockSpec((2, w, LANES), lambda p, i: (p, i, 0)),
                  pl.BlockSpec((4, s, LANES), lambda p, i: (p, 0, 0)),
                  pl.BlockSpec((s, LANES), lambda p, i: (0, 0))],
        out_specs=pl.BlockSpec((w, 2 * LANES), lambda p, i: (i, p)),
        out_shape=jax.ShapeDtypeStruct((s, h * LANES), BF),
        scratch_shapes=_sweep_scratch(tc),
        compiler_params=_cparams("arbitrary", "arbitrary"),
        name="mla_attn",
    )(qn, qr, kvh, kr)


def _fox_kernel(q_ref, k_ref, v_ref, ck_ref, cum_ref, og_ref, o_ref, s0_ref, s1_ref, m_ref, acc_ref, *, tc):
    pair = pl.program_id(0)
    qi = pl.program_id(1)
    lane = lax.broadcasted_iota(jnp.int32, (tc, LANES), 1)
    _init_state(m_ref, acc_ref)

    def stream(si):
        qs = [q_ref[si, c * tc:(c + 1) * tc, :] for c in range(2)]
        cqs = [jnp.sum(jnp.where(lane == 2 * pair + si, cum_ref[c * tc:(c + 1) * tc, :], 0.0), axis=1,
                       keepdims=True) for c in range(2)]

        def kside(off, width):
            return k_ref[si, pl.ds(off, width), :], ck_ref[si, :, pl.ds(off, width)]

        def logits(c, kc):
            return lax.dot_general(qs[c], kc[0], _NT, preferred_element_type=F32) - kc[1]

        def vp(off, width):
            return _with_ones(v_ref[si, pl.ds(off, width), :])

        return kside, logits, vp, cqs

    _causal_sweep(qi, tc, (s0_ref, s1_ref), m_ref, acc_ref, [stream(0), stream(1)])
    for si in range(2):
        cols = slice(si * LANES, (si + 1) * LANES)
        for c in range(2):
            rows = slice(c * tc, (c + 1) * tc)
            o_ref[rows, cols] = (_normalized(acc_ref[2 * si + c]) * og_ref[rows, cols]).astype(o_ref.dtype)


def _fox_attention(q, k, v, cum_t, cum, sig_og):
    h, s, _ = q.shape
    tc = min(512, s // 2)
    w = 2 * tc
    hm = pl.BlockSpec((2, s, LANES), lambda p, i: (p, 0, 0))
    return pl.pallas_call(
        functools.partial(_fox_kernel, tc=tc),
        grid=(h // 2, s // w),
        in_specs=[pl.BlockSpec((2, w, LANES), lambda p, i: (p, i, 0)), hm, hm,
                  pl.BlockSpec((2, 1, s), lambda p, i: (p, 0, 0)),
                  pl.BlockSpec((w, LANES), lambda p, i: (i, 0)),
                  pl.BlockSpec((w, 2 * LANES), lambda p, i: (i, p))],
        out_specs=pl.BlockSpec((w, 2 * LANES), lambda p, i: (i, p)),
        out_shape=jax.ShapeDtypeStruct((s, h * LANES), BF),
        scratch_shapes=_sweep_scratch(tc),
        compiler_params=_cparams("arbitrary", "arbitrary"),
        name="fox_attn",
    )(q, k, v, cum_t, cum, sig_og)


def _cumsum_kernel(x_ref, o_ref, ot_ref, carry_ref, *, t, out_scale):
    @pl.when(pl.program_id(0) == 0)
    def _():
        carry_ref[...] = jnp.zeros_like(carry_ref)

    x = x_ref[...]
    rows = lax.broadcasted_iota(jnp.int32, (t, t), 0)
    cols = lax.broadcasted_iota(jnp.int32, (t, t), 1)
    tri = jnp.where(cols <= rows, 1.0, 0.0).astype(BF)
    hi = x.astype(BF)
    r1 = x - hi.astype(F32)
    mid = r1.astype(BF)
    lo = (r1 - mid.astype(F32)).astype(BF)
    cum = (jnp.dot(tri, hi, preferred_element_type=F32) + jnp.dot(tri, mid, preferred_element_type=F32)
           + jnp.dot(tri, lo, preferred_element_type=F32)) + carry_ref[...]
    scaled = cum * out_scale
    o_ref[...] = scaled
    ot_ref[...] = scaled.T
    carry_ref[...] = cum[t - 1:t, :]


def _cumsum_tokens(x, out_scale):
    s, n = x.shape
    t = min(256, s)
    return pl.pallas_call(
        functools.partial(_cumsum_kernel, t=t, out_scale=out_scale),
        grid=(s // t,),
        in_specs=[pl.BlockSpec((t, n), lambda i: (i, 0))],
        out_specs=[pl.BlockSpec((t, n), lambda i: (i, 0)), pl.BlockSpec((n, t), lambda i: (0, i))],
        out_shape=[jax.ShapeDtypeStruct((s, n), F32), jax.ShapeDtypeStruct((n, s), F32)],
        scratch_shapes=[pltpu.VMEM((1, n), F32)],
        compiler_params=_cparams("arbitrary"),
        name="cumsum",
    )(x)


def _compress_kernel(a_ref, pos_ref, w1_ref, w2_ref, o_ref):
    a = a_ref[0]
    nc = a.shape[0]
    p1 = jnp.dot((a + pos_ref[0, 0:1, :]).astype(BF), w1_ref[0, 0], preferred_element_type=F32)
    p2 = jnp.dot((a + pos_ref[0, 1:2, :]).astype(BF), w1_ref[0, 1], preferred_element_type=F32)
    h = p1 + pltpu.roll(p2, nc - 1, 0)
    act = h * jax.nn.sigmoid(h)
    o_ref[0] = jnp.dot(act.astype(BF), w2_ref[0], preferred_element_type=F32).astype(o_ref.dtype)


def _nsa_compress(kv_cmp, pos, w1, w2):
    c, nc, gw = kv_cmp.shape
    return pl.pallas_call(
        _compress_kernel,
        grid=(c,),
        in_specs=[pl.BlockSpec((1, nc, gw), lambda i: (i, 0, 0)),
                  pl.BlockSpec((1, 2, gw), lambda i: (i // 2, 0, 0)),
                  pl.BlockSpec((1, 2, gw, CMP_HIDDEN), lambda i: (i // 2, 0, 0, 0)),
                  pl.BlockSpec((1, CMP_HIDDEN, NSA_HEAD_DIM), lambda i: (i // 2, 0, 0))],
        out_specs=pl.BlockSpec((1, nc, NSA_HEAD_DIM), lambda i: (i, 0, 0)),
        out_shape=jax.ShapeDtypeStruct((c, nc, NSA_HEAD_DIM), BF),
        compiler_params=_cparams("arbitrary"),
        name="nsa_compress",
    )(kv_cmp, pos, w1, w2)


def _bias_table_kernel(brev_ref, basc_ref, tbl_ref, tb_ref, asc_ref, *, s, mc):
    tbl = tbl_ref[0] * LOG2E

    def lookup(bkt):
        out = jnp.zeros(bkt.shape, F32)
        for b in range(REL_BUCKETS):
            out = jnp.where(bkt == b, tbl[:, b:b + 1], out)
        return out

    rev = lookup(brev_ref[...])
    asc_ref[0] = lookup(basc_ref[...])
    for m in range(mc + 1):
        win = rev[:, s - Q_BLOCK * m:s - Q_BLOCK * m + 2 * Q_BLOCK]
        rolled = pltpu.roll(jnp.broadcast_to(win, (Q_BLOCK, 2 * Q_BLOCK)), Q_BLOCK + 1, 1,
                            stride=1, stride_axis=0)
        tb_ref[0, m] = rolled[:, :Q_BLOCK]


def _t5_bucket(dist):
    max_exact = REL_BUCKETS // 2
    d = jnp.maximum(dist, 0)
    ratio = jnp.log(jnp.maximum(d, max_exact).astype(F32) / max_exact) / math.log(REL_MAX_DIST / max_exact)
    large = jnp.minimum(max_exact + (ratio * (REL_BUCKETS - max_exact)).astype(jnp.int32), REL_BUCKETS - 1)
    return jnp.where(d < max_exact, d, large)


def _bias_tables(rel_bias, s):
    assert Q_BLOCK == SUBLANES * CMP_STRIDE
    h = rel_bias.shape[1]
    qblocks = s // Q_BLOCK
    max_exact = REL_BUCKETS // 2
    d_const = int(math.ceil(max_exact * (REL_MAX_DIST / max_exact)
                            ** ((REL_BUCKETS - max_exact - 1) / (REL_BUCKETS - max_exact)))) + 1
    mc = min(-(-(d_const + Q_BLOCK - 1) // Q_BLOCK), qblocks - 1)
    nrev = s + 2 * Q_BLOCK
    nasc = s + 3 * Q_BLOCK
    brev = _t5_bucket(s + Q_BLOCK - 1 - jnp.arange(nrev, dtype=jnp.int32))[None, :]
    basc = _t5_bucket(jnp.arange(nasc, dtype=jnp.int32) - 2 * Q_BLOCK)[None, :]
    tb, asc = pl.pallas_call(
        functools.partial(_bias_table_kernel, s=s, mc=mc),
        grid=(h,),
        in_specs=[pl.BlockSpec((1, nrev), lambda i: (0, 0)),
                  pl.BlockSpec((1, nasc), lambda i: (0, 0)),
                  pl.BlockSpec((1, 1, REL_BUCKETS), lambda i: (i, 0, 0))],
        out_specs=[pl.BlockSpec((1, mc + 1, Q_BLOCK, Q_BLOCK), lambda i: (i, 0, 0, 0)),
                   pl.BlockSpec((1, 1, nasc), lambda i: (i, 0, 0))],
        out_shape=[jax.ShapeDtypeStruct((h, mc + 1, Q_BLOCK, Q_BLOCK), F32),
                   jax.ShapeDtypeStruct((h, 1, nasc), F32)],
        compiler_params=_cparams("arbitrary"),
        name="bias_tables",
    )(brev, basc, rel_bias.T.reshape(h, 1, REL_BUCKETS))
    asc = asc[:, 0, :]
    base = 2 * Q_BLOCK - (CMP_LEN - 1)
    segs = [asc[:, base - CMP_STRIDE * nn:base - CMP_STRIDE * nn + s].reshape(h, qblocks, Q_BLOCK)
            for nn in range(SUBLANES)]
    trc = jnp.flip(jnp.stack(segs, axis=2), axis=1).reshape(h, qblocks * SUBLANES, Q_BLOCK)
    return tb, jnp.pad(trc, ((0, 0), (0, qblocks * SUBLANES), (0, 0)))


def _nsa_cmp_kernel(q_ref, kc_ref, vct_ref, trc_ref, smt_ref, gate_ref, oc_ref, sel_ref, *, qblocks, topn, nb):
    for c in range(nb):
        _nsa_cmp_block(pl.program_id(1) * nb + c, slice(c * Q_BLOCK, (c + 1) * Q_BLOCK), q_ref, kc_ref, vct_ref,
                       trc_ref, smt_ref, gate_ref, oc_ref, sel_ref, qblocks=qblocks, topn=topn)


def _nsa_cmp_block(qb, rows, q_ref, kc_ref, vct_ref, trc_ref, smt_ref, gate_ref, oc_ref, sel_ref, *, qblocks, topn):
    rq = NSA_GROUP * Q_BLOCK
    q = q_ref[:, rows, :].reshape(rq, NSA_HEAD_DIM)
    kc = kc_ref[0]
    nc = kc.shape[0]
    nslc = smt_ref.shape[0]
    s = lax.dot_general(kc, q, _NT, preferred_element_type=F32)
    n_io = lax.broadcasted_iota(jnp.int32, (nc, Q_BLOCK), 0)
    i_io = lax.broadcasted_iota(jnp.int32, (nc, Q_BLOCK), 1)
    mask = (n_io * CMP_STRIDE + (CMP_LEN - 1)) <= (qb * Q_BLOCK + i_io)
    off = pl.multiple_of((qblocks - 1 - qb) * SUBLANES, SUBLANES)
    ps = []
    for r in range(NSA_GROUP):
        l = jnp.where(mask, s[:, r * Q_BLOCK:(r + 1) * Q_BLOCK] + trc_ref[r, pl.ds(off, nc), :], NEG_INF)
        m = jnp.max(l, axis=0, keepdims=True)
        p = jnp.where(mask, jnp.exp2(l - m), 0.0)
        den = jnp.maximum(jnp.sum(p, axis=0, keepdims=True), 1e-30)
        ps.append(p / den)
    p_all = jnp.concatenate(ps, axis=1).astype(BF)
    oc_t = jnp.dot(vct_ref[0], p_all, preferred_element_type=F32)
    imp4 = jnp.dot(smt_ref[...], p_all, preferred_element_type=F32)
    imp = imp4[:, 0:Q_BLOCK]
    for r in range(1, NSA_GROUP):
        imp = imp + imp4[:, r * Q_BLOCK:(r + 1) * Q_BLOCK]

    j_io = lax.broadcasted_iota(jnp.int32, (nslc, Q_BLOCK), 0)
    t_io = qb * Q_BLOCK + lax.broadcasted_iota(jnp.int32, (nslc, Q_BLOCK), 1)
    cur = t_io >> int(math.log2(SLC_LEN))
    forced = jnp.logical_or(j_io == 0, jnp.logical_or(j_io == cur, j_io == cur - 1))
    n_forced = 3
    val = jnp.where(forced, -3e38, jnp.where(j_io <= cur, imp, -1e9))
    sel = jnp.where(forced, 1.0, 0.0)
    for _ in range(max(topn - n_forced, 0)):
        mx = jnp.max(val, axis=0, keepdims=True)
        cand = jnp.where(val == mx, j_io, nslc)
        jmin = jnp.min(cand, axis=0, keepdims=True)
        pick = j_io == jmin
        sel = jnp.where(pick, 1.0, sel)
        val = jnp.where(pick, -3e38, val)
    sel_ref[0, rows, :] = jnp.where(sel.T > 0.5, 0.0, NEG_INF).astype(sel_ref.dtype)

    gates = gate_ref[0, rows, :]
    for r in range(NSA_GROUP):
        o_r = oc_t[:, r * Q_BLOCK:(r + 1) * Q_BLOCK].T
        oc_ref[rows, r * NSA_HEAD_DIM:(r + 1) * NSA_HEAD_DIM] = o_r * gates[:, 3 * r:3 * r + 1]


def _nsa_cmp(q, kc, vct, trc, smt, gates):
    h, s, d = q.shape
    g = NSA_KV_HEADS
    qblocks = s // Q_BLOCK
    nc = kc.shape[1]
    nslc = smt.shape[0]
    topn = min(SLC_TOPN, nslc)
    nb = 2
    qrows = nb * Q_BLOCK
    return pl.pallas_call(
        functools.partial(_nsa_cmp_kernel, qblocks=qblocks, topn=topn, nb=nb),
        grid=(g, qblocks // nb),
        in_specs=[pl.BlockSpec((NSA_GROUP, qrows, d), lambda gg, i: (gg, i, 0)),
                  pl.BlockSpec((1, nc, d), lambda gg, i: (gg, 0, 0)),
                  pl.BlockSpec((1, d, nc), lambda gg, i: (gg, 0, 0)),
                  pl.BlockSpec((NSA_GROUP, trc.shape[1], Q_BLOCK), lambda gg, i: (gg, 0, 0)),
                  pl.BlockSpec(smt.shape, lambda gg, i: (0, 0)),
                  pl.BlockSpec((1, qrows, LANES), lambda gg, i: (gg, i, 0))],
        out_specs=[pl.BlockSpec((qrows, NSA_GROUP * d), lambda gg, i: (i, gg)),
                   pl.BlockSpec((1, qrows, nslc), lambda gg, i: (gg, i, 0))],
        out_shape=[jax.ShapeDtypeStruct((s, h * d), F32),
                   jax.ShapeDtypeStruct((g, s, nslc), BF)],
        compiler_params=_cparams("arbitrary", "arbitrary"),
        name="nsa_cmp",
    )(q, kc, vct, trc, smt, gates)


def _nsa_sw_kernel(q_ref, kv_ref, tb_ref, sel_ref, et_ref, gate_ref, oc_ref,
                   o_ref, s0_ref, s1_ref, m_ref, acc_ref, part_ref, *, tk, mc, nch):
    i = pl.program_id(0)
    ngrp = NSA_KV_HEADS
    rq = NSA_GROUP * Q_BLOCK
    nsub = tk // Q_BLOCK
    s_bufs = (s0_ref, s1_ref)
    qbs = [i * nch + c for c in range(nch)]
    qs = [[q_ref[g * NSA_GROUP:(g + 1) * NSA_GROUP, c * Q_BLOCK:(c + 1) * Q_BLOCK, :].reshape(rq, NSA_HEAD_DIM)
           for c in range(nch)] for g in range(ngrp)]

    def bias_tile(g, r, c, kb0, nblk):
        return jnp.concatenate(
            [tb_ref[g * NSA_GROUP + r, jnp.clip(qbs[c] - (kb0 + b), 0, mc)] for b in range(nblk)], axis=1)

    def gated(g, c, o, branch, base):
        gates = gate_ref[g, c * Q_BLOCK:(c + 1) * Q_BLOCK, :]
        outs = []
        for r in range(NSA_GROUP):
            o_r = o[r * Q_BLOCK:(r + 1) * Q_BLOCK, :] * gates[:, 3 * r + branch:3 * r + branch + 1]
            outs.append(base[:, r * NSA_HEAD_DIM:(r + 1) * NSA_HEAD_DIM] + o_r)
        return jnp.concatenate(outs, axis=1)

    _init_state(m_ref, acc_ref)
    qa = [[jnp.concatenate([qs[g][c], jnp.concatenate(
        [sel_ref[g, c * Q_BLOCK:(c + 1) * Q_BLOCK, :]] * NSA_GROUP, axis=0)], axis=1) for c in range(nch)]
        for g in range(ngrp)]

    def pre(g, j, causal):
        off = pl.multiple_of(j * tk, tk)
        ka = jnp.concatenate([kv_ref[g, pl.ds(off, tk), :], et_ref[pl.ds(off, tk), :]], axis=1)
        for c in range(nch):
            s = lax.dot_general(qa[g][c], ka, _NT, preferred_element_type=F32)
            if causal:
                kk = off + lax.broadcasted_iota(jnp.int32, (Q_BLOCK, tk), 1)
                ii = qbs[c] * Q_BLOCK + lax.broadcasted_iota(jnp.int32, (Q_BLOCK, tk), 0)
                future = jnp.where(kk <= ii, 0.0, NEG_INF)
            for r in range(NSA_GROUP):
                bias = bias_tile(g, r, c, j * nsub, nsub)
                if causal:
                    bias = bias + future
                s_bufs[g][c, r * Q_BLOCK:(r + 1) * Q_BLOCK, :] = s[r * Q_BLOCK:(r + 1) * Q_BLOCK, :] + bias

    def process(g, j):
        vp = _with_ones(kv_ref[ngrp + g, pl.ds(pl.multiple_of(j * tk, tk), tk), :])
        for c in range(nch):
            _flash_update(s_bufs[g].at[c], vp, m_ref.at[g * nch + c], acc_ref.at[g * nch + c])

    n_past = (i * nch) // nsub

    def body(t, carry):
        pre(1, t, False)
        process(0, t)
        pre(0, t + 1, False)
        process(1, t)
        return carry

    pre(0, n_past, True)
    pre(1, n_past, True)
    process(0, n_past)
    pre(0, 0, False)
    process(1, n_past)

    wk = WINDOW + Q_BLOCK
    wblk = wk // Q_BLOCK
    for g in range(ngrp):
        cols = slice(g * rq, (g + 1) * rq)
        for c in range(nch):
            rows = slice(c * Q_BLOCK, (c + 1) * Q_BLOCK)
            kb0 = jnp.maximum(qbs[c] - WINDOW // Q_BLOCK, 0)
            off = pl.multiple_of(kb0 * Q_BLOCK, Q_BLOCK)
            s = lax.dot_general(qs[g][c], kv_ref[2 * ngrp + g, pl.ds(off, wk), :], _NT, preferred_element_type=F32)
            rel = (qbs[c] * Q_BLOCK + lax.broadcasted_iota(jnp.int32, (Q_BLOCK, wk), 0)) - (
                off + lax.broadcasted_iota(jnp.int32, (Q_BLOCK, wk), 1))
            outside = jnp.where(jnp.logical_and(rel >= 0, rel < WINDOW), 0.0, NEG_INF)
            l = jnp.concatenate([s[r * Q_BLOCK:(r + 1) * Q_BLOCK, :] + (bias_tile(g, r, c, kb0, wblk) + outside)
                                 for r in range(NSA_GROUP)], axis=0)
            p = jnp.exp2(l - jnp.max(l, axis=1, keepdims=True)).astype(BF)
            ow = jnp.dot(p, _with_ones(kv_ref[3 * ngrp + g, pl.ds(off, wk), :]), preferred_element_type=F32)
            part_ref[rows, cols] = gated(g, c, _normalized(ow), 2, oc_ref[rows, cols])

    lax.fori_loop(0, n_past, body, 0)

    for g in range(ngrp):
        cols = slice(g * rq, (g + 1) * rq)
        for c in range(nch):
            rows = slice(c * Q_BLOCK, (c + 1) * Q_BLOCK)
            o_ref[rows, cols] = gated(g, c, _normalized(acc_ref[g * nch + c]), 1,
                                      part_ref[rows, cols]).astype(o_ref.dtype)


def _nsa_sw(q, kvsw, tb, sel, et, gates, oc):
    h, s, d = q.shape
    g = NSA_KV_HEADS
    nch = 2
    qrows = nch * Q_BLOCK
    tk = min(512, s)
    mc = tb.shape[1] - 1
    nslc = sel.shape[2]
    rq = NSA_GROUP * Q_BLOCK
    assert tk % qrows == 0 and WINDOW + Q_BLOCK <= s and NSA_GROUP * d == rq

    def resident(arr):
        return pl.BlockSpec(arr.shape, lambda i, nd=arr.ndim: (0,) * nd, pipeline_mode=pl.Buffered(1))

    return pl.pallas_call(
        functools.partial(_nsa_sw_kernel, tk=tk, mc=mc, nch=nch),
        grid=(s // qrows,),
        in_specs=[pl.BlockSpec((h, qrows, d), lambda i: (0, i, 0)),
                  resident(kvsw), resident(tb),
                  pl.BlockSpec((g, qrows, nslc), lambda i: (0, i, 0)),
                  resident(et),
                  pl.BlockSpec((g, qrows, LANES), lambda i: (0, i, 0)),
                  pl.BlockSpec((qrows, h * d), lambda i: (i, 0))],
        out_specs=pl.BlockSpec((qrows, h * d), lambda i: (i, 0)),
        out_shape=jax.ShapeDtypeStruct((s, h * d), BF),
        scratch_shapes=[pltpu.VMEM((nch, rq, tk), F32), pltpu.VMEM((nch, rq, tk), F32),
                        pltpu.VMEM((g * nch, rq, 1), F32), pltpu.VMEM((g * nch, rq, 2 * LANES), F32),
                        pltpu.VMEM((qrows, h * d), F32)],
        compiler_params=_cparams("arbitrary"),
        name="nsa_sel_win",
    )(q, kvsw, tb, sel, et, gates, oc)


def _selection_map_t(nc, nslc):
    n = np.arange(nc)[None, :] * CMP_STRIDE
    j0 = np.arange(nslc)[:, None] * SLC_LEN
    valid = np.arange(nc)[None, :] < nc - 1
    return jnp.asarray(((n < j0 + SLC_LEN) & (n + CMP_LEN > j0) & valid).astype(np.float32), dtype=BF)


def _block_onehot(s, nslc):
    tok = np.arange(s)[:, None]
    j = np.arange(nslc)[None, :]
    return jnp.asarray((tok // SLC_LEN == j).astype(np.float32), dtype=BF)


def _rope_tables(positions):
    half = MLA_ROPE // 2
    inv = ROPE_THETA ** (-jnp.arange(half, dtype=F32) / half)
    ang = positions.astype(F32)[:, None] * inv
    cos, sin = jnp.cos(ang), jnp.sin(ang)
    z = jnp.zeros_like(cos)
    zpad = jnp.zeros((positions.shape[0], LANES - MLA_ROPE), F32)
    c = jnp.concatenate([cos, cos, zpad], axis=1)
    s1 = jnp.concatenate([-sin, z, zpad], axis=1)
    s2 = jnp.concatenate([z, sin, zpad], axis=1)
    return c, s1, s2


def _pad_cols(w, n):
    return jnp.pad(w, ((0, 0), (0, n - w.shape[1])))


def _pad_rows(w, n):
    return jnp.pad(w, ((0, n - w.shape[0]), (0, 0)))


def _nsa_mla_mixer(hn, pos, rel_bias, w_in_all, w_out_all, e, gate_b, pos_k, w1_k, w2_k, pos_v, w1_v, w2_v,
                   q_norm, w_uq, kv_norm, w_ukv, x, g1):
    s = hn.shape[0]
    d = NSA_HEAD_DIM
    nq = NSA_HEADS * d
    nkv = 2 * NSA_KV_HEADS * d
    w_in_t = jnp.swapaxes(w_in_all, 1, 2)
    dm = w_in_t.shape[2]
    o0 = nq + 3 * nkv
    w_g_t = w_in_t[e, o0:o0 + 3 * NSA_HEADS]; o0 += 3 * NSA_HEADS
    w_lat_t = w_in_t[e, o0:o0 + MLA_Q_RANK + MLA_KV_RANK]; o0 += MLA_Q_RANK + MLA_KV_RANK
    w_kr_t = w_in_t[e, o0:o0 + MLA_ROPE]

    q_nsa = _proj([hn], [(w_in_t, dm, (e, 0), 0)], functools.partial(_epi_scale, d ** -0.5 * LOG2E), n=nq,
                  out_dtype=BF, head_major=True, tn=1024, w_transposed=True, name="proj_q_nsa")
    kv_cmp = _proj([hn], [(w_in_t, dm, (e, 0), nq)], _epi_id, n=nkv, out_dtype=F32, head_major=True,
                   w_transposed=True, name="proj_kv_cmp")
    kv_sw = _proj([hn], [(w_in_t, dm, (e, 0), nq + nkv)], _epi_id, n=2 * nkv, out_dtype=BF, head_major=True,
                  w_transposed=True, name="proj_kv_sw")
    per_g = 3 * NSA_GROUP
    w_gp_t = jnp.concatenate([_pad_rows(w_g_t[g * per_g:(g + 1) * per_g], LANES) for g in range(NSA_KV_HEADS)], 0)
    b_gp = jnp.concatenate([_pad_cols(gate_b[None, g * per_g:(g + 1) * per_g], LANES)
                            for g in range(NSA_KV_HEADS)], 1)
    gates = _proj([hn], [(w_gp_t[None], dm, (0, 0), 0)], _epi_sigmoid_bias, [(b_gp, "row")],
                  n=NSA_KV_HEADS * LANES, out_dtype=F32, head_major=True, w_transposed=True,
                  name="proj_gates")

    nc = s // CMP_STRIDE
    gw = CMP_STRIDE * d
    pos_kv = jnp.stack([pos_k.reshape(2, gw), pos_v.reshape(2, gw)])
    w1_kv = jnp.stack([w1_k.reshape(2, gw, CMP_HIDDEN), w1_v.reshape(2, gw, CMP_HIDDEN)]).astype(BF)
    w2_kv = jnp.stack([w2_k, w2_v]).astype(BF)
    kvc = _nsa_compress(kv_cmp.reshape(2 * NSA_KV_HEADS, nc, gw), pos_kv, w1_kv, w2_kv)
    kc = kvc[:NSA_KV_HEADS]
    vct = jnp.swapaxes(kvc[NSA_KV_HEADS:], 1, 2)

    nslc = s // SLC_LEN
    tb, trc = _bias_tables(rel_bias, s)
    smt = _selection_map_t(nc, nslc)
    oc, sel = _nsa_cmp(q_nsa, kc, vct, trc, smt, gates)
    o_nsa = _nsa_sw(q_nsa, kv_sw, tb, sel, _block_onehot(s, nslc), gates, oc)

    nlat = MLA_Q_RANK + MLA_KV_RANK
    lat = _proj([hn], [(w_lat_t[None], dm, (0, 0), 0)], _epi_rmsnorm,
                [(jnp.concatenate([q_norm, kv_norm])[None, :], "row")], n=nlat, out_dtype=BF, tn=MLA_Q_RANK,
                w_transposed=True, name="proj_mla_latent")
    c, s1, s2 = _rope_tables(pos)
    kr = _proj([hn], [(_pad_rows(w_kr_t, LANES)[None], dm, (0, 0), 0)], functools.partial(_epi_rope, 1.0),
               [(c, "rowtile"), (s1, "rowtile"), (s2, "rowtile")], n=LANES, out_dtype=BF, w_transposed=True,
               name="proj_mla_kr")
    scale = (MLA_NOPE + MLA_ROPE) ** -0.5 * LOG2E
    w_uq3 = w_uq.reshape(MLA_Q_RANK, MLA_HEADS, MLA_NOPE + MLA_ROPE)
    w_uqn = w_uq3[:, :, :MLA_NOPE].reshape(MLA_Q_RANK, MLA_HEADS * MLA_NOPE)
    w_uqr = jnp.pad(w_uq3[:, :, MLA_NOPE:], ((0, 0), (0, 0), (0, LANES - MLA_ROPE))).reshape(
        MLA_Q_RANK, MLA_HEADS * LANES)
    qn = _proj([lat], [(w_uqn, MLA_Q_RANK, 0, 0)], functools.partial(_epi_scale, scale),
               n=MLA_HEADS * MLA_NOPE, out_dtype=BF, head_major=True, tn=1024, lhs_col_block=[0],
               name="proj_mla_qn")
    qr = _proj([lat], [(w_uqr, MLA_Q_RANK, 0, 0)], functools.partial(_epi_rope, scale),
               [(c, "rowtile"), (s1, "rowtile"), (s2, "rowtile")], n=MLA_HEADS * LANES, out_dtype=BF,
               head_major=True, lhs_col_block=[0], name="proj_mla_qr")
    kvh = _proj([lat], [(w_ukv, MLA_KV_RANK, 0, 0)], _epi_id, n=MLA_HEADS * (MLA_NOPE + MLA_V), out_dtype=BF,
                head_major=True, tn=1024, lhs_col_block=[1], name="proj_mla_kv")
    o_mla = _mla_attention(qn, qr, kvh, kr)

    return _proj([o_nsa, o_mla], [(w_out_all, nq, (e, 0), 0), (w_out_all, MLA_HEADS * MLA_V, (e, 1), 0)],
                 _epi_residual, [(x, "tile"), (g1, "row")], n=w_out_all.shape[2], out_dtype=F32, tn=1024,
                 name="proj_even_out")


def _fox_mixer(hn, w_in_all, w_out_all, o, f_b, q_norm, k_norm, x, g1):
    d = D_MODEL
    dh = FOX_HEAD_DIM
    w_in_t = jnp.swapaxes(w_in_all, 1, 2)
    w_f_t = w_in_t[o, 3 * d:3 * d + FOX_HEADS]
    w_og_t = w_in_t[o, 3 * d + FOX_HEADS:]
    q = _proj([hn], [(w_in_t, d, (o, 0), 0)], functools.partial(_epi_headnorm, dh ** -0.5 * LOG2E),
              [(q_norm[None, :], "const")], n=d, out_dtype=BF, head_major=True, tn=1024, w_transposed=True,
              name="proj_fox_q")
    k = _proj([hn], [(w_in_t, d, (o, 0), d)], functools.partial(_epi_headnorm, 1.0),
              [(k_norm[None, :], "const")], n=d, out_dtype=BF, head_major=True, tn=1024, w_transposed=True,
              name="proj_fox_k")
    v = _proj([hn], [(w_in_t, d, (o, 0), 2 * d)], _epi_id, n=d, out_dtype=BF, head_major=True, tn=1024,
              w_transposed=True, name="proj_fox_v")
    lf = _proj([hn], [(_pad_rows(w_f_t, LANES)[None], d, (0, 0), 0)], _epi_logsigmoid_bias,
               [(_pad_cols(f_b[None, :], LANES), "row")], n=LANES, out_dtype=F32, w_transposed=True,
               name="proj_fox_f")
    sig_og = _proj([hn], [(w_og_t[None], d, (0, 0), 0)], _epi_sigmoid, n=d, out_dtype=F32, tn=1024,
                   w_transposed=True, name="proj_fox_og")
    cum, cum_t = _cumsum_tokens(lf, LOG2E)
    cum_t = cum_t[:FOX_HEADS].reshape(FOX_HEADS, 1, -1)
    att = _fox_attention(q, k, v, cum_t, cum, sig_og)
    return _proj([att], [(w_out_all, d, (o, 0), 0)], _epi_residual, [(x, "tile"), (g1, "row")], n=d,
                 out_dtype=F32, tn=1024, name="proj_fox_out")


def kernel(x, c, positions, rel_bias, ada_w, ada_b, norm_mix, norm_ffn, ffn_w1, ffn_w3, ffn_w2, even_w_in, even_w_out, nsa_gate_b, nsa_cmp_pos_k, nsa_cmp_w1_k, nsa_cmp_w2_k, nsa_cmp_pos_v, nsa_cmp_w1_v, nsa_cmp_w2_v, mla_q_norm, mla_w_uq, mla_kv_norm, mla_w_ukv, fox_w_in, fox_w_out, fox_f_b, fox_q_norm, fox_k_norm, final_norm):
    b, s, d = x.shape
    assert b == 1 and d == D_MODEL and s % 1024 == 0
    xs = x[0]
    pos = positions[0]
    mod = _adaln(c, ada_w, ada_b)
    depth = ada_w.shape[0]
    hn = None
    for i in range(depth):
        sh1, sc1, g1, sh2, sc2, g2 = [mod[i:i + 1, k * d:(k + 1) * d] for k in range(6)]
        if hn is None:
            hn = _normmod(xs, norm_mix[i][None, :], sc1, sh1)
        if i % 2 == 0:
            e = i // 2
            xs = _nsa_mla_mixer(hn, pos, rel_bias, even_w_in, even_w_out, e, nsa_gate_b[e],
                                nsa_cmp_pos_k[e], nsa_cmp_w1_k[e], nsa_cmp_w2_k[e],
                                nsa_cmp_pos_v[e], nsa_cmp_w1_v[e], nsa_cmp_w2_v[e],
                                mla_q_norm[e], mla_w_uq[e], mla_kv_norm[e], mla_w_ukv[e], xs, g1)
        else:
            o = i // 2
            xs = _fox_mixer(hn, fox_w_in, fox_w_out, o, fox_f_b[o], fox_q_norm[o], fox_k_norm[o], xs, g1)
        if i == depth - 1:
            xs = _ffn(xs, norm_ffn[i][None, :], sc2, sh2, g2, ffn_w1, ffn_w3, ffn_w2, i,
                      final_gain=final_norm[None, :])
        else:
            nxt = (norm_mix[i + 1][None, :], mod[i + 1:i + 2, d:2 * d], mod[i + 1:i + 2, 0:d])
            xs, hn = _ffn(xs, norm_ffn[i][None, :], sc2, sh2, g2, ffn_w1, ffn_w3, ffn_w2, i, next_mod=nxt)
    return xs[None]
```

```python
import functools
import math

import numpy as np
import jax
import jax.numpy as jnp
from jax import lax
from jax.experimental import pallas as pl
from jax.experimental.pallas import tpu as pltpu

D_MODEL = 2048
DEPTH = 2
EPS = 1e-6
NEG_INF = -1e30

NSA_HEADS = 8
NSA_KV_HEADS = 2
NSA_GROUP = NSA_HEADS // NSA_KV_HEADS
NSA_HEAD_DIM = 128
CMP_LEN = 32
CMP_STRIDE = 16
CMP_HIDDEN = 256
SLC_LEN = 64
SLC_TOPN = 16
WINDOW = 512
Q_BLOCK = 128

MLA_HEADS = 8
MLA_Q_RANK = 512
MLA_KV_RANK = 512
MLA_NOPE = 128
MLA_ROPE = 64
MLA_V = 128
ROPE_THETA = 10000.0

FOX_HEADS = 16
FOX_HEAD_DIM = D_MODEL // FOX_HEADS

REL_BUCKETS = 32
REL_MAX_DIST = 4096

FFN_HIDDEN = ((8 * D_MODEL + 2) // 3 + 255) // 256 * 256

LANES = 128
SUBLANES = 8
VMEM_LIMIT_BYTES = 56 * 1024 * 1024
LOG2E = math.log2(math.e)

BF = jnp.bfloat16
F32 = jnp.float32
_NT = (((1,), (1,)), ((), ()))


def _cparams(*sem):
    return pltpu.CompilerParams(dimension_semantics=sem, vmem_limit_bytes=VMEM_LIMIT_BYTES)


def _rms(x):
    return x * lax.rsqrt(jnp.mean(x * x, axis=-1, keepdims=True) + EPS)


def _adaln_kernel(c_ref, w_ref, b_ref, o_ref):
    c = c_ref[...]
    cond = c * jax.nn.sigmoid(c)
    acc = jnp.dot(cond.astype(BF), w_ref[0].astype(BF), preferred_element_type=F32)
    o_ref[0] = acc + b_ref[0]


def _adaln(c, ada_w, ada_b):
    depth, d, n = ada_w.shape
    tn = 1024
    c8 = jnp.broadcast_to(c, (SUBLANES, d))
    out = pl.pallas_call(
        _adaln_kernel,
        grid=(depth, n // tn),
        in_specs=[pl.BlockSpec((SUBLANES, d), lambda i, j: (0, 0)),
                  pl.BlockSpec((1, d, tn), lambda i, j: (i, 0, j)),
                  pl.BlockSpec((1, 1, tn), lambda i, j: (i, 0, j))],
        out_specs=pl.BlockSpec((1, SUBLANES, tn), lambda i, j: (i, 0, j)),
        out_shape=jax.ShapeDtypeStruct((depth, SUBLANES, n), F32),
        compiler_params=_cparams("arbitrary", "arbitrary"),
        name="adaln",
    )(c8, ada_w, ada_b.reshape(depth, 1, n))
    return out[:, 0, :]


def _normmod_kernel(x_ref, g_ref, sc_ref, sh_ref, o_ref):
    y = _rms(x_ref[...])
    o_ref[...] = ((y * g_ref[...]) * (1.0 + sc_ref[...]) + sh_ref[...]).astype(o_ref.dtype)


def _normmod(x, g, sc, sh):
    s, d = x.shape
    tm = 512
    row = pl.BlockSpec((1, d), lambda i: (0, 0))
    return pl.pallas_call(
        _normmod_kernel,
        grid=(s // tm,),
        in_specs=[pl.BlockSpec((tm, d), lambda i: (i, 0)), row, row, row],
        out_specs=pl.BlockSpec((tm, d), lambda i: (i, 0)),
        out_shape=jax.ShapeDtypeStruct((s, d), BF),
        compiler_params=_cparams("arbitrary"),
        name="normmod",
    )(x, g, sc, sh)


def _proj_kernel(*refs, n_lhs, n_epi, epi, head_major, w_transposed):
    lhs = refs[:n_lhs]
    ws = refs[n_lhs:2 * n_lhs]
    epis = refs[2 * n_lhs:2 * n_lhs + n_epi]
    o_ref = refs[2 * n_lhs + n_epi]
    wbf = refs[2 * n_lhs + n_epi + 1:]

    @pl.when(pl.program_id(1) == 0)
    def _():
        for w, wb in zip(ws, wbf):
            wb[...] = w[...].astype(BF)

    acc = None
    for a, wb in zip(lhs, wbf):
        if w_transposed:
            d = lax.dot_general(a[...], wb[...], _NT, preferred_element_type=F32)
        else:
            d = jnp.dot(a[...], wb[...], preferred_element_type=F32)
        acc = d if acc is None else acc + d
    res = epi(acc, *[e[...] for e in epis])
    if head_major:
        for r in range(o_ref.shape[0]):
            o_ref[r] = res[:, r * LANES:(r + 1) * LANES].astype(o_ref.dtype)
    else:
        o_ref[...] = res.astype(o_ref.dtype)


def _proj(lhs, ws, epi, epi_in=(), *, n, out_dtype, head_major=False, tm=1024, tn=512,
          lhs_col_block=None, w_transposed=False, name="proj"):
    m = lhs[0].shape[0]
    tm = min(tm, m)
    tn = min(tn, n)
    if lhs_col_block is None:
        lhs_col_block = [0] * len(lhs)
    in_specs = []
    for (_, k, _, _), cb in zip(ws, lhs_col_block):
        in_specs.append(pl.BlockSpec((tm, k), lambda j, i, cb=cb: (i, cb)))
    for arr, k, rb, col0 in ws:
        assert col0 % tn == 0
        if w_transposed:
            in_specs.append(pl.BlockSpec((None, tn, k),
                                         lambda j, i, rb=rb, cb0=col0 // tn: (rb[0], cb0 + j, rb[1])))
        elif arr.ndim == 3:
            in_specs.append(pl.BlockSpec((None, k, tn),
                                         lambda j, i, rb=rb, cb0=col0 // tn: (rb[0], rb[1], cb0 + j)))
        else:
            in_specs.append(pl.BlockSpec((k, tn), lambda j, i, rb=rb, cb0=col0 // tn: (rb, cb0 + j)))
    arrays = list(lhs) + [w[0] for w in ws]
    for arr, kind in epi_in:
        if kind == "row":
            in_specs.append(pl.BlockSpec((1, tn), lambda j, i: (0, j)))
        elif kind == "const":
            in_specs.append(pl.BlockSpec(arr.shape, lambda j, i: (0, 0)))
        elif kind == "tile":
            in_specs.append(pl.BlockSpec((tm, tn), lambda j, i: (i, j)))
        elif kind == "rowtile":
            in_specs.append(pl.BlockSpec((tm, arr.shape[1]), lambda j, i: (i, 0)))
        else:
            raise ValueError(kind)
        arrays.append(arr)
    if head_major:
        hpt = tn // LANES
        out_spec = pl.BlockSpec((hpt, tm, LANES), lambda j, i: (j, i, 0))
        out_shape = jax.ShapeDtypeStruct((n // LANES, m, LANES), out_dtype)
    else:
        out_spec = pl.BlockSpec((tm, tn), lambda j, i: (i, j))
        out_shape = jax.ShapeDtypeStruct((m, n), out_dtype)
    kern = functools.partial(_proj_kernel, n_lhs=len(lhs), n_epi=len(epi_in), epi=epi,
                             head_major=head_major, w_transposed=w_transposed)
    return pl.pallas_call(
        kern,
        grid=(n // tn, m // tm),
        in_specs=in_specs,
        out_specs=out_spec,
        out_shape=out_shape,
        scratch_shapes=[pltpu.VMEM((tn, k) if w_transposed else (k, tn), BF) for _, k, _, _ in ws],
        compiler_params=_cparams("arbitrary", "arbitrary"),
        name=name,
    )(*arrays)


def _epi_id(acc):
    return acc


def _epi_scale(scale, acc):
    return acc * scale


def _epi_sigmoid_bias(acc, b):
    return jax.nn.sigmoid(acc + b)


def _epi_logsigmoid_bias(acc, b):
    return jax.nn.log_sigmoid(acc + b)


def _epi_sigmoid(acc):
    return jax.nn.sigmoid(acc)


def _epi_rmsnorm(acc, g):
    return _rms(acc) * g


def _epi_headnorm(scale, acc, g):
    outs = []
    for r in range(acc.shape[1] // LANES):
        outs.append(_rms(acc[:, r * LANES:(r + 1) * LANES]) * g * scale)
    return jnp.concatenate(outs, axis=1)


def _epi_rope(scale, acc, c, s1, s2):
    reps = acc.shape[1] // LANES
    half = MLA_ROPE // 2
    if reps > 1:
        c = jnp.concatenate([c] * reps, axis=1)
        s1 = jnp.concatenate([s1] * reps, axis=1)
        s2 = jnp.concatenate([s2] * reps, axis=1)
    n = acc.shape[1]
    out = acc * c + pltpu.roll(acc, n - half, 1) * s1 + pltpu.roll(acc, half, 1) * s2
    return out * scale


def _epi_residual(acc, x, g):
    return x + g * acc


def _ffn_kernel(*refs, mode):
    x_ref, g_ref, sc_ref, sh_ref, g2_ref, w1_ref, w3_ref, w2_ref = refs[:8]
    if mode == "final":
        fn_ref, o_ref, hn_ref = refs[8:]
    elif mode == "next":
        gn_ref, scn_ref, shn_ref, o_ref, hnext_ref, hn_ref = refs[8:]
    else:
        o_ref, hn_ref = refs[8:]
    f = pl.program_id(1)
    tm = x_ref.shape[0]
    halves = [slice(r * (tm // 2), (r + 1) * (tm // 2)) for r in range(2)]

    @pl.when(f == 0)
    def _():
        for rows in halves:
            x = x_ref[rows, :]
            hn_ref[rows, :] = ((_rms(x) * g_ref[...]) * (1.0 + sc_ref[...]) + sh_ref[...]).astype(BF)
            o_ref[rows, :] = x

    w1 = w1_ref[...].astype(BF)
    w3 = w3_ref[...].astype(BF)
    w2 = w2_ref[...].astype(BF)
    for rows in halves:
        h = hn_ref[rows, :]
        h1 = jnp.dot(h, w1, preferred_element_type=F32)
        h3 = jnp.dot(h, w3, preferred_element_type=F32)
        a = (h1 * jax.nn.sigmoid(h1)) * h3
        o_ref[rows, :] += g2_ref[...] * jnp.dot(a.astype(BF), w2, preferred_element_type=F32)

    if mode != "plain":
        @pl.when(f == pl.num_programs(1) - 1)
        def _():
            for rows in halves:
                y = _rms(o_ref[rows, :])
                if mode == "final":
                    o_ref[rows, :] = y * fn_ref[...]
                else:
                    hnext_ref[rows, :] = ((y * gn_ref[...]) * (1.0 + scn_ref[...]) + shn_ref[...]).astype(BF)


def _ffn(x, g, sc, sh, g2, w1, w3, w2, layer, final_gain=None, next_mod=None):
    assert final_gain is None or next_mod is None
    s, d = x.shape
    fdim = w1.shape[2]
    tm = min(1024, s)
    tf = 256
    row = pl.BlockSpec((1, d), lambda i, f: (0, 0))
    tile = pl.BlockSpec((tm, d), lambda i, f: (i, 0))
    in_specs = [pl.BlockSpec((tm, d), lambda i, f: (i, 0), pipeline_mode=pl.Buffered(1)), row, row, row, row,
                pl.BlockSpec((None, d, tf), lambda i, f: (layer, 0, f)),
                pl.BlockSpec((None, d, tf), lambda i, f: (layer, 0, f)),
                pl.BlockSpec((None, tf, d), lambda i, f: (layer, f, 0))]
    arrays = [x, g, sc, sh, g2, w1, w3, w2]
    out_specs, out_shape, mode = tile, jax.ShapeDtypeStruct((s, d), F32), "plain"
    if final_gain is not None:
        in_specs.append(row)
        arrays.append(final_gain)
        mode = "final"
    elif next_mod is not None:
        in_specs += [row, row, row]
        arrays += list(next_mod)
        out_specs = [tile, tile]
        out_shape = [out_shape, jax.ShapeDtypeStruct((s, d), BF)]
        mode = "next"
    return pl.pallas_call(
        functools.partial(_ffn_kernel, mode=mode),
        grid=(s // tm, fdim // tf),
        in_specs=in_specs,
        out_specs=out_specs,
        out_shape=out_shape,
        scratch_shapes=[pltpu.VMEM((tm, d), BF)],
        compiler_params=_cparams("arbitrary", "arbitrary"),
        name="ffn",
    )(*arrays)


def _flash_update(s, vp, m_ref, acc_ref, row_shift=None):
    m_prev = m_ref[...]
    m_tile = jnp.max(s[...], axis=1, keepdims=True)
    if row_shift is not None:
        m_tile = m_tile + row_shift
    m_new = jnp.maximum(m_prev, m_tile)
    alpha = jnp.exp2(m_prev - m_new)
    sub = m_new if row_shift is None else m_new - row_shift
    p = jnp.exp2(s[...] - sub).astype(BF)
    acc_ref[...] = alpha * acc_ref[...] + jnp.dot(p, vp, preferred_element_type=F32)
    m_ref[...] = m_new


def _init_state(m_ref, acc_ref):
    m_ref[...] = jnp.full_like(m_ref, NEG_INF)
    acc_ref[...] = jnp.zeros_like(acc_ref)


def _with_ones(v):
    return jnp.concatenate([v, jnp.ones(v.shape, v.dtype)], axis=1)


def _normalized(acc):
    return acc[:, :LANES] / acc[:, LANES:]


def _causal_mask(t):
    rows = lax.broadcasted_iota(jnp.int32, (t, t), 0)
    cols = lax.broadcasted_iota(jnp.int32, (t, t), 1)
    return cols <= rows


def _causal_sweep(qi, tc, s_bufs, m_ref, acc_ref, streams, nsplit):
    tw = 2 * tc

    doff = pl.multiple_of(qi * tw, tw)

    def pre_diag(si, c):
        kside_fn, logits_fn, _, _ = streams[si]
        if c == 0:
            s_bufs[si][0, :, 0:tc] = jnp.where(_causal_mask(tc), logits_fn(0, kside_fn(doff, tc)), NEG_INF)
        else:
            rows = tc + lax.broadcasted_iota(jnp.int32, (tc, tw), 0)
            cols = lax.broadcasted_iota(jnp.int32, (tc, tw), 1)
            s_bufs[si][1] = jnp.where(cols <= rows, logits_fn(1, kside_fn(doff, tw)), NEG_INF)

    def process_diag(si, c):
        _, _, vp_fn, shifts = streams[si]
        if c == 0:
            _flash_update(s_bufs[si].at[0, :, 0:tc], vp_fn(doff, tc), m_ref.at[2 * si], acc_ref.at[2 * si],
                          row_shift=shifts[0])
        else:
            _flash_update(s_bufs[si].at[1], vp_fn(doff, tw), m_ref.at[2 * si + 1], acc_ref.at[2 * si + 1],
                          row_shift=shifts[1])

    pre_diag(0, 0)
    pre_diag(0, 1)
    kside_fn0, logits_fn0, _, _ = streams[0]
    kside0 = kside_fn0(0, tw)
    for c in range(2):
        pre_diag(1, c)
        process_diag(0, c)
    for c in range(2):
        s_bufs[0][c] = logits_fn0(c, kside0)
        process_diag(1, c)

    def half(sp, tp, sq, tq_):
        kside_fn, logits_fn, _, _ = streams[sp]
        _, _, vp_fn, shifts = streams[sq]
        kside = kside_fn(pl.multiple_of(tp * tw, tw), tw)
        vp = vp_fn(pl.multiple_of(tq_ * tw, tw), tw)
        rs = tc // nsplit
        for c in range(2):
            for hh in range(nsplit):
                rows = slice(hh * rs, (hh + 1) * rs)
                s_bufs[sp][c, rows, :] = logits_fn(c, kside, rows)
                _flash_update(s_bufs[sq].at[c, rows, :], vp, m_ref.at[2 * sq + c, rows, :],
                              acc_ref.at[2 * sq + c, rows, :],
                              row_shift=None if shifts[c] is None else shifts[c][rows, :])

    def body(t, carry):
        half(1, t, 0, t)
        half(0, t + 1, 1, t)
        return carry

    lax.fori_loop(0, qi, body, 0)


def _sweep_scratch(tc):
    return [pltpu.VMEM((2, tc, 2 * tc), F32), pltpu.VMEM((2, tc, 2 * tc), F32),
            pltpu.VMEM((4, tc, 1), F32), pltpu.VMEM((4, tc, 2 * LANES), F32)]


def _mla_kernel(qn_ref, qr_ref, kv_ref, kr_ref, o_ref, s0_ref, s1_ref, m_ref, acc_ref, *, tc):
    qi = pl.program_id(1)
    _init_state(m_ref, acc_ref)

    def stream(si):
        qs = [jnp.concatenate([qn_ref[si, c * tc:(c + 1) * tc, :], qr_ref[si, c * tc:(c + 1) * tc, :]], axis=1)
              for c in range(2)]

        def kside(off, width):
            return jnp.concatenate([kv_ref[2 * si, pl.ds(off, width), :], kr_ref[pl.ds(off, width), :]], axis=1)

        def logits(c, k, rows=slice(None)):
            return lax.dot_general(qs[c][rows, :], k, _NT, preferred_element_type=F32)

        def vp(off, width):
            return _with_ones(kv_ref[2 * si + 1, pl.ds(off, width), :])

        return kside, logits, vp, (None, None)

    _causal_sweep(qi, tc, (s0_ref, s1_ref), m_ref, acc_ref, [stream(0), stream(1)], nsplit=2)
    for si in range(2):
        for c in range(2):
            o_ref[c * tc:(c + 1) * tc, si * LANES:(si + 1) * LANES] = _normalized(
                acc_ref[2 * si + c]).astype(o_ref.dtype)


def _mla_attention(qn, qr, kvh, kr):
    h, s, _ = qn.shape
    tc = min(512, s // 2)
    w = 2 * tc
    return pl.pallas_call(
        functools.partial(_mla_kernel, tc=tc),
        grid=(h // 2, s // w),
        in_specs=[pl.BlockSpec((2, w, LANES), lambda p, i: (p, i, 0)),
                  pl.BlockSpec((2, w, LANES), lambda p, i: (p, i, 0)),
                  pl.BlockSpec((4, s, LANES), lambda p, i: (p, 0, 0)),
                  pl.BlockSpec((s, LANES), lambda p, i: (0, 0))],
        out_specs=pl.BlockSpec((w, 2 * LANES), lambda p, i: (i, p)),
        out_shape=jax.ShapeDtypeStruct((s, h * LANES), BF),
        scratch_shapes=_sweep_scratch(tc),
        compiler_params=_cparams("arbitrary", "arbitrary"),
        name="mla_attn",
    )(qn, qr, kvh, kr)


def _fox_kernel(q_ref, k_ref, v_ref, ck_ref, cum_ref, og_ref, o_ref, s0_ref, s1_ref, m_ref, acc_ref, *, tc):
    pair = pl.program_id(0)
    qi = pl.program_id(1)
    lane = lax.broadcasted_iota(jnp.int32, (tc, LANES), 1)
    _init_state(m_ref, acc_ref)

    def stream(si):
        qs = [q_ref[si, c * tc:(c + 1) * tc, :] for c in range(2)]
        cqs = [jnp.sum(jnp.where(lane == 2 * pair + si, cum_ref[c * tc:(c + 1) * tc, :], 0.0), axis=1,
                       keepdims=True) for c in range(2)]

        def kside(off, width):
            return k_ref[si, pl.ds(off, width), :], ck_ref[si, :, pl.ds(off, width)]

        def logits(c, kc, rows=slice(None)):
            return lax.dot_general(qs[c][rows, :], kc[0], _NT, preferred_element_type=F32) - kc[1]

        def vp(off, width):
            return _with_ones(v_ref[si, pl.ds(off, width), :])

        return kside, logits, vp, cqs

    _causal_sweep(qi, tc, (s0_ref, s1_ref), m_ref, acc_ref, [stream(0), stream(1)], nsplit=4)
    for si in range(2):
        cols = slice(si * LANES, (si + 1) * LANES)
        for c in range(2):
            rows = slice(c * tc, (c + 1) * tc)
            o_ref[rows, cols] = (_normalized(acc_ref[2 * si + c]) * og_ref[rows, cols]).astype(o_ref.dtype)


def _fox_attention(q, k, v, cum_t, cum, sig_og):
    h, s, _ = q.shape
    tc = min(512, s // 2)
    w = 2 * tc
    hm = pl.BlockSpec((2, s, LANES), lambda p, i: (p, 0, 0))
    return pl.pallas_call(
        functools.partial(_fox_kernel, tc=tc),
        grid=(h // 2, s // w),
        in_specs=[pl.BlockSpec((2, w, LANES), lambda p, i: (p, i, 0)), hm, hm,
                  pl.BlockSpec((2, 1, s), lambda p, i: (p, 0, 0)),
                  pl.BlockSpec((w, LANES), lambda p, i: (i, 0)),
                  pl.BlockSpec((w, 2 * LANES), lambda p, i: (i, p))],
        out_specs=pl.BlockSpec((w, 2 * LANES), lambda p, i: (i, p)),
        out_shape=jax.ShapeDtypeStruct((s, h * LANES), BF),
        scratch_shapes=_sweep_scratch(tc),
        compiler_params=_cparams("arbitrary", "arbitrary"),
        name="fox_attn",
    )(q, k, v, cum_t, cum, sig_og)


def _cumsum_kernel(x_ref, o_ref, ot_ref, carry_ref, *, t, out_scale):
    @pl.when(pl.program_id(0) == 0)
    def _():
        carry_ref[...] = jnp.zeros_like(carry_ref)

    x = x_ref[...]
    rows = lax.broadcasted_iota(jnp.int32, (t, t), 0)
    cols = lax.broadcasted_iota(jnp.int32, (t, t), 1)
    tri = jnp.where(cols <= rows, 1.0, 0.0).astype(BF)
    hi = x.astype(BF)
    r1 = x - hi.astype(F32)
    mid = r1.astype(BF)
    lo = (r1 - mid.astype(F32)).astype(BF)
    cum = (jnp.dot(tri, hi, preferred_element_type=F32) + jnp.dot(tri, mid, preferred_element_type=F32)
           + jnp.dot(tri, lo, preferred_element_type=F32)) + carry_ref[...]
    scaled = cum * out_scale
    o_ref[...] = scaled
    ot_ref[...] = scaled.T
    carry_ref[...] = cum[t - 1:t, :]


def _cumsum_tokens(x, out_scale):
    s, n = x.shape
    t = min(256, s)
    return pl.pallas_call(
        functools.partial(_cumsum_kernel, t=t, out_scale=out_scale),
        grid=(s // t,),
        in_specs=[pl.BlockSpec((t, n), lambda i: (i, 0))],
        out_specs=[pl.BlockSpec((t, n), lambda i: (i, 0)), pl.BlockSpec((n, t), lambda i: (0, i))],
        out_shape=[jax.ShapeDtypeStruct((s, n), F32), jax.ShapeDtypeStruct((n, s), F32)],
        scratch_shapes=[pltpu.VMEM((1, n), F32)],
        compiler_params=_cparams("arbitrary"),
        name="cumsum",
    )(x)


def _compress_kernel(a_ref, pos_ref, w1_ref, w2_ref, o_ref):
    a = a_ref[0]
    nc = a.shape[0]
    p1 = jnp.dot((a + pos_ref[0, 0:1, :]).astype(BF), w1_ref[0, 0], preferred_element_type=F32)
    p2 = jnp.dot((a + pos_ref[0, 1:2, :]).astype(BF), w1_ref[0, 1], preferred_element_type=F32)
    h = p1 + pltpu.roll(p2, nc - 1, 0)
    act = h * jax.nn.sigmoid(h)
    o_ref[0] = jnp.dot(act.astype(BF), w2_ref[0], preferred_element_type=F32).astype(o_ref.dtype)


def _nsa_compress(kv_cmp, pos, w1, w2):
    c, nc, gw = kv_cmp.shape
    return pl.pallas_call(
        _compress_kernel,
        grid=(c,),
        in_specs=[pl.BlockSpec((1, nc, gw), lambda i: (i, 0, 0)),
                  pl.BlockSpec((1, 2, gw), lambda i: (i // 2, 0, 0)),
                  pl.BlockSpec((1, 2, gw, CMP_HIDDEN), lambda i: (i // 2, 0, 0, 0)),
                  pl.BlockSpec((1, CMP_HIDDEN, NSA_HEAD_DIM), lambda i: (i // 2, 0, 0))],
        out_specs=pl.BlockSpec((1, nc, NSA_HEAD_DIM), lambda i: (i, 0, 0)),
        out_shape=jax.ShapeDtypeStruct((c, nc, NSA_HEAD_DIM), BF),
        compiler_params=_cparams("arbitrary"),
        name="nsa_compress",
    )(kv_cmp, pos, w1, w2)


def _bias_table_kernel(brev_ref, basc_ref, tbl_ref, tb_ref, asc_ref, *, s, mc):
    tbl = tbl_ref[0] * LOG2E

    def lookup(bkt):
        out = jnp.zeros(bkt.shape, F32)
        for b in range(REL_BUCKETS):
            out = jnp.where(bkt == b, tbl[:, b:b + 1], out)
        return out

    rev = lookup(brev_ref[...])
    asc_ref[0] = lookup(basc_ref[...])
    for m in range(mc + 1):
        win = rev[:, s - Q_BLOCK * m:s - Q_BLOCK * m + 2 * Q_BLOCK]
        rolled = pltpu.roll(jnp.broadcast_to(win, (Q_BLOCK, 2 * Q_BLOCK)), Q_BLOCK + 1, 1,
                            stride=1, stride_axis=0)
        tb_ref[0, m] = rolled[:, :Q_BLOCK]


def _t5_bucket(dist):
    max_exact = REL_BUCKETS // 2
    d = jnp.maximum(dist, 0)
    ratio = jnp.log(jnp.maximum(d, max_exact).astype(F32) / max_exact) / math.log(REL_MAX_DIST / max_exact)
    large = jnp.minimum(max_exact + (ratio * (REL_BUCKETS - max_exact)).astype(jnp.int32), REL_BUCKETS - 1)
    return jnp.where(d < max_exact, d, large)


def _bias_tables(rel_bias, s):
    assert Q_BLOCK == SUBLANES * CMP_STRIDE
    h = rel_bias.shape[1]
    qblocks = s // Q_BLOCK
    max_exact = REL_BUCKETS // 2
    d_const = int(math.ceil(max_exact * (REL_MAX_DIST / max_exact)
                            ** ((REL_BUCKETS - max_exact - 1) / (REL_BUCKETS - max_exact)))) + 1
    mc = min(-(-(d_const + Q_BLOCK - 1) // Q_BLOCK), qblocks - 1)
    nrev = s + 2 * Q_BLOCK
    nasc = s + 3 * Q_BLOCK
    brev = _t5_bucket(s + Q_BLOCK - 1 - jnp.arange(nrev, dtype=jnp.int32))[None, :]
    basc = _t5_bucket(jnp.arange(nasc, dtype=jnp.int32) - 2 * Q_BLOCK)[None, :]
    tb, asc = pl.pallas_call(
        functools.partial(_bias_table_kernel, s=s, mc=mc),
        grid=(h,),
        in_specs=[pl.BlockSpec((1, nrev), lambda i: (0, 0)),
                  pl.BlockSpec((1, nasc), lambda i: (0, 0)),
                  pl.BlockSpec((1, 1, REL_BUCKETS), lambda i: (i, 0, 0))],
        out_specs=[pl.BlockSpec((1, mc + 1, Q_BLOCK, Q_BLOCK), lambda i: (i, 0, 0, 0)),
                   pl.BlockSpec((1, 1, nasc), lambda i: (i, 0, 0))],
        out_shape=[jax.ShapeDtypeStruct((h, mc + 1, Q_BLOCK, Q_BLOCK), F32),
                   jax.ShapeDtypeStruct((h, 1, nasc), F32)],
        compiler_params=_cparams("arbitrary"),
        name="bias_tables",
    )(brev, basc, rel_bias.T.reshape(h, 1, REL_BUCKETS))
    asc = asc[:, 0, :]
    base = 2 * Q_BLOCK - (CMP_LEN - 1)
    segs = [asc[:, base - CMP_STRIDE * nn:base - CMP_STRIDE * nn + s].reshape(h, qblocks, Q_BLOCK)
            for nn in range(SUBLANES)]
    trc = jnp.flip(jnp.stack(segs, axis=2), axis=1).reshape(h, qblocks * SUBLANES, Q_BLOCK)
    return tb, jnp.pad(trc, ((0, 0), (0, qblocks * SUBLANES), (0, 0)))


def _nsa_cmp_kernel(q_ref, kc_ref, vct_ref, trc_ref, smt_ref, gate_ref, oc_ref, sel_ref, *, qblocks, topn, nb):
    for c in range(nb):
        _nsa_cmp_block(pl.program_id(1) * nb + c, slice(c * Q_BLOCK, (c + 1) * Q_BLOCK), q_ref, kc_ref, vct_ref,
                       trc_ref, smt_ref, gate_ref, oc_ref, sel_ref, qblocks=qblocks, topn=topn)


def _nsa_cmp_block(qb, rows, q_ref, kc_ref, vct_ref, trc_ref, smt_ref, gate_ref, oc_ref, sel_ref, *, qblocks, topn):
    rq = NSA_GROUP * Q_BLOCK
    q = q_ref[:, rows, :].reshape(rq, NSA_HEAD_DIM)
    kc = kc_ref[0]
    nc = kc.shape[0]
    nslc = smt_ref.shape[0]
    s = lax.dot_general(kc, q, _NT, preferred_element_type=F32)
    n_io = lax.broadcasted_iota(jnp.int32, (nc, Q_BLOCK), 0)
    i_io = lax.broadcasted_iota(jnp.int32, (nc, Q_BLOCK), 1)
    mask = (n_io * CMP_STRIDE + (CMP_LEN - 1)) <= (qb * Q_BLOCK + i_io)
    off = pl.multiple_of((qblocks - 1 - qb) * SUBLANES, SUBLANES)
    ps = []
    for r in range(NSA_GROUP):
        l = jnp.where(mask, s[:, r * Q_BLOCK:(r + 1) * Q_BLOCK] + trc_ref[r, pl.ds(off, nc), :], NEG_INF)
        m = jnp.max(l, axis=0, keepdims=True)
        p = jnp.where(mask, jnp.exp2(l - m), 0.0)
        den = jnp.maximum(jnp.sum(p, axis=0, keepdims=True), 1e-30)
        ps.append(p / den)
    p_all = jnp.concatenate(ps, axis=1).astype(BF)
    oc_t = jnp.dot(vct_ref[0], p_all, preferred_element_type=F32)
    imp4 = jnp.dot(smt_ref[...], p_all, preferred_element_type=F32)
    imp = imp4[:, 0:Q_BLOCK]
    for r in range(1, NSA_GROUP):
        imp = imp + imp4[:, r * Q_BLOCK:(r + 1) * Q_BLOCK]

    j_io = lax.broadcasted_iota(jnp.int32, (nslc, Q_BLOCK), 0)
    t_io = qb * Q_BLOCK + lax.broadcasted_iota(jnp.int32, (nslc, Q_BLOCK), 1)
    cur = t_io >> int(math.log2(SLC_LEN))
    forced = jnp.logical_or(j_io == 0, jnp.logical_or(j_io == cur, j_io == cur - 1))
    n_forced = 3
    val = jnp.where(forced, -3e38, jnp.where(j_io <= cur, imp, -1e9))
    sel = jnp.where(forced, 1.0, 0.0)
    for _ in range(max(topn - n_forced, 0)):
        mx = jnp.max(val, axis=0, keepdims=True)
        cand = jnp.where(val == mx, j_io, nslc)
        jmin = jnp.min(cand, axis=0, keepdims=True)
        pick = j_io == jmin
        sel = jnp.where(pick, 1.0, sel)
        val = jnp.where(pick, -3e38, val)
    sel_ref[0, rows, :] = jnp.where(sel.T > 0.5, 0.0, NEG_INF).astype(sel_ref.dtype)

    gates = gate_ref[0, rows, :]
    for r in range(NSA_GROUP):
        o_r = oc_t[:, r * Q_BLOCK:(r + 1) * Q_BLOCK].T
        oc_ref[rows, r * NSA_HEAD_DIM:(r + 1) * NSA_HEAD_DIM] = o_r * gates[:, 3 * r:3 * r + 1]


def _nsa_cmp(q, kc, vct, trc, smt, gates):
    h, s, d = q.shape
    g = NSA_KV_HEADS
    qblocks = s // Q_BLOCK
    nc = kc.shape[1]
    nslc = smt.shape[0]
    topn = min(SLC_TOPN, nslc)
    nb = 2
    qrows = nb * Q_BLOCK
    return pl.pallas_call(
        functools.partial(_nsa_cmp_kernel, qblocks=qblocks, topn=topn, nb=nb),
        grid=(g, qblocks // nb),
        in_specs=[pl.BlockSpec((NSA_GROUP, qrows, d), lambda gg, i: (gg, i, 0)),
                  pl.BlockSpec((1, nc, d), lambda gg, i: (gg, 0, 0)),
                  pl.BlockSpec((1, d, nc), lambda gg, i: (gg, 0, 0)),
                  pl.BlockSpec((NSA_GROUP, trc.shape[1], Q_BLOCK), lambda gg, i: (gg, 0, 0)),
                  pl.BlockSpec(smt.shape, lambda gg, i: (0, 0)),
                  pl.BlockSpec((1, qrows, LANES), lambda gg, i: (gg, i, 0))],
        out_specs=[pl.BlockSpec((qrows, NSA_GROUP * d), lambda gg, i: (i, gg)),
                   pl.BlockSpec((1, qrows, nslc), lambda gg, i: (gg, i, 0))],
        out_shape=[jax.ShapeDtypeStruct((s, h * d), F32),
                   jax.ShapeDtypeStruct((g, s, nslc), BF)],
        compiler_params=_cparams("arbitrary", "arbitrary"),
        name="nsa_cmp",
    )(q, kc, vct, trc, smt, gates)


def _nsa_sw_kernel(q_ref, kv_ref, tb_ref, sel_ref, et_ref, gate_ref, oc_ref,
                   o_ref, s0_ref, s1_ref, m_ref, acc_ref, part_ref, *, tk, mc, nch):
    i = pl.program_id(0)
    ngrp = NSA_KV_HEADS
    rq = NSA_GROUP * Q_BLOCK
    nsub = tk // Q_BLOCK
    s_bufs = (s0_ref, s1_ref)
    qbs = [i * nch + c for c in range(nch)]
    qs = [[q_ref[g * NSA_GROUP:(g + 1) * NSA_GROUP, c * Q_BLOCK:(c + 1) * Q_BLOCK, :].reshape(rq, NSA_HEAD_DIM)
           for c in range(nch)] for g in range(ngrp)]

    def bias_tile(g, r, c, kb0, nblk):
        return jnp.concatenate(
            [tb_ref[g * NSA_GROUP + r, jnp.clip(qbs[c] - (kb0 + b), 0, mc)] for b in range(nblk)], axis=1)

    def gated(g, c, o, branch, base):
        gates = gate_ref[g, c * Q_BLOCK:(c + 1) * Q_BLOCK, :]
        outs = []
        for r in range(NSA_GROUP):
            o_r = o[r * Q_BLOCK:(r + 1) * Q_BLOCK, :] * gates[:, 3 * r + branch:3 * r + branch + 1]
            outs.append(base[:, r * NSA_HEAD_DIM:(r + 1) * NSA_HEAD_DIM] + o_r)
        return jnp.concatenate(outs, axis=1)

    _init_state(m_ref, acc_ref)
    qa = [[jnp.concatenate([qs[g][c], jnp.concatenate(
        [sel_ref[g, c * Q_BLOCK:(c + 1) * Q_BLOCK, :]] * NSA_GROUP, axis=0)], axis=1) for c in range(nch)]
        for g in range(ngrp)]

    def pre(g, j, causal):
        off = pl.multiple_of(j * tk, tk)
        ka = jnp.concatenate([kv_ref[g, pl.ds(off, tk), :], et_ref[pl.ds(off, tk), :]], axis=1)
        for c in range(nch):
            s = lax.dot_general(qa[g][c], ka, _NT, preferred_element_type=F32)
            if causal:
                kk = off + lax.broadcasted_iota(jnp.int32, (Q_BLOCK, tk), 1)
                ii = qbs[c] * Q_BLOCK + lax.broadcasted_iota(jnp.int32, (Q_BLOCK, tk), 0)
                future = jnp.where(kk <= ii, 0.0, NEG_INF)
            for r in range(NSA_GROUP):
                bias = bias_tile(g, r, c, j * nsub, nsub)
                if causal:
                    bias = bias + future
                s_bufs[g][c, r * Q_BLOCK:(r + 1) * Q_BLOCK, :] = s[r * Q_BLOCK:(r + 1) * Q_BLOCK, :] + bias

    def process(g, j):
        vp = _with_ones(kv_ref[ngrp + g, pl.ds(pl.multiple_of(j * tk, tk), tk), :])
        for c in range(nch):
            _flash_update(s_bufs[g].at[c], vp, m_ref.at[g * nch + c], acc_ref.at[g * nch + c])

    n_past = (i * nch) // nsub

    def body(t, carry):
        pre(1, t, False)
        process(0, t)
        pre(0, t + 1, False)
        process(1, t)
        return carry

    pre(0, n_past, True)
    pre(1, n_past, True)
    process(0, n_past)
    pre(0, 0, False)
    process(1, n_past)

    wk = WINDOW + Q_BLOCK
    wblk = wk // Q_BLOCK
    for g in range(ngrp):
        cols = slice(g * rq, (g + 1) * rq)
        for c in range(nch):
            rows = slice(c * Q_BLOCK, (c + 1) * Q_BLOCK)
            kb0 = jnp.maximum(qbs[c] - WINDOW // Q_BLOCK, 0)
            off = pl.multiple_of(kb0 * Q_BLOCK, Q_BLOCK)
            s = lax.dot_general(qs[g][c], kv_ref[2 * ngrp + g, pl.ds(off, wk), :], _NT, preferred_element_type=F32)
            rel = (qbs[c] * Q_BLOCK + lax.broadcasted_iota(jnp.int32, (Q_BLOCK, wk), 0)) - (
                off + lax.broadcasted_iota(jnp.int32, (Q_BLOCK, wk), 1))
            outside = jnp.where(jnp.logical_and(rel >= 0, rel < WINDOW), 0.0, NEG_INF)
            l = jnp.concatenate([s[r * Q_BLOCK:(r + 1) * Q_BLOCK, :] + (bias_tile(g, r, c, kb0, wblk) + outside)
                                 for r in range(NSA_GROUP)], axis=0)
            p = jnp.exp2(l - jnp.max(l, axis=1, keepdims=True)).astype(BF)
            ow = jnp.dot(p, _with_ones(kv_ref[3 * ngrp + g, pl.ds(off, wk), :]), preferred_element_type=F32)
            part_ref[rows, cols] = gated(g, c, _normalized(ow), 2, oc_ref[rows, cols])

    lax.fori_loop(0, n_past, body, 0)

    for g in range(ngrp):
        cols = slice(g * rq, (g + 1) * rq)
        for c in range(nch):
            rows = slice(c * Q_BLOCK, (c + 1) * Q_BLOCK)
            o_ref[rows, cols] = gated(g, c, _normalized(acc_ref[g * nch + c]), 1,
                                      part_ref[rows, cols]).astype(o_ref.dtype)


def _nsa_sw(q, kvsw, tb, sel, et, gates, oc):
    h, s, d = q.shape
    g = NSA_KV_HEADS
    nch = 2
    qrows = nch * Q_BLOCK
    tk = min(512, s)
    mc = tb.shape[1] - 1
    nslc = sel.shape[2]
    rq = NSA_GROUP * Q_BLOCK
    assert tk % qrows == 0 and WINDOW + Q_BLOCK <= s and NSA_GROUP * d == rq

    def resident(arr):
        return pl.BlockSpec(arr.shape, lambda i, nd=arr.ndim: (0,) * nd, pipeline_mode=pl.Buffered(1))

    return pl.pallas_call(
        functools.partial(_nsa_sw_kernel, tk=tk, mc=mc, nch=nch),
        grid=(s // qrows,),
        in_specs=[pl.BlockSpec((h, qrows, d), lambda i: (0, i, 0)),
                  resident(kvsw), resident(tb),
                  pl.BlockSpec((g, qrows, nslc), lambda i: (0, i, 0)),
                  resident(et),
                  pl.BlockSpec((g, qrows, LANES), lambda i: (0, i, 0)),
                  pl.BlockSpec((qrows, h * d), lambda i: (i, 0))],
        out_specs=pl.BlockSpec((qrows, h * d), lambda i: (i, 0)),
        out_shape=jax.ShapeDtypeStruct((s, h * d), BF),
        scratch_shapes=[pltpu.VMEM((nch, rq, tk), F32), pltpu.VMEM((nch, rq, tk), F32),
                        pltpu.VMEM((g * nch, rq, 1), F32), pltpu.VMEM((g * nch, rq, 2 * LANES), F32),
                        pltpu.VMEM((qrows, h * d), F32)],
        compiler_params=_cparams("arbitrary"),
        name="nsa_sel_win",
    )(q, kvsw, tb, sel, et, gates, oc)


def _selection_map_t(nc, nslc):
    n = np.arange(nc)[None, :] * CMP_STRIDE
    j0 = np.arange(nslc)[:, None] * SLC_LEN
    valid = np.arange(nc)[None, :] < nc - 1
    return jnp.asarray(((n < j0 + SLC_LEN) & (n + CMP_LEN > j0) & valid).astype(np.float32), dtype=BF)


def _block_onehot(s, nslc):
    tok = np.arange(s)[:, None]
    j = np.arange(nslc)[None, :]
    return jnp.asarray((tok // SLC_LEN == j).astype(np.float32), dtype=BF)


def _rope_tables(positions):
    half = MLA_ROPE // 2
    inv = ROPE_THETA ** (-jnp.arange(half, dtype=F32) / half)
    ang = positions.astype(F32)[:, None] * inv
    cos, sin = jnp.cos(ang), jnp.sin(ang)
    z = jnp.zeros_like(cos)
    zpad = jnp.zeros((positions.shape[0], LANES - MLA_ROPE), F32)
    c = jnp.concatenate([cos, cos, zpad], axis=1)
    s1 = jnp.concatenate([-sin, z, zpad], axis=1)
    s2 = jnp.concatenate([z, sin, zpad], axis=1)
    return c, s1, s2


def _pad_cols(w, n):
    return jnp.pad(w, ((0, 0), (0, n - w.shape[1])))


def _pad_rows(w, n):
    return jnp.pad(w, ((0, n - w.shape[0]), (0, 0)))


def _nsa_mla_mixer(hn, pos, rel_bias, w_in_all, w_out_all, e, gate_b, pos_k, w1_k, w2_k, pos_v, w1_v, w2_v,
                   q_norm, w_uq, kv_norm, w_ukv, x, g1):
    s = hn.shape[0]
    d = NSA_HEAD_DIM
    nq = NSA_HEADS * d
    nkv = 2 * NSA_KV_HEADS * d
    w_in_t = jnp.swapaxes(w_in_all, 1, 2)
    dm = w_in_t.shape[2]
    o0 = nq + 3 * nkv
    w_g_t = w_in_t[e, o0:o0 + 3 * NSA_HEADS]; o0 += 3 * NSA_HEADS
    w_lat_t = w_in_t[e, o0:o0 + MLA_Q_RANK + MLA_KV_RANK]; o0 += MLA_Q_RANK + MLA_KV_RANK
    w_kr_t = w_in_t[e, o0:o0 + MLA_ROPE]

    q_nsa = _proj([hn], [(w_in_t, dm, (e, 0), 0)], functools.partial(_epi_scale, d ** -0.5 * LOG2E), n=nq,
                  out_dtype=BF, head_major=True, tn=1024, w_transposed=True, name="proj_q_nsa")
    kv_cmp = _proj([hn], [(w_in_t, dm, (e, 0), nq)], _epi_id, n=nkv, out_dtype=F32, head_major=True,
                   w_transposed=True, name="proj_kv_cmp")
    kv_sw = _proj([hn], [(w_in_t, dm, (e, 0), nq + nkv)], _epi_id, n=2 * nkv, out_dtype=BF, head_major=True,
                  w_transposed=True, name="proj_kv_sw")
    per_g = 3 * NSA_GROUP
    w_gp_t = jnp.concatenate([_pad_rows(w_g_t[g * per_g:(g + 1) * per_g], LANES) for g in range(NSA_KV_HEADS)], 0)
    b_gp = jnp.concatenate([_pad_cols(gate_b[None, g * per_g:(g + 1) * per_g], LANES)
                            for g in range(NSA_KV_HEADS)], 1)
    gates = _proj([hn], [(w_gp_t[None], dm, (0, 0), 0)], _epi_sigmoid_bias, [(b_gp, "row")],
                  n=NSA_KV_HEADS * LANES, out_dtype=F32, head_major=True, w_transposed=True,
                  name="proj_gates")

    nc = s // CMP_STRIDE
    gw = CMP_STRIDE * d
    pos_kv = jnp.stack([pos_k.reshape(2, gw), pos_v.reshape(2, gw)])
    w1_kv = jnp.stack([w1_k.reshape(2, gw, CMP_HIDDEN), w1_v.reshape(2, gw, CMP_HIDDEN)]).astype(BF)
    w2_kv = jnp.stack([w2_k, w2_v]).astype(BF)
    kvc = _nsa_compress(kv_cmp.reshape(2 * NSA_KV_HEADS, nc, gw), pos_kv, w1_kv, w2_kv)
    kc = kvc[:NSA_KV_HEADS]
    vct = jnp.swapaxes(kvc[NSA_KV_HEADS:], 1, 2)

    nslc = s // SLC_LEN
    tb, trc = _bias_tables(rel_bias, s)
    smt = _selection_map_t(nc, nslc)
    oc, sel = _nsa_cmp(q_nsa, kc, vct, trc, smt, gates)
    o_nsa = _nsa_sw(q_nsa, kv_sw, tb, sel, _block_onehot(s, nslc), gates, oc)

    nlat = MLA_Q_RANK + MLA_KV_RANK
    lat = _proj([hn], [(w_lat_t[None], dm, (0, 0), 0)], _epi_rmsnorm,
                [(jnp.concatenate([q_norm, kv_norm])[None, :], "row")], n=nlat, out_dtype=BF, tn=MLA_Q_RANK,
                w_transposed=True, name="proj_mla_latent")
    c, s1, s2 = _rope_tables(pos)
    kr = _proj([hn], [(_pad_rows(w_kr_t, LANES)[None], dm, (0, 0), 0)], functools.partial(_epi_rope, 1.0),
               [(c, "rowtile"), (s1, "rowtile"), (s2, "rowtile")], n=LANES, out_dtype=BF, w_transposed=True,
               name="proj_mla_kr")
    scale = (MLA_NOPE + MLA_ROPE) ** -0.5 * LOG2E
    w_uq3 = w_uq.reshape(MLA_Q_RANK, MLA_HEADS, MLA_NOPE + MLA_ROPE)
    w_uqn = w_uq3[:, :, :MLA_NOPE].reshape(MLA_Q_RANK, MLA_HEADS * MLA_NOPE)
    w_uqr = jnp.pad(w_uq3[:, :, MLA_NOPE:], ((0, 0), (0, 0), (0, LANES - MLA_ROPE))).reshape(
        MLA_Q_RANK, MLA_HEADS * LANES)
    qn = _proj([lat], [(w_uqn, MLA_Q_RANK, 0, 0)], functools.partial(_epi_scale, scale),
               n=MLA_HEADS * MLA_NOPE, out_dtype=BF, head_major=True, tn=1024, lhs_col_block=[0],
               name="proj_mla_qn")
    qr = _proj([lat], [(w_uqr, MLA_Q_RANK, 0, 0)], functools.partial(_epi_rope, scale),
               [(c, "rowtile"), (s1, "rowtile"), (s2, "rowtile")], n=MLA_HEADS * LANES, out_dtype=BF,
               head_major=True, lhs_col_block=[0], name="proj_mla_qr")
    kvh = _proj([lat], [(w_ukv, MLA_KV_RANK, 0, 0)], _epi_id, n=MLA_HEADS * (MLA_NOPE + MLA_V), out_dtype=BF,
                head_major=True, tn=1024, lhs_col_block=[1], name="proj_mla_kv")
    o_mla = _mla_attention(qn, qr, kvh, kr)

    return _proj([o_nsa, o_mla], [(w_out_all, nq, (e, 0), 0), (w_out_all, MLA_HEADS * MLA_V, (e, 1), 0)],
                 _epi_residual, [(x, "tile"), (g1, "row")], n=w_out_all.shape[2], out_dtype=F32, tn=1024,
                 name="proj_even_out")


def _fox_mixer(hn, w_in_all, w_out_all, o, f_b, q_norm, k_norm, x, g1):
    d = D_MODEL
    dh = FOX_HEAD_DIM
    w_in_t = jnp.swapaxes(w_in_all, 1, 2)
    w_f_t = w_in_t[o, 3 * d:3 * d + FOX_HEADS]
    w_og_t = w_in_t[o, 3 * d + FOX_HEADS:]
    q = _proj([hn], [(w_in_t, d, (o, 0), 0)], functools.partial(_epi_headnorm, dh ** -0.5 * LOG2E),
              [(q_norm[None, :], "const")], n=d, out_dtype=BF, head_major=True, tn=1024, w_transposed=True,
              name="proj_fox_q")
    k = _proj([hn], [(w_in_t, d, (o, 0), d)], functools.partial(_epi_headnorm, 1.0),
              [(k_norm[None, :], "const")], n=d, out_dtype=BF, head_major=True, tn=1024, w_transposed=True,
              name="proj_fox_k")
    v = _proj([hn], [(w_in_t, d, (o, 0), 2 * d)], _epi_id, n=d, out_dtype=BF, head_major=True, tn=1024,
              w_transposed=True, name="proj_fox_v")
    lf = _proj([hn], [(_pad_rows(w_f_t, LANES)[None], d, (0, 0), 0)], _epi_logsigmoid_bias,
               [(_pad_cols(f_b[None, :], LANES), "row")], n=LANES, out_dtype=F32, w_transposed=True,
               name="proj_fox_f")
    sig_og = _proj([hn], [(w_og_t[None], d, (0, 0), 0)], _epi_sigmoid, n=d, out_dtype=F32, tn=1024,
                   w_transposed=True, name="proj_fox_og")
    cum, cum_t = _cumsum_tokens(lf, LOG2E)
    cum_t = cum_t[:FOX_HEADS].reshape(FOX_HEADS, 1, -1)
    att = _fox_attention(q, k, v, cum_t, cum, sig_og)
    return _proj([att], [(w_out_all, d, (o, 0), 0)], _epi_residual, [(x, "tile"), (g1, "row")], n=d,
                 out_dtype=F32, tn=1024, name="proj_fox_out")


def kernel(x, c, positions, rel_bias, ada_w, ada_b, norm_mix, norm_ffn, ffn_w1, ffn_w3, ffn_w2, even_w_in, even_w_out, nsa_gate_b, nsa_cmp_pos_k, nsa_cmp_w1_k, nsa_cmp_w2_k, nsa_cmp_pos_v, nsa_cmp_w1_v, nsa_cmp_w2_v, mla_q_norm, mla_w_uq, mla_kv_norm, mla_w_ukv, fox_w_in, fox_w_out, fox_f_b, fox_q_norm, fox_k_norm, final_norm):
    b, s, d = x.shape
    assert b == 1 and d == D_MODEL and s % 1024 == 0
    xs = x[0]
    pos = positions[0]
    mod = _adaln(c, ada_w, ada_b)
    depth = ada_w.shape[0]
    hn = None
    for i in range(depth):
        sh1, sc1, g1, sh2, sc2, g2 = [mod[i:i + 1, k * d:(k + 1) * d] for k in range(6)]
        if hn is None:
            hn = _normmod(xs, norm_mix[i][None, :], sc1, sh1)
        if i % 2 == 0:
            e = i // 2
            xs = _nsa_mla_mixer(hn, pos, rel_bias, even_w_in, even_w_out, e, nsa_gate_b[e],
                                nsa_cmp_pos_k[e], nsa_cmp_w1_k[e], nsa_cmp_w2_k[e],
                                nsa_cmp_pos_v[e], nsa_cmp_w1_v[e], nsa_cmp_w2_v[e],
                                mla_q_norm[e], mla_w_uq[e], mla_kv_norm[e], mla_w_ukv[e], xs, g1)
        else:
            o = i // 2
            xs = _fox_mixer(hn, fox_w_in, fox_w_out, o, fox_f_b[o], fox_q_norm[o], fox_k_norm[o], xs, g1)
        if i == depth - 1:
            xs = _ffn(xs, norm_ffn[i][None, :], sc2, sh2, g2, ffn_w1, ffn_w3, ffn_w2, i,
                      final_gain=final_norm[None, :])
        else:
            nxt = (norm_mix[i + 1][None, :], mod[i + 1:i + 2, d:2 * d], mod[i + 1:i + 2, 0:d])
            xs, hn = _ffn(xs, norm_ffn[i][None, :], sc2, sh2, g2, ffn_w1, ffn_w3, ffn_w2, i, next_mod=nxt)
    return xs[None]
```

```python
import functools
import math

import numpy as np
import jax
import jax.numpy as jnp
from jax import lax
from jax.experimental import pallas as pl
from jax.experimental.pallas import tpu as pltpu

D_MODEL = 2048
DEPTH = 2
EPS = 1e-6
NEG_INF = -1e30

NSA_HEADS = 8
NSA_KV_HEADS = 2
NSA_GROUP = NSA_HEADS // NSA_KV_HEADS
NSA_HEAD_DIM = 128
CMP_LEN = 32
CMP_STRIDE = 16
CMP_HIDDEN = 256
SLC_LEN = 64
SLC_TOPN = 16
WINDOW = 512
Q_BLOCK = 128

MLA_HEADS = 8
MLA_Q_RANK = 512
MLA_KV_RANK = 512
MLA_NOPE = 128
MLA_ROPE = 64
MLA_V = 128
ROPE_THETA = 10000.0

FOX_HEADS = 16
FOX_HEAD_DIM = D_MODEL // FOX_HEADS

REL_BUCKETS = 32
REL_MAX_DIST = 4096

FFN_HIDDEN = ((8 * D_MODEL + 2) // 3 + 255) // 256 * 256

LANES = 128
SUBLANES = 8
VMEM_LIMIT_BYTES = 56 * 1024 * 1024
LOG2E = math.log2(math.e)

BF = jnp.bfloat16
F32 = jnp.float32
_NT = (((1,), (1,)), ((), ()))


def _cparams(*sem):
    return pltpu.CompilerParams(dimension_semantics=sem, vmem_limit_bytes=VMEM_LIMIT_BYTES)


def _rms(x):
    return x * lax.rsqrt(jnp.mean(x * x, axis=-1, keepdims=True) + EPS)


def _adaln_kernel(c_ref, w_ref, b_ref, o_ref):
    c = c_ref[...]
    cond = c * jax.nn.sigmoid(c)
    acc = jnp.dot(cond.astype(BF), w_ref[0].astype(BF), preferred_element_type=F32)
    o_ref[0] = acc + b_ref[0]


def _adaln(c, ada_w, ada_b):
    depth, d, n = ada_w.shape
    tn = 1024
    c8 = jnp.broadcast_to(c, (SUBLANES, d))
    out = pl.pallas_call(
        _adaln_kernel,
        grid=(depth, n // tn),
        in_specs=[pl.BlockSpec((SUBLANES, d), lambda i, j: (0, 0)),
                  pl.BlockSpec((1, d, tn), lambda i, j: (i, 0, j)),
                  pl.BlockSpec((1, 1, tn), lambda i, j: (i, 0, j))],
        out_specs=pl.BlockSpec((1, SUBLANES, tn), lambda i, j: (i, 0, j)),
        out_shape=jax.ShapeDtypeStruct((depth, SUBLANES, n), F32),
        compiler_params=_cparams("arbitrary", "arbitrary"),
        name="adaln",
    )(c8, ada_w, ada_b.reshape(depth, 1, n))
    return out[:, 0, :]


def _normmod_kernel(x_ref, g_ref, sc_ref, sh_ref, o_ref):
    y = _rms(x_ref[...])
    o_ref[...] = ((y * g_ref[...]) * (1.0 + sc_ref[...]) + sh_ref[...]).astype(o_ref.dtype)


def _normmod(x, g, sc, sh):
    s, d = x.shape
    tm = 512
    row = pl.BlockSpec((1, d), lambda i: (0, 0))
    return pl.pallas_call(
        _normmod_kernel,
        grid=(s // tm,),
        in_specs=[pl.BlockSpec((tm, d), lambda i: (i, 0)), row, row, row],
        out_specs=pl.BlockSpec((tm, d), lambda i: (i, 0)),
        out_shape=jax.ShapeDtypeStruct((s, d), BF),
        compiler_params=_cparams("arbitrary"),
        name="normmod",
    )(x, g, sc, sh)


def _proj_kernel(*refs, n_lhs, n_epi, epi, head_major, w_transposed):
    lhs = refs[:n_lhs]
    ws = refs[n_lhs:2 * n_lhs]
    epis = refs[2 * n_lhs:2 * n_lhs + n_epi]
    o_ref = refs[2 * n_lhs + n_epi]
    wbf = refs[2 * n_lhs + n_epi + 1:]

    @pl.when(pl.program_id(1) == 0)
    def _():
        for w, wb in zip(ws, wbf):
            wb[...] = w[...].astype(BF)

    acc = None
    for a, wb in zip(lhs, wbf):
        if w_transposed:
            d = lax.dot_general(a[...], wb[...], _NT, preferred_element_type=F32)
        else:
            d = jnp.dot(a[...], wb[...], preferred_element_type=F32)
        acc = d if acc is None else acc + d
    res = epi(acc, *[e[...] for e in epis])
    if head_major:
        for r in range(o_ref.shape[0]):
            o_ref[r] = res[:, r * LANES:(r + 1) * LANES].astype(o_ref.dtype)
    else:
        o_ref[...] = res.astype(o_ref.dtype)


def _proj(lhs, ws, epi, epi_in=(), *, n, out_dtype, head_major=False, tm=1024, tn=512,
          lhs_col_block=None, w_transposed=False, name="proj"):
    m = lhs[0].shape[0]
    tm = min(tm, m)
    tn = min(tn, n)
    if lhs_col_block is None:
        lhs_col_block = [0] * len(lhs)
    in_specs = []
    for (_, k, _, _), cb in zip(ws, lhs_col_block):
        in_specs.append(pl.BlockSpec((tm, k), lambda j, i, cb=cb: (i, cb)))
    for arr, k, rb, col0 in ws:
        assert col0 % tn == 0
        if w_transposed:
            in_specs.append(pl.BlockSpec((None, tn, k),
                                         lambda j, i, rb=rb, cb0=col0 // tn: (rb[0], cb0 + j, rb[1])))
        elif arr.ndim == 3:
            in_specs.append(pl.BlockSpec((None, k, tn),
                                         lambda j, i, rb=rb, cb0=col0 // tn: (rb[0], rb[1], cb0 + j)))
        else:
            in_specs.append(pl.BlockSpec((k, tn), lambda j, i, rb=rb, cb0=col0 // tn: (rb, cb0 + j)))
    arrays = list(lhs) + [w[0] for w in ws]
    for arr, kind in epi_in:
        if kind == "row":
            in_specs.append(pl.BlockSpec((1, tn), lambda j, i: (0, j)))
        elif kind == "const":
            in_specs.append(pl.BlockSpec(arr.shape, lambda j, i: (0, 0)))
        elif kind == "tile":
            in_specs.append(pl.BlockSpec((tm, tn), lambda j, i: (i, j)))
        elif kind == "rowtile":
            in_specs.append(pl.BlockSpec((tm, arr.shape[1]), lambda j, i: (i, 0)))
        else:
            raise ValueError(kind)
        arrays.append(arr)
    if head_major:
        hpt = tn // LANES
        out_spec = pl.BlockSpec((hpt, tm, LANES), lambda j, i: (j, i, 0))
        out_shape = jax.ShapeDtypeStruct((n // LANES, m, LANES), out_dtype)
    else:
        out_spec = pl.BlockSpec((tm, tn), lambda j, i: (i, j))
        out_shape = jax.ShapeDtypeStruct((m, n), out_dtype)
    kern = functools.partial(_proj_kernel, n_lhs=len(lhs), n_epi=len(epi_in), epi=epi,
                             head_major=head_major, w_transposed=w_transposed)
    return pl.pallas_call(
        kern,
        grid=(n // tn, m // tm),
        in_specs=in_specs,
        out_specs=out_spec,
        out_shape=out_shape,
        scratch_shapes=[pltpu.VMEM((tn, k) if w_transposed else (k, tn), BF) for _, k, _, _ in ws],
        compiler_params=_cparams("arbitrary", "arbitrary"),
        name=name,
    )(*arrays)


def _epi_id(acc):
    return acc


def _epi_scale(scale, acc):
    return acc * scale


def _epi_sigmoid_bias(acc, b):
    return jax.nn.sigmoid(acc + b)


def _epi_logsigmoid_bias(acc, b):
    return jax.nn.log_sigmoid(acc + b)


def _epi_sigmoid(acc):
    return jax.nn.sigmoid(acc)


def _epi_rmsnorm(acc, g):
    return _rms(acc) * g


def _epi_headnorm(scale, acc, g):
    outs = []
    for r in range(acc.shape[1] // LANES):
        outs.append(_rms(acc[:, r * LANES:(r + 1) * LANES]) * g * scale)
    return jnp.concatenate(outs, axis=1)


def _epi_rope(scale, acc, c, s1, s2):
    reps = acc.shape[1] // LANES
    half = MLA_ROPE // 2
    if reps > 1:
        c = jnp.concatenate([c] * reps, axis=1)
        s1 = jnp.concatenate([s1] * reps, axis=1)
        s2 = jnp.concatenate([s2] * reps, axis=1)
    n = acc.shape[1]
    out = acc * c + pltpu.roll(acc, n - half, 1) * s1 + pltpu.roll(acc, half, 1) * s2
    return out * scale


def _epi_residual(acc, x, g):
    return x + g * acc


def _ffn_kernel(*refs, mode):
    x_ref, g_ref, sc_ref, sh_ref, g2_ref, w1_ref, w3_ref, w2_ref = refs[:8]
    if mode == "final":
        fn_ref, o_ref, hn_ref = refs[8:]
    elif mode == "next":
        gn_ref, scn_ref, shn_ref, o_ref, hnext_ref, hn_ref = refs[8:]
    else:
        o_ref, hn_ref = refs[8:]
    f = pl.program_id(1)
    tm = x_ref.shape[0]
    halves = [slice(r * (tm // 2), (r + 1) * (tm // 2)) for r in range(2)]

    @pl.when(f == 0)
    def _():
        for rows in halves:
            x = x_ref[rows, :]
            hn_ref[rows, :] = ((_rms(x) * g_ref[...]) * (1.0 + sc_ref[...]) + sh_ref[...]).astype(BF)
            o_ref[rows, :] = x

    w1 = w1_ref[...].astype(BF)
    w3 = w3_ref[...].astype(BF)
    w2 = w2_ref[...].astype(BF)
    for rows in halves:
        h = hn_ref[rows, :]
        h1 = jnp.dot(h, w1, preferred_element_type=F32)
        h3 = jnp.dot(h, w3, preferred_element_type=F32)
        a = (h1 * jax.nn.sigmoid(h1)) * h3
        o_ref[rows, :] += g2_ref[...] * jnp.dot(a.astype(BF), w2, preferred_element_type=F32)

    if mode != "plain":
        @pl.when(f == pl.num_programs(1) - 1)
        def _():
            for rows in halves:
                y = _rms(o_ref[rows, :])
                if mode == "final":
                    o_ref[rows, :] = y * fn_ref[...]
                else:
                    hnext_ref[rows, :] = ((y * gn_ref[...]) * (1.0 + scn_ref[...]) + shn_ref[...]).astype(BF)


def _ffn(x, g, sc, sh, g2, w1, w3, w2, layer, final_gain=None, next_mod=None):
    assert final_gain is None or next_mod is None
    s, d = x.shape
    fdim = w1.shape[2]
    tm = min(1024, s)
    tf = 256
    row = pl.BlockSpec((1, d), lambda i, f: (0, 0))
    tile = pl.BlockSpec((tm, d), lambda i, f: (i, 0))
    in_specs = [pl.BlockSpec((tm, d), lambda i, f: (i, 0), pipeline_mode=pl.Buffered(1)), row, row, row, row,
                pl.BlockSpec((None, d, tf), lambda i, f: (layer, 0, f)),
                pl.BlockSpec((None, d, tf), lambda i, f: (layer, 0, f)),
                pl.BlockSpec((None, tf, d), lambda i, f: (layer, f, 0))]
    arrays = [x, g, sc, sh, g2, w1, w3, w2]
    out_specs, out_shape, mode = tile, jax.ShapeDtypeStruct((s, d), F32), "plain"
    if final_gain is not None:
        in_specs.append(row)
        arrays.append(final_gain)
        mode = "final"
    elif next_mod is not None:
        in_specs += [row, row, row]
        arrays += list(next_mod)
        out_specs = [tile, tile]
        out_shape = [out_shape, jax.ShapeDtypeStruct((s, d), BF)]
        mode = "next"
    return pl.pallas_call(
        functools.partial(_ffn_kernel, mode=mode),
        grid=(s // tm, fdim // tf),
        in_specs=in_specs,
        out_specs=out_specs,
        out_shape=out_shape,
        scratch_shapes=[pltpu.VMEM((tm, d), BF)],
        compiler_params=_cparams("arbitrary", "arbitrary"),
        name="ffn",
    )(*arrays)


def _flash_update(s, vp, m_ref, acc_ref, row_shift=None):
    m_prev = m_ref[...]
    m_tile = jnp.max(s[...], axis=1, keepdims=True)
    if row_shift is not None:
        m_tile = m_tile + row_shift
    m_new = jnp.maximum(m_prev, m_tile)
    alpha = jnp.exp2(m_prev - m_new)
    sub = m_new if row_shift is None else m_new - row_shift
    p = jnp.exp2(s[...] - sub).astype(BF)
    acc_ref[...] = alpha * acc_ref[...] + jnp.dot(p, vp, preferred_element_type=F32)
    m_ref[...] = m_new


def _init_state(m_ref, acc_ref):
    m_ref[...] = jnp.full_like(m_ref, NEG_INF)
    acc_ref[...] = jnp.zeros_like(acc_ref)


def _with_ones(v):
    return jnp.concatenate([v, jnp.ones(v.shape, v.dtype)], axis=1)


def _normalized(acc):
    return acc[:, :LANES] / acc[:, LANES:]


def _causal_mask(t):
    rows = lax.broadcasted_iota(jnp.int32, (t, t), 0)
    cols = lax.broadcasted_iota(jnp.int32, (t, t), 1)
    return cols <= rows


def _causal_sweep(qi, tc, s_bufs, m_ref, acc_ref, streams, nsplit):
    tw = 2 * tc

    doff = pl.multiple_of(qi * tw, tw)

    psplit = 2
    rs = tc // psplit
    pieces = [(c, slice(hh * rs, (hh + 1) * rs)) for c in range(2) for hh in range(psplit)]
    dwidth = (tc, tw)

    def pre_diag(si, ksides, c, rows):
        _, logits_fn, _, _ = streams[si]
        r_io = c * tc + rows.start + lax.broadcasted_iota(jnp.int32, (rs, dwidth[c]), 0)
        c_io = lax.broadcasted_iota(jnp.int32, (rs, dwidth[c]), 1)
        s_bufs[si][c, rows, 0:dwidth[c]] = jnp.where(c_io <= r_io, logits_fn(c, ksides[c], rows), NEG_INF)

    def process_diag(si, vps, c, rows):
        shifts = streams[si][3]
        _flash_update(s_bufs[si].at[c, rows, 0:dwidth[c]], vps[c], m_ref.at[2 * si + c, rows, :],
                      acc_ref.at[2 * si + c, rows, :], row_shift=None if shifts[c] is None else shifts[c][rows, :])

    dks = [[streams[si][0](doff, w) for w in dwidth] for si in range(2)]
    dvs = [[streams[si][2](doff, w) for w in dwidth] for si in range(2)]
    for c, rows in pieces:
        pre_diag(0, dks[0], c, rows)
    for c, rows in pieces:
        pre_diag(1, dks[1], c, rows)
        process_diag(0, dvs[0], c, rows)
    kside0 = streams[0][0](0, tw)
    for c, rows in pieces:
        s_bufs[0][c, rows, :] = streams[0][1](c, kside0, rows)
        process_diag(1, dvs[1], c, rows)

    def half(sp, tp, sq, tq_):
        kside_fn, logits_fn, _, _ = streams[sp]
        _, _, vp_fn, shifts = streams[sq]
        kside = kside_fn(pl.multiple_of(tp * tw, tw), tw)
        vp = vp_fn(pl.multiple_of(tq_ * tw, tw), tw)
        rs = tc // nsplit
        for c in range(2):
            for hh in range(nsplit):
                rows = slice(hh * rs, (hh + 1) * rs)
                s_bufs[sp][c, rows, :] = logits_fn(c, kside, rows)
                _flash_update(s_bufs[sq].at[c, rows, :], vp, m_ref.at[2 * sq + c, rows, :],
                              acc_ref.at[2 * sq + c, rows, :],
                              row_shift=None if shifts[c] is None else shifts[c][rows, :])

    def body(t, carry):
        half(1, t, 0, t)
        half(0, t + 1, 1, t)
        return carry

    lax.fori_loop(0, qi, body, 0)


def _sweep_scratch(tc):
    return [pltpu.VMEM((2, tc, 2 * tc), F32), pltpu.VMEM((2, tc, 2 * tc), F32),
            pltpu.VMEM((4, tc, 1), F32), pltpu.VMEM((4, tc, 2 * LANES), F32)]


def _mla_kernel(qn_ref, qr_ref, kv_ref, kr_ref, o_ref, s0_ref, s1_ref, m_ref, acc_ref, *, tc):
    qi = pl.program_id(1)
    _init_state(m_ref, acc_ref)

    def stream(si):
        qs = [jnp.concatenate([qn_ref[si, c * tc:(c + 1) * tc, :], qr_ref[si, c * tc:(c + 1) * tc, :]], axis=1)
              for c in range(2)]

        def kside(off, width):
            return jnp.concatenate([kv_ref[2 * si, pl.ds(off, width), :], kr_ref[pl.ds(off, width), :]], axis=1)

        def logits(c, k, rows=slice(None)):
            return lax.dot_general(qs[c][rows, :], k, _NT, preferred_element_type=F32)

        def vp(off, width):
            return _with_ones(kv_ref[2 * si + 1, pl.ds(off, width), :])

        return kside, logits, vp, (None, None)

    _causal_sweep(qi, tc, (s0_ref, s1_ref), m_ref, acc_ref, [stream(0), stream(1)], nsplit=2)
    for si in range(2):
        for c in range(2):
            o_ref[c * tc:(c + 1) * tc, si * LANES:(si + 1) * LANES] = _normalized(
                acc_ref[2 * si + c]).astype(o_ref.dtype)


def _mla_attention(qn, qr, kvh, kr):
    h, s, _ = qn.shape
    tc = min(512, s // 2)
    w = 2 * tc
    return pl.pallas_call(
        functools.partial(_mla_kernel, tc=tc),
        grid=(h // 2, s // w),
        in_specs=[pl.BlockSpec((2, w, LANES), lambda p, i: (p, i, 0)),
                  pl.BlockSpec((2, w, LANES), lambda p, i: (p, i, 0)),
                  pl.BlockSpec((4, s, LANES), lambda p, i: (p, 0, 0)),
                  pl.BlockSpec((s, LANES), lambda p, i: (0, 0))],
        out_specs=pl.BlockSpec((w, 2 * LANES), lambda p, i: (i, p)),
        out_shape=jax.ShapeDtypeStruct((s, h * LANES), BF),
        scratch_shapes=_sweep_scratch(tc),
        compiler_params=_cparams("arbitrary", "arbitrary"),
        name="mla_attn",
    )(qn, qr, kvh, kr)


def _fox_kernel(q_ref, k_ref, v_ref, ck_ref, cum_ref, og_ref, o_ref, s0_ref, s1_ref, m_ref, acc_ref, *, tc):
    pair = pl.program_id(0)
    qi = pl.program_id(1)
    lane = lax.broadcasted_iota(jnp.int32, (tc, LANES), 1)
    _init_state(m_ref, acc_ref)

    def stream(si):
        qs = [q_ref[si, c * tc:(c + 1) * tc, :] for c in range(2)]
        cqs = [jnp.sum(jnp.where(lane == 2 * pair + si, cum_ref[c * tc:(c + 1) * tc, :], 0.0), axis=1,
                       keepdims=True) for c in range(2)]

        def kside(off, width):
            return k_ref[si, pl.ds(off, width), :], ck_ref[si, :, pl.ds(off, width)]

        def logits(c, kc, rows=slice(None)):
            return lax.dot_general(qs[c][rows, :], kc[0], _NT, preferred_element_type=F32) - kc[1]

        def vp(off, width):
            return _with_ones(v_ref[si, pl.ds(off, width), :])

        return kside, logits, vp, cqs

    _causal_sweep(qi, tc, (s0_ref, s1_ref), m_ref, acc_ref, [stream(0), stream(1)], nsplit=4)
    for si in range(2):
        cols = slice(si * LANES, (si + 1) * LANES)
        for c in range(2):
            rows = slice(c * tc, (c + 1) * tc)
            o_ref[rows, cols] = (_normalized(acc_ref[2 * si + c]) * og_ref[rows, cols]).astype(o_ref.dtype)


def _fox_attention(q, k, v, cum_t, cum, sig_og):
    h, s, _ = q.shape
    tc = min(512, s // 2)
    w = 2 * tc
    hm = pl.BlockSpec((2, s, LANES), lambda p, i: (p, 0, 0))
    return pl.pallas_call(
        functools.partial(_fox_kernel, tc=tc),
        grid=(h // 2, s // w),
        in_specs=[pl.BlockSpec((2, w, LANES), lambda p, i: (p, i, 0)), hm, hm,
                  pl.BlockSpec((2, 1, s), lambda p, i: (p, 0, 0)),
                  pl.BlockSpec((w, LANES), lambda p, i: (i, 0)),
                  pl.BlockSpec((w, 2 * LANES), lambda p, i: (i, p))],
        out_specs=pl.BlockSpec((w, 2 * LANES), lambda p, i: (i, p)),
        out_shape=jax.ShapeDtypeStruct((s, h * LANES), BF),
        scratch_shapes=_sweep_scratch(tc),
        compiler_params=_cparams("arbitrary", "arbitrary"),
        name="fox_attn",
    )(q, k, v, cum_t, cum, sig_og)


def _cumsum_kernel(x_ref, o_ref, ot_ref, carry_ref, *, t, out_scale):
    @pl.when(pl.program_id(0) == 0)
    def _():
        carry_ref[...] = jnp.zeros_like(carry_ref)

    x = x_ref[...]
    rows = lax.broadcasted_iota(jnp.int32, (t, t), 0)
    cols = lax.broadcasted_iota(jnp.int32, (t, t), 1)
    tri = jnp.where(cols <= rows, 1.0, 0.0).astype(BF)
    hi = x.astype(BF)
    r1 = x - hi.astype(F32)
    mid = r1.astype(BF)
    lo = (r1 - mid.astype(F32)).astype(BF)
    cum = (jnp.dot(tri, hi, preferred_element_type=F32) + jnp.dot(tri, mid, preferred_element_type=F32)
           + jnp.dot(tri, lo, preferred_element_type=F32)) + carry_ref[...]
    scaled = cum * out_scale
    o_ref[...] = scaled
    ot_ref[...] = scaled.T
    carry_ref[...] = cum[t - 1:t, :]


def _cumsum_tokens(x, out_scale):
    s, n = x.shape
    t = min(256, s)
    return pl.pallas_call(
        functools.partial(_cumsum_kernel, t=t, out_scale=out_scale),
        grid=(s // t,),
        in_specs=[pl.BlockSpec((t, n), lambda i: (i, 0))],
        out_specs=[pl.BlockSpec((t, n), lambda i: (i, 0)), pl.BlockSpec((n, t), lambda i: (0, i))],
        out_shape=[jax.ShapeDtypeStruct((s, n), F32), jax.ShapeDtypeStruct((n, s), F32)],
        scratch_shapes=[pltpu.VMEM((1, n), F32)],
        compiler_params=_cparams("arbitrary"),
        name="cumsum",
    )(x)


def _compress_kernel(a_ref, pos_ref, w1_ref, w2_ref, o_ref):
    a = a_ref[0]
    nc = a.shape[0]
    p1 = jnp.dot((a + pos_ref[0, 0:1, :]).astype(BF), w1_ref[0, 0], preferred_element_type=F32)
    p2 = jnp.dot((a + pos_ref[0, 1:2, :]).astype(BF), w1_ref[0, 1], preferred_element_type=F32)
    h = p1 + pltpu.roll(p2, nc - 1, 0)
    act = h * jax.nn.sigmoid(h)
    o_ref[0] = jnp.dot(act.astype(BF), w2_ref[0], preferred_element_type=F32).astype(o_ref.dtype)


def _nsa_compress(kv_cmp, pos, w1, w2):
    c, nc, gw = kv_cmp.shape
    return pl.pallas_call(
        _compress_kernel,
        grid=(c,),
        in_specs=[pl.BlockSpec((1, nc, gw), lambda i: (i, 0, 0)),
                  pl.BlockSpec((1, 2, gw), lambda i: (i // 2, 0, 0)),
                  pl.BlockSpec((1, 2, gw, CMP_HIDDEN), lambda i: (i // 2, 0, 0, 0)),
                  pl.BlockSpec((1, CMP_HIDDEN, NSA_HEAD_DIM), lambda i: (i // 2, 0, 0))],
        out_specs=pl.BlockSpec((1, nc, NSA_HEAD_DIM), lambda i: (i, 0, 0)),
        out_shape=jax.ShapeDtypeStruct((c, nc, NSA_HEAD_DIM), BF),
        compiler_params=_cparams("arbitrary"),
        name="nsa_compress",
    )(kv_cmp, pos, w1, w2)


def _bias_table_kernel(brev_ref, basc_ref, tbl_ref, tb_ref, asc_ref, *, s, mc):
    tbl = tbl_ref[0] * LOG2E

    def lookup(bkt):
        out = jnp.zeros(bkt.shape, F32)
        for b in range(REL_BUCKETS):
            out = jnp.where(bkt == b, tbl[:, b:b + 1], out)
        return out

    rev = lookup(brev_ref[...])
    asc_ref[0] = lookup(basc_ref[...])
    for m in range(mc + 1):
        win = rev[:, s - Q_BLOCK * m:s - Q_BLOCK * m + 2 * Q_BLOCK]
        rolled = pltpu.roll(jnp.broadcast_to(win, (Q_BLOCK, 2 * Q_BLOCK)), Q_BLOCK + 1, 1,
                            stride=1, stride_axis=0)
        tb_ref[0, m] = rolled[:, :Q_BLOCK]


def _t5_bucket(dist):
    max_exact = REL_BUCKETS // 2
    d = jnp.maximum(dist, 0)
    ratio = jnp.log(jnp.maximum(d, max_exact).astype(F32) / max_exact) / math.log(REL_MAX_DIST / max_exact)
    large = jnp.minimum(max_exact + (ratio * (REL_BUCKETS - max_exact)).astype(jnp.int32), REL_BUCKETS - 1)
    return jnp.where(d < max_exact, d, large)


def _bias_tables(rel_bias, s):
    assert Q_BLOCK == SUBLANES * CMP_STRIDE
    h = rel_bias.shape[1]
    qblocks = s // Q_BLOCK
    max_exact = REL_BUCKETS // 2
    d_const = int(math.ceil(max_exact * (REL_MAX_DIST / max_exact)
                            ** ((REL_BUCKETS - max_exact - 1) / (REL_BUCKETS - max_exact)))) + 1
    mc = min(-(-(d_const + Q_BLOCK - 1) // Q_BLOCK), qblocks - 1)
    nrev = s + 2 * Q_BLOCK
    nasc = s + 3 * Q_BLOCK
    brev = _t5_bucket(s + Q_BLOCK - 1 - jnp.arange(nrev, dtype=jnp.int32))[None, :]
    basc = _t5_bucket(jnp.arange(nasc, dtype=jnp.int32) - 2 * Q_BLOCK)[None, :]
    tb, asc = pl.pallas_call(
        functools.partial(_bias_table_kernel, s=s, mc=mc),
        grid=(h,),
        in_specs=[pl.BlockSpec((1, nrev), lambda i: (0, 0)),
                  pl.BlockSpec((1, nasc), lambda i: (0, 0)),
                  pl.BlockSpec((1, 1, REL_BUCKETS), lambda i: (i, 0, 0))],
        out_specs=[pl.BlockSpec((1, mc + 1, Q_BLOCK, Q_BLOCK), lambda i: (i, 0, 0, 0)),
                   pl.BlockSpec((1, 1, nasc), lambda i: (i, 0, 0))],
        out_shape=[jax.ShapeDtypeStruct((h, mc + 1, Q_BLOCK, Q_BLOCK), F32),
                   jax.ShapeDtypeStruct((h, 1, nasc), F32)],
        compiler_params=_cparams("arbitrary"),
        name="bias_tables",
    )(brev, basc, rel_bias.T.reshape(h, 1, REL_BUCKETS))
    asc = asc[:, 0, :]
    base = 2 * Q_BLOCK - (CMP_LEN - 1)
    segs = [asc[:, base - CMP_STRIDE * nn:base - CMP_STRIDE * nn + s].reshape(h, qblocks, Q_BLOCK)
            for nn in range(SUBLANES)]
    trc = jnp.flip(jnp.stack(segs, axis=2), axis=1).reshape(h, qblocks * SUBLANES, Q_BLOCK)
    return tb, jnp.pad(trc, ((0, 0), (0, qblocks * SUBLANES), (0, 0)))


def _nsa_cmp_kernel(q_ref, kc_ref, vct_ref, trc_ref, smt_ref, gate_ref, oc_ref, sel_ref, *, qblocks, topn, nb):
    for c in range(nb):
        _nsa_cmp_block(pl.program_id(1) * nb + c, slice(c * Q_BLOCK, (c + 1) * Q_BLOCK), q_ref, kc_ref, vct_ref,
                       trc_ref, smt_ref, gate_ref, oc_ref, sel_ref, qblocks=qblocks, topn=topn)


def _nsa_cmp_block(qb, rows, q_ref, kc_ref, vct_ref, trc_ref, smt_ref, gate_ref, oc_ref, sel_ref, *, qblocks, topn):
    rq = NSA_GROUP * Q_BLOCK
    q = q_ref[:, rows, :].reshape(rq, NSA_HEAD_DIM)
    kc = kc_ref[0]
    nc = kc.shape[0]
    nslc = smt_ref.shape[0]
    s = lax.dot_general(kc, q, _NT, preferred_element_type=F32)
    n_io = lax.broadcasted_iota(jnp.int32, (nc, Q_BLOCK), 0)
    i_io = lax.broadcasted_iota(jnp.int32, (nc, Q_BLOCK), 1)
    mask = (n_io * CMP_STRIDE + (CMP_LEN - 1)) <= (qb * Q_BLOCK + i_io)
    off = pl.multiple_of((qblocks - 1 - qb) * SUBLANES, SUBLANES)
    ps = []
    for r in range(NSA_GROUP):
        l = jnp.where(mask, s[:, r * Q_BLOCK:(r + 1) * Q_BLOCK] + trc_ref[r, pl.ds(off, nc), :], NEG_INF)
        m = jnp.max(l, axis=0, keepdims=True)
        p = jnp.where(mask, jnp.exp2(l - m), 0.0)
        den = jnp.maximum(jnp.sum(p, axis=0, keepdims=True), 1e-30)
        ps.append(p / den)
    p_all = jnp.concatenate(ps, axis=1).astype(BF)
    oc_t = jnp.dot(vct_ref[0], p_all, preferred_element_type=F32)
    imp4 = jnp.dot(smt_ref[...], p_all, preferred_element_type=F32)
    imp = imp4[:, 0:Q_BLOCK]
    for r in range(1, NSA_GROUP):
        imp = imp + imp4[:, r * Q_BLOCK:(r + 1) * Q_BLOCK]

    j_io = lax.broadcasted_iota(jnp.int32, (nslc, Q_BLOCK), 0)
    t_io = qb * Q_BLOCK + lax.broadcasted_iota(jnp.int32, (nslc, Q_BLOCK), 1)
    cur = t_io >> int(math.log2(SLC_LEN))
    forced = jnp.logical_or(j_io == 0, jnp.logical_or(j_io == cur, j_io == cur - 1))
    n_forced = 3
    val = jnp.where(forced, -3e38, jnp.where(j_io <= cur, imp, -1e9))
    sel = jnp.where(forced, 1.0, 0.0)
    for _ in range(max(topn - n_forced, 0)):
        mx = jnp.max(val, axis=0, keepdims=True)
        cand = jnp.where(val == mx, j_io, nslc)
        jmin = jnp.min(cand, axis=0, keepdims=True)
        pick = j_io == jmin
        sel = jnp.where(pick, 1.0, sel)
        val = jnp.where(pick, -3e38, val)
    sel_ref[0, rows, :] = jnp.where(sel.T > 0.5, 0.0, NEG_INF).astype(sel_ref.dtype)

    gates = gate_ref[0, rows, :]
    for r in range(NSA_GROUP):
        o_r = oc_t[:, r * Q_BLOCK:(r + 1) * Q_BLOCK].T
        oc_ref[rows, r * NSA_HEAD_DIM:(r + 1) * NSA_HEAD_DIM] = o_r * gates[:, 3 * r:3 * r + 1]


def _nsa_cmp(q, kc, vct, trc, smt, gates):
    h, s, d = q.shape
    g = NSA_KV_HEADS
    qblocks = s // Q_BLOCK
    nc = kc.shape[1]
    nslc = smt.shape[0]
    topn = min(SLC_TOPN, nslc)
    nb = 2
    qrows = nb * Q_BLOCK
    return pl.pallas_call(
        functools.partial(_nsa_cmp_kernel, qblocks=qblocks, topn=topn, nb=nb),
        grid=(g, qblocks // nb),
        in_specs=[pl.BlockSpec((NSA_GROUP, qrows, d), lambda gg, i: (gg, i, 0)),
                  pl.BlockSpec((1, nc, d), lambda gg, i: (gg, 0, 0)),
                  pl.BlockSpec((1, d, nc), lambda gg, i: (gg, 0, 0)),
                  pl.BlockSpec((NSA_GROUP, trc.shape[1], Q_BLOCK), lambda gg, i: (gg, 0, 0)),
                  pl.BlockSpec(smt.shape, lambda gg, i: (0, 0)),
                  pl.BlockSpec((1, qrows, LANES), lambda gg, i: (gg, i, 0))],
        out_specs=[pl.BlockSpec((qrows, NSA_GROUP * d), lambda gg, i: (i, gg)),
                   pl.BlockSpec((1, qrows, nslc), lambda gg, i: (gg, i, 0))],
        out_shape=[jax.ShapeDtypeStruct((s, h * d), F32),
                   jax.ShapeDtypeStruct((g, s, nslc), BF)],
        compiler_params=_cparams("arbitrary", "arbitrary"),
        name="nsa_cmp",
    )(q, kc, vct, trc, smt, gates)


def _nsa_sw_kernel(q_ref, kv_ref, tb_ref, sel_ref, et_ref, gate_ref, oc_ref,
                   o_ref, s0_ref, s1_ref, m_ref, acc_ref, part_ref, w_ref, *, tk, mc, nch):
    i = pl.program_id(0)
    ngrp = NSA_KV_HEADS
    rq = NSA_GROUP * Q_BLOCK
    nsub = tk // Q_BLOCK
    s_bufs = (s0_ref, s1_ref)
    qbs = [i * nch + c for c in range(nch)]
    qs = [[q_ref[g * NSA_GROUP:(g + 1) * NSA_GROUP, c * Q_BLOCK:(c + 1) * Q_BLOCK, :].reshape(rq, NSA_HEAD_DIM)
           for c in range(nch)] for g in range(ngrp)]

    def bias_tile(g, r, c, kb0, nblk):
        return jnp.concatenate(
            [tb_ref[g * NSA_GROUP + r, jnp.clip(qbs[c] - (kb0 + b), 0, mc)] for b in range(nblk)], axis=1)

    def gated(g, c, o, branch, base):
        gates = gate_ref[g, c * Q_BLOCK:(c + 1) * Q_BLOCK, :]
        outs = []
        for r in range(NSA_GROUP):
            o_r = o[r * Q_BLOCK:(r + 1) * Q_BLOCK, :] * gates[:, 3 * r + branch:3 * r + branch + 1]
            outs.append(base[:, r * NSA_HEAD_DIM:(r + 1) * NSA_HEAD_DIM] + o_r)
        return jnp.concatenate(outs, axis=1)

    _init_state(m_ref, acc_ref)
    qa = [[jnp.concatenate([qs[g][c], jnp.concatenate(
        [sel_ref[g, c * Q_BLOCK:(c + 1) * Q_BLOCK, :]] * NSA_GROUP, axis=0)], axis=1) for c in range(nch)]
        for g in range(ngrp)]

    def pre(g, j, causal):
        off = pl.multiple_of(j * tk, tk)
        ka = jnp.concatenate([kv_ref[g, pl.ds(off, tk), :], et_ref[pl.ds(off, tk), :]], axis=1)
        for c in range(nch):
            s = lax.dot_general(qa[g][c], ka, _NT, preferred_element_type=F32)
            if causal:
                kk = off + lax.broadcasted_iota(jnp.int32, (Q_BLOCK, tk), 1)
                ii = qbs[c] * Q_BLOCK + lax.broadcasted_iota(jnp.int32, (Q_BLOCK, tk), 0)
                future = jnp.where(kk <= ii, 0.0, NEG_INF)
            for r in range(NSA_GROUP):
                bias = bias_tile(g, r, c, j * nsub, nsub)
                if causal:
                    bias = bias + future
                s_bufs[g][c, r * Q_BLOCK:(r + 1) * Q_BLOCK, :] = s[r * Q_BLOCK:(r + 1) * Q_BLOCK, :] + bias

    def process(g, j):
        vp = _with_ones(kv_ref[ngrp + g, pl.ds(pl.multiple_of(j * tk, tk), tk), :])
        for c in range(nch):
            _flash_update(s_bufs[g].at[c], vp, m_ref.at[g * nch + c], acc_ref.at[g * nch + c])

    n_past = (i * nch) // nsub

    def body(t, carry):
        pre(1, t, False)
        process(0, t)
        pre(0, t + 1, False)
        process(1, t)
        return carry

    wk = WINDOW + Q_BLOCK
    wblk = wk // Q_BLOCK

    def win_off(c):
        kb0 = jnp.maximum(qbs[c] - WINDOW // Q_BLOCK, 0)
        return kb0, pl.multiple_of(kb0 * Q_BLOCK, Q_BLOCK)

    def win_pre(g, c):
        kb0, off = win_off(c)
        s = lax.dot_general(qs[g][c], kv_ref[2 * ngrp + g, pl.ds(off, wk), :], _NT, preferred_element_type=F32)
        rel = (qbs[c] * Q_BLOCK + lax.broadcasted_iota(jnp.int32, (Q_BLOCK, wk), 0)) - (
            off + lax.broadcasted_iota(jnp.int32, (Q_BLOCK, wk), 1))
        outside = jnp.where(jnp.logical_and(rel >= 0, rel < WINDOW), 0.0, NEG_INF)
        for r in range(NSA_GROUP):
            w_ref[g * nch + c, r * Q_BLOCK:(r + 1) * Q_BLOCK, :] = (
                s[r * Q_BLOCK:(r + 1) * Q_BLOCK, :] + (bias_tile(g, r, c, kb0, wblk) + outside))

    def win_proc(g, c):
        _, off = win_off(c)
        l_ref = w_ref.at[g * nch + c]
        p = jnp.exp2(l_ref[...] - jnp.max(l_ref[...], axis=1, keepdims=True)).astype(BF)
        ow = jnp.dot(p, _with_ones(kv_ref[3 * ngrp + g, pl.ds(off, wk), :]), preferred_element_type=F32)
        rows = slice(c * Q_BLOCK, (c + 1) * Q_BLOCK)
        cols = slice(g * rq, (g + 1) * rq)
        part_ref[rows, cols] = gated(g, c, _normalized(ow), 2, oc_ref[rows, cols])

    pre(0, n_past, True)
    pre(1, n_past, True)
    win_pre(0, 0)
    win_pre(0, 1)
    process(0, n_past)
    win_pre(1, 0)
    win_pre(1, 1)
    pre(0, 0, False)
    process(1, n_past)
    for g in range(ngrp):
        for c in range(nch):
            win_proc(g, c)

    lax.fori_loop(0, n_past, body, 0)

    for g in range(ngrp):
        cols = slice(g * rq, (g + 1) * rq)
        for c in range(nch):
            rows = slice(c * Q_BLOCK, (c + 1) * Q_BLOCK)
            o_ref[rows, cols] = gated(g, c, _normalized(acc_ref[g * nch + c]), 1,
                                      part_ref[rows, cols]).astype(o_ref.dtype)


def _nsa_sw(q, kvsw, tb, sel, et, gates, oc):
    h, s, d = q.shape
    g = NSA_KV_HEADS
    nch = 2
    qrows = nch * Q_BLOCK
    tk = min(512, s)
    mc = tb.shape[1] - 1
    nslc = sel.shape[2]
    rq = NSA_GROUP * Q_BLOCK
    assert tk % qrows == 0 and WINDOW + Q_BLOCK <= s and NSA_GROUP * d == rq

    def resident(arr):
        return pl.BlockSpec(arr.shape, lambda i, nd=arr.ndim: (0,) * nd, pipeline_mode=pl.Buffered(1))

    return pl.pallas_call(
        functools.partial(_nsa_sw_kernel, tk=tk, mc=mc, nch=nch),
        grid=(s // qrows,),
        in_specs=[pl.BlockSpec((h, qrows, d), lambda i: (0, i, 0)),
                  resident(kvsw), resident(tb),
                  pl.BlockSpec((g, qrows, nslc), lambda i: (0, i, 0)),
                  resident(et),
                  pl.BlockSpec((g, qrows, LANES), lambda i: (0, i, 0)),
                  pl.BlockSpec((qrows, h * d), lambda i: (i, 0))],
        out_specs=pl.BlockSpec((qrows, h * d), lambda i: (i, 0)),
        out_shape=jax.ShapeDtypeStruct((s, h * d), BF),
        scratch_shapes=[pltpu.VMEM((nch, rq, tk), F32), pltpu.VMEM((nch, rq, tk), F32),
                        pltpu.VMEM((g * nch, rq, 1), F32), pltpu.VMEM((g * nch, rq, 2 * LANES), F32),
                        pltpu.VMEM((qrows, h * d), F32),
                        pltpu.VMEM((g * nch, rq, WINDOW + Q_BLOCK), F32)],
        compiler_params=_cparams("arbitrary"),
        name="nsa_sel_win",
    )(q, kvsw, tb, sel, et, gates, oc)


def _selection_map_t(nc, nslc):
    n = np.arange(nc)[None, :] * CMP_STRIDE
    j0 = np.arange(nslc)[:, None] * SLC_LEN
    valid = np.arange(nc)[None, :] < nc - 1
    return jnp.asarray(((n < j0 + SLC_LEN) & (n + CMP_LEN > j0) & valid).astype(np.float32), dtype=BF)


def _block_onehot(s, nslc):
    tok = np.arange(s)[:, None]
    j = np.arange(nslc)[None, :]
    return jnp.asarray((tok // SLC_LEN == j).astype(np.float32), dtype=BF)


def _rope_tables(positions):
    half = MLA_ROPE // 2
    inv = ROPE_THETA ** (-jnp.arange(half, dtype=F32) / half)
    ang = positions.astype(F32)[:, None] * inv
    cos, sin = jnp.cos(ang), jnp.sin(ang)
    z = jnp.zeros_like(cos)
    zpad = jnp.zeros((positions.shape[0], LANES - MLA_ROPE), F32)
    c = jnp.concatenate([cos, cos, zpad], axis=1)
    s1 = jnp.concatenate([-sin, z, zpad], axis=1)
    s2 = jnp.concatenate([z, sin, zpad], axis=1)
    return c, s1, s2


def _pad_cols(w, n):
    return jnp.pad(w, ((0, 0), (0, n - w.shape[1])))


def _pad_rows(w, n):
    return jnp.pad(w, ((0, n - w.shape[0]), (0, 0)))


def _nsa_mla_mixer(hn, pos, rel_bias, w_in_all, w_out_all, e, gate_b, pos_k, w1_k, w2_k, pos_v, w1_v, w2_v,
                   q_norm, w_uq, kv_norm, w_ukv, x, g1):
    s = hn.shape[0]
    d = NSA_HEAD_DIM
    nq = NSA_HEADS * d
    nkv = 2 * NSA_KV_HEADS * d
    w_in_t = jnp.swapaxes(w_in_all, 1, 2)
    dm = w_in_t.shape[2]
    o0 = nq + 3 * nkv
    w_g_t = w_in_t[e, o0:o0 + 3 * NSA_HEADS]; o0 += 3 * NSA_HEADS
    w_lat_t = w_in_t[e, o0:o0 + MLA_Q_RANK + MLA_KV_RANK]; o0 += MLA_Q_RANK + MLA_KV_RANK
    w_kr_t = w_in_t[e, o0:o0 + MLA_ROPE]

    q_nsa = _proj([hn], [(w_in_t, dm, (e, 0), 0)], functools.partial(_epi_scale, d ** -0.5 * LOG2E), n=nq,
                  out_dtype=BF, head_major=True, tn=1024, w_transposed=True, name="proj_q_nsa")
    kv_cmp = _proj([hn], [(w_in_t, dm, (e, 0), nq)], _epi_id, n=nkv, out_dtype=F32, head_major=True,
                   w_transposed=True, name="proj_kv_cmp")
    kv_sw = _proj([hn], [(w_in_t, dm, (e, 0), nq + nkv)], _epi_id, n=2 * nkv, out_dtype=BF, head_major=True,
                  w_transposed=True, name="proj_kv_sw")
    per_g = 3 * NSA_GROUP
    w_gp_t = jnp.concatenate([_pad_rows(w_g_t[g * per_g:(g + 1) * per_g], LANES) for g in range(NSA_KV_HEADS)], 0)
    b_gp = jnp.concatenate([_pad_cols(gate_b[None, g * per_g:(g + 1) * per_g], LANES)
                            for g in range(NSA_KV_HEADS)], 1)
    gates = _proj([hn], [(w_gp_t[None], dm, (0, 0), 0)], _epi_sigmoid_bias, [(b_gp, "row")],
                  n=NSA_KV_HEADS * LANES, out_dtype=F32, head_major=True, w_transposed=True,
                  name="proj_gates")

    nc = s // CMP_STRIDE
    gw = CMP_STRIDE * d
    pos_kv = jnp.stack([pos_k.reshape(2, gw), pos_v.reshape(2, gw)])
    w1_kv = jnp.stack([w1_k.reshape(2, gw, CMP_HIDDEN), w1_v.reshape(2, gw, CMP_HIDDEN)]).astype(BF)
    w2_kv = jnp.stack([w2_k, w2_v]).astype(BF)
    kvc = _nsa_compress(kv_cmp.reshape(2 * NSA_KV_HEADS, nc, gw), pos_kv, w1_kv, w2_kv)
    kc = kvc[:NSA_KV_HEADS]
    vct = jnp.swapaxes(kvc[NSA_KV_HEADS:], 1, 2)

    nslc = s // SLC_LEN
    tb, trc = _bias_tables(rel_bias, s)
    smt = _selection_map_t(nc, nslc)
    oc, sel = _nsa_cmp(q_nsa, kc, vct, trc, smt, gates)
    o_nsa = _nsa_sw(q_nsa, kv_sw, tb, sel, _block_onehot(s, nslc), gates, oc)

    nlat = MLA_Q_RANK + MLA_KV_RANK
    lat = _proj([hn], [(w_lat_t[None], dm, (0, 0), 0)], _epi_rmsnorm,
                [(jnp.concatenate([q_norm, kv_norm])[None, :], "row")], n=nlat, out_dtype=BF, tn=MLA_Q_RANK,
                w_transposed=True, name="proj_mla_latent")
    c, s1, s2 = _rope_tables(pos)
    kr = _proj([hn], [(_pad_rows(w_kr_t, LANES)[None], dm, (0, 0), 0)], functools.partial(_epi_rope, 1.0),
               [(c, "rowtile"), (s1, "rowtile"), (s2, "rowtile")], n=LANES, out_dtype=BF, w_transposed=True,
               name="proj_mla_kr")
    scale = (MLA_NOPE + MLA_ROPE) ** -0.5 * LOG2E
    w_uq3 = w_uq.reshape(MLA_Q_RANK, MLA_HEADS, MLA_NOPE + MLA_ROPE)
    w_uqn = w_uq3[:, :, :MLA_NOPE].reshape(MLA_Q_RANK, MLA_HEADS * MLA_NOPE)
    w_uqr = jnp.pad(w_uq3[:, :, MLA_NOPE:], ((0, 0), (0, 0), (0, LANES - MLA_ROPE))).reshape(
        MLA_Q_RANK, MLA_HEADS * LANES)
    qn = _proj([lat], [(w_uqn, MLA_Q_RANK, 0, 0)], functools.partial(_epi_scale, scale),
               n=MLA_HEADS * MLA_NOPE, out_dtype=BF, head_major=True, tn=1024, lhs_col_block=[0],
               name="proj_mla_qn")
    qr = _proj([lat], [(w_uqr, MLA_Q_RANK, 0, 0)], functools.partial(_epi_rope, scale),
               [(c, "rowtile"), (s1, "rowtile"), (s2, "rowtile")], n=MLA_HEADS * LANES, out_dtype=BF,
               head_major=True, lhs_col_block=[0], name="proj_mla_qr")
    kvh = _proj([lat], [(w_ukv, MLA_KV_RANK, 0, 0)], _epi_id, n=MLA_HEADS * (MLA_NOPE + MLA_V), out_dtype=BF,
                head_major=True, tn=1024, lhs_col_block=[1], name="proj_mla_kv")
    o_mla = _mla_attention(qn, qr, kvh, kr)

    return _proj([o_nsa, o_mla], [(w_out_all, nq, (e, 0), 0), (w_out_all, MLA_HEADS * MLA_V, (e, 1), 0)],
                 _epi_residual, [(x, "tile"), (g1, "row")], n=w_out_all.shape[2], out_dtype=F32, tn=1024,
                 name="proj_even_out")


def _fox_mixer(hn, w_in_all, w_out_all, o, f_b, q_norm, k_norm, x, g1):
    d = D_MODEL
    dh = FOX_HEAD_DIM
    w_in_t = jnp.swapaxes(w_in_all, 1, 2)
    w_f_t = w_in_t[o, 3 * d:3 * d + FOX_HEADS]
    w_og_t = w_in_t[o, 3 * d + FOX_HEADS:]
    q = _proj([hn], [(w_in_t, d, (o, 0), 0)], functools.partial(_epi_headnorm, dh ** -0.5 * LOG2E),
              [(q_norm[None, :], "const")], n=d, out_dtype=BF, head_major=True, tn=1024, w_transposed=True,
              name="proj_fox_q")
    k = _proj([hn], [(w_in_t, d, (o, 0), d)], functools.partial(_epi_headnorm, 1.0),
              [(k_norm[None, :], "const")], n=d, out_dtype=BF, head_major=True, tn=1024, w_transposed=True,
              name="proj_fox_k")
    v = _proj([hn], [(w_in_t, d, (o, 0), 2 * d)], _epi_id, n=d, out_dtype=BF, head_major=True, tn=1024,
              w_transposed=True, name="proj_fox_v")
    lf = _proj([hn], [(_pad_rows(w_f_t, LANES)[None], d, (0, 0), 0)], _epi_logsigmoid_bias,
               [(_pad_cols(f_b[None, :], LANES), "row")], n=LANES, out_dtype=F32, w_transposed=True,
               name="proj_fox_f")
    sig_og = _proj([hn], [(w_og_t[None], d, (0, 0), 0)], _epi_sigmoid, n=d, out_dtype=F32, tn=1024,
                   w_transposed=True, name="proj_fox_og")
    cum, cum_t = _cumsum_tokens(lf, LOG2E)
    cum_t = cum_t[:FOX_HEADS].reshape(FOX_HEADS, 1, -1)
    att = _fox_attention(q, k, v, cum_t, cum, sig_og)
    return _proj([att], [(w_out_all, d, (o, 0), 0)], _epi_residual, [(x, "tile"), (g1, "row")], n=d,
                 out_dtype=F32, tn=1024, name="proj_fox_out")


def kernel(x, c, positions, rel_bias, ada_w, ada_b, norm_mix, norm_ffn, ffn_w1, ffn_w3, ffn_w2, even_w_in, even_w_out, nsa_gate_b, nsa_cmp_pos_k, nsa_cmp_w1_k, nsa_cmp_w2_k, nsa_cmp_pos_v, nsa_cmp_w1_v, nsa_cmp_w2_v, mla_q_norm, mla_w_uq, mla_kv_norm, mla_w_ukv, fox_w_in, fox_w_out, fox_f_b, fox_q_norm, fox_k_norm, final_norm):
    b, s, d = x.shape
    assert b == 1 and d == D_MODEL and s % 1024 == 0
    xs = x[0]
    pos = positions[0]
    mod = _adaln(c, ada_w, ada_b)
    depth = ada_w.shape[0]
    hn = None
    for i in range(depth):
        sh1, sc1, g1, sh2, sc2, g2 = [mod[i:i + 1, k * d:(k + 1) * d] for k in range(6)]
        if hn is None:
            hn = _normmod(xs, norm_mix[i][None, :], sc1, sh1)
        if i % 2 == 0:
            e = i // 2
            xs = _nsa_mla_mixer(hn, pos, rel_bias, even_w_in, even_w_out, e, nsa_gate_b[e],
                                nsa_cmp_pos_k[e], nsa_cmp_w1_k[e], nsa_cmp_w2_k[e],
                                nsa_cmp_pos_v[e], nsa_cmp_w1_v[e], nsa_cmp_w2_v[e],
                                mla_q_norm[e], mla_w_uq[e], mla_kv_norm[e], mla_w_ukv[e], xs, g1)
        else:
            o = i // 2
            xs = _fox_mixer(hn, fox_w_in, fox_w_out, o, fox_f_b[o], fox_q_norm[o], fox_k_norm[o], xs, g1)
        if i == depth - 1:
            xs = _ffn(xs, norm_ffn[i][None, :], sc2, sh2, g2, ffn_w1, ffn_w3, ffn_w2, i,
                      final_gain=final_norm[None, :])
        else:
            nxt = (norm_mix[i + 1][None, :], mod[i + 1:i + 2, d:2 * d], mod[i + 1:i + 2, 0:d])
            xs, hn = _ffn(xs, norm_ffn[i][None, :], sc2, sh2, g2, ffn_w1, ffn_w3, ffn_w2, i, next_mod=nxt)
    return xs[None]
```

```python
import functools
import math

import numpy as np
import jax
import jax.numpy as jnp
from jax import lax
from jax.experimental import pallas as pl
from jax.experimental.pallas import tpu as pltpu

D_MODEL = 2048
DEPTH = 2
EPS = 1e-6
NEG_INF = -1e30

NSA_HEADS = 8
NSA_KV_HEADS = 2
NSA_GROUP = NSA_HEADS // NSA_KV_HEADS
NSA_HEAD_DIM = 128
CMP_LEN = 32
CMP_STRIDE = 16
CMP_HIDDEN = 256
SLC_LEN = 64
SLC_TOPN = 16
WINDOW = 512
Q_BLOCK = 128

MLA_HEADS = 8
MLA_Q_RANK = 512
MLA_KV_RANK = 512
MLA_NOPE = 128
MLA_ROPE = 64
MLA_V = 128
ROPE_THETA = 10000.0

FOX_HEADS = 16
FOX_HEAD_DIM = D_MODEL // FOX_HEADS

REL_BUCKETS = 32
REL_MAX_DIST = 4096

FFN_HIDDEN = ((8 * D_MODEL + 2) // 3 + 255) // 256 * 256

LANES = 128
SUBLANES = 8
VMEM_LIMIT_BYTES = 56 * 1024 * 1024
LOG2E = math.log2(math.e)

ADALN_COLS = 1024
NORM_ROWS = 512
PROJ_ROWS = 1024
PROJ_COLS = 512
PROJ_WIDE_COLS = 1024
FFN_ROWS = 1024
FFN_HIDDEN_COLS = 256
ATTN_CHAIN_ROWS = 512
NSA_KEY_TILE = 512
NSA_BLOCKS_PER_STEP = 2
CUMSUM_ROWS = 256

BF = jnp.bfloat16
F32 = jnp.float32
_NT = (((1,), (1,)), ((), ()))


def _cparams(*sem):
    return pltpu.CompilerParams(dimension_semantics=sem, vmem_limit_bytes=VMEM_LIMIT_BYTES)


def _rms(x):
    return x * lax.rsqrt(jnp.mean(x * x, axis=-1, keepdims=True) + EPS)


def _adaln_kernel(c_ref, w_ref, b_ref, o_ref):
    c = c_ref[...]
    cond = c * jax.nn.sigmoid(c)
    acc = jnp.dot(cond.astype(BF), w_ref[0].astype(BF), preferred_element_type=F32)
    o_ref[0] = acc + b_ref[0]


def _adaln(c, ada_w, ada_b):
    depth, d, n = ada_w.shape
    tn = ADALN_COLS
    c8 = jnp.broadcast_to(c, (SUBLANES, d))
    out = pl.pallas_call(
        _adaln_kernel,
        grid=(depth, n // tn),
        in_specs=[pl.BlockSpec((SUBLANES, d), lambda i, j: (0, 0)),
                  pl.BlockSpec((1, d, tn), lambda i, j: (i, 0, j)),
                  pl.BlockSpec((1, 1, tn), lambda i, j: (i, 0, j))],
        out_specs=pl.BlockSpec((1, SUBLANES, tn), lambda i, j: (i, 0, j)),
        out_shape=jax.ShapeDtypeStruct((depth, SUBLANES, n), F32),
        compiler_params=_cparams("arbitrary", "arbitrary"),
        name="adaln",
    )(c8, ada_w, ada_b.reshape(depth, 1, n))
    return out[:, 0, :]


def _normmod_kernel(x_ref, g_ref, sc_ref, sh_ref, o_ref):
    y = _rms(x_ref[...])
    o_ref[...] = ((y * g_ref[...]) * (1.0 + sc_ref[...]) + sh_ref[...]).astype(o_ref.dtype)


def _normmod(x, g, sc, sh):
    s, d = x.shape
    tm = NORM_ROWS
    row = pl.BlockSpec((1, d), lambda i: (0, 0))
    return pl.pallas_call(
        _normmod_kernel,
        grid=(s // tm,),
        in_specs=[pl.BlockSpec((tm, d), lambda i: (i, 0)), row, row, row],
        out_specs=pl.BlockSpec((tm, d), lambda i: (i, 0)),
        out_shape=jax.ShapeDtypeStruct((s, d), BF),
        compiler_params=_cparams("arbitrary"),
        name="normmod",
    )(x, g, sc, sh)


def _proj_kernel(*refs, n_lhs, n_epi, epi, head_major, w_transposed):
    lhs = refs[:n_lhs]
    ws = refs[n_lhs:2 * n_lhs]
    epis = refs[2 * n_lhs:2 * n_lhs + n_epi]
    o_ref = refs[2 * n_lhs + n_epi]
    wbf = refs[2 * n_lhs + n_epi + 1:]

    @pl.when(pl.program_id(1) == 0)
    def _():
        for w, wb in zip(ws, wbf):
            wb[...] = w[...].astype(BF)

    acc = None
    for a, wb in zip(lhs, wbf):
        if w_transposed:
            d = lax.dot_general(a[...], wb[...], _NT, preferred_element_type=F32)
        else:
            d = jnp.dot(a[...], wb[...], preferred_element_type=F32)
        acc = d if acc is None else acc + d
    res = epi(acc, *[e[...] for e in epis])
    if head_major:
        for r in range(o_ref.shape[0]):
            o_ref[r] = res[:, r * LANES:(r + 1) * LANES].astype(o_ref.dtype)
    else:
        o_ref[...] = res.astype(o_ref.dtype)


def _proj(lhs, ws, epi, epi_in=(), *, n, out_dtype, head_major=False, tm=PROJ_ROWS, tn=PROJ_COLS,
          lhs_col_block=None, w_transposed=False, name="proj"):
    m = lhs[0].shape[0]
    tm = min(tm, m)
    tn = min(tn, n)
    if lhs_col_block is None:
        lhs_col_block = [0] * len(lhs)
    in_specs = []
    for (_, k, _, _), cb in zip(ws, lhs_col_block):
        in_specs.append(pl.BlockSpec((tm, k), lambda j, i, cb=cb: (i, cb)))
    for arr, k, rb, col0 in ws:
        assert col0 % tn == 0
        if w_transposed:
            in_specs.append(pl.BlockSpec((None, tn, k),
                                         lambda j, i, rb=rb, cb0=col0 // tn: (rb[0], cb0 + j, rb[1])))
        elif arr.ndim == 3:
            in_specs.append(pl.BlockSpec((None, k, tn),
                                         lambda j, i, rb=rb, cb0=col0 // tn: (rb[0], rb[1], cb0 + j)))
        else:
            in_specs.append(pl.BlockSpec((k, tn), lambda j, i, rb=rb, cb0=col0 // tn: (rb, cb0 + j)))
    arrays = list(lhs) + [w[0] for w in ws]
    for arr, kind in epi_in:
        if kind == "row":
            in_specs.append(pl.BlockSpec((1, tn), lambda j, i: (0, j)))
        elif kind == "const":
            in_specs.append(pl.BlockSpec(arr.shape, lambda j, i: (0, 0)))
        elif kind == "tile":
            in_specs.append(pl.BlockSpec((tm, tn), lambda j, i: (i, j)))
        elif kind == "rowtile":
            in_specs.append(pl.BlockSpec((tm, arr.shape[1]), lambda j, i: (i, 0)))
        else:
            raise ValueError(kind)
        arrays.append(arr)
    if head_major:
        hpt = tn // LANES
        out_spec = pl.BlockSpec((hpt, tm, LANES), lambda j, i: (j, i, 0))
        out_shape = jax.ShapeDtypeStruct((n // LANES, m, LANES), out_dtype)
    else:
        out_spec = pl.BlockSpec((tm, tn), lambda j, i: (i, j))
        out_shape = jax.ShapeDtypeStruct((m, n), out_dtype)
    kern = functools.partial(_proj_kernel, n_lhs=len(lhs), n_epi=len(epi_in), epi=epi,
                             head_major=head_major, w_transposed=w_transposed)
    return pl.pallas_call(
        kern,
        grid=(n // tn, m // tm),
        in_specs=in_specs,
        out_specs=out_spec,
        out_shape=out_shape,
        scratch_shapes=[pltpu.VMEM((tn, k) if w_transposed else (k, tn), BF) for _, k, _, _ in ws],
        compiler_params=_cparams("arbitrary", "arbitrary"),
        name=name,
    )(*arrays)


def _epi_id(acc):
    return acc


def _epi_scale(scale, acc):
    return acc * scale


def _epi_sigmoid_bias(acc, b):
    return jax.nn.sigmoid(acc + b)


def _epi_logsigmoid_bias(acc, b):
    return jax.nn.log_sigmoid(acc + b)


def _epi_sigmoid(acc):
    return jax.nn.sigmoid(acc)


def _epi_rmsnorm(acc, g):
    return _rms(acc) * g


def _epi_headnorm(scale, acc, g):
    outs = []
    for r in range(acc.shape[1] // LANES):
        outs.append(_rms(acc[:, r * LANES:(r + 1) * LANES]) * g * scale)
    return jnp.concatenate(outs, axis=1)


def _epi_rope(scale, acc, c, s1, s2):
    reps = acc.shape[1] // LANES
    half = MLA_ROPE // 2
    if reps > 1:
        c = jnp.concatenate([c] * reps, axis=1)
        s1 = jnp.concatenate([s1] * reps, axis=1)
        s2 = jnp.concatenate([s2] * reps, axis=1)
    n = acc.shape[1]
    out = acc * c + pltpu.roll(acc, n - half, 1) * s1 + pltpu.roll(acc, half, 1) * s2
    return out * scale


def _epi_residual(acc, x, g):
    return x + g * acc


def _ffn_kernel(*refs, mode):
    x_ref, g_ref, sc_ref, sh_ref, g2_ref, w1_ref, w3_ref, w2_ref = refs[:8]
    if mode == "final":
        fn_ref, o_ref, hn_ref = refs[8:]
    elif mode == "next":
        gn_ref, scn_ref, shn_ref, o_ref, hnext_ref, hn_ref = refs[8:]
    else:
        o_ref, hn_ref = refs[8:]
    f = pl.program_id(1)
    tm = x_ref.shape[0]
    halves = [slice(r * (tm // 2), (r + 1) * (tm // 2)) for r in range(2)]

    @pl.when(f == 0)
    def _():
        for rows in halves:
            x = x_ref[rows, :]
            hn_ref[rows, :] = ((_rms(x) * g_ref[...]) * (1.0 + sc_ref[...]) + sh_ref[...]).astype(BF)
            o_ref[rows, :] = x

    w1 = w1_ref[...].astype(BF)
    w3 = w3_ref[...].astype(BF)
    w2 = w2_ref[...].astype(BF)
    for rows in halves:
        h = hn_ref[rows, :]
        h1 = jnp.dot(h, w1, preferred_element_type=F32)
        h3 = jnp.dot(h, w3, preferred_element_type=F32)
        a = (h1 * jax.nn.sigmoid(h1)) * h3
        o_ref[rows, :] += g2_ref[...] * jnp.dot(a.astype(BF), w2, preferred_element_type=F32)

    if mode != "plain":
        @pl.when(f == pl.num_programs(1) - 1)
        def _():
            for rows in halves:
                y = _rms(o_ref[rows, :])
                if mode == "final":
                    o_ref[rows, :] = y * fn_ref[...]
                else:
                    hnext_ref[rows, :] = ((y * gn_ref[...]) * (1.0 + scn_ref[...]) + shn_ref[...]).astype(BF)


def _ffn(x, g, sc, sh, g2, w1, w3, w2, layer, final_gain=None, next_mod=None):
    assert final_gain is None or next_mod is None
    s, d = x.shape
    fdim = w1.shape[2]
    tm = min(FFN_ROWS, s)
    tf = FFN_HIDDEN_COLS
    row = pl.BlockSpec((1, d), lambda i, f: (0, 0))
    tile = pl.BlockSpec((tm, d), lambda i, f: (i, 0))
    in_specs = [pl.BlockSpec((tm, d), lambda i, f: (i, 0), pipeline_mode=pl.Buffered(1)), row, row, row, row,
                pl.BlockSpec((None, d, tf), lambda i, f: (layer, 0, f)),
                pl.BlockSpec((None, d, tf), lambda i, f: (layer, 0, f)),
                pl.BlockSpec((None, tf, d), lambda i, f: (layer, f, 0))]
    arrays = [x, g, sc, sh, g2, w1, w3, w2]
    out_specs, out_shape, mode = tile, jax.ShapeDtypeStruct((s, d), F32), "plain"
    if final_gain is not None:
        in_specs.append(row)
        arrays.append(final_gain)
        mode = "final"
    elif next_mod is not None:
        in_specs += [row, row, row]
        arrays += list(next_mod)
        out_specs = [tile, tile]
        out_shape = [out_shape, jax.ShapeDtypeStruct((s, d), BF)]
        mode = "next"
    return pl.pallas_call(
        functools.partial(_ffn_kernel, mode=mode),
        grid=(s // tm, fdim // tf),
        in_specs=in_specs,
        out_specs=out_specs,
        out_shape=out_shape,
        scratch_shapes=[pltpu.VMEM((tm, d), BF)],
        compiler_params=_cparams("arbitrary", "arbitrary"),
        name="ffn",
    )(*arrays)


def _flash_update(s, vp, m_ref, acc_ref, row_shift=None):
    m_prev = m_ref[...]
    m_tile = jnp.max(s[...], axis=1, keepdims=True)
    if row_shift is not None:
        m_tile = m_tile + row_shift
    m_new = jnp.maximum(m_prev, m_tile)
    alpha = jnp.exp2(m_prev - m_new)
    sub = m_new if row_shift is None else m_new - row_shift
    p = jnp.exp2(s[...] - sub).astype(BF)
    acc_ref[...] = alpha * acc_ref[...] + jnp.dot(p, vp, preferred_element_type=F32)
    m_ref[...] = m_new


def _init_state(m_ref, acc_ref):
    m_ref[...] = jnp.full_like(m_ref, NEG_INF)
    acc_ref[...] = jnp.zeros_like(acc_ref)


def _with_ones(v):
    return jnp.concatenate([v, jnp.ones(v.shape, v.dtype)], axis=1)


def _normalized(acc):
    return acc[:, :LANES] / acc[:, LANES:]


def _causal_mask(t):
    rows = lax.broadcasted_iota(jnp.int32, (t, t), 0)
    cols = lax.broadcasted_iota(jnp.int32, (t, t), 1)
    return cols <= rows


def _causal_sweep(qi, tc, s_bufs, m_ref, acc_ref, streams, nsplit):
    tw = 2 * tc

    doff = pl.multiple_of(qi * tw, tw)

    psplit = 2
    rs = tc // psplit
    pieces = [(c, slice(hh * rs, (hh + 1) * rs)) for c in range(2) for hh in range(psplit)]
    dwidth = (tc, tw)

    def pre_diag(si, ksides, c, rows):
        _, logits_fn, _, _ = streams[si]
        r_io = c * tc + rows.start + lax.broadcasted_iota(jnp.int32, (rs, dwidth[c]), 0)
        c_io = lax.broadcasted_iota(jnp.int32, (rs, dwidth[c]), 1)
        s_bufs[si][c, rows, 0:dwidth[c]] = jnp.where(c_io <= r_io, logits_fn(c, ksides[c], rows), NEG_INF)

    def process_diag(si, vps, c, rows):
        shifts = streams[si][3]
        _flash_update(s_bufs[si].at[c, rows, 0:dwidth[c]], vps[c], m_ref.at[2 * si + c, rows, :],
                      acc_ref.at[2 * si + c, rows, :], row_shift=None if shifts[c] is None else shifts[c][rows, :])

    dks = [[streams[si][0](doff, w) for w in dwidth] for si in range(2)]
    dvs = [[streams[si][2](doff, w) for w in dwidth] for si in range(2)]
    for c, rows in pieces:
        pre_diag(0, dks[0], c, rows)
    for c, rows in pieces:
        pre_diag(1, dks[1], c, rows)
        process_diag(0, dvs[0], c, rows)
    kside0 = streams[0][0](0, tw)
    for c, rows in pieces:
        s_bufs[0][c, rows, :] = streams[0][1](c, kside0, rows)
        process_diag(1, dvs[1], c, rows)

    def half(sp, tp, sq, tq_):
        kside_fn, logits_fn, _, _ = streams[sp]
        _, _, vp_fn, shifts = streams[sq]
        kside = kside_fn(pl.multiple_of(tp * tw, tw), tw)
        vp = vp_fn(pl.multiple_of(tq_ * tw, tw), tw)
        rs = tc // nsplit
        for c in range(2):
            for hh in range(nsplit):
                rows = slice(hh * rs, (hh + 1) * rs)
                s_bufs[sp][c, rows, :] = logits_fn(c, kside, rows)
                _flash_update(s_bufs[sq].at[c, rows, :], vp, m_ref.at[2 * sq + c, rows, :],
                              acc_ref.at[2 * sq + c, rows, :],
                              row_shift=None if shifts[c] is None else shifts[c][rows, :])

    def body(t, carry):
        half(1, t, 0, t)
        half(0, t + 1, 1, t)
        return carry

    lax.fori_loop(0, qi, body, 0)


def _sweep_scratch(tc):
    return [pltpu.VMEM((2, tc, 2 * tc), F32), pltpu.VMEM((2, tc, 2 * tc), F32),
            pltpu.VMEM((4, tc, 1), F32), pltpu.VMEM((4, tc, 2 * LANES), F32)]


def _mla_kernel(qn_ref, qr_ref, kv_ref, kr_ref, o_ref, s0_ref, s1_ref, m_ref, acc_ref, *, tc):
    qi = pl.program_id(1)
    _init_state(m_ref, acc_ref)

    def stream(si):
        qs = [jnp.concatenate([qn_ref[si, c * tc:(c + 1) * tc, :], qr_ref[si, c * tc:(c + 1) * tc, :]], axis=1)
              for c in range(2)]

        def kside(off, width):
            return jnp.concatenate([kv_ref[2 * si, pl.ds(off, width), :], kr_ref[pl.ds(off, width), :]], axis=1)

        def logits(c, k, rows=slice(None)):
            return lax.dot_general(qs[c][rows, :], k, _NT, preferred_element_type=F32)

        def vp(off, width):
            return _with_ones(kv_ref[2 * si + 1, pl.ds(off, width), :])

        return kside, logits, vp, (None, None)

    _causal_sweep(qi, tc, (s0_ref, s1_ref), m_ref, acc_ref, [stream(0), stream(1)], nsplit=2)
    for si in range(2):
        for c in range(2):
            o_ref[c * tc:(c + 1) * tc, si * LANES:(si + 1) * LANES] = _normalized(
                acc_ref[2 * si + c]).astype(o_ref.dtype)


def _mla_attention(qn, qr, kvh, kr):
    h, s, _ = qn.shape
    tc = min(ATTN_CHAIN_ROWS, s // 2)
    w = 2 * tc
    return pl.pallas_call(
        functools.partial(_mla_kernel, tc=tc),
        grid=(h // 2, s // w),
        in_specs=[pl.BlockSpec((2, w, LANES), lambda p, i: (p, i, 0)),
                  pl.BlockSpec((2, w, LANES), lambda p, i: (p, i, 0)),
                  pl.BlockSpec((4, s, LANES), lambda p, i: (p, 0, 0)),
                  pl.BlockSpec((s, LANES), lambda p, i: (0, 0))],
        out_specs=pl.BlockSpec((w, 2 * LANES), lambda p, i: (i, p)),
        out_shape=jax.ShapeDtypeStruct((s, h * LANES), BF),
        scratch_shapes=_sweep_scratch(tc),
        compiler_params=_cparams("arbitrary", "arbitrary"),
        name="mla_attn",
    )(qn, qr, kvh, kr)


def _fox_kernel(q_ref, k_ref, v_ref, ck_ref, cum_ref, og_ref, o_ref, s0_ref, s1_ref, m_ref, acc_ref, *, tc):
    pair = pl.program_id(0)
    qi = pl.program_id(1)
    lane = lax.broadcasted_iota(jnp.int32, (tc, LANES), 1)
    _init_state(m_ref, acc_ref)

    def stream(si):
        qs = [q_ref[si, c * tc:(c + 1) * tc, :] for c in range(2)]
        cqs = [jnp.sum(jnp.where(lane == 2 * pair + si, cum_ref[c * tc:(c + 1) * tc, :], 0.0), axis=1,
                       keepdims=True) for c in range(2)]

        def kside(off, width):
            return k_ref[si, pl.ds(off, width), :], ck_ref[si, :, pl.ds(off, width)]

        def logits(c, kc, rows=slice(None)):
            return lax.dot_general(qs[c][rows, :], kc[0], _NT, preferred_element_type=F32) - kc[1]

        def vp(off, width):
            return _with_ones(v_ref[si, pl.ds(off, width), :])

        return kside, logits, vp, cqs

    _causal_sweep(qi, tc, (s0_ref, s1_ref), m_ref, acc_ref, [stream(0), stream(1)], nsplit=4)
    for si in range(2):
        cols = slice(si * LANES, (si + 1) * LANES)
        for c in range(2):
            rows = slice(c * tc, (c + 1) * tc)
            o_ref[rows, cols] = (_normalized(acc_ref[2 * si + c]) * og_ref[rows, cols]).astype(o_ref.dtype)


def _fox_attention(q, k, v, cum_t, cum, sig_og):
    h, s, _ = q.shape
    tc = min(ATTN_CHAIN_ROWS, s // 2)
    w = 2 * tc
    hm = pl.BlockSpec((2, s, LANES), lambda p, i: (p, 0, 0))
    return pl.pallas_call(
        functools.partial(_fox_kernel, tc=tc),
        grid=(h // 2, s // w),
        in_specs=[pl.BlockSpec((2, w, LANES), lambda p, i: (p, i, 0)), hm, hm,
                  pl.BlockSpec((2, 1, s), lambda p, i: (p, 0, 0)),
                  pl.BlockSpec((w, LANES), lambda p, i: (i, 0)),
                  pl.BlockSpec((w, 2 * LANES), lambda p, i: (i, p))],
        out_specs=pl.BlockSpec((w, 2 * LANES), lambda p, i: (i, p)),
        out_shape=jax.ShapeDtypeStruct((s, h * LANES), BF),
        scratch_shapes=_sweep_scratch(tc),
        compiler_params=_cparams("arbitrary", "arbitrary"),
        name="fox_attn",
    )(q, k, v, cum_t, cum, sig_og)


def _cumsum_kernel(x_ref, o_ref, ot_ref, carry_ref, *, t, out_scale):
    @pl.when(pl.program_id(0) == 0)
    def _():
        carry_ref[...] = jnp.zeros_like(carry_ref)

    x = x_ref[...]
    rows = lax.broadcasted_iota(jnp.int32, (t, t), 0)
    cols = lax.broadcasted_iota(jnp.int32, (t, t), 1)
    tri = jnp.where(cols <= rows, 1.0, 0.0).astype(BF)
    hi = x.astype(BF)
    r1 = x - hi.astype(F32)
    mid = r1.astype(BF)
    lo = (r1 - mid.astype(F32)).astype(BF)
    cum = (jnp.dot(tri, hi, preferred_element_type=F32) + jnp.dot(tri, mid, preferred_element_type=F32)
           + jnp.dot(tri, lo, preferred_element_type=F32)) + carry_ref[...]
    scaled = cum * out_scale
    o_ref[...] = scaled
    ot_ref[...] = scaled.T
    carry_ref[...] = cum[t - 1:t, :]


def _cumsum_tokens(x, out_scale):
    s, n = x.shape
    t = min(CUMSUM_ROWS, s)
    return pl.pallas_call(
        functools.partial(_cumsum_kernel, t=t, out_scale=out_scale),
        grid=(s // t,),
        in_specs=[pl.BlockSpec((t, n), lambda i: (i, 0))],
        out_specs=[pl.BlockSpec((t, n), lambda i: (i, 0)), pl.BlockSpec((n, t), lambda i: (0, i))],
        out_shape=[jax.ShapeDtypeStruct((s, n), F32), jax.ShapeDtypeStruct((n, s), F32)],
        scratch_shapes=[pltpu.VMEM((1, n), F32)],
        compiler_params=_cparams("arbitrary"),
        name="cumsum",
    )(x)


def _compress_kernel(a_ref, pos_ref, w1_ref, w2_ref, o_ref):
    a = a_ref[0]
    nc = a.shape[0]
    p1 = jnp.dot((a + pos_ref[0, 0:1, :]).astype(BF), w1_ref[0, 0], preferred_element_type=F32)
    p2 = jnp.dot((a + pos_ref[0, 1:2, :]).astype(BF), w1_ref[0, 1], preferred_element_type=F32)
    h = p1 + pltpu.roll(p2, nc - 1, 0)
    act = h * jax.nn.sigmoid(h)
    o_ref[0] = jnp.dot(act.astype(BF), w2_ref[0], preferred_element_type=F32).astype(o_ref.dtype)


def _nsa_compress(kv_cmp, pos, w1, w2):
    c, nc, gw = kv_cmp.shape
    return pl.pallas_call(
        _compress_kernel,
        grid=(c,),
        in_specs=[pl.BlockSpec((1, nc, gw), lambda i: (i, 0, 0)),
                  pl.BlockSpec((1, 2, gw), lambda i: (i // 2, 0, 0)),
                  pl.BlockSpec((1, 2, gw, CMP_HIDDEN), lambda i: (i // 2, 0, 0, 0)),
                  pl.BlockSpec((1, CMP_HIDDEN, NSA_HEAD_DIM), lambda i: (i // 2, 0, 0))],
        out_specs=pl.BlockSpec((1, nc, NSA_HEAD_DIM), lambda i: (i, 0, 0)),
        out_shape=jax.ShapeDtypeStruct((c, nc, NSA_HEAD_DIM), BF),
        compiler_params=_cparams("arbitrary"),
        name="nsa_compress",
    )(kv_cmp, pos, w1, w2)


def _bias_table_kernel(brev_ref, basc_ref, tbl_ref, tb_ref, asc_ref, *, s, mc):
    tbl = tbl_ref[0] * LOG2E

    def lookup(bkt):
        out = jnp.zeros(bkt.shape, F32)
        for b in range(REL_BUCKETS):
            out = jnp.where(bkt == b, tbl[:, b:b + 1], out)
        return out

    rev = lookup(brev_ref[...])
    asc_ref[0] = lookup(basc_ref[...])
    for m in range(mc + 1):
        win = rev[:, s - Q_BLOCK * m:s - Q_BLOCK * m + 2 * Q_BLOCK]
        rolled = pltpu.roll(jnp.broadcast_to(win, (Q_BLOCK, 2 * Q_BLOCK)), Q_BLOCK + 1, 1,
                            stride=1, stride_axis=0)
        tb_ref[0, m] = rolled[:, :Q_BLOCK]


def _t5_bucket(dist):
    max_exact = REL_BUCKETS // 2
    d = jnp.maximum(dist, 0)
    ratio = jnp.log(jnp.maximum(d, max_exact).astype(F32) / max_exact) / math.log(REL_MAX_DIST / max_exact)
    large = jnp.minimum(max_exact + (ratio * (REL_BUCKETS - max_exact)).astype(jnp.int32), REL_BUCKETS - 1)
    return jnp.where(d < max_exact, d, large)


def _bias_tables(rel_bias, s):
    assert Q_BLOCK == SUBLANES * CMP_STRIDE
    h = rel_bias.shape[1]
    qblocks = s // Q_BLOCK
    max_exact = REL_BUCKETS // 2
    d_const = int(math.ceil(max_exact * (REL_MAX_DIST / max_exact)
                            ** ((REL_BUCKETS - max_exact - 1) / (REL_BUCKETS - max_exact)))) + 1
    mc = min(-(-(d_const + Q_BLOCK - 1) // Q_BLOCK), qblocks - 1)
    nrev = s + 2 * Q_BLOCK
    nasc = s + 3 * Q_BLOCK
    brev = _t5_bucket(s + Q_BLOCK - 1 - jnp.arange(nrev, dtype=jnp.int32))[None, :]
    basc = _t5_bucket(jnp.arange(nasc, dtype=jnp.int32) - 2 * Q_BLOCK)[None, :]
    tb, asc = pl.pallas_call(
        functools.partial(_bias_table_kernel, s=s, mc=mc),
        grid=(h,),
        in_specs=[pl.BlockSpec((1, nrev), lambda i: (0, 0)),
                  pl.BlockSpec((1, nasc), lambda i: (0, 0)),
                  pl.BlockSpec((1, 1, REL_BUCKETS), lambda i: (i, 0, 0))],
        out_specs=[pl.BlockSpec((1, mc + 1, Q_BLOCK, Q_BLOCK), lambda i: (i, 0, 0, 0)),
                   pl.BlockSpec((1, 1, nasc), lambda i: (i, 0, 0))],
        out_shape=[jax.ShapeDtypeStruct((h, mc + 1, Q_BLOCK, Q_BLOCK), F32),
                   jax.ShapeDtypeStruct((h, 1, nasc), F32)],
        compiler_params=_cparams("arbitrary"),
        name="bias_tables",
    )(brev, basc, rel_bias.T.reshape(h, 1, REL_BUCKETS))
    asc = asc[:, 0, :]
    base = 2 * Q_BLOCK - (CMP_LEN - 1)
    segs = [asc[:, base - CMP_STRIDE * nn:base - CMP_STRIDE * nn + s].reshape(h, qblocks, Q_BLOCK)
            for nn in range(SUBLANES)]
    trc = jnp.flip(jnp.stack(segs, axis=2), axis=1).reshape(h, qblocks * SUBLANES, Q_BLOCK)
    return tb, jnp.pad(trc, ((0, 0), (0, qblocks * SUBLANES), (0, 0)))


def _nsa_cmp_kernel(q_ref, kc_ref, vct_ref, trc_ref, smt_ref, gate_ref, oc_ref, sel_ref, *, qblocks, topn, nb):
    for c in range(nb):
        _nsa_cmp_block(pl.program_id(1) * nb + c, slice(c * Q_BLOCK, (c + 1) * Q_BLOCK), q_ref, kc_ref, vct_ref,
                       trc_ref, smt_ref, gate_ref, oc_ref, sel_ref, qblocks=qblocks, topn=topn)


def _nsa_cmp_block(qb, rows, q_ref, kc_ref, vct_ref, trc_ref, smt_ref, gate_ref, oc_ref, sel_ref, *, qblocks, topn):
    rq = NSA_GROUP * Q_BLOCK
    q = q_ref[:, rows, :].reshape(rq, NSA_HEAD_DIM)
    kc = kc_ref[0]
    nc = kc.shape[0]
    nslc = smt_ref.shape[0]
    s = lax.dot_general(kc, q, _NT, preferred_element_type=F32)
    n_io = lax.broadcasted_iota(jnp.int32, (nc, Q_BLOCK), 0)
    i_io = lax.broadcasted_iota(jnp.int32, (nc, Q_BLOCK), 1)
    mask = (n_io * CMP_STRIDE + (CMP_LEN - 1)) <= (qb * Q_BLOCK + i_io)
    off = pl.multiple_of((qblocks - 1 - qb) * SUBLANES, SUBLANES)
    ps = []
    for r in range(NSA_GROUP):
        l = jnp.where(mask, s[:, r * Q_BLOCK:(r + 1) * Q_BLOCK] + trc_ref[r, pl.ds(off, nc), :], NEG_INF)
        m = jnp.max(l, axis=0, keepdims=True)
        p = jnp.where(mask, jnp.exp2(l - m), 0.0)
        den = jnp.maximum(jnp.sum(p, axis=0, keepdims=True), 1e-30)
        ps.append(p / den)
    p_all = jnp.concatenate(ps, axis=1).astype(BF)
    oc_t = jnp.dot(vct_ref[0], p_all, preferred_element_type=F32)
    imp4 = jnp.dot(smt_ref[...], p_all, preferred_element_type=F32)
    imp = imp4[:, 0:Q_BLOCK]
    for r in range(1, NSA_GROUP):
        imp = imp + imp4[:, r * Q_BLOCK:(r + 1) * Q_BLOCK]

    j_io = lax.broadcasted_iota(jnp.int32, (nslc, Q_BLOCK), 0)
    t_io = qb * Q_BLOCK + lax.broadcasted_iota(jnp.int32, (nslc, Q_BLOCK), 1)
    cur = t_io >> int(math.log2(SLC_LEN))
    forced = jnp.logical_or(j_io == 0, jnp.logical_or(j_io == cur, j_io == cur - 1))
    n_forced = 3
    val = jnp.where(forced, -3e38, jnp.where(j_io <= cur, imp, -1e9))
    sel = jnp.where(forced, 1.0, 0.0)
    for _ in range(max(topn - n_forced, 0)):
        mx = jnp.max(val, axis=0, keepdims=True)
        cand = jnp.where(val == mx, j_io, nslc)
        jmin = jnp.min(cand, axis=0, keepdims=True)
        pick = j_io == jmin
        sel = jnp.where(pick, 1.0, sel)
        val = jnp.where(pick, -3e38, val)
    sel_ref[0, rows, :] = jnp.where(sel.T > 0.5, 0.0, NEG_INF).astype(sel_ref.dtype)

    gates = gate_ref[0, rows, :]
    for r in range(NSA_GROUP):
        o_r = oc_t[:, r * Q_BLOCK:(r + 1) * Q_BLOCK].T
        oc_ref[rows, r * NSA_HEAD_DIM:(r + 1) * NSA_HEAD_DIM] = o_r * gates[:, 3 * r:3 * r + 1]


def _nsa_cmp(q, kc, vct, trc, smt, gates):
    h, s, d = q.shape
    g = NSA_KV_HEADS
    qblocks = s // Q_BLOCK
    nc = kc.shape[1]
    nslc = smt.shape[0]
    topn = min(SLC_TOPN, nslc)
    nb = NSA_BLOCKS_PER_STEP
    qrows = nb * Q_BLOCK
    return pl.pallas_call(
        functools.partial(_nsa_cmp_kernel, qblocks=qblocks, topn=topn, nb=nb),
        grid=(g, qblocks // nb),
        in_specs=[pl.BlockSpec((NSA_GROUP, qrows, d), lambda gg, i: (gg, i, 0)),
                  pl.BlockSpec((1, nc, d), lambda gg, i: (gg, 0, 0)),
                  pl.BlockSpec((1, d, nc), lambda gg, i: (gg, 0, 0)),
                  pl.BlockSpec((NSA_GROUP, trc.shape[1], Q_BLOCK), lambda gg, i: (gg, 0, 0)),
                  pl.BlockSpec(smt.shape, lambda gg, i: (0, 0)),
                  pl.BlockSpec((1, qrows, LANES), lambda gg, i: (gg, i, 0))],
        out_specs=[pl.BlockSpec((qrows, NSA_GROUP * d), lambda gg, i: (i, gg)),
                   pl.BlockSpec((1, qrows, nslc), lambda gg, i: (gg, i, 0))],
        out_shape=[jax.ShapeDtypeStruct((s, h * d), F32),
                   jax.ShapeDtypeStruct((g, s, nslc), BF)],
        compiler_params=_cparams("arbitrary", "arbitrary"),
        name="nsa_cmp",
    )(q, kc, vct, trc, smt, gates)


def _nsa_sw_kernel(q_ref, kv_ref, tb_ref, sel_ref, et_ref, gate_ref, oc_ref,
                   o_ref, s0_ref, s1_ref, m_ref, acc_ref, part_ref, w_ref, *, tk, mc, nch):
    i = pl.program_id(0)
    ngrp = NSA_KV_HEADS
    rq = NSA_GROUP * Q_BLOCK
    nsub = tk // Q_BLOCK
    s_bufs = (s0_ref, s1_ref)
    qbs = [i * nch + c for c in range(nch)]
    qs = [[q_ref[g * NSA_GROUP:(g + 1) * NSA_GROUP, c * Q_BLOCK:(c + 1) * Q_BLOCK, :].reshape(rq, NSA_HEAD_DIM)
           for c in range(nch)] for g in range(ngrp)]

    def bias_tile(g, r, c, kb0, nblk):
        return jnp.concatenate(
            [tb_ref[g * NSA_GROUP + r, jnp.clip(qbs[c] - (kb0 + b), 0, mc)] for b in range(nblk)], axis=1)

    def gated(g, c, o, branch, base):
        gates = gate_ref[g, c * Q_BLOCK:(c + 1) * Q_BLOCK, :]
        outs = []
        for r in range(NSA_GROUP):
            o_r = o[r * Q_BLOCK:(r + 1) * Q_BLOCK, :] * gates[:, 3 * r + branch:3 * r + branch + 1]
            outs.append(base[:, r * NSA_HEAD_DIM:(r + 1) * NSA_HEAD_DIM] + o_r)
        return jnp.concatenate(outs, axis=1)

    _init_state(m_ref, acc_ref)
    qa = [[jnp.concatenate([qs[g][c], jnp.concatenate(
        [sel_ref[g, c * Q_BLOCK:(c + 1) * Q_BLOCK, :]] * NSA_GROUP, axis=0)], axis=1) for c in range(nch)]
        for g in range(ngrp)]

    def pre(g, j, causal):
        off = pl.multiple_of(j * tk, tk)
        ka = jnp.concatenate([kv_ref[g, pl.ds(off, tk), :], et_ref[pl.ds(off, tk), :]], axis=1)
        for c in range(nch):
            s = lax.dot_general(qa[g][c], ka, _NT, preferred_element_type=F32)
            if causal:
                kk = off + lax.broadcasted_iota(jnp.int32, (Q_BLOCK, tk), 1)
                ii = qbs[c] * Q_BLOCK + lax.broadcasted_iota(jnp.int32, (Q_BLOCK, tk), 0)
                future = jnp.where(kk <= ii, 0.0, NEG_INF)
            for r in range(NSA_GROUP):
                bias = bias_tile(g, r, c, j * nsub, nsub)
                if causal:
                    bias = bias + future
                s_bufs[g][c, r * Q_BLOCK:(r + 1) * Q_BLOCK, :] = s[r * Q_BLOCK:(r + 1) * Q_BLOCK, :] + bias

    def process(g, j):
        vp = _with_ones(kv_ref[ngrp + g, pl.ds(pl.multiple_of(j * tk, tk), tk), :])
        for c in range(nch):
            _flash_update(s_bufs[g].at[c], vp, m_ref.at[g * nch + c], acc_ref.at[g * nch + c])

    n_past = (i * nch) // nsub

    def body(t, carry):
        pre(1, t, False)
        process(0, t)
        pre(0, t + 1, False)
        process(1, t)
        return carry

    wk = WINDOW + Q_BLOCK
    wblk = wk // Q_BLOCK

    def win_off(c):
        kb0 = jnp.maximum(qbs[c] - WINDOW // Q_BLOCK, 0)
        return kb0, pl.multiple_of(kb0 * Q_BLOCK, Q_BLOCK)

    def win_pre(g, c):
        kb0, off = win_off(c)
        s = lax.dot_general(qs[g][c], kv_ref[2 * ngrp + g, pl.ds(off, wk), :], _NT, preferred_element_type=F32)
        rel = (qbs[c] * Q_BLOCK + lax.broadcasted_iota(jnp.int32, (Q_BLOCK, wk), 0)) - (
            off + lax.broadcasted_iota(jnp.int32, (Q_BLOCK, wk), 1))
        outside = jnp.where(jnp.logical_and(rel >= 0, rel < WINDOW), 0.0, NEG_INF)
        for r in range(NSA_GROUP):
            w_ref[g * nch + c, r * Q_BLOCK:(r + 1) * Q_BLOCK, :] = (
                s[r * Q_BLOCK:(r + 1) * Q_BLOCK, :] + (bias_tile(g, r, c, kb0, wblk) + outside))

    def win_proc(g, c):
        _, off = win_off(c)
        l_ref = w_ref.at[g * nch + c]
        p = jnp.exp2(l_ref[...] - jnp.max(l_ref[...], axis=1, keepdims=True)).astype(BF)
        ow = jnp.dot(p, _with_ones(kv_ref[3 * ngrp + g, pl.ds(off, wk), :]), preferred_element_type=F32)
        rows = slice(c * Q_BLOCK, (c + 1) * Q_BLOCK)
        cols = slice(g * rq, (g + 1) * rq)
        part_ref[rows, cols] = gated(g, c, _normalized(ow), 2, oc_ref[rows, cols])

    pre(0, n_past, True)
    pre(1, n_past, True)
    win_pre(0, 0)
    win_pre(0, 1)
    process(0, n_past)
    win_pre(1, 0)
    win_pre(1, 1)
    pre(0, 0, False)
    process(1, n_past)
    for g in range(ngrp):
        for c in range(nch):
            win_proc(g, c)

    lax.fori_loop(0, n_past, body, 0)

    for g in range(ngrp):
        cols = slice(g * rq, (g + 1) * rq)
        for c in range(nch):
            rows = slice(c * Q_BLOCK, (c + 1) * Q_BLOCK)
            o_ref[rows, cols] = gated(g, c, _normalized(acc_ref[g * nch + c]), 1,
                                      part_ref[rows, cols]).astype(o_ref.dtype)


def _nsa_sw(q, kvsw, tb, sel, et, gates, oc):
    h, s, d = q.shape
    g = NSA_KV_HEADS
    nch = NSA_BLOCKS_PER_STEP
    qrows = nch * Q_BLOCK
    tk = min(NSA_KEY_TILE, s)
    mc = tb.shape[1] - 1
    nslc = sel.shape[2]
    rq = NSA_GROUP * Q_BLOCK
    assert tk % qrows == 0 and WINDOW + Q_BLOCK <= s and NSA_GROUP * d == rq

    def resident(arr):
        return pl.BlockSpec(arr.shape, lambda i, nd=arr.ndim: (0,) * nd, pipeline_mode=pl.Buffered(1))

    return pl.pallas_call(
        functools.partial(_nsa_sw_kernel, tk=tk, mc=mc, nch=nch),
        grid=(s // qrows,),
        in_specs=[pl.BlockSpec((h, qrows, d), lambda i: (0, i, 0)),
                  resident(kvsw), resident(tb),
                  pl.BlockSpec((g, qrows, nslc), lambda i: (0, i, 0)),
                  resident(et),
                  pl.BlockSpec((g, qrows, LANES), lambda i: (0, i, 0)),
                  pl.BlockSpec((qrows, h * d), lambda i: (i, 0))],
        out_specs=pl.BlockSpec((qrows, h * d), lambda i: (i, 0)),
        out_shape=jax.ShapeDtypeStruct((s, h * d), BF),
        scratch_shapes=[pltpu.VMEM((nch, rq, tk), F32), pltpu.VMEM((nch, rq, tk), F32),
                        pltpu.VMEM((g * nch, rq, 1), F32), pltpu.VMEM((g * nch, rq, 2 * LANES), F32),
                        pltpu.VMEM((qrows, h * d), F32),
                        pltpu.VMEM((g * nch, rq, WINDOW + Q_BLOCK), F32)],
        compiler_params=_cparams("arbitrary"),
        name="nsa_sel_win",
    )(q, kvsw, tb, sel, et, gates, oc)


def _selection_map_t(nc, nslc):
    n = np.arange(nc)[None, :] * CMP_STRIDE
    j0 = np.arange(nslc)[:, None] * SLC_LEN
    valid = np.arange(nc)[None, :] < nc - 1
    return jnp.asarray(((n < j0 + SLC_LEN) & (n + CMP_LEN > j0) & valid).astype(np.float32), dtype=BF)


def _block_onehot(s, nslc):
    tok = np.arange(s)[:, None]
    j = np.arange(nslc)[None, :]
    return jnp.asarray((tok // SLC_LEN == j).astype(np.float32), dtype=BF)


def _rope_tables(positions):
    half = MLA_ROPE // 2
    inv = ROPE_THETA ** (-jnp.arange(half, dtype=F32) / half)
    ang = positions.astype(F32)[:, None] * inv
    cos, sin = jnp.cos(ang), jnp.sin(ang)
    z = jnp.zeros_like(cos)
    zpad = jnp.zeros((positions.shape[0], LANES - MLA_ROPE), F32)
    c = jnp.concatenate([cos, cos, zpad], axis=1)
    s1 = jnp.concatenate([-sin, z, zpad], axis=1)
    s2 = jnp.concatenate([z, sin, zpad], axis=1)
    return c, s1, s2


def _pad_cols(w, n):
    return jnp.pad(w, ((0, 0), (0, n - w.shape[1])))


def _pad_rows(w, n):
    return jnp.pad(w, ((0, n - w.shape[0]), (0, 0)))


def _nsa_mla_mixer(hn, pos, rel_bias, w_in_all, w_out_all, e, gate_b, pos_k, w1_k, w2_k, pos_v, w1_v, w2_v,
                   q_norm, w_uq, kv_norm, w_ukv, x, g1):
    s = hn.shape[0]
    d = NSA_HEAD_DIM
    nq = NSA_HEADS * d
    nkv = 2 * NSA_KV_HEADS * d
    w_in_t = jnp.swapaxes(w_in_all, 1, 2)
    dm = w_in_t.shape[2]
    o0 = nq + 3 * nkv
    w_g_t = w_in_t[e, o0:o0 + 3 * NSA_HEADS]; o0 += 3 * NSA_HEADS
    w_lat_t = w_in_t[e, o0:o0 + MLA_Q_RANK + MLA_KV_RANK]; o0 += MLA_Q_RANK + MLA_KV_RANK
    w_kr_t = w_in_t[e, o0:o0 + MLA_ROPE]

    q_nsa = _proj([hn], [(w_in_t, dm, (e, 0), 0)], functools.partial(_epi_scale, d ** -0.5 * LOG2E), n=nq,
                  out_dtype=BF, head_major=True, tn=PROJ_WIDE_COLS, w_transposed=True, name="proj_q_nsa")
    kv_cmp = _proj([hn], [(w_in_t, dm, (e, 0), nq)], _epi_id, n=nkv, out_dtype=F32, head_major=True,
                   w_transposed=True, name="proj_kv_cmp")
    kv_sw = _proj([hn], [(w_in_t, dm, (e, 0), nq + nkv)], _epi_id, n=2 * nkv, out_dtype=BF, head_major=True,
                  w_transposed=True, name="proj_kv_sw")
    per_g = 3 * NSA_GROUP
    w_gp_t = jnp.concatenate([_pad_rows(w_g_t[g * per_g:(g + 1) * per_g], LANES) for g in range(NSA_KV_HEADS)], 0)
    b_gp = jnp.concatenate([_pad_cols(gate_b[None, g * per_g:(g + 1) * per_g], LANES)
                            for g in range(NSA_KV_HEADS)], 1)
    gates = _proj([hn], [(w_gp_t[None], dm, (0, 0), 0)], _epi_sigmoid_bias, [(b_gp, "row")],
                  n=NSA_KV_HEADS * LANES, out_dtype=F32, head_major=True, w_transposed=True,
                  name="proj_gates")

    nc = s // CMP_STRIDE
    gw = CMP_STRIDE * d
    pos_kv = jnp.stack([pos_k.reshape(2, gw), pos_v.reshape(2, gw)])
    w1_kv = jnp.stack([w1_k.reshape(2, gw, CMP_HIDDEN), w1_v.reshape(2, gw, CMP_HIDDEN)]).astype(BF)
    w2_kv = jnp.stack([w2_k, w2_v]).astype(BF)
    kvc = _nsa_compress(kv_cmp.reshape(2 * NSA_KV_HEADS, nc, gw), pos_kv, w1_kv, w2_kv)
    kc = kvc[:NSA_KV_HEADS]
    vct = jnp.swapaxes(kvc[NSA_KV_HEADS:], 1, 2)

    nslc = s // SLC_LEN
    tb, trc = _bias_tables(rel_bias, s)
    smt = _selection_map_t(nc, nslc)
    oc, sel = _nsa_cmp(q_nsa, kc, vct, trc, smt, gates)
    o_nsa = _nsa_sw(q_nsa, kv_sw, tb, sel, _block_onehot(s, nslc), gates, oc)

    nlat = MLA_Q_RANK + MLA_KV_RANK
    lat = _proj([hn], [(w_lat_t[None], dm, (0, 0), 0)], _epi_rmsnorm,
                [(jnp.concatenate([q_norm, kv_norm])[None, :], "row")], n=nlat, out_dtype=BF, tn=MLA_Q_RANK,
                w_transposed=True, name="proj_mla_latent")
    c, s1, s2 = _rope_tables(pos)
    kr = _proj([hn], [(_pad_rows(w_kr_t, LANES)[None], dm, (0, 0), 0)], functools.partial(_epi_rope, 1.0),
               [(c, "rowtile"), (s1, "rowtile"), (s2, "rowtile")], n=LANES, out_dtype=BF, w_transposed=True,
               name="proj_mla_kr")
    scale = (MLA_NOPE + MLA_ROPE) ** -0.5 * LOG2E
    w_uq3 = w_uq.reshape(MLA_Q_RANK, MLA_HEADS, MLA_NOPE + MLA_ROPE)
    w_uqn = w_uq3[:, :, :MLA_NOPE].reshape(MLA_Q_RANK, MLA_HEADS * MLA_NOPE)
    w_uqr = jnp.pad(w_uq3[:, :, MLA_NOPE:], ((0, 0), (0, 0), (0, LANES - MLA_ROPE))).reshape(
        MLA_Q_RANK, MLA_HEADS * LANES)
    qn = _proj([lat], [(w_uqn, MLA_Q_RANK, 0, 0)], functools.partial(_epi_scale, scale),
               n=MLA_HEADS * MLA_NOPE, out_dtype=BF, head_major=True, tn=PROJ_WIDE_COLS, lhs_col_block=[0],
               name="proj_mla_qn")
    qr = _proj([lat], [(w_uqr, MLA_Q_RANK, 0, 0)], functools.partial(_epi_rope, scale),
               [(c, "rowtile"), (s1, "rowtile"), (s2, "rowtile")], n=MLA_HEADS * LANES, out_dtype=BF,
               head_major=True, lhs_col_block=[0], name="proj_mla_qr")
    kvh = _proj([lat], [(w_ukv, MLA_KV_RANK, 0, 0)], _epi_id, n=MLA_HEADS * (MLA_NOPE + MLA_V), out_dtype=BF,
                head_major=True, tn=PROJ_WIDE_COLS, lhs_col_block=[1], name="proj_mla_kv")
    o_mla = _mla_attention(qn, qr, kvh, kr)

    return _proj([o_nsa, o_mla], [(w_out_all, nq, (e, 0), 0), (w_out_all, MLA_HEADS * MLA_V, (e, 1), 0)],
                 _epi_residual, [(x, "tile"), (g1, "row")], n=w_out_all.shape[2], out_dtype=F32, tn=PROJ_WIDE_COLS,
                 name="proj_even_out")


def _fox_mixer(hn, w_in_all, w_out_all, o, f_b, q_norm, k_norm, x, g1):
    d = D_MODEL
    dh = FOX_HEAD_DIM
    w_in_t = jnp.swapaxes(w_in_all, 1, 2)
    w_f_t = w_in_t[o, 3 * d:3 * d + FOX_HEADS]
    w_og_t = w_in_t[o, 3 * d + FOX_HEADS:]
    q = _proj([hn], [(w_in_t, d, (o, 0), 0)], functools.partial(_epi_headnorm, dh ** -0.5 * LOG2E),
              [(q_norm[None, :], "const")], n=d, out_dtype=BF, head_major=True, tn=PROJ_WIDE_COLS, w_transposed=True,
              name="proj_fox_q")
    k = _proj([hn], [(w_in_t, d, (o, 0), d)], functools.partial(_epi_headnorm, 1.0),
              [(k_norm[None, :], "const")], n=d, out_dtype=BF, head_major=True, tn=PROJ_WIDE_COLS, w_transposed=True,
              name="proj_fox_k")
    v = _proj([hn], [(w_in_t, d, (o, 0), 2 * d)], _epi_id, n=d, out_dtype=BF, head_major=True, tn=PROJ_WIDE_COLS,
              w_transposed=True, name="proj_fox_v")
    lf = _proj([hn], [(_pad_rows(w_f_t, LANES)[None], d, (0, 0), 0)], _epi_logsigmoid_bias,
               [(_pad_cols(f_b[None, :], LANES), "row")], n=LANES, out_dtype=F32, w_transposed=True,
               name="proj_fox_f")
    sig_og = _proj([hn], [(w_og_t[None], d, (0, 0), 0)], _epi_sigmoid, n=d, out_dtype=F32, tn=PROJ_WIDE_COLS,
                   w_transposed=True, name="proj_fox_og")
    cum, cum_t = _cumsum_tokens(lf, LOG2E)
    cum_t = cum_t[:FOX_HEADS].reshape(FOX_HEADS, 1, -1)
    att = _fox_attention(q, k, v, cum_t, cum, sig_og)
    return _proj([att], [(w_out_all, d, (o, 0), 0)], _epi_residual, [(x, "tile"), (g1, "row")], n=d,
                 out_dtype=F32, tn=PROJ_WIDE_COLS, name="proj_fox_out")


def kernel(x, c, positions, rel_bias, ada_w, ada_b, norm_mix, norm_ffn, ffn_w1, ffn_w3, ffn_w2, even_w_in, even_w_out, nsa_gate_b, nsa_cmp_pos_k, nsa_cmp_w1_k, nsa_cmp_w2_k, nsa_cmp_pos_v, nsa_cmp_w1_v, nsa_cmp_w2_v, mla_q_norm, mla_w_uq, mla_kv_norm, mla_w_ukv, fox_w_in, fox_w_out, fox_f_b, fox_q_norm, fox_k_norm, final_norm):
    b, s, d = x.shape
    assert b == 1 and d == D_MODEL and s % 1024 == 0
    xs = x[0]
    pos = positions[0]
    mod = _adaln(c, ada_w, ada_b)
    depth = ada_w.shape[0]
    hn = None
    for i in range(depth):
        sh1, sc1, g1, sh2, sc2, g2 = [mod[i:i + 1, k * d:(k + 1) * d] for k in range(6)]
        if hn is None:
            hn = _normmod(xs, norm_mix[i][None, :], sc1, sh1)
        if i % 2 == 0:
            e = i // 2
            xs = _nsa_mla_mixer(hn, pos, rel_bias, even_w_in, even_w_out, e, nsa_gate_b[e],
                                nsa_cmp_pos_k[e], nsa_cmp_w1_k[e], nsa_cmp_w2_k[e],
                                nsa_cmp_pos_v[e], nsa_cmp_w1_v[e], nsa_cmp_w2_v[e],
                                mla_q_norm[e], mla_w_uq[e], mla_kv_norm[e], mla_w_ukv[e], xs, g1)
        else:
            o = i // 2
            xs = _fox_mixer(hn, fox_w_in, fox_w_out, o, fox_f_b[o], fox_q_norm[o], fox_k_norm[o], xs, g1)
        if i == depth - 1:
            xs = _ffn(xs, norm_ffn[i][None, :], sc2, sh2, g2, ffn_w1, ffn_w3, ffn_w2, i,
                      final_gain=final_norm[None, :])
        else:
            nxt = (norm_mix[i + 1][None, :], mod[i + 1:i + 2, d:2 * d], mod[i + 1:i + 2, 0:d])
            xs, hn = _ffn(xs, norm_ffn[i][None, :], sc2, sh2, g2, ffn_w1, ffn_w3, ffn_w2, i, next_mod=nxt)
    return xs[None]
```

```python
import functools
import math

import numpy as np
import jax
import jax.numpy as jnp
from jax import lax
from jax.experimental import pallas as pl
from jax.experimental.pallas import tpu as pltpu

D_MODEL = 2048
DEPTH = 2
EPS = 1e-6
NEG_INF = -1e30

NSA_HEADS = 8
NSA_KV_HEADS = 2
NSA_GROUP = NSA_HEADS // NSA_KV_HEADS
NSA_HEAD_DIM = 128
CMP_LEN = 32
CMP_STRIDE = 16
CMP_HIDDEN = 256
SLC_LEN = 64
SLC_TOPN = 16
WINDOW = 512
Q_BLOCK = 128

MLA_HEADS = 8
MLA_Q_RANK = 512
MLA_KV_RANK = 512
MLA_NOPE = 128
MLA_ROPE = 64
MLA_V = 128
ROPE_THETA = 10000.0

FOX_HEADS = 16
FOX_HEAD_DIM = D_MODEL // FOX_HEADS

REL_BUCKETS = 32
REL_MAX_DIST = 4096

FFN_HIDDEN = ((8 * D_MODEL + 2) // 3 + 255) // 256 * 256

LANES = 128
SUBLANES = 8
VMEM_LIMIT_BYTES = 56 * 1024 * 1024
LOG2E = math.log2(math.e)

ADALN_COLS = 1024
NORM_ROWS = 512
PROJ_ROWS = 1024
PROJ_COLS = 512
PROJ_WIDE_COLS = 1024
FFN_ROWS = 1024
FFN_HIDDEN_COLS = 256
ATTN_CHAIN_ROWS = 512
NSA_KEY_TILE = 512
NSA_BLOCKS_PER_STEP = 2
NSA_CMP_BLOCKS_PER_STEP = 4
CUMSUM_ROWS = 256

BF = jnp.bfloat16
F32 = jnp.float32
_NT = (((1,), (1,)), ((), ()))


def _cparams(*sem):
    return pltpu.CompilerParams(dimension_semantics=sem, vmem_limit_bytes=VMEM_LIMIT_BYTES)


def _rms(x):
    return x * lax.rsqrt(jnp.mean(x * x, axis=-1, keepdims=True) + EPS)


def _adaln_kernel(c_ref, w_ref, b_ref, o_ref):
    c = c_ref[...]
    cond = c * jax.nn.sigmoid(c)
    acc = jnp.dot(cond.astype(BF), w_ref[0].astype(BF), preferred_element_type=F32)
    o_ref[0] = acc + b_ref[0]


def _adaln(c, ada_w, ada_b):
    depth, d, n = ada_w.shape
    tn = ADALN_COLS
    c8 = jnp.broadcast_to(c, (SUBLANES, d))
    out = pl.pallas_call(
        _adaln_kernel,
        grid=(depth, n // tn),
        in_specs=[pl.BlockSpec((SUBLANES, d), lambda i, j: (0, 0)),
                  pl.BlockSpec((1, d, tn), lambda i, j: (i, 0, j)),
                  pl.BlockSpec((1, 1, tn), lambda i, j: (i, 0, j))],
        out_specs=pl.BlockSpec((1, SUBLANES, tn), lambda i, j: (i, 0, j)),
        out_shape=jax.ShapeDtypeStruct((depth, SUBLANES, n), F32),
        compiler_params=_cparams("arbitrary", "arbitrary"),
        name="adaln",
    )(c8, ada_w, ada_b.reshape(depth, 1, n))
    return out[:, 0, :]


def _normmod_kernel(x_ref, g_ref, sc_ref, sh_ref, o_ref):
    y = _rms(x_ref[...])
    o_ref[...] = ((y * g_ref[...]) * (1.0 + sc_ref[...]) + sh_ref[...]).astype(o_ref.dtype)


def _normmod(x, g, sc, sh):
    s, d = x.shape
    tm = NORM_ROWS
    row = pl.BlockSpec((1, d), lambda i: (0, 0))
    return pl.pallas_call(
        _normmod_kernel,
        grid=(s // tm,),
        in_specs=[pl.BlockSpec((tm, d), lambda i: (i, 0)), row, row, row],
        out_specs=pl.BlockSpec((tm, d), lambda i: (i, 0)),
        out_shape=jax.ShapeDtypeStruct((s, d), BF),
        compiler_params=_cparams("arbitrary"),
        name="normmod",
    )(x, g, sc, sh)


def _proj_kernel(*refs, n_lhs, n_epi, epi, head_major, w_transposed):
    lhs = refs[:n_lhs]
    ws = refs[n_lhs:2 * n_lhs]
    epis = refs[2 * n_lhs:2 * n_lhs + n_epi]
    o_ref = refs[2 * n_lhs + n_epi]
    wbf = refs[2 * n_lhs + n_epi + 1:]

    @pl.when(pl.program_id(1) == 0)
    def _():
        for w, wb in zip(ws, wbf):
            wb[...] = w[...].astype(BF)

    acc = None
    for a, wb in zip(lhs, wbf):
        if w_transposed:
            d = lax.dot_general(a[...], wb[...], _NT, preferred_element_type=F32)
        else:
            d = jnp.dot(a[...], wb[...], preferred_element_type=F32)
        acc = d if acc is None else acc + d
    res = epi(acc, *[e[...] for e in epis])
    if head_major:
        for r in range(o_ref.shape[0]):
            o_ref[r] = res[:, r * LANES:(r + 1) * LANES].astype(o_ref.dtype)
    else:
        o_ref[...] = res.astype(o_ref.dtype)


def _proj(lhs, ws, epi, epi_in=(), *, n, out_dtype, head_major=False, tm=PROJ_ROWS, tn=PROJ_COLS,
          lhs_col_block=None, w_transposed=False, name="proj"):
    m = lhs[0].shape[0]
    tm = min(tm, m)
    tn = min(tn, n)
    if lhs_col_block is None:
        lhs_col_block = [0] * len(lhs)
    in_specs = []
    for (_, k, _, _), cb in zip(ws, lhs_col_block):
        in_specs.append(pl.BlockSpec((tm, k), lambda j, i, cb=cb: (i, cb)))
    for arr, k, rb, col0 in ws:
        assert col0 % tn == 0
        if w_transposed:
            in_specs.append(pl.BlockSpec((None, tn, k),
                                         lambda j, i, rb=rb, cb0=col0 // tn: (rb[0], cb0 + j, rb[1])))
        elif arr.ndim == 3:
            in_specs.append(pl.BlockSpec((None, k, tn),
                                         lambda j, i, rb=rb, cb0=col0 // tn: (rb[0], rb[1], cb0 + j)))
        else:
            in_specs.append(pl.BlockSpec((k, tn), lambda j, i, rb=rb, cb0=col0 // tn: (rb, cb0 + j)))
    arrays = list(lhs) + [w[0] for w in ws]
    for arr, kind in epi_in:
        if kind == "row":
            in_specs.append(pl.BlockSpec((1, tn), lambda j, i: (0, j)))
        elif kind == "const":
            in_specs.append(pl.BlockSpec(arr.shape, lambda j, i: (0, 0)))
        elif kind == "tile":
            in_specs.append(pl.BlockSpec((tm, tn), lambda j, i: (i, j)))
        elif kind == "rowtile":
            in_specs.append(pl.BlockSpec((tm, arr.shape[1]), lambda j, i: (i, 0)))
        else:
            raise ValueError(kind)
        arrays.append(arr)
    if head_major:
        hpt = tn // LANES
        out_spec = pl.BlockSpec((hpt, tm, LANES), lambda j, i: (j, i, 0))
        out_shape = jax.ShapeDtypeStruct((n // LANES, m, LANES), out_dtype)
    else:
        out_spec = pl.BlockSpec((tm, tn), lambda j, i: (i, j))
        out_shape = jax.ShapeDtypeStruct((m, n), out_dtype)
    kern = functools.partial(_proj_kernel, n_lhs=len(lhs), n_epi=len(epi_in), epi=epi,
                             head_major=head_major, w_transposed=w_transposed)
    return pl.pallas_call(
        kern,
        grid=(n // tn, m // tm),
        in_specs=in_specs,
        out_specs=out_spec,
        out_shape=out_shape,
        scratch_shapes=[pltpu.VMEM((tn, k) if w_transposed else (k, tn), BF) for _, k, _, _ in ws],
        compiler_params=_cparams("arbitrary", "arbitrary"),
        name=name,
    )(*arrays)


def _epi_id(acc):
    return acc


def _epi_scale(scale, acc):
    return acc * scale


def _epi_sigmoid_bias(acc, b):
    return jax.nn.sigmoid(acc + b)


def _epi_logsigmoid_bias(acc, b):
    return jax.nn.log_sigmoid(acc + b)


def _epi_sigmoid(acc):
    return jax.nn.sigmoid(acc)


def _epi_rmsnorm(acc, g):
    return _rms(acc) * g


def _epi_headnorm(scale, acc, g):
    outs = []
    for r in range(acc.shape[1] // LANES):
        outs.append(_rms(acc[:, r * LANES:(r + 1) * LANES]) * g * scale)
    return jnp.concatenate(outs, axis=1)


def _epi_rope(scale, acc, c, s1, s2):
    reps = acc.shape[1] // LANES
    half = MLA_ROPE // 2
    if reps > 1:
        c = jnp.concatenate([c] * reps, axis=1)
        s1 = jnp.concatenate([s1] * reps, axis=1)
        s2 = jnp.concatenate([s2] * reps, axis=1)
    n = acc.shape[1]
    out = acc * c + pltpu.roll(acc, n - half, 1) * s1 + pltpu.roll(acc, half, 1) * s2
    return out * scale


def _epi_residual(acc, x, g):
    return x + g * acc


def _ffn_kernel(*refs, mode):
    x_ref, g_ref, sc_ref, sh_ref, g2_ref, w1_ref, w3_ref, w2_ref = refs[:8]
    if mode == "final":
        fn_ref, o_ref, hn_ref = refs[8:]
    elif mode == "next":
        gn_ref, scn_ref, shn_ref, o_ref, hnext_ref, hn_ref = refs[8:]
    else:
        o_ref, hn_ref = refs[8:]
    f = pl.program_id(1)
    tm = x_ref.shape[0]
    halves = [slice(r * (tm // 2), (r + 1) * (tm // 2)) for r in range(2)]

    @pl.when(f == 0)
    def _():
        for rows in halves:
            x = x_ref[rows, :]
            hn_ref[rows, :] = ((_rms(x) * g_ref[...]) * (1.0 + sc_ref[...]) + sh_ref[...]).astype(BF)
            o_ref[rows, :] = x

    w1 = w1_ref[...].astype(BF)
    w3 = w3_ref[...].astype(BF)
    w2 = w2_ref[...].astype(BF)
    for rows in halves:
        h = hn_ref[rows, :]
        h1 = jnp.dot(h, w1, preferred_element_type=F32)
        h3 = jnp.dot(h, w3, preferred_element_type=F32)
        a = (h1 * jax.nn.sigmoid(h1)) * h3
        o_ref[rows, :] += g2_ref[...] * jnp.dot(a.astype(BF), w2, preferred_element_type=F32)

    if mode != "plain":
        @pl.when(f == pl.num_programs(1) - 1)
        def _():
            for rows in halves:
                y = _rms(o_ref[rows, :])
                if mode == "final":
                    o_ref[rows, :] = y * fn_ref[...]
                else:
                    hnext_ref[rows, :] = ((y * gn_ref[...]) * (1.0 + scn_ref[...]) + shn_ref[...]).astype(BF)


def _ffn(x, g, sc, sh, g2, w1, w3, w2, layer, final_gain=None, next_mod=None):
    assert final_gain is None or next_mod is None
    s, d = x.shape
    fdim = w1.shape[2]
    tm = min(FFN_ROWS, s)
    tf = FFN_HIDDEN_COLS
    row = pl.BlockSpec((1, d), lambda i, f: (0, 0))
    tile = pl.BlockSpec((tm, d), lambda i, f: (i, 0))
    in_specs = [pl.BlockSpec((tm, d), lambda i, f: (i, 0), pipeline_mode=pl.Buffered(1)), row, row, row, row,
                pl.BlockSpec((None, d, tf), lambda i, f: (layer, 0, f)),
                pl.BlockSpec((None, d, tf), lambda i, f: (layer, 0, f)),
                pl.BlockSpec((None, tf, d), lambda i, f: (layer, f, 0))]
    arrays = [x, g, sc, sh, g2, w1, w3, w2]
    out_specs, out_shape, mode = tile, jax.ShapeDtypeStruct((s, d), F32), "plain"
    if final_gain is not None:
        in_specs.append(row)
        arrays.append(final_gain)
        mode = "final"
    elif next_mod is not None:
        in_specs += [row, row, row]
        arrays += list(next_mod)
        out_specs = [tile, tile]
        out_shape = [out_shape, jax.ShapeDtypeStruct((s, d), BF)]
        mode = "next"
    return pl.pallas_call(
        functools.partial(_ffn_kernel, mode=mode),
        grid=(s // tm, fdim // tf),
        in_specs=in_specs,
        out_specs=out_specs,
        out_shape=out_shape,
        scratch_shapes=[pltpu.VMEM((tm, d), BF)],
        compiler_params=_cparams("arbitrary", "arbitrary"),
        name="ffn",
    )(*arrays)


def _flash_update(s, vp, m_ref, acc_ref, row_shift=None):
    m_prev = m_ref[...]
    m_tile = jnp.max(s[...], axis=1, keepdims=True)
    if row_shift is not None:
        m_tile = m_tile + row_shift
    m_new = jnp.maximum(m_prev, m_tile)
    alpha = jnp.exp2(m_prev - m_new)
    sub = m_new if row_shift is None else m_new - row_shift
    p = jnp.exp2(s[...] - sub).astype(BF)
    acc_ref[...] = alpha * acc_ref[...] + jnp.dot(p, vp, preferred_element_type=F32)
    m_ref[...] = m_new


def _init_state(m_ref, acc_ref):
    m_ref[...] = jnp.full_like(m_ref, NEG_INF)
    acc_ref[...] = jnp.zeros_like(acc_ref)


def _with_ones(v):
    return jnp.concatenate([v, jnp.ones(v.shape, v.dtype)], axis=1)


def _normalized(acc):
    return acc[:, :LANES] / acc[:, LANES:]


def _causal_mask(t):
    rows = lax.broadcasted_iota(jnp.int32, (t, t), 0)
    cols = lax.broadcasted_iota(jnp.int32, (t, t), 1)
    return cols <= rows


def _causal_sweep(qi, tc, s_bufs, m_ref, acc_ref, streams, nsplit):
    tw = 2 * tc

    doff = pl.multiple_of(qi * tw, tw)

    psplit = 2
    rs = tc // psplit
    pieces = [(c, slice(hh * rs, (hh + 1) * rs)) for c in range(2) for hh in range(psplit)]
    dwidth = (tc, tw)

    def pre_diag(si, ksides, c, rows):
        _, logits_fn, _, _ = streams[si]
        r_io = c * tc + rows.start + lax.broadcasted_iota(jnp.int32, (rs, dwidth[c]), 0)
        c_io = lax.broadcasted_iota(jnp.int32, (rs, dwidth[c]), 1)
        s_bufs[si][c, rows, 0:dwidth[c]] = jnp.where(c_io <= r_io, logits_fn(c, ksides[c], rows), NEG_INF)

    def process_diag(si, vps, c, rows):
        shifts = streams[si][3]
        _flash_update(s_bufs[si].at[c, rows, 0:dwidth[c]], vps[c], m_ref.at[2 * si + c, rows, :],
                      acc_ref.at[2 * si + c, rows, :], row_shift=None if shifts[c] is None else shifts[c][rows, :])

    dks = [[streams[si][0](doff, w) for w in dwidth] for si in range(2)]
    dvs = [[streams[si][2](doff, w) for w in dwidth] for si in range(2)]
    for c, rows in pieces:
        pre_diag(0, dks[0], c, rows)
    for c, rows in pieces:
        pre_diag(1, dks[1], c, rows)
        process_diag(0, dvs[0], c, rows)
    kside0 = streams[0][0](0, tw)
    for c, rows in pieces:
        s_bufs[0][c, rows, :] = streams[0][1](c, kside0, rows)
        process_diag(1, dvs[1], c, rows)

    def half(sp, tp, sq, tq_):
        kside_fn, logits_fn, _, _ = streams[sp]
        _, _, vp_fn, shifts = streams[sq]
        kside = kside_fn(pl.multiple_of(tp * tw, tw), tw)
        vp = vp_fn(pl.multiple_of(tq_ * tw, tw), tw)
        rs = tc // nsplit
        for c in range(2):
            for hh in range(nsplit):
                rows = slice(hh * rs, (hh + 1) * rs)
                s_bufs[sp][c, rows, :] = logits_fn(c, kside, rows)
                _flash_update(s_bufs[sq].at[c, rows, :], vp, m_ref.at[2 * sq + c, rows, :],
                              acc_ref.at[2 * sq + c, rows, :],
                              row_shift=None if shifts[c] is None else shifts[c][rows, :])

    def body(t, carry):
        half(1, t, 0, t)
        half(0, t + 1, 1, t)
        return carry

    lax.fori_loop(0, qi, body, 0)


def _sweep_scratch(tc):
    return [pltpu.VMEM((2, tc, 2 * tc), F32), pltpu.VMEM((2, tc, 2 * tc), F32),
            pltpu.VMEM((4, tc, 1), F32), pltpu.VMEM((4, tc, 2 * LANES), F32)]


def _mla_kernel(qn_ref, qr_ref, kv_ref, kr_ref, o_ref, s0_ref, s1_ref, m_ref, acc_ref, *, tc):
    qi = pl.program_id(1)
    _init_state(m_ref, acc_ref)

    def stream(si):
        qs = [jnp.concatenate([qn_ref[si, c * tc:(c + 1) * tc, :], qr_ref[si, c * tc:(c + 1) * tc, :]], axis=1)
              for c in range(2)]

        def kside(off, width):
            return jnp.concatenate([kv_ref[2 * si, pl.ds(off, width), :], kr_ref[pl.ds(off, width), :]], axis=1)

        def logits(c, k, rows=slice(None)):
            return lax.dot_general(qs[c][rows, :], k, _NT, preferred_element_type=F32)

        def vp(off, width):
            return _with_ones(kv_ref[2 * si + 1, pl.ds(off, width), :])

        return kside, logits, vp, (None, None)

    _causal_sweep(qi, tc, (s0_ref, s1_ref), m_ref, acc_ref, [stream(0), stream(1)], nsplit=2)
    for si in range(2):
        for c in range(2):
            o_ref[c * tc:(c + 1) * tc, si * LANES:(si + 1) * LANES] = _normalized(
                acc_ref[2 * si + c]).astype(o_ref.dtype)


def _mla_attention(qn, qr, kvh, kr):
    h, s, _ = qn.shape
    tc = min(ATTN_CHAIN_ROWS, s // 2)
    w = 2 * tc
    return pl.pallas_call(
        functools.partial(_mla_kernel, tc=tc),
        grid=(h // 2, s // w),
        in_specs=[pl.BlockSpec((2, w, LANES), lambda p, i: (p, i, 0)),
                  pl.BlockSpec((2, w, LANES), lambda p, i: (p, i, 0)),
                  pl.BlockSpec((4, s, LANES), lambda p, i: (p, 0, 0)),
                  pl.BlockSpec((s, LANES), lambda p, i: (0, 0))],
        out_specs=pl.BlockSpec((w, 2 * LANES), lambda p, i: (i, p)),
        out_shape=jax.ShapeDtypeStruct((s, h * LANES), BF),
        scratch_shapes=_sweep_scratch(tc),
        compiler_params=_cparams("arbitrary", "arbitrary"),
        name="mla_attn",
    )(qn, qr, kvh, kr)


def _fox_kernel(q_ref, k_ref, v_ref, ck_ref, cum_ref, og_ref, o_ref, s0_ref, s1_ref, m_ref, acc_ref, *, tc):
    pair = pl.program_id(0)
    qi = pl.program_id(1)
    lane = lax.broadcasted_iota(jnp.int32, (tc, LANES), 1)
    _init_state(m_ref, acc_ref)

    def stream(si):
        qs = [q_ref[si, c * tc:(c + 1) * tc, :] for c in range(2)]
        cqs = [jnp.sum(jnp.where(lane == 2 * pair + si, cum_ref[c * tc:(c + 1) * tc, :], 0.0), axis=1,
                       keepdims=True) for c in range(2)]

        def kside(off, width):
            return k_ref[si, pl.ds(off, width), :], ck_ref[si, :, pl.ds(off, width)]

        def logits(c, kc, rows=slice(None)):
            return lax.dot_general(qs[c][rows, :], kc[0], _NT, preferred_element_type=F32) - kc[1]

        def vp(off, width):
            return _with_ones(v_ref[si, pl.ds(off, width), :])

        return kside, logits, vp, cqs

    _causal_sweep(qi, tc, (s0_ref, s1_ref), m_ref, acc_ref, [stream(0), stream(1)], nsplit=4)
    for si in range(2):
        cols = slice(si * LANES, (si + 1) * LANES)
        for c in range(2):
            rows = slice(c * tc, (c + 1) * tc)
            o_ref[rows, cols] = (_normalized(acc_ref[2 * si + c]) * og_ref[rows, cols]).astype(o_ref.dtype)


def _fox_attention(q, k, v, cum_t, cum, sig_og):
    h, s, _ = q.shape
    tc = min(ATTN_CHAIN_ROWS, s // 2)
    w = 2 * tc
    hm = pl.BlockSpec((2, s, LANES), lambda p, i: (p, 0, 0))
    return pl.pallas_call(
        functools.partial(_fox_kernel, tc=tc),
        grid=(h // 2, s // w),
        in_specs=[pl.BlockSpec((2, w, LANES), lambda p, i: (p, i, 0)), hm, hm,
                  pl.BlockSpec((2, 1, s), lambda p, i: (p, 0, 0)),
                  pl.BlockSpec((w, LANES), lambda p, i: (i, 0)),
                  pl.BlockSpec((w, 2 * LANES), lambda p, i: (i, p))],
        out_specs=pl.BlockSpec((w, 2 * LANES), lambda p, i: (i, p)),
        out_shape=jax.ShapeDtypeStruct((s, h * LANES), BF),
        scratch_shapes=_sweep_scratch(tc),
        compiler_params=_cparams("arbitrary", "arbitrary"),
        name="fox_attn",
    )(q, k, v, cum_t, cum, sig_og)


def _cumsum_kernel(x_ref, o_ref, ot_ref, carry_ref, *, t, out_scale):
    @pl.when(pl.program_id(0) == 0)
    def _():
        carry_ref[...] = jnp.zeros_like(carry_ref)

    x = x_ref[...]
    rows = lax.broadcasted_iota(jnp.int32, (t, t), 0)
    cols = lax.broadcasted_iota(jnp.int32, (t, t), 1)
    tri = jnp.where(cols <= rows, 1.0, 0.0).astype(BF)
    hi = x.astype(BF)
    r1 = x - hi.astype(F32)
    mid = r1.astype(BF)
    lo = (r1 - mid.astype(F32)).astype(BF)
    cum = (jnp.dot(tri, hi, preferred_element_type=F32) + jnp.dot(tri, mid, preferred_element_type=F32)
           + jnp.dot(tri, lo, preferred_element_type=F32)) + carry_ref[...]
    scaled = cum * out_scale
    o_ref[...] = scaled
    ot_ref[...] = scaled.T
    carry_ref[...] = cum[t - 1:t, :]


def _cumsum_tokens(x, out_scale):
    s, n = x.shape
    t = min(CUMSUM_ROWS, s)
    return pl.pallas_call(
        functools.partial(_cumsum_kernel, t=t, out_scale=out_scale),
        grid=(s // t,),
        in_specs=[pl.BlockSpec((t, n), lambda i: (i, 0))],
        out_specs=[pl.BlockSpec((t, n), lambda i: (i, 0)), pl.BlockSpec((n, t), lambda i: (0, i))],
        out_shape=[jax.ShapeDtypeStruct((s, n), F32), jax.ShapeDtypeStruct((n, s), F32)],
        scratch_shapes=[pltpu.VMEM((1, n), F32)],
        compiler_params=_cparams("arbitrary"),
        name="cumsum",
    )(x)


def _compress_kernel(a_ref, pos_ref, w1_ref, w2_ref, o_ref):
    a = a_ref[0]
    nc = a.shape[0]
    p1 = jnp.dot((a + pos_ref[0, 0:1, :]).astype(BF), w1_ref[0, 0], preferred_element_type=F32)
    p2 = jnp.dot((a + pos_ref[0, 1:2, :]).astype(BF), w1_ref[0, 1], preferred_element_type=F32)
    h = p1 + pltpu.roll(p2, nc - 1, 0)
    act = h * jax.nn.sigmoid(h)
    o_ref[0] = jnp.dot(act.astype(BF), w2_ref[0], preferred_element_type=F32).astype(o_ref.dtype)


def _nsa_compress(kv_cmp, pos, w1, w2):
    c, nc, gw = kv_cmp.shape
    return pl.pallas_call(
        _compress_kernel,
        grid=(c,),
        in_specs=[pl.BlockSpec((1, nc, gw), lambda i: (i, 0, 0)),
                  pl.BlockSpec((1, 2, gw), lambda i: (i // 2, 0, 0)),
                  pl.BlockSpec((1, 2, gw, CMP_HIDDEN), lambda i: (i // 2, 0, 0, 0)),
                  pl.BlockSpec((1, CMP_HIDDEN, NSA_HEAD_DIM), lambda i: (i // 2, 0, 0))],
        out_specs=pl.BlockSpec((1, nc, NSA_HEAD_DIM), lambda i: (i, 0, 0)),
        out_shape=jax.ShapeDtypeStruct((c, nc, NSA_HEAD_DIM), BF),
        compiler_params=_cparams("arbitrary"),
        name="nsa_compress",
    )(kv_cmp, pos, w1, w2)


def _bias_table_kernel(brev_ref, basc_ref, tbl_ref, tb_ref, asc_ref, *, s, mc):
    tbl = tbl_ref[0] * LOG2E

    def lookup(bkt):
        out = jnp.zeros(bkt.shape, F32)
        for b in range(REL_BUCKETS):
            out = jnp.where(bkt == b, tbl[:, b:b + 1], out)
        return out

    rev = lookup(brev_ref[...])
    asc_ref[0] = lookup(basc_ref[...])
    for m in range(mc + 1):
        win = rev[:, s - Q_BLOCK * m:s - Q_BLOCK * m + 2 * Q_BLOCK]
        rolled = pltpu.roll(jnp.broadcast_to(win, (Q_BLOCK, 2 * Q_BLOCK)), Q_BLOCK + 1, 1,
                            stride=1, stride_axis=0)
        tb_ref[0, m] = rolled[:, :Q_BLOCK]


def _t5_bucket(dist):
    max_exact = REL_BUCKETS // 2
    d = jnp.maximum(dist, 0)
    ratio = jnp.log(jnp.maximum(d, max_exact).astype(F32) / max_exact) / math.log(REL_MAX_DIST / max_exact)
    large = jnp.minimum(max_exact + (ratio * (REL_BUCKETS - max_exact)).astype(jnp.int32), REL_BUCKETS - 1)
    return jnp.where(d < max_exact, d, large)


def _bias_tables(rel_bias, s):
    assert Q_BLOCK == SUBLANES * CMP_STRIDE
    h = rel_bias.shape[1]
    qblocks = s // Q_BLOCK
    max_exact = REL_BUCKETS // 2
    d_const = int(math.ceil(max_exact * (REL_MAX_DIST / max_exact)
                            ** ((REL_BUCKETS - max_exact - 1) / (REL_BUCKETS - max_exact)))) + 1
    mc = min(-(-(d_const + Q_BLOCK - 1) // Q_BLOCK), qblocks - 1)
    nrev = s + 2 * Q_BLOCK
    nasc = s + 3 * Q_BLOCK
    brev = _t5_bucket(s + Q_BLOCK - 1 - jnp.arange(nrev, dtype=jnp.int32))[None, :]
    basc = _t5_bucket(jnp.arange(nasc, dtype=jnp.int32) - 2 * Q_BLOCK)[None, :]
    tb, asc = pl.pallas_call(
        functools.partial(_bias_table_kernel, s=s, mc=mc),
        grid=(h,),
        in_specs=[pl.BlockSpec((1, nrev), lambda i: (0, 0)),
                  pl.BlockSpec((1, nasc), lambda i: (0, 0)),
                  pl.BlockSpec((1, 1, REL_BUCKETS), lambda i: (i, 0, 0))],
        out_specs=[pl.BlockSpec((1, mc + 1, Q_BLOCK, Q_BLOCK), lambda i: (i, 0, 0, 0)),
                   pl.BlockSpec((1, 1, nasc), lambda i: (i, 0, 0))],
        out_shape=[jax.ShapeDtypeStruct((h, mc + 1, Q_BLOCK, Q_BLOCK), F32),
                   jax.ShapeDtypeStruct((h, 1, nasc), F32)],
        compiler_params=_cparams("arbitrary"),
        name="bias_tables",
    )(brev, basc, rel_bias.T.reshape(h, 1, REL_BUCKETS))
    asc = asc[:, 0, :]
    base = 2 * Q_BLOCK - (CMP_LEN - 1)
    segs = [asc[:, base - CMP_STRIDE * nn:base - CMP_STRIDE * nn + s].reshape(h, qblocks, Q_BLOCK)
            for nn in range(SUBLANES)]
    trc = jnp.flip(jnp.stack(segs, axis=2), axis=1).reshape(h, qblocks * SUBLANES, Q_BLOCK)
    return tb, jnp.pad(trc, ((0, 0), (0, qblocks * SUBLANES), (0, 0)))


def _nsa_cmp_kernel(q_ref, kc_ref, vct_ref, trc_ref, smt_ref, gate_ref, oc_ref, sel_ref, *, qblocks, topn, nb):
    for c in range(nb):
        _nsa_cmp_block(pl.program_id(1) * nb + c, slice(c * Q_BLOCK, (c + 1) * Q_BLOCK), q_ref, kc_ref, vct_ref,
                       trc_ref, smt_ref, gate_ref, oc_ref, sel_ref, qblocks=qblocks, topn=topn)


def _nsa_cmp_block(qb, rows, q_ref, kc_ref, vct_ref, trc_ref, smt_ref, gate_ref, oc_ref, sel_ref, *, qblocks, topn):
    rq = NSA_GROUP * Q_BLOCK
    q = q_ref[:, rows, :].reshape(rq, NSA_HEAD_DIM)
    kc = kc_ref[0]
    nc = kc.shape[0]
    nslc = smt_ref.shape[0]
    s = lax.dot_general(kc, q, _NT, preferred_element_type=F32)
    n_io = lax.broadcasted_iota(jnp.int32, (nc, Q_BLOCK), 0)
    i_io = lax.broadcasted_iota(jnp.int32, (nc, Q_BLOCK), 1)
    mask = (n_io * CMP_STRIDE + (CMP_LEN - 1)) <= (qb * Q_BLOCK + i_io)
    off = pl.multiple_of((qblocks - 1 - qb) * SUBLANES, SUBLANES)
    ps = []
    for r in range(NSA_GROUP):
        l = jnp.where(mask, s[:, r * Q_BLOCK:(r + 1) * Q_BLOCK] + trc_ref[r, pl.ds(off, nc), :], NEG_INF)
        m = jnp.max(l, axis=0, keepdims=True)
        p = jnp.where(mask, jnp.exp2(l - m), 0.0)
        den = jnp.maximum(jnp.sum(p, axis=0, keepdims=True), 1e-30)
        ps.append(p / den)
    p_all = jnp.concatenate(ps, axis=1).astype(BF)
    oc_t = jnp.dot(vct_ref[0], p_all, preferred_element_type=F32)
    imp4 = jnp.dot(smt_ref[...], p_all, preferred_element_type=F32)
    imp = imp4[:, 0:Q_BLOCK]
    for r in range(1, NSA_GROUP):
        imp = imp + imp4[:, r * Q_BLOCK:(r + 1) * Q_BLOCK]

    j_io = lax.broadcasted_iota(jnp.int32, (nslc, Q_BLOCK), 0)
    t_io = qb * Q_BLOCK + lax.broadcasted_iota(jnp.int32, (nslc, Q_BLOCK), 1)
    cur = t_io >> int(math.log2(SLC_LEN))
    forced = jnp.logical_or(j_io == 0, jnp.logical_or(j_io == cur, j_io == cur - 1))
    n_forced = 3
    val = jnp.where(forced, -3e38, jnp.where(j_io <= cur, imp, -1e9))
    sel = jnp.where(forced, 1.0, 0.0)
    for _ in range(max(topn - n_forced, 0)):
        mx = jnp.max(val, axis=0, keepdims=True)
        cand = jnp.where(val == mx, j_io, nslc)
        jmin = jnp.min(cand, axis=0, keepdims=True)
        pick = j_io == jmin
        sel = jnp.where(pick, 1.0, sel)
        val = jnp.where(pick, -3e38, val)
    sel_ref[0, rows, :] = jnp.where(sel.T > 0.5, 0.0, NEG_INF).astype(sel_ref.dtype)

    gates = gate_ref[0, rows, :]
    for r in range(NSA_GROUP):
        o_r = oc_t[:, r * Q_BLOCK:(r + 1) * Q_BLOCK].T
        oc_ref[rows, r * NSA_HEAD_DIM:(r + 1) * NSA_HEAD_DIM] = o_r * gates[:, 3 * r:3 * r + 1]


def _nsa_cmp(q, kc, vct, trc, smt, gates):
    h, s, d = q.shape
    g = NSA_KV_HEADS
    qblocks = s // Q_BLOCK
    nc = kc.shape[1]
    nslc = smt.shape[0]
    topn = min(SLC_TOPN, nslc)
    nb = NSA_CMP_BLOCKS_PER_STEP
    qrows = nb * Q_BLOCK
    return pl.pallas_call(
        functools.partial(_nsa_cmp_kernel, qblocks=qblocks, topn=topn, nb=nb),
        grid=(g, qblocks // nb),
        in_specs=[pl.BlockSpec((NSA_GROUP, qrows, d), lambda gg, i: (gg, i, 0)),
                  pl.BlockSpec((1, nc, d), lambda gg, i: (gg, 0, 0)),
                  pl.BlockSpec((1, d, nc), lambda gg, i: (gg, 0, 0)),
                  pl.BlockSpec((NSA_GROUP, trc.shape[1], Q_BLOCK), lambda gg, i: (gg, 0, 0)),
                  pl.BlockSpec(smt.shape, lambda gg, i: (0, 0)),
                  pl.BlockSpec((1, qrows, LANES), lambda gg, i: (gg, i, 0))],
        out_specs=[pl.BlockSpec((qrows, NSA_GROUP * d), lambda gg, i: (i, gg)),
                   pl.BlockSpec((1, qrows, nslc), lambda gg, i: (gg, i, 0))],
        out_shape=[jax.ShapeDtypeStruct((s, h * d), F32),
                   jax.ShapeDtypeStruct((g, s, nslc), BF)],
        compiler_params=_cparams("arbitrary", "arbitrary"),
        name="nsa_cmp",
    )(q, kc, vct, trc, smt, gates)


def _nsa_sw_kernel(q_ref, kv_ref, tb_ref, sel_ref, et_ref, gate_ref, oc_ref,
                   o_ref, s0_ref, s1_ref, m_ref, acc_ref, part_ref, w_ref, *, tk, mc, nch):
    i = pl.program_id(0)
    ngrp = NSA_KV_HEADS
    rq = NSA_GROUP * Q_BLOCK
    nsub = tk // Q_BLOCK
    s_bufs = (s0_ref, s1_ref)
    qbs = [i * nch + c for c in range(nch)]
    qs = [[q_ref[g * NSA_GROUP:(g + 1) * NSA_GROUP, c * Q_BLOCK:(c + 1) * Q_BLOCK, :].reshape(rq, NSA_HEAD_DIM)
           for c in range(nch)] for g in range(ngrp)]

    def bias_tile(g, r, c, kb0, nblk):
        return jnp.concatenate(
            [tb_ref[g * NSA_GROUP + r, jnp.clip(qbs[c] - (kb0 + b), 0, mc)] for b in range(nblk)], axis=1)

    def gated(g, c, o, branch, base):
        gates = gate_ref[g, c * Q_BLOCK:(c + 1) * Q_BLOCK, :]
        outs = []
        for r in range(NSA_GROUP):
            o_r = o[r * Q_BLOCK:(r + 1) * Q_BLOCK, :] * gates[:, 3 * r + branch:3 * r + branch + 1]
            outs.append(base[:, r * NSA_HEAD_DIM:(r + 1) * NSA_HEAD_DIM] + o_r)
        return jnp.concatenate(outs, axis=1)

    _init_state(m_ref, acc_ref)
    qa = [[jnp.concatenate([qs[g][c], jnp.concatenate(
        [sel_ref[g, c * Q_BLOCK:(c + 1) * Q_BLOCK, :]] * NSA_GROUP, axis=0)], axis=1) for c in range(nch)]
        for g in range(ngrp)]

    def pre(g, j, causal):
        off = pl.multiple_of(j * tk, tk)
        ka = jnp.concatenate([kv_ref[g, pl.ds(off, tk), :], et_ref[pl.ds(off, tk), :]], axis=1)
        for c in range(nch):
            s = lax.dot_general(qa[g][c], ka, _NT, preferred_element_type=F32)
            if causal:
                kk = off + lax.broadcasted_iota(jnp.int32, (Q_BLOCK, tk), 1)
                ii = qbs[c] * Q_BLOCK + lax.broadcasted_iota(jnp.int32, (Q_BLOCK, tk), 0)
                future = jnp.where(kk <= ii, 0.0, NEG_INF)
            for r in range(NSA_GROUP):
                bias = bias_tile(g, r, c, j * nsub, nsub)
                if causal:
                    bias = bias + future
                s_bufs[g][c, r * Q_BLOCK:(r + 1) * Q_BLOCK, :] = s[r * Q_BLOCK:(r + 1) * Q_BLOCK, :] + bias

    def process(g, j):
        vp = _with_ones(kv_ref[ngrp + g, pl.ds(pl.multiple_of(j * tk, tk), tk), :])
        for c in range(nch):
            _flash_update(s_bufs[g].at[c], vp, m_ref.at[g * nch + c], acc_ref.at[g * nch + c])

    n_past = (i * nch) // nsub

    def body(t, carry):
        pre(1, t, False)
        process(0, t)
        pre(0, t + 1, False)
        process(1, t)
        return carry

    wk = WINDOW + Q_BLOCK
    wblk = wk // Q_BLOCK

    def win_off(c):
        kb0 = jnp.maximum(qbs[c] - WINDOW // Q_BLOCK, 0)
        return kb0, pl.multiple_of(kb0 * Q_BLOCK, Q_BLOCK)

    def win_pre(g, c):
        kb0, off = win_off(c)
        s = lax.dot_general(qs[g][c], kv_ref[2 * ngrp + g, pl.ds(off, wk), :], _NT, preferred_element_type=F32)
        rel = (qbs[c] * Q_BLOCK + lax.broadcasted_iota(jnp.int32, (Q_BLOCK, wk), 0)) - (
            off + lax.broadcasted_iota(jnp.int32, (Q_BLOCK, wk), 1))
        outside = jnp.where(jnp.logical_and(rel >= 0, rel < WINDOW), 0.0, NEG_INF)
        for r in range(NSA_GROUP):
            w_ref[g * nch + c, r * Q_BLOCK:(r + 1) * Q_BLOCK, :] = (
                s[r * Q_BLOCK:(r + 1) * Q_BLOCK, :] + (bias_tile(g, r, c, kb0, wblk) + outside))

    def win_proc(g, c):
        _, off = win_off(c)
        l_ref = w_ref.at[g * nch + c]
        p = jnp.exp2(l_ref[...] - jnp.max(l_ref[...], axis=1, keepdims=True)).astype(BF)
        ow = jnp.dot(p, _with_ones(kv_ref[3 * ngrp + g, pl.ds(off, wk), :]), preferred_element_type=F32)
        rows = slice(c * Q_BLOCK, (c + 1) * Q_BLOCK)
        cols = slice(g * rq, (g + 1) * rq)
        part_ref[rows, cols] = gated(g, c, _normalized(ow), 2, oc_ref[rows, cols])

    pre(0, n_past, True)
    pre(1, n_past, True)
    win_pre(0, 0)
    win_pre(0, 1)
    process(0, n_past)
    win_pre(1, 0)
    win_pre(1, 1)
    pre(0, 0, False)
    process(1, n_past)
    for g in range(ngrp):
        for c in range(nch):
            win_proc(g, c)

    lax.fori_loop(0, n_past, body, 0)

    for g in range(ngrp):
        cols = slice(g * rq, (g + 1) * rq)
        for c in range(nch):
            rows = slice(c * Q_BLOCK, (c + 1) * Q_BLOCK)
            o_ref[rows, cols] = gated(g, c, _normalized(acc_ref[g * nch + c]), 1,
                                      part_ref[rows, cols]).astype(o_ref.dtype)


def _nsa_sw(q, kvsw, tb, sel, et, gates, oc):
    h, s, d = q.shape
    g = NSA_KV_HEADS
    nch = NSA_BLOCKS_PER_STEP
    qrows = nch * Q_BLOCK
    tk = min(NSA_KEY_TILE, s)
    mc = tb.shape[1] - 1
    nslc = sel.shape[2]
    rq = NSA_GROUP * Q_BLOCK
    assert tk % qrows == 0 and WINDOW + Q_BLOCK <= s and NSA_GROUP * d == rq

    def resident(arr):
        return pl.BlockSpec(arr.shape, lambda i, nd=arr.ndim: (0,) * nd, pipeline_mode=pl.Buffered(1))

    return pl.pallas_call(
        functools.partial(_nsa_sw_kernel, tk=tk, mc=mc, nch=nch),
        grid=(s // qrows,),
        in_specs=[pl.BlockSpec((h, qrows, d), lambda i: (0, i, 0)),
                  resident(kvsw), resident(tb),
                  pl.BlockSpec((g, qrows, nslc), lambda i: (0, i, 0)),
                  resident(et),
                  pl.BlockSpec((g, qrows, LANES), lambda i: (0, i, 0)),
                  pl.BlockSpec((qrows, h * d), lambda i: (i, 0))],
        out_specs=pl.BlockSpec((qrows, h * d), lambda i: (i, 0)),
        out_shape=jax.ShapeDtypeStruct((s, h * d), BF),
        scratch_shapes=[pltpu.VMEM((nch, rq, tk), F32), pltpu.VMEM((nch, rq, tk), F32),
                        pltpu.VMEM((g * nch, rq, 1), F32), pltpu.VMEM((g * nch, rq, 2 * LANES), F32),
                        pltpu.VMEM((qrows, h * d), F32),
                        pltpu.VMEM((g * nch, rq, WINDOW + Q_BLOCK), F32)],
        compiler_params=_cparams("arbitrary"),
        name="nsa_sel_win",
    )(q, kvsw, tb, sel, et, gates, oc)


def _selection_map_t(nc, nslc):
    n = np.arange(nc)[None, :] * CMP_STRIDE
    j0 = np.arange(nslc)[:, None] * SLC_LEN
    valid = np.arange(nc)[None, :] < nc - 1
    return jnp.asarray(((n < j0 + SLC_LEN) & (n + CMP_LEN > j0) & valid).astype(np.float32), dtype=BF)


def _block_onehot(s, nslc):
    tok = np.arange(s)[:, None]
    j = np.arange(nslc)[None, :]
    return jnp.asarray((tok // SLC_LEN == j).astype(np.float32), dtype=BF)


def _rope_tables(positions):
    half = MLA_ROPE // 2
    inv = ROPE_THETA ** (-jnp.arange(half, dtype=F32) / half)
    ang = positions.astype(F32)[:, None] * inv
    cos, sin = jnp.cos(ang), jnp.sin(ang)
    z = jnp.zeros_like(cos)
    zpad = jnp.zeros((positions.shape[0], LANES - MLA_ROPE), F32)
    c = jnp.concatenate([cos, cos, zpad], axis=1)
    s1 = jnp.concatenate([-sin, z, zpad], axis=1)
    s2 = jnp.concatenate([z, sin, zpad], axis=1)
    return c, s1, s2


def _pad_cols(w, n):
    return jnp.pad(w, ((0, 0), (0, n - w.shape[1])))


def _pad_rows(w, n):
    return jnp.pad(w, ((0, n - w.shape[0]), (0, 0)))


def _nsa_mla_mixer(hn, pos, rel_bias, w_in_all, w_out_all, e, gate_b, pos_k, w1_k, w2_k, pos_v, w1_v, w2_v,
                   q_norm, w_uq, kv_norm, w_ukv, x, g1):
    s = hn.shape[0]
    d = NSA_HEAD_DIM
    nq = NSA_HEADS * d
    nkv = 2 * NSA_KV_HEADS * d
    w_in_t = jnp.swapaxes(w_in_all, 1, 2)
    dm = w_in_t.shape[2]
    o0 = nq + 3 * nkv
    w_g_t = w_in_t[e, o0:o0 + 3 * NSA_HEADS]; o0 += 3 * NSA_HEADS
    w_lat_t = w_in_t[e, o0:o0 + MLA_Q_RANK + MLA_KV_RANK]; o0 += MLA_Q_RANK + MLA_KV_RANK
    w_kr_t = w_in_t[e, o0:o0 + MLA_ROPE]

    q_nsa = _proj([hn], [(w_in_t, dm, (e, 0), 0)], functools.partial(_epi_scale, d ** -0.5 * LOG2E), n=nq,
                  out_dtype=BF, head_major=True, tn=PROJ_WIDE_COLS, w_transposed=True, name="proj_q_nsa")
    kv_cmp = _proj([hn], [(w_in_t, dm, (e, 0), nq)], _epi_id, n=nkv, out_dtype=F32, head_major=True,
                   w_transposed=True, name="proj_kv_cmp")
    kv_sw = _proj([hn], [(w_in_t, dm, (e, 0), nq + nkv)], _epi_id, n=2 * nkv, out_dtype=BF, head_major=True,
                  w_transposed=True, name="proj_kv_sw")
    per_g = 3 * NSA_GROUP
    w_gp_t = jnp.concatenate([_pad_rows(w_g_t[g * per_g:(g + 1) * per_g], LANES) for g in range(NSA_KV_HEADS)], 0)
    b_gp = jnp.concatenate([_pad_cols(gate_b[None, g * per_g:(g + 1) * per_g], LANES)
                            for g in range(NSA_KV_HEADS)], 1)
    gates = _proj([hn], [(w_gp_t[None], dm, (0, 0), 0)], _epi_sigmoid_bias, [(b_gp, "row")],
                  n=NSA_KV_HEADS * LANES, out_dtype=F32, head_major=True, w_transposed=True,
                  name="proj_gates")

    nc = s // CMP_STRIDE
    gw = CMP_STRIDE * d
    pos_kv = jnp.stack([pos_k.reshape(2, gw), pos_v.reshape(2, gw)])
    w1_kv = jnp.stack([w1_k.reshape(2, gw, CMP_HIDDEN), w1_v.reshape(2, gw, CMP_HIDDEN)]).astype(BF)
    w2_kv = jnp.stack([w2_k, w2_v]).astype(BF)
    kvc = _nsa_compress(kv_cmp.reshape(2 * NSA_KV_HEADS, nc, gw), pos_kv, w1_kv, w2_kv)
    kc = kvc[:NSA_KV_HEADS]
    vct = jnp.swapaxes(kvc[NSA_KV_HEADS:], 1, 2)

    nslc = s // SLC_LEN
    tb, trc = _bias_tables(rel_bias, s)
    smt = _selection_map_t(nc, nslc)
    oc, sel = _nsa_cmp(q_nsa, kc, vct, trc, smt, gates)
    o_nsa = _nsa_sw(q_nsa, kv_sw, tb, sel, _block_onehot(s, nslc), gates, oc)

    nlat = MLA_Q_RANK + MLA_KV_RANK
    lat = _proj([hn], [(w_lat_t[None], dm, (0, 0), 0)], _epi_rmsnorm,
                [(jnp.concatenate([q_norm, kv_norm])[None, :], "row")], n=nlat, out_dtype=BF, tn=MLA_Q_RANK,
                w_transposed=True, name="proj_mla_latent")
    c, s1, s2 = _rope_tables(pos)
    kr = _proj([hn], [(_pad_rows(w_kr_t, LANES)[None], dm, (0, 0), 0)], functools.partial(_epi_rope, 1.0),
               [(c, "rowtile"), (s1, "rowtile"), (s2, "rowtile")], n=LANES, out_dtype=BF, w_transposed=True,
               name="proj_mla_kr")
    scale = (MLA_NOPE + MLA_ROPE) ** -0.5 * LOG2E
    w_uq3 = w_uq.reshape(MLA_Q_RANK, MLA_HEADS, MLA_NOPE + MLA_ROPE)
    w_uqn = w_uq3[:, :, :MLA_NOPE].reshape(MLA_Q_RANK, MLA_HEADS * MLA_NOPE)
    w_uqr = jnp.pad(w_uq3[:, :, MLA_NOPE:], ((0, 0), (0, 0), (0, LANES - MLA_ROPE))).reshape(
        MLA_Q_RANK, MLA_HEADS * LANES)
    qn = _proj([lat], [(w_uqn, MLA_Q_RANK, 0, 0)], functools.partial(_epi_scale, scale),
               n=MLA_HEADS * MLA_NOPE, out_dtype=BF, head_major=True, tn=PROJ_WIDE_COLS, lhs_col_block=[0],
               name="proj_mla_qn")
    qr = _proj([lat], [(w_uqr, MLA_Q_RANK, 0, 0)], functools.partial(_epi_rope, scale),
               [(c, "rowtile"), (s1, "rowtile"), (s2, "rowtile")], n=MLA_HEADS * LANES, out_dtype=BF,
               head_major=True, lhs_col_block=[0], name="proj_mla_qr")
    kvh = _proj([lat], [(w_ukv, MLA_KV_RANK, 0, 0)], _epi_id, n=MLA_HEADS * (MLA_NOPE + MLA_V), out_dtype=BF,
                head_major=True, tn=PROJ_WIDE_COLS, lhs_col_block=[1], name="proj_mla_kv")
    o_mla = _mla_attention(qn, qr, kvh, kr)

    return _proj([o_nsa, o_mla], [(w_out_all, nq, (e, 0), 0), (w_out_all, MLA_HEADS * MLA_V, (e, 1), 0)],
                 _epi_residual, [(x, "tile"), (g1, "row")], n=w_out_all.shape[2], out_dtype=F32, tn=PROJ_WIDE_COLS,
                 name="proj_even_out")


def _fox_mixer(hn, w_in_all, w_out_all, o, f_b, q_norm, k_norm, x, g1):
    d = D_MODEL
    dh = FOX_HEAD_DIM
    w_in_t = jnp.swapaxes(w_in_all, 1, 2)
    w_f_t = w_in_t[o, 3 * d:3 * d + FOX_HEADS]
    w_og_t = w_in_t[o, 3 * d + FOX_HEADS:]
    q = _proj([hn], [(w_in_t, d, (o, 0), 0)], functools.partial(_epi_headnorm, dh ** -0.5 * LOG2E),
              [(q_norm[None, :], "const")], n=d, out_dtype=BF, head_major=True, tn=PROJ_WIDE_COLS, w_transposed=True,
              name="proj_fox_q")
    k = _proj([hn], [(w_in_t, d, (o, 0), d)], functools.partial(_epi_headnorm, 1.0),
              [(k_norm[None, :], "const")], n=d, out_dtype=BF, head_major=True, tn=PROJ_WIDE_COLS, w_transposed=True,
              name="proj_fox_k")
    v = _proj([hn], [(w_in_t, d, (o, 0), 2 * d)], _epi_id, n=d, out_dtype=BF, head_major=True, tn=PROJ_WIDE_COLS,
              w_transposed=True, name="proj_fox_v")
    lf = _proj([hn], [(_pad_rows(w_f_t, LANES)[None], d, (0, 0), 0)], _epi_logsigmoid_bias,
               [(_pad_cols(f_b[None, :], LANES), "row")], n=LANES, out_dtype=F32, w_transposed=True,
               name="proj_fox_f")
    sig_og = _proj([hn], [(w_og_t[None], d, (0, 0), 0)], _epi_sigmoid, n=d, out_dtype=F32, tn=PROJ_WIDE_COLS,
                   w_transposed=True, name="proj_fox_og")
    cum, cum_t = _cumsum_tokens(lf, LOG2E)
    cum_t = cum_t[:FOX_HEADS].reshape(FOX_HEADS, 1, -1)
    att = _fox_attention(q, k, v, cum_t, cum, sig_og)
    return _proj([att], [(w_out_all, d, (o, 0), 0)], _epi_residual, [(x, "tile"), (g1, "row")], n=d,
                 out_dtype=F32, tn=PROJ_WIDE_COLS, name="proj_fox_out")


def kernel(x, c, positions, rel_bias, ada_w, ada_b, norm_mix, norm_ffn, ffn_w1, ffn_w3, ffn_w2, even_w_in, even_w_out, nsa_gate_b, nsa_cmp_pos_k, nsa_cmp_w1_k, nsa_cmp_w2_k, nsa_cmp_pos_v, nsa_cmp_w1_v, nsa_cmp_w2_v, mla_q_norm, mla_w_uq, mla_kv_norm, mla_w_ukv, fox_w_in, fox_w_out, fox_f_b, fox_q_norm, fox_k_norm, final_norm):
    b, s, d = x.shape
    assert b == 1 and d == D_MODEL and s % 1024 == 0
    xs = x[0]
    pos = positions[0]
    mod = _adaln(c, ada_w, ada_b)
    depth = ada_w.shape[0]
    hn = None
    for i in range(depth):
        sh1, sc1, g1, sh2, sc2, g2 = [mod[i:i + 1, k * d:(k + 1) * d] for k in range(6)]
        if hn is None:
            hn = _normmod(xs, norm_mix[i][None, :], sc1, sh1)
        if i % 2 == 0:
            e = i // 2
            xs = _nsa_mla_mixer(hn, pos, rel_bias, even_w_in, even_w_out, e, nsa_gate_b[e],
                                nsa_cmp_pos_k[e], nsa_cmp_w1_k[e], nsa_cmp_w2_k[e],
                                nsa_cmp_pos_v[e], nsa_cmp_w1_v[e], nsa_cmp_w2_v[e],
                                mla_q_norm[e], mla_w_uq[e], mla_kv_norm[e], mla_w_ukv[e], xs, g1)
        else:
            o = i // 2
            xs = _fox_mixer(hn, fox_w_in, fox_w_out, o, fox_f_b[o], fox_q_norm[o], fox_k_norm[o], xs, g1)
        if i == depth - 1:
            xs = _ffn(xs, norm_ffn[i][None, :], sc2, sh2, g2, ffn_w1, ffn_w3, ffn_w2, i,
                      final_gain=final_norm[None, :])
        else:
            nxt = (norm_mix[i + 1][None, :], mod[i + 1:i + 2, d:2 * d], mod[i + 1:i + 2, 0:d])
            xs, hn = _ffn(xs, norm_ffn[i][None, :], sc2, sh2, g2, ffn_w1, ffn_w3, ffn_w2, i, next_mod=nxt)
    return xs[None]
```
